```python
import math
import jax
import jax.numpy as jnp
from jax import lax
import numpy as np

D_MODEL = 1024
BATCH = 8
SEQ = 4096
DEPTH = 1

EPS = 1e-6
NEG = -1e30
SGU_WIDTH = 1024
SGU_GROUPS = 8
SGU_GROUP_DIM = SGU_WIDTH // SGU_GROUPS
CHUNK = 128
ATT_HEADS = 8
ATT_HEAD_DIM = 128
ATT_WIDTH = ATT_HEADS * ATT_HEAD_DIM
MOBA_BLOCK = 256
MOBA_TOPK = 3
Q_BLOCK = 64
NUM_BUCKETS = 32
MAX_DISTANCE = 128
IN_WIDTH = 2 * SGU_WIDTH + 3 * ATT_WIDTH
N_BRANCH = 2
MEM_LEN = 256
X_HEADS = 4
X_HEAD_DIM = 128
X_WIDTH = X_HEADS * X_HEAD_DIM
N_GROUPS = 8
EXPERTS_PER_GROUP = 8
N_EXPERTS = N_GROUPS * EXPERTS_PER_GROUP
TOP_K_EXPERTS = 2
EXPERT_HIDDEN = 512
MOE_BLOCK = 256

kernel_name = 'hybrid_sgu_moba_hmoe_block'


def rmsnorm(x, g):
    xf = x.astype(jnp.float32)
    y = xf * lax.rsqrt(jnp.mean(xf * xf, axis=-1, keepdims=True) + EPS)
    return (y * g.astype(jnp.float32)).astype(x.dtype)


def rel_bucket(dist):
    n = jnp.maximum(dist, 0)
    max_exact = NUM_BUCKETS // 2
    nf = jnp.maximum(n, 1).astype(jnp.float32)
    large = max_exact + (jnp.log(nf / max_exact) / math.log(MAX_DISTANCE / max_exact)
                         * (NUM_BUCKETS - max_exact)).astype(jnp.int32)
    large = jnp.minimum(large, NUM_BUCKETS - 1)
    return jnp.where(n < max_exact, n, large)


def chunked_sgu(u_pre, v_pre, sgu_norm, w_spatial, b_spatial):
    B, S, _ = u_pre.shape
    u = jax.nn.gelu(u_pre)
    v = rmsnorm(jax.nn.gelu(v_pre), sgu_norm)
    v = v.reshape(B, S // CHUNK, CHUNK, SGU_GROUPS, SGU_GROUP_DIM)
    causal = jnp.tril(jnp.ones((CHUNK, CHUNK), dtype=bool))
    w = jnp.where(causal[None], w_spatial, jnp.zeros((), w_spatial.dtype))
    mixed = jnp.einsum('gts,bcsgd->bctgd', w, v) + b_spatial.T[None, None, :, :, None]
    return u * mixed.reshape(B, S, SGU_WIDTH)


def moba_attention(q, k, v, rel_bias):
    B, S, H, Dh = q.shape
    Sp = -(-S // MOBA_BLOCK) * MOBA_BLOCK
    pad = ((0, 0), (0, Sp - S), (0, 0), (0, 0))
    q, k, v = jnp.pad(q, pad), jnp.pad(k, pad), jnp.pad(v, pad)
    NB = Sp // MOBA_BLOCK
    NQ = Sp // Q_BLOCK
    K = min(MOBA_TOPK, NB)
    scale = Dh ** -0.5
    kb = k.transpose(0, 2, 1, 3).reshape(B, H, NB, MOBA_BLOCK, Dh)
    vb = v.transpose(0, 2, 1, 3).reshape(B, H, NB, MOBA_BLOCK, Dh)
    kmean = jnp.mean(kb.astype(jnp.float32), axis=3)
    qb = q.transpose(0, 2, 1, 3).reshape(B, H, NQ, Q_BLOCK, Dh).transpose(0, 2, 1, 3, 4)
    bias_hb = rel_bias.T.astype(jnp.float32)
    offs = jnp.arange(MOBA_BLOCK, dtype=jnp.int32)
    h_idx = jnp.arange(H)[:, None, None, None]

    def per_batch(args):
        qb_b, kb_b, vb_b, km_b = args

        def per_qblock(a):
            qblk, iq = a
            t = iq * Q_BLOCK + jnp.arange(Q_BLOCK, dtype=jnp.int32)
            cur = (iq * Q_BLOCK) // MOBA_BLOCK
            gate = jnp.einsum('hqd,hnd->hqn', qblk.astype(jnp.float32), km_b)
            past = jnp.arange(NB) < cur
            gate = jnp.where(past[None, None, :], gate, NEG)
            _, idx = lax.top_k(gate, K)
            valid = jnp.arange(K) < cur
            kg = jax.vmap(lambda kbh, ih: kbh[ih])(kb_b, idx)
            vg = jax.vmap(lambda vbh, ih: vbh[ih])(vb_b, idx)
            s_sel = jnp.einsum('hqd,hqrkd->hqrk', qblk, kg).astype(jnp.float32) * scale
            kpos_sel = idx[..., None] * MOBA_BLOCK + offs
            dist_sel = t[None, :, None, None] - kpos_sel
            s_sel = s_sel + bias_hb[h_idx, rel_bucket(dist_sel)]
            s_sel = jnp.where(valid[None, None, :, None], s_sel, NEG)
            ko = lax.dynamic_index_in_dim(kb_b, cur, axis=1, keepdims=False)
            vo = lax.dynamic_index_in_dim(vb_b, cur, axis=1, keepdims=False)
            s_own = jnp.einsum('hqd,hkd->hqk', qblk, ko).astype(jnp.float32) * scale
            dist_own = t[:, None] - (cur * MOBA_BLOCK + offs)[None, :]
            s_own = s_own + bias_hb[:, rel_bucket(dist_own)]
            s_own = jnp.where(dist_own[None] >= 0, s_own, NEG)
            logits = jnp.concatenate([s_sel.reshape(H, Q_BLOCK, K * MOBA_BLOCK), s_own], axis=-1)
            p = jax.nn.softmax(logits, axis=-1).astype(vb_b.dtype)
            p_sel = p[..., :K * MOBA_BLOCK].reshape(H, Q_BLOCK, K, MOBA_BLOCK)
            p_own = p[..., K * MOBA_BLOCK:]
            return (jnp.einsum('hqrk,hqrkd->hqd', p_sel, vg)
                    + jnp.einsum('hqk,hkd->hqd', p_own, vo))

        return lax.map(per_qblock, (qb_b, jnp.arange(NQ, dtype=jnp.int32)))

    out = lax.map(per_batch, (qb, kb, vb, kmean))
    out = out.transpose(0, 1, 3, 2, 4).reshape(B, Sp, H * Dh)
    return out[:, :S]


def token_mixer(xn, w_in, w_gate, b_gate, sgu_norm, w_spatial, b_spatial,
                w_branch_a, w_branch_b, rel_bias, w_o):
    B, S, D = xn.shape
    z = xn @ w_in
    u_pre, v_pre, q, k, v = jnp.split(
        z, [SGU_WIDTH, 2 * SGU_WIDTH, 2 * SGU_WIDTH + ATT_WIDTH, 2 * SGU_WIDTH + 2 * ATT_WIDTH], axis=-1)
    y_a = chunked_sgu(u_pre, v_pre, sgu_norm, w_spatial, b_spatial) @ w_branch_a
    att = moba_attention(q.reshape(B, S, ATT_HEADS, ATT_HEAD_DIM),
                         k.reshape(B, S, ATT_HEADS, ATT_HEAD_DIM),
                         v.reshape(B, S, ATT_HEADS, ATT_HEAD_DIM), rel_bias)
    y_b = att @ w_branch_b
    gates = jax.nn.sigmoid((xn @ w_gate).astype(jnp.float32) + b_gate.astype(jnp.float32))
    gates = gates.reshape(B, S, N_BRANCH, D).astype(xn.dtype)
    merged = gates[:, :, 0] * y_a + gates[:, :, 1] * y_b
    return merged @ w_o


def mem_cross_attention(xn, memn, w_xq, w_xkv, w_xo):
    B, S, _ = xn.shape
    M = memn.shape[1]
    q = (xn @ w_xq).reshape(B, S, X_HEADS, X_HEAD_DIM)
    kv = (memn @ w_xkv).reshape(B, M, 2, X_HEADS, X_HEAD_DIM)
    k, v = kv[:, :, 0], kv[:, :, 1]
    s = jnp.einsum('bqhd,bmhd->bhqm', q, k).astype(jnp.float32) * (X_HEAD_DIM ** -0.5)
    p = jax.nn.softmax(s, axis=-1).astype(v.dtype)
    o = jnp.einsum('bhqm,bmhd->bqhd', p, v).reshape(B, S, X_WIDTH)
    return o @ w_xo


def hier_moe(xn, w_rg, b_rg, w_re, b_re, w_e_gate, w_e_up, w_e_down):
    B, S, D = xn.shape
    T = B * S
    xt = xn.reshape(T, D)
    g_logits = (xt @ w_rg).astype(jnp.float32) + b_rg.astype(jnp.float32)
    p_g = jax.nn.softmax(g_logits, axis=-1)
    gsel = jnp.argmax(g_logits, axis=-1).astype(jnp.int32)
    pg_sel = jnp.take_along_axis(p_g, gsel[:, None], axis=-1)
    e_logits = ((xt @ w_re).astype(jnp.float32) + b_re.astype(jnp.float32)).reshape(T, N_GROUPS, EXPERTS_PER_GROUP)
    e_logits = jnp.take_along_axis(e_logits, gsel[:, None, None], axis=1)[:, 0]
    p_e = jax.nn.softmax(e_logits, axis=-1)
    top_p, top_e = lax.top_k(p_e, TOP_K_EXPERTS)
    top_p = top_p / jnp.sum(top_p, axis=-1, keepdims=True)
    weights = pg_sel * top_p
    expert_id = gsel[:, None] * EXPERTS_PER_GROUP + top_e
    P = T * TOP_K_EXPERTS
    eid = expert_id.reshape(P)
    tok = jnp.repeat(jnp.arange(T, dtype=jnp.int32), TOP_K_EXPERTS)
    wt = weights.reshape(P)
    order = jnp.argsort(eid)
    eid_s, tok_s, wt_s = eid[order], tok[order], wt[order]
    counts = jnp.bincount(eid, length=N_EXPERTS)
    starts = jnp.cumsum(counts) - counts
    padded = (counts + MOE_BLOCK - 1) // MOE_BLOCK * MOE_BLOCK
    pad_ends = jnp.cumsum(padded)
    pad_starts = pad_ends - padded
    dest = pad_starts[eid_s] + jnp.arange(P, dtype=jnp.int32) - starts[eid_s]
    NBLK = -(-P // MOE_BLOCK) + N_EXPERTS
    buf = jnp.zeros((NBLK * MOE_BLOCK, D), xt.dtype).at[dest].set(xt[tok_s])
    blk_expert = jnp.minimum(
        jnp.searchsorted(pad_ends, jnp.arange(NBLK, dtype=jnp.int32) * MOE_BLOCK, side='right'),
        N_EXPERTS - 1)

    def expert_block(a):
        xb, e = a
        hdn = jax.nn.silu(xb @ w_e_gate[e]) * (xb @ w_e_up[e])
        return hdn @ w_e_down[e]

    ybuf = lax.map(expert_block, (buf.reshape(NBLK, MOE_BLOCK, D), blk_expert)).reshape(NBLK * MOE_BLOCK, D)
    y = jnp.zeros((T, D), ybuf.dtype).at[tok_s].add(wt_s[:, None].astype(ybuf.dtype) * ybuf[dest])
    return y.reshape(B, S, D)


def setup_inputs(seed: int = 0) -> dict:
    key = jax.random.key(seed)
    ks = jax.random.split(key, 32)
    f32 = jnp.float32
    L = DEPTH
    D = D_MODEL

    def nrm(k, shape, scale):
        return jax.random.normal(k, shape, f32) * scale

    return {
        'x': nrm(ks[0], (BATCH, SEQ, D), 1.0),
        'mem': nrm(ks[1], (BATCH, MEM_LEN, D), 1.0),
        'norm_mix': 1.0 + nrm(ks[2], (L, D), 0.1),
        'w_in': nrm(ks[3], (L, D, IN_WIDTH), D ** -0.5),
        'w_gate': nrm(ks[4], (L, D, N_BRANCH * D), D ** -0.5),
        'b_gate': nrm(ks[5], (L, N_BRANCH * D), 0.1),
        'sgu_norm': 1.0 + nrm(ks[6], (L, SGU_WIDTH), 0.1),
        'w_spatial': nrm(ks[7], (L, SGU_GROUPS, CHUNK, CHUNK), CHUNK ** -0.5),
        'b_spatial': 1.0 + nrm(ks[8], (L, SGU_GROUPS, CHUNK), 0.1),
        'w_branch_a': nrm(ks[9], (L, SGU_WIDTH, D), SGU_WIDTH ** -0.5),
        'w_branch_b': nrm(ks[10], (L, ATT_WIDTH, D), ATT_WIDTH ** -0.5),
        'rel_bias': nrm(ks[11], (NUM_BUCKETS, ATT_HEADS), 0.3),
        'w_o': nrm(ks[12], (L, D, D), D ** -0.5),
        'norm_x': 1.0 + nrm(ks[13], (L, D), 0.1),
        'norm_mem': 1.0 + nrm(ks[14], (L, D), 0.1),
        'w_xq': nrm(ks[15], (L, D, X_WIDTH), D ** -0.5),
        'w_xkv': nrm(ks[16], (L, D, 2 * X_WIDTH), D ** -0.5),
        'w_xo': nrm(ks[17], (L, X_WIDTH, D), X_WIDTH ** -0.5),
        'norm_ffn': 1.0 + nrm(ks[18], (L, D), 0.1),
        'w_router_group': nrm(ks[19], (L, D, N_GROUPS), D ** -0.5),
        'b_router_group': nrm(ks[20], (L, N_GROUPS), 0.01),
        'w_router_expert': nrm(ks[21], (L, D, N_EXPERTS), D ** -0.5),
        'b_router_expert': nrm(ks[22], (L, N_EXPERTS), 0.01),
        'w_e_gate': nrm(ks[23], (L, N_EXPERTS, D, EXPERT_HIDDEN), D ** -0.5),
        'w_e_up': nrm(ks[24], (L, N_EXPERTS, D, EXPERT_HIDDEN), D ** -0.5),
        'w_e_down': nrm(ks[25], (L, N_EXPERTS, EXPERT_HIDDEN, D), EXPERT_HIDDEN ** -0.5),
        'norm_final': 1.0 + nrm(ks[26], (D,), 0.1),
    }


def reference(x, mem, norm_mix, w_in, w_gate, b_gate, sgu_norm, w_spatial, b_spatial,
              w_branch_a, w_branch_b, rel_bias, w_o, norm_x, norm_mem, w_xq, w_xkv, w_xo,
              norm_ffn, w_router_group, b_router_group, w_router_expert, b_router_expert,
              w_e_gate, w_e_up, w_e_down, norm_final):
    h = x
    for l in range(DEPTH):
        h = h + token_mixer(rmsnorm(h, norm_mix[l]), w_in[l], w_gate[l], b_gate[l], sgu_norm[l],
                            w_spatial[l], b_spatial[l], w_branch_a[l], w_branch_b[l], rel_bias, w_o[l])
        h = h + mem_cross_attention(rmsnorm(h, norm_x[l]), rmsnorm(mem, norm_mem[l]),
                                    w_xq[l], w_xkv[l], w_xo[l])
        h = h + hier_moe(rmsnorm(h, norm_ffn[l]), w_router_group[l], b_router_group[l],
                         w_router_expert[l], b_router_expert[l], w_e_gate[l], w_e_up[l], w_e_down[l])
    return rmsnorm(h, norm_final)
```

```python
import functools
import math

import numpy as np
import jax
import jax.numpy as jnp
from jax import lax
from jax.experimental import pallas as pl
from jax.experimental.pallas import tpu as pltpu

F32 = jnp.float32
BF16 = jnp.bfloat16

EPS = 1e-6
NEG = -1e30
LANE = 128

SGU_GROUPS = 8
CHUNK = 128
ATT_HEADS = 8
ATT_HEAD_DIM = 128
MOBA_BLOCK = 256
MOBA_TOPK = 3
NUM_BUCKETS = 32
MAX_DISTANCE = 128
X_HEADS = 4
X_HEAD_DIM = 128
N_GROUPS = 8
EXPERTS_PER_GROUP = 8
N_EXPERTS = N_GROUPS * EXPERTS_PER_GROUP
MOE_BLOCK = 256

VMEM_LIMIT = 56 * 1024 * 1024

_NT = (((1,), (1,)), ((), ()))


def _rms(x, g):
    return x * lax.rsqrt(jnp.mean(x * x, axis=-1, keepdims=True) + EPS) * g


def _gelu(x):
    c = math.sqrt(2.0 / math.pi)
    return x * (0.5 * (1.0 + jnp.tanh(c * (x + 0.044715 * (x * x * x)))))


def _dot(a, b):
    return jnp.dot(a, b, preferred_element_type=F32)


def _const_spec(shape):
    nd = len(shape)
    return pl.BlockSpec(shape, lambda *_: (0,) * nd, pipeline_mode=pl.Buffered(1))


def _params(sem):
    return pltpu.CompilerParams(dimension_semantics=sem, vmem_limit_bytes=VMEM_LIMIT)


def _memkv_kernel(mem_ref, g_ref, w_ref, o_ref):
    mn = _rms(mem_ref[...], g_ref[...]).astype(BF16)
    o_ref[...] = _dot(mn, w_ref[...]).astype(BF16)


def _memkv(mem2d, g, w_bf):
    n, d = mem2d.shape
    tm = 256
    return pl.pallas_call(
        _memkv_kernel,
        out_shape=jax.ShapeDtypeStruct((n, w_bf.shape[1]), BF16),
        grid=(n // tm,),
        in_specs=[pl.BlockSpec((tm, d), lambda i: (i, 0)),
                  _const_spec(g.shape), _const_spec(w_bf.shape)],
        out_specs=pl.BlockSpec((tm, w_bf.shape[1]), lambda i: (i, 0)),
        compiler_params=_params(("parallel",)),
        name="memkv",
    )(mem2d, g, w_bf)


def _inproj_kernel(x_ref, g_ref, win_ref, sg_ref, wsp_ref, bsp_ref,
                   sgu_ref, q_ref, k_ref, v_ref, km_ref, *, tm, width, scale):
    xn = _rms(x_ref[...], g_ref[...]).astype(BF16)
    u = _gelu(_dot(xn, win_ref[:, 0:width]))
    vv = _gelu(_dot(xn, win_ref[:, width:2 * width]))
    vn = _rms(vv, sg_ref[...]).astype(BF16)
    gd = width // SGU_GROUPS
    row = lax.broadcasted_iota(jnp.int32, (CHUNK, CHUNK), 0)
    col = lax.broadcasted_iota(jnp.int32, (CHUNK, CHUNK), 1)
    causal = col <= row
    wsp = [jnp.where(causal, wsp_ref[g], 0.0).astype(BF16) for g in range(SGU_GROUPS)]
    bias = bsp_ref[...]
    for c in range(tm // CHUNK):
        rs = slice(c * CHUNK, (c + 1) * CHUNK)
        mixed = jnp.concatenate(
            [_dot(wsp[g], vn[rs, g * gd:(g + 1) * gd]) for g in range(SGU_GROUPS)], axis=1)
        sgu_ref[rs, :] = (u[rs, :] * (mixed + bias)).astype(BF16)
    q_ref[...] = (_dot(xn, win_ref[:, 2 * width:3 * width]) * scale).astype(BF16)
    k = _dot(xn, win_ref[:, 3 * width:4 * width])
    k_ref[...] = k.astype(BF16)
    for j in range(tm // MOBA_BLOCK):
        km_ref[0, j:j + 1, :] = jnp.mean(k[j * MOBA_BLOCK:(j + 1) * MOBA_BLOCK, :], axis=0, keepdims=True)
    v_ref[...] = _dot(xn, win_ref[:, 4 * width:5 * width]).astype(BF16)


def _inproj(x2d, g, win_bf, sg, wsp, bsp_full, tm):
    t, d = x2d.shape
    width = win_bf.shape[1] // 5
    nkm = tm // MOBA_BLOCK
    row_spec = pl.BlockSpec((tm, width), lambda i: (i, 0))
    act = jax.ShapeDtypeStruct((t, width), BF16)
    return pl.pallas_call(
        functools.partial(_inproj_kernel, tm=tm, width=width, scale=ATT_HEAD_DIM ** -0.5),
        out_shape=(act, act, act, act, jax.ShapeDtypeStruct((t // tm, nkm, width), F32)),
        grid=(t // tm,),
        in_specs=[pl.BlockSpec((tm, d), lambda i: (i, 0)),
                  _const_spec(g.shape), _const_spec(win_bf.shape), _const_spec(sg.shape),
                  _const_spec(wsp.shape), _const_spec(bsp_full.shape)],
        out_specs=(row_spec, row_spec, row_spec, row_spec,
                   pl.BlockSpec((1, nkm, width), lambda i: (i, 0, 0))),
        compiler_params=_params(("parallel",)),
        name="inproj",
    )(x2d, g, win_bf, sg, wsp, bsp_full)


def _bucket_uppers():
    d = np.arange(0, 4 * MAX_DISTANCE)
    max_exact = NUM_BUCKETS // 2
    nf = np.maximum(d, 1).astype(np.float32)
    large = max_exact + (np.log(nf / max_exact) / math.log(MAX_DISTANCE / max_exact)
                         * (NUM_BUCKETS - max_exact)).astype(np.int32)
    bucket = np.where(d < max_exact, d, np.minimum(large, NUM_BUCKETS - 1))
    return [int(d[bucket > b].min()) for b in range(NUM_BUCKETS - 1)]


def _relbias_kernel(rb_ref, own_ref, prev_ref, *, tq, uppers):
    h = pl.program_id(0)
    i = lax.broadcasted_iota(jnp.int32, (tq, tq), 0)
    j = lax.broadcasted_iota(jnp.int32, (tq, tq), 1)

    def table(dist):
        val = jnp.full((tq, tq), rb_ref[h, NUM_BUCKETS - 1], F32)
        for b in range(NUM_BUCKETS - 2, -1, -1):
            val = jnp.where(dist < uppers[b], rb_ref[h, b], val)
        return val

    d = i - j
    own_ref[0] = jnp.where(d >= 0, table(d), NEG)
    prev_ref[0] = table(d + tq)


def _relbias(rel_bias_t, tq):
    nh = rel_bias_t.shape[0]
    tile = jax.ShapeDtypeStruct((nh, tq, tq), F32)
    spec = pl.BlockSpec((1, tq, tq), lambda h: (h, 0, 0))
    return pl.pallas_call(
        functools.partial(_relbias_kernel, tq=tq, uppers=_bucket_uppers()),
        out_shape=(tile, tile),
        grid=(nh,),
        in_specs=[pl.BlockSpec(memory_space=pltpu.SMEM)],
        out_specs=(spec, spec),
        compiler_params=_params(("parallel",)),
        name="relbias",
    )(rel_bias_t)


def _attn_kernel(c31_ref, q_ref, k_ref, v_ref, km_ref, own_ref, prev_ref, o_ref,
                 selb_scr, m_scr, l_scr, acc_scr, *, tq, nb):
    h = pl.program_id(1)
    cur = pl.program_id(2)
    q = q_ref[...]

    gate = lax.dot_general(km_ref[0], q.astype(F32), _NT,
                           precision=lax.Precision.HIGHEST, preferred_element_type=F32)
    blk = lax.broadcasted_iota(jnp.int32, (nb, tq), 0)
    past = blk < cur
    g = jnp.where(past, gate, NEG)
    sel = jnp.zeros((nb, tq), F32)
    for _ in range(MOBA_TOPK):
        mx = jnp.max(g, axis=0, keepdims=True)
        first = jnp.min(jnp.where(g == mx, blk, nb), axis=0, keepdims=True)
        hit = blk == first
        sel = jnp.where(hit & past, 1.0, sel)
        g = jnp.where(hit, -jnp.inf, g)
    negm_t = ((sel - 1.0) * (-NEG)).T
    for r in range(nb):
        selb_scr[r] = jnp.broadcast_to(negm_t[:, r:r + 1], (tq, LANE))

    def block_kv(n):
        rows = pl.ds(pl.multiple_of(n * tq, tq), tq)
        return k_ref[rows, :], v_ref[rows, :]

    def scores(n, add_lo, add_hi):
        kb, vb = block_kv(n)
        s = lax.dot_general(q, kb, _NT, preferred_element_type=F32)
        return jnp.concatenate([s[:, :LANE] + add_lo, s[:, LANE:] + add_hi], axis=1), vb

    def update(s, vb):
        m_old = m_scr[...]
        m_new = jnp.maximum(m_old, jnp.max(s, axis=-1, keepdims=True))
        alpha = jnp.exp(m_old - m_new)
        p = jnp.exp(s - m_new)
        l_scr[...] = alpha * l_scr[...] + jnp.sum(p, axis=-1, keepdims=True)
        acc_scr[...] = alpha * acc_scr[...] + _dot(p.astype(BF16), vb)
        m_scr[...] = m_new

    own = own_ref[0]
    s, vb = scores(cur, own[:, :LANE], own[:, LANE:])
    m0 = jnp.max(s, axis=-1, keepdims=True)
    p = jnp.exp(s - m0)
    m_scr[...] = m0
    l_scr[...] = jnp.sum(p, axis=-1, keepdims=True)
    acc_scr[...] = _dot(p.astype(BF16), vb)

    @pl.when(cur >= 1)
    def _():
        prev = prev_ref[0]
        mask = selb_scr[cur - 1]
        s, vb = scores(cur - 1, prev[:, :LANE] + mask, prev[:, LANE:] + mask)
        update(s, vb)

    c31 = c31_ref[h]

    def far(n, carry):
        add = selb_scr[n] + c31
        s, vb = scores(n, add, add)
        update(s, vb)
        return carry

    lax.fori_loop(0, jnp.maximum(cur - 1, 0), far, 0)
    o_ref[...] = (acc_scr[...] / l_scr[...]).astype(BF16)


def _attn(c31, q, k, v, kmean, own, prev, batch, seq):
    t, width = q.shape
    tq = MOBA_BLOCK
    nb = seq // tq
    nh = width // ATT_HEAD_DIM
    kv_spec = pl.BlockSpec((seq, ATT_HEAD_DIM), lambda b, h, i: (b, h))
    tile_spec = pl.BlockSpec((1, tq, tq), lambda b, h, i: (h, 0, 0))
    q_spec = pl.BlockSpec((tq, ATT_HEAD_DIM), lambda b, h, i: (b * nb + i, h))
    return pl.pallas_call(
        functools.partial(_attn_kernel, tq=tq, nb=nb),
        out_shape=jax.ShapeDtypeStruct((t, width), BF16),
        grid=(batch, nh, nb),
        in_specs=[pl.BlockSpec(memory_space=pltpu.SMEM),
                  q_spec, kv_spec, kv_spec,
                  pl.BlockSpec((1, nb, ATT_HEAD_DIM), lambda b, h, i: (b, 0, h)),
                  tile_spec, tile_spec],
        out_specs=q_spec,
        scratch_shapes=[pltpu.VMEM((nb, tq, LANE), F32),
                        pltpu.VMEM((tq, 1), F32), pltpu.VMEM((tq, 1), F32),
                        pltpu.VMEM((tq, ATT_HEAD_DIM), F32)],
        compiler_params=_params(("parallel", "parallel", "arbitrary")),
        name="attn",
    )(c31, q, k, v, kmean, own, prev)


def _post_kernel(x_ref, sgu_ref, att_ref, kvm_ref, nmix_ref, wg_ref, bg_ref, wa_ref, wb_ref,
                 wo_ref, nx_ref, wxq_ref, wxo_ref, nffn_ref, wr_ref, br_ref,
                 h2_ref, xn3_ref, route_ref, cnt_ref, carry_scr, *, tm, d, xw):
    @pl.when(pl.program_id(0) == 0)
    def _():
        carry_scr[...] = jnp.zeros_like(carry_scr)

    x = x_ref[...]
    xn = _rms(x, nmix_ref[...]).astype(BF16)
    gates = jax.nn.sigmoid(_dot(xn, wg_ref[...]) + bg_ref[...])
    ya = _dot(sgu_ref[...], wa_ref[...])
    yb = _dot(att_ref[...], wb_ref[...])
    merged = (gates[:, :d] * ya + gates[:, d:] * yb).astype(BF16)
    h1 = x + _dot(merged, wo_ref[...])

    xn2 = _rms(h1, nx_ref[...]).astype(BF16)
    qx = (_dot(xn2, wxq_ref[...]) * (X_HEAD_DIM ** -0.5)).astype(BF16)
    outs = []
    for hh in range(X_HEADS):
        cs = slice(hh * X_HEAD_DIM, (hh + 1) * X_HEAD_DIM)
        vs = slice(xw + hh * X_HEAD_DIM, xw + (hh + 1) * X_HEAD_DIM)
        s = lax.dot_general(qx[:, cs], kvm_ref[:, cs], _NT, preferred_element_type=F32)
        p = jnp.exp(s - jnp.max(s, axis=-1, keepdims=True))
        o = _dot(p.astype(BF16), kvm_ref[:, vs]) / jnp.sum(p, axis=-1, keepdims=True)
        outs.append(o.astype(BF16))
    h2 = h1 + _dot(jnp.concatenate(outs, axis=1), wxo_ref[...])
    h2_ref[...] = h2

    xn3 = _rms(h2, nffn_ref[...])
    xn3_ref[...] = xn3
    logits = jnp.dot(xn3, wr_ref[...], precision=lax.Precision.HIGHEST,
                     preferred_element_type=F32) + br_ref[...]
    lane = lax.broadcasted_iota(jnp.int32, (tm, LANE), 1)
    is_g = lane < N_GROUPS
    gl = jnp.where(is_g, logits, -jnp.inf)
    gmax = jnp.max(gl, axis=-1, keepdims=True)
    gsel = jnp.min(jnp.where(gl == gmax, lane, LANE), axis=-1, keepdims=True)
    pg = 1.0 / jnp.sum(jnp.exp(gl - gmax), axis=-1, keepdims=True)
    eidx = lane - N_GROUPS
    in_group = (eidx >= 0) & (eidx < N_EXPERTS) & ((eidx // EXPERTS_PER_GROUP) == gsel)
    el = jnp.where(in_group, logits, -jnp.inf)
    m1 = jnp.max(el, axis=-1, keepdims=True)
    i1 = jnp.min(jnp.where(el == m1, lane, LANE), axis=-1, keepdims=True)
    el2 = jnp.where(lane == i1, -jnp.inf, el)
    m2 = jnp.max(el2, axis=-1, keepdims=True)
    i2 = jnp.min(jnp.where(el2 == m2, lane, LANE), axis=-1, keepdims=True)
    e2 = jnp.exp(m2 - m1)
    w1 = pg / (1.0 + e2)
    w2 = pg * e2 / (1.0 + e2)
    eid1 = i1 - N_GROUPS
    eid2 = i2 - N_GROUPS

    oh1 = lane == eid1
    oh2 = lane == eid2
    r = lax.broadcasted_iota(jnp.int32, (tm, tm), 0)
    c = lax.broadcasted_iota(jnp.int32, (tm, tm), 1)
    tri = jnp.where(c < r, 1.0, 0.0).astype(BF16)
    oh1f = jnp.where(oh1, 1.0, 0.0)
    oh2f = jnp.where(oh2, 1.0, 0.0)
    c1 = _dot(tri, oh1f.astype(BF16))
    c2 = _dot(tri, oh2f.astype(BF16))
    tot1 = jnp.sum(oh1f, axis=0, keepdims=True)
    tot2 = jnp.sum(oh2f, axis=0, keepdims=True)
    carry = carry_scr[...]
    r1 = jnp.sum(oh1f * (c1 + carry), axis=-1, keepdims=True)
    r2 = jnp.sum(oh2f * (c2 + carry + tot1), axis=-1, keepdims=True)
    carry = carry + tot1 + tot2
    carry_scr[...] = carry
    cnt_ref[...] = carry

    route = jnp.where(lane == 0, eid1.astype(F32), 0.0)
    route = jnp.where(lane == 1, eid2.astype(F32), route)
    route = jnp.where(lane == 2, w1, route)
    route = jnp.where(lane == 3, w2, route)
    route = jnp.where(lane == 4, r1, route)
    route = jnp.where(lane == 5, r2, route)
    route_ref[...] = route


def _post(x2d, sgu, att, kvm, consts, tm, seq):
    t, d = x2d.shape
    mlen = kvm.shape[0] // (t // seq)
    xw = kvm.shape[1] // 2
    tiles_per_batch = seq // tm
    row = lambda w: pl.BlockSpec((tm, w), lambda i: (i, 0))
    return pl.pallas_call(
        functools.partial(_post_kernel, tm=tm, d=d, xw=xw),
        out_shape=(jax.ShapeDtypeStruct((t, d), F32), jax.ShapeDtypeStruct((t, d), F32),
                   jax.ShapeDtypeStruct((t, LANE), F32), jax.ShapeDtypeStruct((1, LANE), F32)),
        grid=(t // tm,),
        in_specs=[row(d), row(sgu.shape[1]), row(att.shape[1]),
                  pl.BlockSpec((mlen, kvm.shape[1]), lambda i: (i // tiles_per_batch, 0))]
                 + [_const_spec(c.shape) for c in consts],
        out_specs=(row(d), row(d), row(LANE), pl.BlockSpec((1, LANE), lambda i: (0, 0))),
        scratch_shapes=[pltpu.VMEM((1, LANE), F32)],
        compiler_params=_params(("arbitrary",)),
        name="post",
    )(x2d, sgu, att, kvm, *consts)


def _scatter_kernel(dest_ref, x_ref, buf_in_ref, buf_ref, sem, *, tm):
    del buf_in_ref

    def row_copy(r, slot):
        return pltpu.make_async_copy(x_ref.at[pl.ds(r, 1)],
                                     buf_ref.at[pl.ds(dest_ref[0, 0, 2 * r + slot], 1)], sem)

    def start(r, carry):
        row_copy(r, 0).start()
        row_copy(r, 1).start()
        return carry

    def wait(r, carry):
        row_copy(r, 0).wait()
        row_copy(r, 1).wait()
        return carry

    lax.fori_loop(0, tm, start, 0)
    lax.fori_loop(0, tm, wait, 0)


def _scatter(dest3, xn3, buf0, tm):
    t, d = xn3.shape
    return pl.pallas_call(
        functools.partial(_scatter_kernel, tm=tm),
        out_shape=jax.ShapeDtypeStruct(buf0.shape, buf0.dtype),
        grid=(t // tm,),
        in_specs=[pl.BlockSpec((1, 1, 2 * tm), lambda i: (i, 0, 0), memory_space=pltpu.SMEM),
                  pl.BlockSpec((tm, d), lambda i: (i, 0)),
                  pl.BlockSpec(memory_space=pl.ANY)],
        out_specs=pl.BlockSpec(memory_space=pl.ANY),
        scratch_shapes=[pltpu.SemaphoreType.DMA],
        input_output_aliases={2: 0},
        compiler_params=_params(("arbitrary",)),
        name="scatter",
    )(dest3, xn3, buf0)


def _expert_kernel(be_ref, nused_ref, x_ref, wg_ref, wu_ref, wd_ref, y_ref, wg_bf, wu_bf, wd_bf):
    j = pl.program_id(0)
    e = be_ref[j]
    e_prev = be_ref[jnp.maximum(j - 1, 0)]

    @pl.when((j == 0) | (e != e_prev))
    def _():
        wg_bf[...] = wg_ref[...].astype(BF16)
        wu_bf[...] = wu_ref[...].astype(BF16)
        wd_bf[...] = wd_ref[...].astype(BF16)

    @pl.when(j < nused_ref[0])
    def _():
        xb = x_ref[...].astype(BF16)
        hid = jax.nn.silu(_dot(xb, wg_bf[...])) * _dot(xb, wu_bf[...])
        y_ref[...] = _dot(hid.astype(BF16), wd_bf[...])

    @pl.when(j >= nused_ref[0])
    def _():
        y_ref[...] = jnp.zeros_like(y_ref)


def _expert(blk_expert, nused, buf, w_gate, w_up, w_down):
    n, d = buf.shape
    eh = w_gate.shape[2]
    nblk = n // MOE_BLOCK
    grid_spec = pltpu.PrefetchScalarGridSpec(
        num_scalar_prefetch=2,
        grid=(nblk,),
        in_specs=[pl.BlockSpec((MOE_BLOCK, d), lambda j, be, nu: (j, 0)),
                  pl.BlockSpec((None, d, eh), lambda j, be, nu: (be[j], 0, 0)),
                  pl.BlockSpec((None, d, eh), lambda j, be, nu: (be[j], 0, 0)),
                  pl.BlockSpec((None, eh, d), lambda j, be, nu: (be[j], 0, 0))],
        out_specs=pl.BlockSpec((MOE_BLOCK, d), lambda j, be, nu: (j, 0)),
        scratch_shapes=[pltpu.VMEM((d, eh), BF16), pltpu.VMEM((d, eh), BF16),
                        pltpu.VMEM((eh, d), BF16)],
    )
    return pl.pallas_call(
        _expert_kernel,
        out_shape=jax.ShapeDtypeStruct((n, d), F32),
        grid_spec=grid_spec,
        compiler_params=_params(("arbitrary",)),
        name="expert",
    )(blk_expert, nused, buf, w_gate, w_up, w_down)


def _combine_kernel(dest_ref, h2_ref, route_ref, g_ref, ybuf_ref, o_ref, rows_scr, sem, *, tm):
    def row_copy(r, slot):
        return pltpu.make_async_copy(ybuf_ref.at[pl.ds(dest_ref[0, 0, 2 * r + slot], 1)],
                                     rows_scr.at[slot, pl.ds(r, 1)], sem)

    def start(r, carry):
        row_copy(r, 0).start()
        row_copy(r, 1).start()
        return carry

    def wait(r, carry):
        row_copy(r, 0).wait()
        row_copy(r, 1).wait()
        return carry

    lax.fori_loop(0, tm, start, 0)
    lax.fori_loop(0, tm, wait, 0)
    route = route_ref[...]
    y = route[:, 2:3] * rows_scr[0] + route[:, 3:4] * rows_scr[1]
    o_ref[...] = _rms(h2_ref[...] + y, g_ref[...])


def _combine(dest3, h2, route, g, ybuf, tm):
    t, d = h2.shape
    return pl.pallas_call(
        functools.partial(_combine_kernel, tm=tm),
        out_shape=jax.ShapeDtypeStruct((t, d), F32),
        grid=(t // tm,),
        in_specs=[pl.BlockSpec((1, 1, 2 * tm), lambda i: (i, 0, 0), memory_space=pltpu.SMEM),
                  pl.BlockSpec((tm, d), lambda i: (i, 0)),
                  pl.BlockSpec((tm, LANE), lambda i: (i, 0)),
                  _const_spec(g.shape),
                  pl.BlockSpec(memory_space=pl.ANY)],
        out_specs=pl.BlockSpec((tm, d), lambda i: (i, 0)),
        scratch_shapes=[pltpu.VMEM((2, tm, d), F32), pltpu.SemaphoreType.DMA],
        compiler_params=_params(("arbitrary",)),
        name="combine",
    )(dest3, h2, route, g, ybuf)


def kernel(x, mem, norm_mix, w_in, w_gate, b_gate, sgu_norm, w_spatial, b_spatial, w_branch_a,
           w_branch_b, rel_bias, w_o, norm_x, norm_mem, w_xq, w_xkv, w_xo, norm_ffn,
           w_router_group, b_router_group, w_router_expert, b_router_expert,
           w_e_gate, w_e_up, w_e_down, norm_final):
    batch, seq, d = x.shape
    assert norm_mix.shape[0] == 1, "one layer"
    assert seq % MOBA_BLOCK == 0 and d % LANE == 0
    t = batch * seq
    tm_in, tm_post, tm_moe = 512, 256, 256
    row = lambda a: a.reshape(1, -1).astype(F32)
    bf = lambda a: a.astype(BF16)

    x2d = x.reshape(t, d)
    kvm = _memkv(mem.reshape(-1, d), row(norm_mem[0]), bf(w_xkv[0]))

    width = w_in.shape[2] // 5
    bsp_full = jnp.repeat(b_spatial[0].T, width // SGU_GROUPS, axis=1)
    sgu, q, k, v, kmean = _inproj(x2d, row(norm_mix[0]), bf(w_in[0]), row(sgu_norm[0]),
                                  w_spatial[0], bsp_full, tm_in)

    rel_t = rel_bias.T.astype(F32)
    own, prev = _relbias(rel_t, MOBA_BLOCK)
    att = _attn(rel_t[:, NUM_BUCKETS - 1], q, k, v,
                kmean.reshape(batch, seq // MOBA_BLOCK, width), own, prev, batch, seq)

    w_router = jnp.zeros((d, LANE), F32)
    w_router = w_router.at[:, :N_GROUPS].set(w_router_group[0])
    w_router = w_router.at[:, N_GROUPS:N_GROUPS + N_EXPERTS].set(w_router_expert[0])
    b_router = jnp.zeros((1, LANE), F32)
    b_router = b_router.at[0, :N_GROUPS].set(b_router_group[0])
    b_router = b_router.at[0, N_GROUPS:N_GROUPS + N_EXPERTS].set(b_router_expert[0])
    consts = [row(norm_mix[0]), bf(w_gate[0]), row(b_gate[0]), bf(w_branch_a[0]), bf(w_branch_b[0]),
              bf(w_o[0]), row(norm_x[0]), bf(w_xq[0]), bf(w_xo[0]), row(norm_ffn[0]),
              w_router, b_router]
    h2, xn3, route, cnt = _post(x2d, sgu, att, kvm, consts, tm_post, seq)

    counts = cnt[0, :N_EXPERTS].astype(jnp.int32)
    padded = (counts + MOE_BLOCK - 1) // MOE_BLOCK * MOE_BLOCK
    pad_ends = jnp.cumsum(padded)
    pad_starts = pad_ends - padded
    eid = route[:, 0:2].astype(jnp.int32)
    dest = pad_starts[eid] + route[:, 4:6].astype(jnp.int32)
    nblk = -(-(2 * t) // MOE_BLOCK) + N_EXPERTS
    blk_expert = jnp.minimum(
        jnp.searchsorted(pad_ends, jnp.arange(nblk, dtype=jnp.int32) * MOE_BLOCK, side='right'),
        N_EXPERTS - 1).astype(jnp.int32)
    nused = (pad_ends[-1:] // MOE_BLOCK).astype(jnp.int32)
    dest3 = dest.reshape(t // tm_moe, 1, 2 * tm_moe)

    buf = _scatter(dest3, xn3, jnp.zeros((nblk * MOE_BLOCK, d), F32), tm_moe)
    ybuf = _expert(blk_expert, nused, buf, w_e_gate[0], w_e_up[0], w_e_down[0])
    out = _combine(dest3, h2, route, row(norm_final), ybuf, tm_moe)
    return out.reshape(batch, seq, d)
```

```python
import functools
import math

import numpy as np
import jax
import jax.numpy as jnp
from jax import lax
from jax.experimental import pallas as pl
from jax.experimental.pallas import tpu as pltpu

F32 = jnp.float32
BF16 = jnp.bfloat16

EPS = 1e-6
NEG = -1e30
LOG2E = math.log2(math.e)
LANE = 128

SGU_GROUPS = 8
CHUNK = 128
ATT_HEADS = 8
ATT_HEAD_DIM = 128
MOBA_BLOCK = 256
MOBA_TOPK = 3
NUM_BUCKETS = 32
MAX_DISTANCE = 128
X_HEADS = 4
X_HEAD_DIM = 128
N_GROUPS = 8
EXPERTS_PER_GROUP = 8
N_EXPERTS = N_GROUPS * EXPERTS_PER_GROUP
MOE_BLOCK = 256

VMEM_LIMIT = 56 * 1024 * 1024

_NT = (((1,), (1,)), ((), ()))


def _rms(x, g):
    return x * lax.rsqrt(jnp.mean(x * x, axis=-1, keepdims=True) + EPS) * g


def _gelu(x):
    c = math.sqrt(2.0 / math.pi)
    return x * (0.5 * (1.0 + jnp.tanh(c * (x + 0.044715 * (x * x * x)))))


def _dot(a, b):
    return jnp.dot(a, b, preferred_element_type=F32)


def _const_spec(shape):
    nd = len(shape)
    return pl.BlockSpec(shape, lambda *_: (0,) * nd, pipeline_mode=pl.Buffered(1))


def _params(sem):
    return pltpu.CompilerParams(dimension_semantics=sem, vmem_limit_bytes=VMEM_LIMIT)


def _memkv_kernel(mem_ref, g_ref, w_ref, o_ref):
    mn = _rms(mem_ref[...], g_ref[...]).astype(BF16)
    o_ref[...] = _dot(mn, w_ref[...]).astype(BF16)


def _memkv(mem2d, g, w_bf):
    n, d = mem2d.shape
    tm = 256
    return pl.pallas_call(
        _memkv_kernel,
        out_shape=jax.ShapeDtypeStruct((n, w_bf.shape[1]), BF16),
        grid=(n // tm,),
        in_specs=[pl.BlockSpec((tm, d), lambda i: (i, 0)),
                  _const_spec(g.shape), _const_spec(w_bf.shape)],
        out_specs=pl.BlockSpec((tm, w_bf.shape[1]), lambda i: (i, 0)),
        compiler_params=_params(("parallel",)),
        name="memkv",
    )(mem2d, g, w_bf)


def _inproj_kernel(x_ref, g_ref, win_ref, sg_ref, wsp_ref, bsp_ref,
                   sgu_ref, q_ref, k_ref, v_ref, km_ref, *, tm, width, scale):
    xn = _rms(x_ref[...], g_ref[...]).astype(BF16)
    u = _gelu(_dot(xn, win_ref[:, 0:width]))
    vv = _gelu(_dot(xn, win_ref[:, width:2 * width]))
    vn = _rms(vv, sg_ref[...]).astype(BF16)
    gd = width // SGU_GROUPS
    row = lax.broadcasted_iota(jnp.int32, (CHUNK, CHUNK), 0)
    col = lax.broadcasted_iota(jnp.int32, (CHUNK, CHUNK), 1)
    causal = col <= row
    wsp = [jnp.where(causal, wsp_ref[g], 0.0).astype(BF16) for g in range(SGU_GROUPS)]
    bias = bsp_ref[...]
    for c in range(tm // CHUNK):
        rs = slice(c * CHUNK, (c + 1) * CHUNK)
        mixed = jnp.concatenate(
            [_dot(wsp[g], vn[rs, g * gd:(g + 1) * gd]) for g in range(SGU_GROUPS)], axis=1)
        sgu_ref[rs, :] = (u[rs, :] * (mixed + bias)).astype(BF16)
    q_ref[...] = (_dot(xn, win_ref[:, 2 * width:3 * width]) * scale).astype(BF16)
    k = _dot(xn, win_ref[:, 3 * width:4 * width])
    k_ref[...] = k.astype(BF16)
    for j in range(tm // MOBA_BLOCK):
        km_ref[0, j:j + 1, :] = jnp.mean(k[j * MOBA_BLOCK:(j + 1) * MOBA_BLOCK, :], axis=0, keepdims=True)
    v_ref[...] = _dot(xn, win_ref[:, 4 * width:5 * width]).astype(BF16)


def _inproj(x2d, g, win_bf, sg, wsp, bsp_full, tm):
    t, d = x2d.shape
    width = win_bf.shape[1] // 5
    nkm = tm // MOBA_BLOCK
    row_spec = pl.BlockSpec((tm, width), lambda i: (i, 0))
    act = jax.ShapeDtypeStruct((t, width), BF16)
    return pl.pallas_call(
        functools.partial(_inproj_kernel, tm=tm, width=width, scale=ATT_HEAD_DIM ** -0.5 * LOG2E),
        out_shape=(act, act, act, act, jax.ShapeDtypeStruct((t // tm, nkm, width), F32)),
        grid=(t // tm,),
        in_specs=[pl.BlockSpec((tm, d), lambda i: (i, 0)),
                  _const_spec(g.shape), _const_spec(win_bf.shape), _const_spec(sg.shape),
                  _const_spec(wsp.shape), _const_spec(bsp_full.shape)],
        out_specs=(row_spec, row_spec, row_spec, row_spec,
                   pl.BlockSpec((1, nkm, width), lambda i: (i, 0, 0))),
        compiler_params=_params(("parallel",)),
        name="inproj",
    )(x2d, g, win_bf, sg, wsp, bsp_full)


def _bucket_uppers():
    d = np.arange(0, 4 * MAX_DISTANCE)
    max_exact = NUM_BUCKETS // 2
    nf = np.maximum(d, 1).astype(np.float32)
    large = max_exact + (np.log(nf / max_exact) / math.log(MAX_DISTANCE / max_exact)
                         * (NUM_BUCKETS - max_exact)).astype(np.int32)
    bucket = np.where(d < max_exact, d, np.minimum(large, NUM_BUCKETS - 1))
    return [int(d[bucket > b].min()) for b in range(NUM_BUCKETS - 1)]


def _relbias_kernel(rb_ref, own_ref, prev_ref, *, tq, uppers):
    h = pl.program_id(0)
    key = lax.broadcasted_iota(jnp.int32, (tq, tq), 0)
    qry = lax.broadcasted_iota(jnp.int32, (tq, tq), 1)

    def table(dist):
        val = jnp.full((tq, tq), rb_ref[h, NUM_BUCKETS - 1] * LOG2E, F32)
        for b in range(NUM_BUCKETS - 2, -1, -1):
            val = jnp.where(dist < uppers[b], rb_ref[h, b] * LOG2E, val)
        return val

    d = qry - key
    own_ref[0] = jnp.where(d >= 0, table(d), NEG)
    prev_ref[0] = table(d + tq)


def _relbias(rel_bias_t, tq):
    nh = rel_bias_t.shape[0]
    tile = jax.ShapeDtypeStruct((nh, tq, tq), F32)
    spec = pl.BlockSpec((1, tq, tq), lambda h: (h, 0, 0))
    return pl.pallas_call(
        functools.partial(_relbias_kernel, tq=tq, uppers=_bucket_uppers()),
        out_shape=(tile, tile),
        grid=(nh,),
        in_specs=[pl.BlockSpec(memory_space=pltpu.SMEM)],
        out_specs=(spec, spec),
        compiler_params=_params(("parallel",)),
        name="relbias",
    )(rel_bias_t)


def _attn_kernel(c31_ref, q_ref, k_ref, v_ref, km_ref, own_ref, prev_ref, o_ref,
                 vt_scr, add_scr, s_scr, acc_scr, *, tq, nb, hg):
    hd = ATT_HEAD_DIM
    heads = range(hg)
    cols = [slice(g * hd, (g + 1) * hd) for g in heads]
    c31 = [c31_ref[pl.program_id(1) * hg + g] * LOG2E for g in heads]
    for g in heads:
        for n in range(nb):
            vt_scr[g, n // 2, :, (n % 2) * tq:(n % 2 + 1) * tq] = (
                v_ref[n * tq:(n + 1) * tq, cols[g]].astype(F32).T.astype(BF16))

    def part_max(s):
        return jnp.max(s.reshape(s.shape[0] // 8, 8, tq), axis=0)

    def q_tile(cur, carry):
        rows = pl.ds(pl.multiple_of(cur * tq, tq), tq)
        q = [q_ref[rows, cols[g]] for g in heads]
        blk = lax.broadcasted_iota(jnp.int32, (nb, tq), 0)
        past = blk < cur

        negm_prev = []
        for g in heads:
            gate = lax.dot_general(km_ref[0, :, cols[g]], q[g].astype(F32), _NT,
                                   precision=lax.Precision.HIGHEST, preferred_element_type=F32)
            gv = jnp.where(past, gate, NEG)
            sel = jnp.zeros((nb, tq), F32)
            for _ in range(MOBA_TOPK):
                mx = jnp.max(gv, axis=0, keepdims=True)
                first = jnp.min(jnp.where(gv == mx, blk, nb), axis=0, keepdims=True)
                hit = blk == first
                sel = jnp.where(hit & past, 1.0, sel)
                gv = jnp.where(hit, -jnp.inf, gv)
            negm = (sel - 1.0) * (-NEG)
            add_scr[g] = jnp.where(blk < cur - 1, negm + c31[g], NEG)
            negm_prev.append(jnp.sum(jnp.where(blk == cur - 1, negm, 0.0), axis=0, keepdims=True))

        def far_pair(i, mx8):
            out = []
            for g in heads:
                kb = k_ref[pl.ds(pl.multiple_of(i * (2 * tq), 2 * tq), 2 * tq), cols[g]]
                s2 = lax.dot_general(kb, q[g], _NT, preferred_element_type=F32)
                s_lo = s2[:tq] + add_scr[g, pl.ds(2 * i, 1), :]
                s_hi = s2[tq:] + add_scr[g, pl.ds(2 * i + 1, 1), :]
                s_scr[g, i, :tq, :] = s_lo
                s_scr[g, i, tq:, :] = s_hi
                out.append(jnp.maximum(mx8[g], jnp.maximum(part_max(s_lo), part_max(s_hi))))
            return tuple(out)

        nfar = jnp.maximum(cur - 1, 0)
        mx8 = lax.fori_loop(0, (nfar + 1) // 2, far_pair,
                            tuple(jnp.full((8, tq), NEG, F32) for _ in heads))

        def one_block(n, add_tiles, mx8):
            out = []
            for g in heads:
                kb = k_ref[pl.ds(pl.multiple_of(n * tq, tq), tq), cols[g]]
                s = lax.dot_general(kb, q[g], _NT, preferred_element_type=F32) + add_tiles[g]
                s_scr[g, n // 2, pl.ds(pl.multiple_of((n % 2) * tq, tq), tq), :] = s
                out.append(jnp.maximum(mx8[g], part_max(s)))
            return tuple(out)

        mx8 = lax.cond(cur >= 1,
                       lambda a: one_block(cur - 1, [prev_ref[g] + negm_prev[g] for g in heads], a),
                       lambda a: a, mx8)
        mx8 = one_block(cur, [own_ref[g] for g in heads], mx8)

        @pl.when(cur % 2 == 0)
        def _():
            for g in heads:
                s_scr[g, cur // 2, tq:, :] = jnp.full((tq, tq), NEG, F32)

        m = [jnp.max(mx8[g], axis=0, keepdims=True) for g in heads]
        acc_scr[...] = jnp.zeros_like(acc_scr)

        def pv_pair(i, l8):
            out = []
            for g in heads:
                p = jnp.exp2(s_scr[g, i] - m[g])
                acc_scr[g] += _dot(vt_scr[g, i], p.astype(BF16))
                out.append(l8[g] + jnp.sum(p.reshape(2 * tq // 8, 8, tq), axis=0))
            return tuple(out)

        l8 = lax.fori_loop(0, cur // 2 + 1, pv_pair, tuple(jnp.zeros((8, tq), F32) for _ in heads))
        for g in heads:
            l = jnp.sum(l8[g], axis=0, keepdims=True)
            o_ref[rows, cols[g]] = (acc_scr[g] / l).T.astype(BF16)
        return carry

    lax.fori_loop(0, nb, q_tile, 0)


def _attn(c31, q, k, v, kmean, own, prev, batch, seq):
    t, width = q.shape
    tq = MOBA_BLOCK
    nb = seq // tq
    hd = ATT_HEAD_DIM
    hg = 2
    nh = width // hd
    assert nb % 2 == 0 and nh % hg == 0
    seq_spec = pl.BlockSpec((seq, hg * hd), lambda b, h: (b, h))
    tile_spec = pl.BlockSpec((hg, tq, tq), lambda b, h: (h, 0, 0))
    return pl.pallas_call(
        functools.partial(_attn_kernel, tq=tq, nb=nb, hg=hg),
        out_shape=jax.ShapeDtypeStruct((t, width), BF16),
        grid=(batch, nh // hg),
        in_specs=[pl.BlockSpec(memory_space=pltpu.SMEM),
                  seq_spec, seq_spec, seq_spec,
                  pl.BlockSpec((1, nb, hg * hd), lambda b, h: (b, 0, h)),
                  tile_spec, tile_spec],
        out_specs=seq_spec,
        scratch_shapes=[pltpu.VMEM((hg, nb // 2, hd, 2 * tq), BF16),
                        pltpu.VMEM((hg, nb, tq), F32),
                        pltpu.VMEM((hg, nb // 2, 2 * tq, tq), F32),
                        pltpu.VMEM((hg, hd, tq), F32)],
        compiler_params=_params(("parallel", "parallel")),
        name="attn",
    )(c31, q, k, v, kmean, own, prev)


def _post_kernel(x_ref, sgu_ref, att_ref, kvm_ref, nmix_ref, wg_ref, bg_ref, wa_ref, wb_ref,
                 wo_ref, nx_ref, wxq_ref, wxo_ref, nffn_ref, wr_ref, br_ref,
                 h2_ref, xn3_ref, route_ref, cnt_ref, carry_scr, *, tm, d, xw):
    @pl.when(pl.program_id(0) == 0)
    def _():
        carry_scr[...] = jnp.zeros_like(carry_scr)

    x = x_ref[...]
    xn = _rms(x, nmix_ref[...]).astype(BF16)
    gates = jax.nn.sigmoid(_dot(xn, wg_ref[...]) + bg_ref[...])
    ya = _dot(sgu_ref[...], wa_ref[...])
    yb = _dot(att_ref[...], wb_ref[...])
    merged = (gates[:, :d] * ya + gates[:, d:] * yb).astype(BF16)
    h1 = x + _dot(merged, wo_ref[...])

    xn2 = _rms(h1, nx_ref[...]).astype(BF16)
    qx = (_dot(xn2, wxq_ref[...]) * (X_HEAD_DIM ** -0.5)).astype(BF16)
    outs = []
    for hh in range(X_HEADS):
        cs = slice(hh * X_HEAD_DIM, (hh + 1) * X_HEAD_DIM)
        vs = slice(xw + hh * X_HEAD_DIM, xw + (hh + 1) * X_HEAD_DIM)
        s = lax.dot_general(qx[:, cs], kvm_ref[:, cs], _NT, preferred_element_type=F32)
        p = jnp.exp(s - jnp.max(s, axis=-1, keepdims=True))
        o = _dot(p.astype(BF16), kvm_ref[:, vs]) / jnp.sum(p, axis=-1, keepdims=True)
        outs.append(o.astype(BF16))
    h2 = h1 + _dot(jnp.concatenate(outs, axis=1), wxo_ref[...])
    h2_ref[...] = h2

    xn3 = _rms(h2, nffn_ref[...])
    xn3_ref[...] = xn3
    logits = jnp.dot(xn3, wr_ref[...], precision=lax.Precision.HIGHEST,
                     preferred_element_type=F32) + br_ref[...]
    lane = lax.broadcasted_iota(jnp.int32, (tm, LANE), 1)
    is_g = lane < N_GROUPS
    gl = jnp.where(is_g, logits, -jnp.inf)
    gmax = jnp.max(gl, axis=-1, keepdims=True)
    gsel = jnp.min(jnp.where(gl == gmax, lane, LANE), axis=-1, keepdims=True)
    pg = 1.0 / jnp.sum(jnp.exp(gl - gmax), axis=-1, keepdims=True)
    eidx = lane - N_GROUPS
    in_group = (eidx >= 0) & (eidx < N_EXPERTS) & ((eidx // EXPERTS_PER_GROUP) == gsel)
    el = jnp.where(in_group, logits, -jnp.inf)
    m1 = jnp.max(el, axis=-1, keepdims=True)
    i1 = jnp.min(jnp.where(el == m1, lane, LANE), axis=-1, keepdims=True)
    el2 = jnp.where(lane == i1, -jnp.inf, el)
    m2 = jnp.max(el2, axis=-1, keepdims=True)
    i2 = jnp.min(jnp.where(el2 == m2, lane, LANE), axis=-1, keepdims=True)
    e2 = jnp.exp(m2 - m1)
    w1 = pg / (1.0 + e2)
    w2 = pg * e2 / (1.0 + e2)
    eid1 = i1 - N_GROUPS
    eid2 = i2 - N_GROUPS

    oh1 = lane == eid1
    oh2 = lane == eid2
    r = lax.broadcasted_iota(jnp.int32, (tm, tm), 0)
    c = lax.broadcasted_iota(jnp.int32, (tm, tm), 1)
    tri = jnp.where(c < r, 1.0, 0.0).astype(BF16)
    oh1f = jnp.where(oh1, 1.0, 0.0)
    oh2f = jnp.where(oh2, 1.0, 0.0)
    c1 = _dot(tri, oh1f.astype(BF16))
    c2 = _dot(tri, oh2f.astype(BF16))
    tot1 = jnp.sum(oh1f, axis=0, keepdims=True)
    tot2 = jnp.sum(oh2f, axis=0, keepdims=True)
    carry = carry_scr[...]
    r1 = jnp.sum(oh1f * (c1 + carry), axis=-1, keepdims=True)
    r2 = jnp.sum(oh2f * (c2 + carry + tot1), axis=-1, keepdims=True)
    carry = carry + tot1 + tot2
    carry_scr[...] = carry
    cnt_ref[...] = carry

    route = jnp.where(lane == 0, eid1.astype(F32), 0.0)
    route = jnp.where(lane == 1, eid2.astype(F32), route)
    route = jnp.where(lane == 2, w1, route)
    route = jnp.where(lane == 3, w2, route)
    route = jnp.where(lane == 4, r1, route)
    route = jnp.where(lane == 5, r2, route)
    route_ref[...] = route


def _post(x2d, sgu, att, kvm, consts, tm, seq):
    t, d = x2d.shape
    mlen = kvm.shape[0] // (t // seq)
    xw = kvm.shape[1] // 2
    tiles_per_batch = seq // tm
    row = lambda w: pl.BlockSpec((tm, w), lambda i: (i, 0))
    return pl.pallas_call(
        functools.partial(_post_kernel, tm=tm, d=d, xw=xw),
        out_shape=(jax.ShapeDtypeStruct((t, d), F32), jax.ShapeDtypeStruct((t, d), F32),
                   jax.ShapeDtypeStruct((t, LANE), F32), jax.ShapeDtypeStruct((1, LANE), F32)),
        grid=(t // tm,),
        in_specs=[row(d), row(sgu.shape[1]), row(att.shape[1]),
                  pl.BlockSpec((mlen, kvm.shape[1]), lambda i: (i // tiles_per_batch, 0))]
                 + [_const_spec(c.shape) for c in consts],
        out_specs=(row(d), row(d), row(LANE), pl.BlockSpec((1, LANE), lambda i: (0, 0))),
        scratch_shapes=[pltpu.VMEM((1, LANE), F32)],
        compiler_params=_params(("arbitrary",)),
        name="post",
    )(x2d, sgu, att, kvm, *consts)


def _scatter_kernel(dest_ref, x_ref, buf_in_ref, buf_ref, sem, *, tm):
    del buf_in_ref

    def row_copy(r, slot):
        return pltpu.make_async_copy(x_ref.at[pl.ds(r, 1)],
                                     buf_ref.at[pl.ds(dest_ref[0, 0, slot * tm + r], 1)], sem)

    def start(r, carry):
        row_copy(r, 0).start()
        row_copy(r, 1).start()
        return carry

    def wait(r, carry):
        row_copy(r, 0).wait()
        row_copy(r, 1).wait()
        return carry

    lax.fori_loop(0, tm, start, 0)
    lax.fori_loop(0, tm, wait, 0)


def _scatter(dest3, xn3, buf0, tm):
    t, d = xn3.shape
    return pl.pallas_call(
        functools.partial(_scatter_kernel, tm=tm),
        out_shape=jax.ShapeDtypeStruct(buf0.shape, buf0.dtype),
        grid=(t // tm,),
        in_specs=[pl.BlockSpec((1, 1, 2 * tm), lambda i: (i, 0, 0), memory_space=pltpu.SMEM),
                  pl.BlockSpec((tm, d), lambda i: (i, 0)),
                  pl.BlockSpec(memory_space=pl.ANY)],
        out_specs=pl.BlockSpec(memory_space=pl.ANY),
        scratch_shapes=[pltpu.SemaphoreType.DMA],
        input_output_aliases={2: 0},
        compiler_params=_params(("arbitrary",)),
        name="scatter",
    )(dest3, xn3, buf0)


def _expert_kernel(be_ref, nused_ref, x_ref, wg_ref, wu_ref, wd_ref, y_ref, wg_bf, wu_bf, wd_bf):
    j = pl.program_id(0)
    e = be_ref[j]
    e_prev = be_ref[jnp.maximum(j - 1, 0)]

    @pl.when((j == 0) | (e != e_prev))
    def _():
        wg_bf[...] = wg_ref[...].astype(BF16)
        wu_bf[...] = wu_ref[...].astype(BF16)
        wd_bf[...] = wd_ref[...].astype(BF16)

    @pl.when(j < nused_ref[0])
    def _():
        xb = x_ref[...].astype(BF16)
        hid = jax.nn.silu(_dot(xb, wg_bf[...])) * _dot(xb, wu_bf[...])
        y_ref[...] = _dot(hid.astype(BF16), wd_bf[...])

    @pl.when(j >= nused_ref[0])
    def _():
        y_ref[...] = jnp.zeros_like(y_ref)


def _expert(blk_expert, nused, buf, w_gate, w_up, w_down):
    n, d = buf.shape
    eh = w_gate.shape[2]
    nblk = n // MOE_BLOCK
    grid_spec = pltpu.PrefetchScalarGridSpec(
        num_scalar_prefetch=2,
        grid=(nblk,),
        in_specs=[pl.BlockSpec((MOE_BLOCK, d), lambda j, be, nu: (j, 0)),
                  pl.BlockSpec((None, d, eh), lambda j, be, nu: (be[j], 0, 0)),
                  pl.BlockSpec((None, d, eh), lambda j, be, nu: (be[j], 0, 0)),
                  pl.BlockSpec((None, eh, d), lambda j, be, nu: (be[j], 0, 0))],
        out_specs=pl.BlockSpec((MOE_BLOCK, d), lambda j, be, nu: (j, 0)),
        scratch_shapes=[pltpu.VMEM((d, eh), BF16), pltpu.VMEM((d, eh), BF16),
                        pltpu.VMEM((eh, d), BF16)],
    )
    return pl.pallas_call(
        _expert_kernel,
        out_shape=jax.ShapeDtypeStruct((n, d), F32),
        grid_spec=grid_spec,
        compiler_params=_params(("arbitrary",)),
        name="expert",
    )(blk_expert, nused, buf, w_gate, w_up, w_down)


def _combine_kernel(dest_ref, h2_ref, route_ref, g_ref, ybuf_ref, o_ref, rows_scr, sem, *, tm):
    def row_copy(r, slot):
        return pltpu.make_async_copy(ybuf_ref.at[pl.ds(dest_ref[0, 0, slot * tm + r], 1)],
                                     rows_scr.at[slot, pl.ds(r, 1)], sem)

    def start(r, carry):
        row_copy(r, 0).start()
        row_copy(r, 1).start()
        return carry

    def wait(r, carry):
        row_copy(r, 0).wait()
        row_copy(r, 1).wait()
        return carry

    lax.fori_loop(0, tm, start, 0)
    lax.fori_loop(0, tm, wait, 0)
    route = route_ref[...]
    y = route[:, 2:3] * rows_scr[0] + route[:, 3:4] * rows_scr[1]
    o_ref[...] = _rms(h2_ref[...] + y, g_ref[...])


def _combine(dest3, h2, route, g, ybuf, tm):
    t, d = h2.shape
    return pl.pallas_call(
        functools.partial(_combine_kernel, tm=tm),
        out_shape=jax.ShapeDtypeStruct((t, d), F32),
        grid=(t // tm,),
        in_specs=[pl.BlockSpec((1, 1, 2 * tm), lambda i: (i, 0, 0), memory_space=pltpu.SMEM),
                  pl.BlockSpec((tm, d), lambda i: (i, 0)),
                  pl.BlockSpec((tm, LANE), lambda i: (i, 0)),
                  _const_spec(g.shape),
                  pl.BlockSpec(memory_space=pl.ANY)],
        out_specs=pl.BlockSpec((tm, d), lambda i: (i, 0)),
        scratch_shapes=[pltpu.VMEM((2, tm, d), F32), pltpu.SemaphoreType.DMA],
        compiler_params=_params(("arbitrary",)),
        name="combine",
    )(dest3, h2, route, g, ybuf)


def kernel(x, mem, norm_mix, w_in, w_gate, b_gate, sgu_norm, w_spatial, b_spatial, w_branch_a,
           w_branch_b, rel_bias, w_o, norm_x, norm_mem, w_xq, w_xkv, w_xo, norm_ffn,
           w_router_group, b_router_group, w_router_expert, b_router_expert,
           w_e_gate, w_e_up, w_e_down, norm_final):
    batch, seq, d = x.shape
    assert norm_mix.shape[0] == 1, "one layer"
    assert seq % MOBA_BLOCK == 0 and d % LANE == 0
    t = batch * seq
    tm_in, tm_post, tm_moe = 512, 256, 256
    row = lambda a: a.reshape(1, -1).astype(F32)
    bf = lambda a: a.astype(BF16)

    x2d = x.reshape(t, d)
    kvm = _memkv(mem.reshape(-1, d), row(norm_mem[0]), bf(w_xkv[0]))

    width = w_in.shape[2] // 5
    bsp_full = jnp.repeat(b_spatial[0].T, width // SGU_GROUPS, axis=1)
    sgu, q, k, v, kmean = _inproj(x2d, row(norm_mix[0]), bf(w_in[0]), row(sgu_norm[0]),
                                  w_spatial[0], bsp_full, tm_in)

    rel_t = rel_bias.T.astype(F32)
    own, prev = _relbias(rel_t, MOBA_BLOCK)
    att = _attn(rel_t[:, NUM_BUCKETS - 1], q, k, v,
                kmean.reshape(batch, seq // MOBA_BLOCK, width), own, prev, batch, seq)

    w_router = jnp.zeros((d, LANE), F32)
    w_router = w_router.at[:, :N_GROUPS].set(w_router_group[0])
    w_router = w_router.at[:, N_GROUPS:N_GROUPS + N_EXPERTS].set(w_router_expert[0])
    b_router = jnp.zeros((1, LANE), F32)
    b_router = b_router.at[0, :N_GROUPS].set(b_router_group[0])
    b_router = b_router.at[0, N_GROUPS:N_GROUPS + N_EXPERTS].set(b_router_expert[0])
    consts = [row(norm_mix[0]), bf(w_gate[0]), row(b_gate[0]), bf(w_branch_a[0]), bf(w_branch_b[0]),
              bf(w_o[0]), row(norm_x[0]), bf(w_xq[0]), bf(w_xo[0]), row(norm_ffn[0]),
              w_router, b_router]
    h2, xn3, route, cnt = _post(x2d, sgu, att, kvm, consts, tm_post, seq)

    counts = cnt[0, :N_EXPERTS].astype(jnp.int32)
    padded = (counts + MOE_BLOCK - 1) // MOE_BLOCK * MOE_BLOCK
    pad_ends = jnp.cumsum(padded)
    pad_starts = pad_ends - padded
    route_t = route[:, :8].T
    dest = [pad_starts[route_t[s].astype(jnp.int32)] + route_t[4 + s].astype(jnp.int32) for s in range(2)]
    nblk = -(-(2 * t) // MOE_BLOCK) + N_EXPERTS
    blk_start = jnp.arange(nblk, dtype=jnp.int32) * MOE_BLOCK
    blk_expert = jnp.minimum(
        jnp.sum((pad_ends[None, :] <= blk_start[:, None]).astype(jnp.int32), axis=1), N_EXPERTS - 1)
    nused = (pad_ends[-1:] // MOE_BLOCK).astype(jnp.int32)
    dest3 = jnp.concatenate([dst.reshape(t // tm_moe, 1, tm_moe) for dst in dest], axis=2)

    buf = _scatter(dest3, xn3, jnp.zeros((nblk * MOE_BLOCK, d), F32), tm_moe)
    ybuf = _expert(blk_expert, nused, buf, w_e_gate[0], w_e_up[0], w_e_down[0])
    out = _combine(dest3, h2, route, row(norm_final), ybuf, tm_moe)
    return out.reshape(batch, seq, d)
```

```python
import functools
import math

import numpy as np
import jax
import jax.numpy as jnp
from jax import lax
from jax.experimental import pallas as pl
from jax.experimental.pallas import tpu as pltpu

F32 = jnp.float32
BF16 = jnp.bfloat16

EPS = 1e-6
NEG = -1e30
LOG2E = math.log2(math.e)
LANE = 128

SGU_GROUPS = 8
CHUNK = 128
ATT_HEADS = 8
ATT_HEAD_DIM = 128
MOBA_BLOCK = 256
MOBA_TOPK = 3
NUM_BUCKETS = 32
MAX_DISTANCE = 128
X_HEADS = 4
X_HEAD_DIM = 128
N_GROUPS = 8
EXPERTS_PER_GROUP = 8
N_EXPERTS = N_GROUPS * EXPERTS_PER_GROUP
MOE_BLOCK = 256

VMEM_LIMIT = 56 * 1024 * 1024

_NT = (((1,), (1,)), ((), ()))


def _rms(x, g):
    return x * lax.rsqrt(jnp.mean(x * x, axis=-1, keepdims=True) + EPS) * g


def _gelu(x):
    c = math.sqrt(2.0 / math.pi)
    return x * (0.5 * (1.0 + jnp.tanh(c * (x + 0.044715 * (x * x * x)))))


def _dot(a, b):
    return jnp.dot(a, b, preferred_element_type=F32)


def _const_spec(shape):
    nd = len(shape)
    return pl.BlockSpec(shape, lambda *_: (0,) * nd, pipeline_mode=pl.Buffered(1))


def _params(sem):
    return pltpu.CompilerParams(dimension_semantics=sem, vmem_limit_bytes=VMEM_LIMIT)


def _memkv_kernel(mem_ref, g_ref, w_ref, o_ref):
    mn = _rms(mem_ref[...], g_ref[...]).astype(BF16)
    o_ref[...] = _dot(mn, w_ref[...]).astype(BF16)


def _memkv(mem2d, g, w_bf):
    n, d = mem2d.shape
    tm = 256
    return pl.pallas_call(
        _memkv_kernel,
        out_shape=jax.ShapeDtypeStruct((n, w_bf.shape[1]), BF16),
        grid=(n // tm,),
        in_specs=[pl.BlockSpec((tm, d), lambda i: (i, 0)),
                  _const_spec(g.shape), _const_spec(w_bf.shape)],
        out_specs=pl.BlockSpec((tm, w_bf.shape[1]), lambda i: (i, 0)),
        compiler_params=_params(("parallel",)),
        name="memkv",
    )(mem2d, g, w_bf)


def _inproj_kernel(x_ref, g_ref, win_ref, sg_ref, wsp_ref, bsp_ref,
                   sgu_ref, q_ref, k_ref, v_ref, km_ref, *, tm, width, scale):
    xn = _rms(x_ref[...], g_ref[...]).astype(BF16)
    u = _gelu(_dot(xn, win_ref[:, 0:width]))
    vv = _gelu(_dot(xn, win_ref[:, width:2 * width]))
    vn = _rms(vv, sg_ref[...]).astype(BF16)
    gd = width // SGU_GROUPS
    row = lax.broadcasted_iota(jnp.int32, (CHUNK, CHUNK), 0)
    col = lax.broadcasted_iota(jnp.int32, (CHUNK, CHUNK), 1)
    causal = col <= row
    wsp = [jnp.where(causal, wsp_ref[g], 0.0).astype(BF16) for g in range(SGU_GROUPS)]
    bias = bsp_ref[...]
    for c in range(tm // CHUNK):
        rs = slice(c * CHUNK, (c + 1) * CHUNK)
        mixed = jnp.concatenate(
            [_dot(wsp[g], vn[rs, g * gd:(g + 1) * gd]) for g in range(SGU_GROUPS)], axis=1)
        sgu_ref[rs, :] = (u[rs, :] * (mixed + bias)).astype(BF16)
    q_ref[...] = (_dot(xn, win_ref[:, 2 * width:3 * width]) * scale).astype(BF16)
    k = _dot(xn, win_ref[:, 3 * width:4 * width])
    k_ref[...] = k.astype(BF16)
    for j in range(tm // MOBA_BLOCK):
        km_ref[0, j:j + 1, :] = jnp.mean(k[j * MOBA_BLOCK:(j + 1) * MOBA_BLOCK, :], axis=0, keepdims=True)
    v_ref[...] = _dot(xn, win_ref[:, 4 * width:5 * width]).astype(BF16)


def _inproj(x2d, g, win_bf, sg, wsp, bsp_full, tm):
    t, d = x2d.shape
    width = win_bf.shape[1] // 5
    nkm = tm // MOBA_BLOCK
    row_spec = pl.BlockSpec((tm, width), lambda i: (i, 0))
    act = jax.ShapeDtypeStruct((t, width), BF16)
    return pl.pallas_call(
        functools.partial(_inproj_kernel, tm=tm, width=width, scale=ATT_HEAD_DIM ** -0.5 * LOG2E),
        out_shape=(act, act, act, act, jax.ShapeDtypeStruct((t // tm, nkm, width), F32)),
        grid=(t // tm,),
        in_specs=[pl.BlockSpec((tm, d), lambda i: (i, 0)),
                  _const_spec(g.shape), _const_spec(win_bf.shape), _const_spec(sg.shape),
                  _const_spec(wsp.shape), _const_spec(bsp_full.shape)],
        out_specs=(row_spec, row_spec, row_spec, row_spec,
                   pl.BlockSpec((1, nkm, width), lambda i: (i, 0, 0))),
        compiler_params=_params(("parallel",)),
        name="inproj",
    )(x2d, g, win_bf, sg, wsp, bsp_full)


def _bucket_uppers():
    d = np.arange(0, 4 * MAX_DISTANCE)
    max_exact = NUM_BUCKETS // 2
    nf = np.maximum(d, 1).astype(np.float32)
    large = max_exact + (np.log(nf / max_exact) / math.log(MAX_DISTANCE / max_exact)
                         * (NUM_BUCKETS - max_exact)).astype(np.int32)
    bucket = np.where(d < max_exact, d, np.minimum(large, NUM_BUCKETS - 1))
    return [int(d[bucket > b].min()) for b in range(NUM_BUCKETS - 1)]


def _relbias_kernel(rb_ref, own_ref, prev_ref, *, tq, uppers):
    h = pl.program_id(0)
    key = lax.broadcasted_iota(jnp.int32, (tq, tq), 0)
    qry = lax.broadcasted_iota(jnp.int32, (tq, tq), 1)

    def table(dist):
        val = jnp.full((tq, tq), rb_ref[h, NUM_BUCKETS - 1] * LOG2E, F32)
        for b in range(NUM_BUCKETS - 2, -1, -1):
            val = jnp.where(dist < uppers[b], rb_ref[h, b] * LOG2E, val)
        return val

    d = qry - key
    own_ref[0] = jnp.where(d >= 0, table(d), NEG)
    prev_ref[0] = table(d + tq)


def _relbias(rel_bias_t, tq):
    nh = rel_bias_t.shape[0]
    tile = jax.ShapeDtypeStruct((nh, tq, tq), F32)
    spec = pl.BlockSpec((1, tq, tq), lambda h: (h, 0, 0))
    return pl.pallas_call(
        functools.partial(_relbias_kernel, tq=tq, uppers=_bucket_uppers()),
        out_shape=(tile, tile),
        grid=(nh,),
        in_specs=[pl.BlockSpec(memory_space=pltpu.SMEM)],
        out_specs=(spec, spec),
        compiler_params=_params(("parallel",)),
        name="relbias",
    )(rel_bias_t)


def _attn_kernel(c31_ref, q_ref, k_ref, v_ref, km_ref, own_ref, prev_ref, o_ref,
                 vt_scr, add_scr, s_scr, acc_scr, *, tq, nb, hg):
    hd = ATT_HEAD_DIM
    heads = range(hg)
    cols = [slice(g * hd, (g + 1) * hd) for g in heads]
    c31 = [c31_ref[pl.program_id(1) * hg + g] * LOG2E for g in heads]
    for g in heads:
        for n in range(nb):
            vt_scr[g, n // 2, :, (n % 2) * tq:(n % 2 + 1) * tq] = (
                v_ref[n * tq:(n + 1) * tq, cols[g]].astype(F32).T.astype(BF16))

    def part_max(s):
        return jnp.max(s.reshape(s.shape[0] // 8, 8, tq), axis=0)

    def q_tile(cur, carry):
        rows = pl.ds(pl.multiple_of(cur * tq, tq), tq)
        q = [q_ref[rows, cols[g]] for g in heads]
        blk = lax.broadcasted_iota(jnp.int32, (nb, tq), 0)
        past = blk < cur

        negm_prev = []
        for g in heads:
            gate = lax.dot_general(km_ref[0, :, cols[g]], q[g].astype(F32), _NT,
                                   precision=lax.Precision.HIGHEST, preferred_element_type=F32)
            gv = jnp.where(past, gate, NEG)
            sel = jnp.zeros((nb, tq), F32)
            for _ in range(MOBA_TOPK):
                mx = jnp.max(gv, axis=0, keepdims=True)
                first = jnp.min(jnp.where(gv == mx, blk, nb), axis=0, keepdims=True)
                hit = blk == first
                sel = jnp.where(hit & past, 1.0, sel)
                gv = jnp.where(hit, -jnp.inf, gv)
            negm = (sel - 1.0) * (-NEG)
            add_scr[g] = jnp.where(blk < cur - 1, negm + c31[g], NEG)
            negm_prev.append(jnp.sum(jnp.where(blk == cur - 1, negm, 0.0), axis=0, keepdims=True))

        def far_pair(i, mx8):
            out = []
            for g in heads:
                kb = k_ref[pl.ds(pl.multiple_of(i * (2 * tq), 2 * tq), 2 * tq), cols[g]]
                s2 = lax.dot_general(kb, q[g], _NT, preferred_element_type=F32)
                s_lo = s2[:tq] + add_scr[g, pl.ds(2 * i, 1), :]
                s_hi = s2[tq:] + add_scr[g, pl.ds(2 * i + 1, 1), :]
                s_scr[g, i, :tq, :] = s_lo
                s_scr[g, i, tq:, :] = s_hi
                out.append(jnp.maximum(mx8[g], jnp.maximum(part_max(s_lo), part_max(s_hi))))
            return tuple(out)

        nfar = jnp.maximum(cur - 1, 0)
        mx8 = lax.fori_loop(0, (nfar + 1) // 2, far_pair,
                            tuple(jnp.full((8, tq), NEG, F32) for _ in heads))

        def one_block(n, add_tiles, mx8):
            out = []
            for g in heads:
                kb = k_ref[pl.ds(pl.multiple_of(n * tq, tq), tq), cols[g]]
                s = lax.dot_general(kb, q[g], _NT, preferred_element_type=F32) + add_tiles[g]
                s_scr[g, n // 2, pl.ds(pl.multiple_of((n % 2) * tq, tq), tq), :] = s
                out.append(jnp.maximum(mx8[g], part_max(s)))
            return tuple(out)

        mx8 = lax.cond(cur >= 1,
                       lambda a: one_block(cur - 1, [prev_ref[g] + negm_prev[g] for g in heads], a),
                       lambda a: a, mx8)
        mx8 = one_block(cur, [own_ref[g] for g in heads], mx8)

        @pl.when(cur % 2 == 0)
        def _():
            for g in heads:
                s_scr[g, cur // 2, tq:, :] = jnp.full((tq, tq), NEG, F32)

        m = [jnp.max(mx8[g], axis=0, keepdims=True) for g in heads]
        acc_scr[...] = jnp.zeros_like(acc_scr)

        def pv_pair(i, l8):
            out = []
            for g in heads:
                p = jnp.exp2(s_scr[g, i] - m[g])
                acc_scr[g] += _dot(vt_scr[g, i], p.astype(BF16))
                out.append(l8[g] + jnp.sum(p.reshape(2 * tq // 8, 8, tq), axis=0))
            return tuple(out)

        l8 = lax.fori_loop(0, cur // 2 + 1, pv_pair, tuple(jnp.zeros((8, tq), F32) for _ in heads))
        for g in heads:
            l = jnp.sum(l8[g], axis=0, keepdims=True)
            o_ref[rows, cols[g]] = (acc_scr[g] / l).T.astype(BF16)
        return carry

    lax.fori_loop(0, nb, q_tile, 0)


def _attn(c31, q, k, v, kmean, own, prev, batch, seq):
    t, width = q.shape
    tq = MOBA_BLOCK
    nb = seq // tq
    hd = ATT_HEAD_DIM
    hg = 2
    nh = width // hd
    assert nb % 2 == 0 and nh % hg == 0
    seq_spec = pl.BlockSpec((seq, hg * hd), lambda b, h: (b, h))
    tile_spec = pl.BlockSpec((hg, tq, tq), lambda b, h: (h, 0, 0))
    return pl.pallas_call(
        functools.partial(_attn_kernel, tq=tq, nb=nb, hg=hg),
        out_shape=jax.ShapeDtypeStruct((t, width), BF16),
        grid=(batch, nh // hg),
        in_specs=[pl.BlockSpec(memory_space=pltpu.SMEM),
                  seq_spec, seq_spec, seq_spec,
                  pl.BlockSpec((1, nb, hg * hd), lambda b, h: (b, 0, h)),
                  tile_spec, tile_spec],
        out_specs=seq_spec,
        scratch_shapes=[pltpu.VMEM((hg, nb // 2, hd, 2 * tq), BF16),
                        pltpu.VMEM((hg, nb, tq), F32),
                        pltpu.VMEM((hg, nb // 2, 2 * tq, tq), F32),
                        pltpu.VMEM((hg, hd, tq), F32)],
        compiler_params=_params(("parallel", "parallel")),
        name="attn",
    )(c31, q, k, v, kmean, own, prev)


def _post_kernel(x_ref, sgu_ref, att_ref, kvm_ref, nmix_ref, wg_ref, bg_ref, wa_ref, wb_ref,
                 wo_ref, nx_ref, wxq_ref, wxo_ref, nffn_ref, wr_ref, br_ref,
                 h2_ref, xn3_ref, route_ref, cnt_ref, carry_scr, *, tm, ts, d, xw):
    @pl.when(pl.program_id(0) == 0)
    def _():
        carry_scr[...] = jnp.zeros_like(carry_scr)

    lane = lax.broadcasted_iota(jnp.int32, (ts, LANE), 1)
    r = lax.broadcasted_iota(jnp.int32, (ts, ts), 0)
    c = lax.broadcasted_iota(jnp.int32, (ts, ts), 1)
    tri = jnp.where(c < r, 1.0, 0.0).astype(BF16)

    def sub_tile(rows):
        x = x_ref[rows, :]
        xn = _rms(x, nmix_ref[...]).astype(BF16)
        gates = jax.nn.sigmoid(_dot(xn, wg_ref[...]) + bg_ref[...])
        ya = _dot(sgu_ref[rows, :], wa_ref[...])
        yb = _dot(att_ref[rows, :], wb_ref[...])
        merged = (gates[:, :d] * ya + gates[:, d:] * yb).astype(BF16)
        h1 = x + _dot(merged, wo_ref[...])

        xn2 = _rms(h1, nx_ref[...]).astype(BF16)
        qx = (_dot(xn2, wxq_ref[...]) * (X_HEAD_DIM ** -0.5)).astype(BF16)
        outs = []
        for hh in range(X_HEADS):
            cs = slice(hh * X_HEAD_DIM, (hh + 1) * X_HEAD_DIM)
            vs = slice(xw + hh * X_HEAD_DIM, xw + (hh + 1) * X_HEAD_DIM)
            s = lax.dot_general(qx[:, cs], kvm_ref[:, cs], _NT, preferred_element_type=F32)
            p = jnp.exp(s - jnp.max(s, axis=-1, keepdims=True))
            o = _dot(p.astype(BF16), kvm_ref[:, vs]) / jnp.sum(p, axis=-1, keepdims=True)
            outs.append(o.astype(BF16))
        h2 = h1 + _dot(jnp.concatenate(outs, axis=1), wxo_ref[...])
        h2_ref[rows, :] = h2

        xn3 = _rms(h2, nffn_ref[...])
        xn3_ref[rows, :] = xn3
        x_hi = xn3.astype(BF16)
        x_lo = (xn3 - x_hi.astype(F32)).astype(BF16)
        logits = _dot(jnp.concatenate([x_hi, x_lo, x_hi], axis=1), wr_ref[...]) + br_ref[...]
        is_g = lane < N_GROUPS
        gl = jnp.where(is_g, logits, -jnp.inf)
        gmax = jnp.max(gl, axis=-1, keepdims=True)
        gsel = jnp.min(jnp.where(gl == gmax, lane, LANE), axis=-1, keepdims=True)
        pg = 1.0 / jnp.sum(jnp.exp(gl - gmax), axis=-1, keepdims=True)
        eidx = lane - N_GROUPS
        in_group = (eidx >= 0) & (eidx < N_EXPERTS) & ((eidx // EXPERTS_PER_GROUP) == gsel)
        el = jnp.where(in_group, logits, -jnp.inf)
        m1 = jnp.max(el, axis=-1, keepdims=True)
        i1 = jnp.min(jnp.where(el == m1, lane, LANE), axis=-1, keepdims=True)
        el2 = jnp.where(lane == i1, -jnp.inf, el)
        m2 = jnp.max(el2, axis=-1, keepdims=True)
        i2 = jnp.min(jnp.where(el2 == m2, lane, LANE), axis=-1, keepdims=True)
        e2 = jnp.exp(m2 - m1)
        w1 = pg / (1.0 + e2)
        w2 = pg * e2 / (1.0 + e2)
        eid1 = i1 - N_GROUPS
        eid2 = i2 - N_GROUPS

        oh1f = jnp.where(lane == eid1, 1.0, 0.0)
        oh2f = jnp.where(lane == eid2, 1.0, 0.0)
        c1 = _dot(tri, oh1f.astype(BF16))
        c2 = _dot(tri, oh2f.astype(BF16))
        tot1 = jnp.sum(oh1f, axis=0, keepdims=True)
        tot2 = jnp.sum(oh2f, axis=0, keepdims=True)
        return eid1, eid2, w1, w2, oh1f, oh2f, c1, c2, tot1, tot2

    parts = [sub_tile(slice(j * ts, (j + 1) * ts)) for j in range(tm // ts)]

    carry = carry_scr[...]
    for j, (eid1, eid2, w1, w2, oh1f, oh2f, c1, c2, tot1, tot2) in enumerate(parts):
        r1 = jnp.sum(oh1f * (c1 + carry), axis=-1, keepdims=True)
        r2 = jnp.sum(oh2f * (c2 + carry + tot1), axis=-1, keepdims=True)
        carry = carry + tot1 + tot2
        route = jnp.where(lane == 0, eid1.astype(F32), 0.0)
        route = jnp.where(lane == 1, eid2.astype(F32), route)
        route = jnp.where(lane == 2, w1, route)
        route = jnp.where(lane == 3, w2, route)
        route = jnp.where(lane == 4, r1, route)
        route = jnp.where(lane == 5, r2, route)
        route_ref[j * ts:(j + 1) * ts, :] = route
    carry_scr[...] = carry
    cnt_ref[...] = carry


def _post(x2d, sgu, att, kvm, consts, tm, seq):
    t, d = x2d.shape
    mlen = kvm.shape[0] // (t // seq)
    xw = kvm.shape[1] // 2
    tiles_per_batch = seq // tm
    row = lambda w: pl.BlockSpec((tm, w), lambda i: (i, 0))
    return pl.pallas_call(
        functools.partial(_post_kernel, tm=tm, ts=min(tm, 256), d=d, xw=xw),
        out_shape=(jax.ShapeDtypeStruct((t, d), F32), jax.ShapeDtypeStruct((t, d), F32),
                   jax.ShapeDtypeStruct((t, LANE), F32), jax.ShapeDtypeStruct((1, LANE), F32)),
        grid=(t // tm,),
        in_specs=[row(d), row(sgu.shape[1]), row(att.shape[1]),
                  pl.BlockSpec((mlen, kvm.shape[1]), lambda i: (i // tiles_per_batch, 0))]
                 + [_const_spec(c.shape) for c in consts],
        out_specs=(row(d), row(d), row(LANE), pl.BlockSpec((1, LANE), lambda i: (0, 0))),
        scratch_shapes=[pltpu.VMEM((1, LANE), F32)],
        compiler_params=_params(("arbitrary",)),
        name="post",
    )(x2d, sgu, att, kvm, *consts)


def _scatter_kernel(dest_ref, x_ref, buf_in_ref, buf_ref, sem, *, tm):
    del buf_in_ref

    def row_copy(r, slot):
        return pltpu.make_async_copy(x_ref.at[pl.ds(r, 1)],
                                     buf_ref.at[pl.ds(dest_ref[0, 0, slot * tm + r], 1)], sem)

    def start(r, carry):
        row_copy(r, 0).start()
        row_copy(r, 1).start()
        return carry

    def wait(r, carry):
        row_copy(r, 0).wait()
        row_copy(r, 1).wait()
        return carry

    lax.fori_loop(0, tm, start, 0)
    lax.fori_loop(0, tm, wait, 0)


def _scatter(dest3, xn3, buf0, tm):
    t, d = xn3.shape
    return pl.pallas_call(
        functools.partial(_scatter_kernel, tm=tm),
        out_shape=jax.ShapeDtypeStruct(buf0.shape, buf0.dtype),
        grid=(t // tm,),
        in_specs=[pl.BlockSpec((1, 1, 2 * tm), lambda i: (i, 0, 0), memory_space=pltpu.SMEM),
                  pl.BlockSpec((tm, d), lambda i: (i, 0)),
                  pl.BlockSpec(memory_space=pl.ANY)],
        out_specs=pl.BlockSpec(memory_space=pl.ANY),
        scratch_shapes=[pltpu.SemaphoreType.DMA],
        input_output_aliases={2: 0},
        compiler_params=_params(("arbitrary",)),
        name="scatter",
    )(dest3, xn3, buf0)


def _expert_kernel(be_ref, first_ref, nxt_ref, nused_ref, x_ref, wg_hbm, wu_hbm, wd_hbm, y_ref,
                   wg_st, wu_st, wd_st, wg_bf, wu_bf, wd_bf, sem):
    j = pl.program_id(0)

    def fetch(e):
        return (pltpu.make_async_copy(wg_hbm.at[e], wg_st, sem.at[0]),
                pltpu.make_async_copy(wu_hbm.at[e], wu_st, sem.at[1]),
                pltpu.make_async_copy(wd_hbm.at[e], wd_st, sem.at[2]))

    @pl.when(j == 0)
    def _():
        for cp in fetch(be_ref[0]):
            cp.start()

    @pl.when(first_ref[j] == 1)
    def _():
        for cp in fetch(be_ref[j]):
            cp.wait()
        wg_bf[...] = wg_st[...].astype(BF16)
        wu_bf[...] = wu_st[...].astype(BF16)
        wd_bf[...] = wd_st[...].astype(BF16)

        @pl.when(nxt_ref[j] >= 0)
        def _():
            for cp in fetch(nxt_ref[j]):
                cp.start()

    @pl.when(j < nused_ref[0])
    def _():
        xb = x_ref[...].astype(BF16)
        hid = jax.nn.silu(_dot(xb, wg_bf[...])) * _dot(xb, wu_bf[...])
        y_ref[...] = _dot(hid.astype(BF16), wd_bf[...])

    @pl.when(j >= nused_ref[0])
    def _():
        y_ref[...] = jnp.zeros_like(y_ref)


def _expert(blk_expert, first, nxt, nused, buf, w_gate, w_up, w_down):
    n, d = buf.shape
    eh = w_gate.shape[2]
    nblk = n // MOE_BLOCK
    hbm = pl.BlockSpec(memory_space=pl.ANY)
    grid_spec = pltpu.PrefetchScalarGridSpec(
        num_scalar_prefetch=4,
        grid=(nblk,),
        in_specs=[pl.BlockSpec((MOE_BLOCK, d), lambda j, *_: (j, 0)), hbm, hbm, hbm],
        out_specs=pl.BlockSpec((MOE_BLOCK, d), lambda j, *_: (j, 0)),
        scratch_shapes=[pltpu.VMEM((d, eh), F32), pltpu.VMEM((d, eh), F32), pltpu.VMEM((eh, d), F32),
                        pltpu.VMEM((d, eh), BF16), pltpu.VMEM((d, eh), BF16), pltpu.VMEM((eh, d), BF16),
                        pltpu.SemaphoreType.DMA((3,))],
    )
    return pl.pallas_call(
        _expert_kernel,
        out_shape=jax.ShapeDtypeStruct((n, d), F32),
        grid_spec=grid_spec,
        compiler_params=_params(("arbitrary",)),
        name="expert",
    )(blk_expert, first, nxt, nused, buf, w_gate, w_up, w_down)


def _combine_kernel(dest_ref, h2_ref, route_ref, g_ref, ybuf_ref, o_ref, rows_scr, sem, *, tm):
    def row_copy(r, slot):
        return pltpu.make_async_copy(ybuf_ref.at[pl.ds(dest_ref[0, 0, slot * tm + r], 1)],
                                     rows_scr.at[slot, pl.ds(r, 1)], sem)

    def start(r, carry):
        row_copy(r, 0).start()
        row_copy(r, 1).start()
        return carry

    def wait(r, carry):
        row_copy(r, 0).wait()
        row_copy(r, 1).wait()
        return carry

    lax.fori_loop(0, tm, start, 0)
    lax.fori_loop(0, tm, wait, 0)
    route = route_ref[...]
    y = route[:, 2:3] * rows_scr[0] + route[:, 3:4] * rows_scr[1]
    o_ref[...] = _rms(h2_ref[...] + y, g_ref[...])


def _combine(dest3, h2, route, g, ybuf, tm):
    t, d = h2.shape
    return pl.pallas_call(
        functools.partial(_combine_kernel, tm=tm),
        out_shape=jax.ShapeDtypeStruct((t, d), F32),
        grid=(t // tm,),
        in_specs=[pl.BlockSpec((1, 1, 2 * tm), lambda i: (i, 0, 0), memory_space=pltpu.SMEM),
                  pl.BlockSpec((tm, d), lambda i: (i, 0)),
                  pl.BlockSpec((tm, LANE), lambda i: (i, 0)),
                  _const_spec(g.shape),
                  pl.BlockSpec(memory_space=pl.ANY)],
        out_specs=pl.BlockSpec((tm, d), lambda i: (i, 0)),
        scratch_shapes=[pltpu.VMEM((2, tm, d), F32), pltpu.SemaphoreType.DMA],
        compiler_params=_params(("arbitrary",)),
        name="combine",
    )(dest3, h2, route, g, ybuf)


def kernel(x, mem, norm_mix, w_in, w_gate, b_gate, sgu_norm, w_spatial, b_spatial, w_branch_a,
           w_branch_b, rel_bias, w_o, norm_x, norm_mem, w_xq, w_xkv, w_xo, norm_ffn,
           w_router_group, b_router_group, w_router_expert, b_router_expert,
           w_e_gate, w_e_up, w_e_down, norm_final):
    batch, seq, d = x.shape
    assert norm_mix.shape[0] == 1, "one layer"
    assert seq % MOBA_BLOCK == 0 and d % LANE == 0
    t = batch * seq
    tm_in, tm_post, tm_moe = 512, 512, 256
    row = lambda a: a.reshape(1, -1).astype(F32)
    bf = lambda a: a.astype(BF16)

    x2d = x.reshape(t, d)
    kvm = _memkv(mem.reshape(-1, d), row(norm_mem[0]), bf(w_xkv[0]))

    width = w_in.shape[2] // 5
    bsp_full = jnp.repeat(b_spatial[0].T, width // SGU_GROUPS, axis=1)
    sgu, q, k, v, kmean = _inproj(x2d, row(norm_mix[0]), bf(w_in[0]), row(sgu_norm[0]),
                                  w_spatial[0], bsp_full, tm_in)

    rel_t = rel_bias.T.astype(F32)
    own, prev = _relbias(rel_t, MOBA_BLOCK)
    att = _attn(rel_t[:, NUM_BUCKETS - 1], q, k, v,
                kmean.reshape(batch, seq // MOBA_BLOCK, width), own, prev, batch, seq)

    lane_pad = LANE - N_GROUPS - N_EXPERTS
    w_router = jnp.concatenate([w_router_group[0], w_router_expert[0], jnp.zeros((d, lane_pad), F32)], axis=1)
    b_router = jnp.concatenate([b_router_group[0], b_router_expert[0], jnp.zeros((lane_pad,), F32)])
    wr_hi = bf(w_router)
    wr_lo = bf(w_router - wr_hi.astype(F32))
    consts = [row(norm_mix[0]), bf(w_gate[0]), row(b_gate[0]), bf(w_branch_a[0]), bf(w_branch_b[0]),
              bf(w_o[0]), row(norm_x[0]), bf(w_xq[0]), bf(w_xo[0]), row(norm_ffn[0]),
              jnp.concatenate([wr_hi, wr_hi, wr_lo], axis=0), row(b_router)]
    h2, xn3, route, cnt = _post(x2d, sgu, att, kvm, consts, tm_post, seq)

    counts = cnt[0, :N_EXPERTS].astype(jnp.int32)
    padded = (counts + MOE_BLOCK - 1) // MOE_BLOCK * MOE_BLOCK
    pad_ends = jnp.cumsum(padded)
    pad_starts = pad_ends - padded
    route_t = route[:, :8].T
    dest = [pad_starts[route_t[s].astype(jnp.int32)] + route_t[4 + s].astype(jnp.int32) for s in range(2)]
    nblk = -(-(2 * t) // MOE_BLOCK) + N_EXPERTS
    blk_start = jnp.arange(nblk, dtype=jnp.int32) * MOE_BLOCK
    blk_expert = jnp.minimum(
        jnp.sum((pad_ends[None, :] <= blk_start[:, None]).astype(jnp.int32), axis=1), N_EXPERTS - 1)
    nused = (pad_ends[-1:] // MOE_BLOCK).astype(jnp.int32)
    blk_idx = jnp.arange(nblk, dtype=jnp.int32)
    prev_expert = jnp.concatenate([jnp.full((1,), -1, jnp.int32), blk_expert[:-1]])
    first = ((blk_idx < nused[0]) & (blk_expert != prev_expert)).astype(jnp.int32)
    nxt_idx = blk_idx + (padded // MOE_BLOCK)[blk_expert]
    nxt = jnp.where((first == 1) & (nxt_idx < nused[0]),
                    blk_expert[jnp.minimum(nxt_idx, nblk - 1)], -1).astype(jnp.int32)
    dest3 = jnp.concatenate([dst.reshape(t // tm_moe, 1, tm_moe) for dst in dest], axis=2)

    buf = _scatter(dest3, xn3, jnp.zeros((nblk * MOE_BLOCK, d), F32), tm_moe)
    ybuf = _expert(blk_expert, first, nxt, nused, buf, w_e_gate[0], w_e_up[0], w_e_down[0])
    out = _combine(dest3, h2, route, row(norm_final), ybuf, tm_moe)
    return out.reshape(batch, seq, d)
```

```python
import functools
import math

import numpy as np
import jax
import jax.numpy as jnp
from jax import lax
from jax.experimental import pallas as pl
from jax.experimental.pallas import tpu as pltpu

F32 = jnp.float32
BF16 = jnp.bfloat16

EPS = 1e-6
NEG = -1e30
LOG2E = math.log2(math.e)
LANE = 128

SGU_GROUPS = 8
CHUNK = 128
ATT_HEADS = 8
ATT_HEAD_DIM = 128
MOBA_BLOCK = 256
MOBA_TOPK = 3
NUM_BUCKETS = 32
MAX_DISTANCE = 128
X_HEADS = 4
X_HEAD_DIM = 128
N_GROUPS = 8
EXPERTS_PER_GROUP = 8
N_EXPERTS = N_GROUPS * EXPERTS_PER_GROUP
MOE_BLOCK = 256

VMEM_LIMIT = 56 * 1024 * 1024

_NT = (((1,), (1,)), ((), ()))


def _rms(x, g):
    return x * lax.rsqrt(jnp.mean(x * x, axis=-1, keepdims=True) + EPS) * g


def _gelu(x):
    c = math.sqrt(2.0 / math.pi)
    return x * (0.5 * (1.0 + jnp.tanh(c * (x + 0.044715 * (x * x * x)))))


def _dot(a, b):
    return jnp.dot(a, b, preferred_element_type=F32)


def _const_spec(shape):
    nd = len(shape)
    return pl.BlockSpec(shape, lambda *_: (0,) * nd, pipeline_mode=pl.Buffered(1))


def _params(sem):
    return pltpu.CompilerParams(dimension_semantics=sem, vmem_limit_bytes=VMEM_LIMIT)


def _memkv_kernel(mem_ref, g_ref, w_ref, o_ref):
    mn = _rms(mem_ref[...], g_ref[...]).astype(BF16)
    o_ref[...] = _dot(mn, w_ref[...]).astype(BF16)


def _memkv(mem2d, g, w_bf):
    n, d = mem2d.shape
    tm = 256
    return pl.pallas_call(
        _memkv_kernel,
        out_shape=jax.ShapeDtypeStruct((n, w_bf.shape[1]), BF16),
        grid=(n // tm,),
        in_specs=[pl.BlockSpec((tm, d), lambda i: (i, 0)),
                  _const_spec(g.shape), _const_spec(w_bf.shape)],
        out_specs=pl.BlockSpec((tm, w_bf.shape[1]), lambda i: (i, 0)),
        compiler_params=_params(("parallel",)),
        name="memkv",
    )(mem2d, g, w_bf)


def _inproj_kernel(x_ref, g_ref, win_ref, sg_ref, wsp_ref, bsp_ref,
                   sgu_ref, q_ref, k_ref, v_ref, km_ref, *, tm, width, scale):
    xn = _rms(x_ref[...], g_ref[...]).astype(BF16)
    u = _gelu(_dot(xn, win_ref[:, 0:width]))
    vv = _gelu(_dot(xn, win_ref[:, width:2 * width]))
    vn = _rms(vv, sg_ref[...]).astype(BF16)
    gd = width // SGU_GROUPS
    row = lax.broadcasted_iota(jnp.int32, (CHUNK, CHUNK), 0)
    col = lax.broadcasted_iota(jnp.int32, (CHUNK, CHUNK), 1)
    causal = col <= row
    wsp = [jnp.where(causal, wsp_ref[g], 0.0).astype(BF16) for g in range(SGU_GROUPS)]
    bias = bsp_ref[...]
    for c in range(tm // CHUNK):
        rs = slice(c * CHUNK, (c + 1) * CHUNK)
        mixed = jnp.concatenate(
            [_dot(wsp[g], vn[rs, g * gd:(g + 1) * gd]) for g in range(SGU_GROUPS)], axis=1)
        sgu_ref[rs, :] = (u[rs, :] * (mixed + bias)).astype(BF16)
    q_ref[...] = (_dot(xn, win_ref[:, 2 * width:3 * width]) * scale).astype(BF16)
    k = _dot(xn, win_ref[:, 3 * width:4 * width])
    k_ref[...] = k.astype(BF16)
    for j in range(tm // MOBA_BLOCK):
        km_ref[0, j:j + 1, :] = jnp.mean(k[j * MOBA_BLOCK:(j + 1) * MOBA_BLOCK, :], axis=0, keepdims=True)
    v_ref[...] = _dot(xn, win_ref[:, 4 * width:5 * width]).astype(BF16)


def _inproj(x2d, g, win_bf, sg, wsp, bsp_full, tm):
    t, d = x2d.shape
    width = win_bf.shape[1] // 5
    nkm = tm // MOBA_BLOCK
    row_spec = pl.BlockSpec((tm, width), lambda i: (i, 0))
    act = jax.ShapeDtypeStruct((t, width), BF16)
    return pl.pallas_call(
        functools.partial(_inproj_kernel, tm=tm, width=width, scale=ATT_HEAD_DIM ** -0.5 * LOG2E),
        out_shape=(act, act, act, act, jax.ShapeDtypeStruct((t // tm, nkm, width), F32)),
        grid=(t // tm,),
        in_specs=[pl.BlockSpec((tm, d), lambda i: (i, 0)),
                  _const_spec(g.shape), _const_spec(win_bf.shape), _const_spec(sg.shape),
                  _const_spec(wsp.shape), _const_spec(bsp_full.shape)],
        out_specs=(row_spec, row_spec, row_spec, row_spec,
                   pl.BlockSpec((1, nkm, width), lambda i: (i, 0, 0))),
        compiler_params=_params(("parallel",)),
        name="inproj",
    )(x2d, g, win_bf, sg, wsp, bsp_full)


def _bucket_uppers():
    d = np.arange(0, 4 * MAX_DISTANCE)
    max_exact = NUM_BUCKETS // 2
    nf = np.maximum(d, 1).astype(np.float32)
    large = max_exact + (np.log(nf / max_exact) / math.log(MAX_DISTANCE / max_exact)
                         * (NUM_BUCKETS - max_exact)).astype(np.int32)
    bucket = np.where(d < max_exact, d, np.minimum(large, NUM_BUCKETS - 1))
    return [int(d[bucket > b].min()) for b in range(NUM_BUCKETS - 1)]


def _relbias_kernel(rb_ref, own_ref, prev_ref, *, tq, uppers):
    h = pl.program_id(0)
    key = lax.broadcasted_iota(jnp.int32, (tq, tq), 0)
    qry = lax.broadcasted_iota(jnp.int32, (tq, tq), 1)

    def table(dist):
        val = jnp.full((tq, tq), rb_ref[h, NUM_BUCKETS - 1] * LOG2E, F32)
        for b in range(NUM_BUCKETS - 2, -1, -1):
            val = jnp.where(dist < uppers[b], rb_ref[h, b] * LOG2E, val)
        return val

    d = qry - key
    own_ref[0] = jnp.where(d >= 0, table(d), NEG)
    prev_ref[0] = table(d + tq)


def _relbias(rel_bias_t, tq):
    nh = rel_bias_t.shape[0]
    tile = jax.ShapeDtypeStruct((nh, tq, tq), F32)
    spec = pl.BlockSpec((1, tq, tq), lambda h: (h, 0, 0))
    return pl.pallas_call(
        functools.partial(_relbias_kernel, tq=tq, uppers=_bucket_uppers()),
        out_shape=(tile, tile),
        grid=(nh,),
        in_specs=[pl.BlockSpec(memory_space=pltpu.SMEM)],
        out_specs=(spec, spec),
        compiler_params=_params(("parallel",)),
        name="relbias",
    )(rel_bias_t)


def _attn_kernel(c31_ref, q_ref, k_ref, v_ref, km_ref, own_ref, prev_ref, o_ref,
                 vt_scr, add_scr, s_scr, acc_scr, *, tq, nb, hg):
    hd = ATT_HEAD_DIM
    heads = range(hg)
    cols = [slice(g * hd, (g + 1) * hd) for g in heads]
    c31 = [c31_ref[pl.program_id(1) * hg + g] * LOG2E for g in heads]
    for g in heads:
        for n in range(nb):
            vt_scr[g, n // 2, :, (n % 2) * tq:(n % 2 + 1) * tq] = (
                v_ref[n * tq:(n + 1) * tq, cols[g]].astype(F32).T.astype(BF16))

    def part_max(s):
        return jnp.max(s.reshape(s.shape[0] // 8, 8, tq), axis=0)

    def q_tile(cur, carry):
        rows = pl.ds(pl.multiple_of(cur * tq, tq), tq)
        q = [q_ref[rows, cols[g]] for g in heads]
        blk = lax.broadcasted_iota(jnp.int32, (nb, tq), 0)
        past = blk < cur

        negm_prev = []
        for g in heads:
            gate = lax.dot_general(km_ref[0, :, cols[g]], q[g].astype(F32), _NT,
                                   precision=lax.Precision.HIGHEST, preferred_element_type=F32)
            gv = jnp.where(past, gate, NEG)
            sel = jnp.zeros((nb, tq), F32)
            for _ in range(MOBA_TOPK):
                mx = jnp.max(gv, axis=0, keepdims=True)
                first = jnp.min(jnp.where(gv == mx, blk, nb), axis=0, keepdims=True)
                hit = blk == first
                sel = jnp.where(hit & past, 1.0, sel)
                gv = jnp.where(hit, -jnp.inf, gv)
            negm = (sel - 1.0) * (-NEG)
            add_scr[g] = jnp.where(blk < cur - 1, negm + c31[g], NEG)
            negm_prev.append(jnp.sum(jnp.where(blk == cur - 1, negm, 0.0), axis=0, keepdims=True))

        def far_pair(i, mx8):
            out = []
            for g in heads:
                kb = k_ref[pl.ds(pl.multiple_of(i * (2 * tq), 2 * tq), 2 * tq), cols[g]]
                s2 = lax.dot_general(kb, q[g], _NT, preferred_element_type=F32)
                s_lo = s2[:tq] + add_scr[g, pl.ds(2 * i, 1), :]
                s_hi = s2[tq:] + add_scr[g, pl.ds(2 * i + 1, 1), :]
                s_scr[g, i, :tq, :] = s_lo
                s_scr[g, i, tq:, :] = s_hi
                out.append(jnp.maximum(mx8[g], jnp.maximum(part_max(s_lo), part_max(s_hi))))
            return tuple(out)

        nfar = jnp.maximum(cur - 1, 0)
        mx8 = lax.fori_loop(0, (nfar + 1) // 2, far_pair,
                            tuple(jnp.full((8, tq), NEG, F32) for _ in heads))

        def one_block(n, add_tiles, mx8):
            out = []
            for g in heads:
                kb = k_ref[pl.ds(pl.multiple_of(n * tq, tq), tq), cols[g]]
                s = lax.dot_general(kb, q[g], _NT, preferred_element_type=F32) + add_tiles[g]
                s_scr[g, n // 2, pl.ds(pl.multiple_of((n % 2) * tq, tq), tq), :] = s
                out.append(jnp.maximum(mx8[g], part_max(s)))
            return tuple(out)

        mx8 = lax.cond(cur >= 1,
                       lambda a: one_block(cur - 1, [prev_ref[g] + negm_prev[g] for g in heads], a),
                       lambda a: a, mx8)
        mx8 = one_block(cur, [own_ref[g] for g in heads], mx8)

        @pl.when(cur % 2 == 0)
        def _():
            for g in heads:
                s_scr[g, cur // 2, tq:, :] = jnp.full((tq, tq), NEG, F32)

        m = [jnp.max(mx8[g], axis=0, keepdims=True) for g in heads]
        acc_scr[...] = jnp.zeros_like(acc_scr)

        def pv_pair(i, l8):
            out = []
            for g in heads:
                p = jnp.exp2(s_scr[g, i] - m[g])
                acc_scr[g] += _dot(vt_scr[g, i], p.astype(BF16))
                out.append(l8[g] + jnp.sum(p.reshape(2 * tq // 8, 8, tq), axis=0))
            return tuple(out)

        l8 = lax.fori_loop(0, cur // 2 + 1, pv_pair, tuple(jnp.zeros((8, tq), F32) for _ in heads))
        for g in heads:
            l = jnp.sum(l8[g], axis=0, keepdims=True)
            o_ref[rows, cols[g]] = (acc_scr[g] / l).T.astype(BF16)
        return carry

    lax.fori_loop(0, nb, q_tile, 0)


def _attn(c31, q, k, v, kmean, own, prev, batch, seq):
    t, width = q.shape
    tq = MOBA_BLOCK
    nb = seq // tq
    hd = ATT_HEAD_DIM
    hg = 4
    nh = width // hd
    assert nb % 2 == 0 and nh % hg == 0
    seq_spec = pl.BlockSpec((seq, hg * hd), lambda b, h: (b, h), pipeline_mode=pl.Buffered(1))
    tile_spec = pl.BlockSpec((hg, tq, tq), lambda b, h: (h, 0, 0))
    return pl.pallas_call(
        functools.partial(_attn_kernel, tq=tq, nb=nb, hg=hg),
        out_shape=jax.ShapeDtypeStruct((t, width), BF16),
        grid=(batch, nh // hg),
        in_specs=[pl.BlockSpec(memory_space=pltpu.SMEM),
                  seq_spec, seq_spec, seq_spec,
                  pl.BlockSpec((1, nb, hg * hd), lambda b, h: (b, 0, h)),
                  tile_spec, tile_spec],
        out_specs=seq_spec,
        scratch_shapes=[pltpu.VMEM((hg, nb // 2, hd, 2 * tq), BF16),
                        pltpu.VMEM((hg, nb, tq), F32),
                        pltpu.VMEM((hg, nb // 2, 2 * tq, tq), F32),
                        pltpu.VMEM((hg, hd, tq), F32)],
        compiler_params=_params(("parallel", "parallel")),
        name="attn",
    )(c31, q, k, v, kmean, own, prev)


def _post_kernel(x_ref, sgu_ref, att_ref, kvm_ref, nmix_ref, wg_ref, bg_ref, wa_ref, wb_ref,
                 wo_ref, nx_ref, wxq_ref, wxo_ref, nffn_ref, wr_ref, br_ref,
                 h2_ref, xn3_ref, route_ref, cnt_ref, carry_scr, *, tm, ts, d, xw):
    @pl.when(pl.program_id(0) == 0)
    def _():
        carry_scr[...] = jnp.zeros_like(carry_scr)

    lane = lax.broadcasted_iota(jnp.int32, (ts, LANE), 1)
    r = lax.broadcasted_iota(jnp.int32, (ts, ts), 0)
    c = lax.broadcasted_iota(jnp.int32, (ts, ts), 1)
    tri = jnp.where(c < r, 1.0, 0.0).astype(BF16)

    def sub_tile(rows):
        x = x_ref[rows, :]
        xn = _rms(x, nmix_ref[...]).astype(BF16)
        gates = jax.nn.sigmoid(_dot(xn, wg_ref[...]) + bg_ref[...])
        ya = _dot(sgu_ref[rows, :], wa_ref[...])
        yb = _dot(att_ref[rows, :], wb_ref[...])
        merged = (gates[:, :d] * ya + gates[:, d:] * yb).astype(BF16)
        h1 = x + _dot(merged, wo_ref[...])

        xn2 = _rms(h1, nx_ref[...]).astype(BF16)
        qx = (_dot(xn2, wxq_ref[...]) * (X_HEAD_DIM ** -0.5)).astype(BF16)
        outs = []
        for hh in range(X_HEADS):
            cs = slice(hh * X_HEAD_DIM, (hh + 1) * X_HEAD_DIM)
            vs = slice(xw + hh * X_HEAD_DIM, xw + (hh + 1) * X_HEAD_DIM)
            s = lax.dot_general(qx[:, cs], kvm_ref[:, cs], _NT, preferred_element_type=F32)
            p = jnp.exp(s - jnp.max(s, axis=-1, keepdims=True))
            o = _dot(p.astype(BF16), kvm_ref[:, vs]) / jnp.sum(p, axis=-1, keepdims=True)
            outs.append(o.astype(BF16))
        h2 = h1 + _dot(jnp.concatenate(outs, axis=1), wxo_ref[...])
        h2_ref[rows, :] = h2

        xn3 = _rms(h2, nffn_ref[...])
        xn3_ref[rows, :] = xn3
        x_hi = xn3.astype(BF16)
        x_lo = (xn3 - x_hi.astype(F32)).astype(BF16)
        logits = _dot(jnp.concatenate([x_hi, x_lo, x_hi], axis=1), wr_ref[...]) + br_ref[...]
        is_g = lane < N_GROUPS
        gl = jnp.where(is_g, logits, -jnp.inf)
        gmax = jnp.max(gl, axis=-1, keepdims=True)
        gsel = jnp.min(jnp.where(gl == gmax, lane, LANE), axis=-1, keepdims=True)
        pg = 1.0 / jnp.sum(jnp.exp(gl - gmax), axis=-1, keepdims=True)
        eidx = lane - N_GROUPS
        in_group = (eidx >= 0) & (eidx < N_EXPERTS) & ((eidx // EXPERTS_PER_GROUP) == gsel)
        el = jnp.where(in_group, logits, -jnp.inf)
        m1 = jnp.max(el, axis=-1, keepdims=True)
        i1 = jnp.min(jnp.where(el == m1, lane, LANE), axis=-1, keepdims=True)
        el2 = jnp.where(lane == i1, -jnp.inf, el)
        m2 = jnp.max(el2, axis=-1, keepdims=True)
        i2 = jnp.min(jnp.where(el2 == m2, lane, LANE), axis=-1, keepdims=True)
        e2 = jnp.exp(m2 - m1)
        w1 = pg / (1.0 + e2)
        w2 = pg * e2 / (1.0 + e2)
        eid1 = i1 - N_GROUPS
        eid2 = i2 - N_GROUPS

        oh1f = jnp.where(lane == eid1, 1.0, 0.0)
        oh2f = jnp.where(lane == eid2, 1.0, 0.0)
        c1 = _dot(tri, oh1f.astype(BF16))
        c2 = _dot(tri, oh2f.astype(BF16))
        tot1 = jnp.sum(oh1f, axis=0, keepdims=True)
        tot2 = jnp.sum(oh2f, axis=0, keepdims=True)
        return eid1, eid2, w1, w2, oh1f, oh2f, c1, c2, tot1, tot2

    parts = [sub_tile(slice(j * ts, (j + 1) * ts)) for j in range(tm // ts)]

    carry = carry_scr[...]
    for j, (eid1, eid2, w1, w2, oh1f, oh2f, c1, c2, tot1, tot2) in enumerate(parts):
        r1 = jnp.sum(oh1f * (c1 + carry), axis=-1, keepdims=True)
        r2 = jnp.sum(oh2f * (c2 + carry + tot1), axis=-1, keepdims=True)
        carry = carry + tot1 + tot2
        route = jnp.where(lane == 0, eid1.astype(F32), 0.0)
        route = jnp.where(lane == 1, eid2.astype(F32), route)
        route = jnp.where(lane == 2, w1, route)
        route = jnp.where(lane == 3, w2, route)
        route = jnp.where(lane == 4, r1, route)
        route = jnp.where(lane == 5, r2, route)
        route_ref[j * ts:(j + 1) * ts, :] = route
    carry_scr[...] = carry
    cnt_ref[...] = carry


def _post(x2d, sgu, att, kvm, consts, tm, seq):
    t, d = x2d.shape
    mlen = kvm.shape[0] // (t // seq)
    xw = kvm.shape[1] // 2
    tiles_per_batch = seq // tm
    row = lambda w: pl.BlockSpec((tm, w), lambda i: (i, 0))
    return pl.pallas_call(
        functools.partial(_post_kernel, tm=tm, ts=min(tm, 256), d=d, xw=xw),
        out_shape=(jax.ShapeDtypeStruct((t, d), F32), jax.ShapeDtypeStruct((t, d), F32),
                   jax.ShapeDtypeStruct((t, LANE), F32), jax.ShapeDtypeStruct((1, LANE), F32)),
        grid=(t // tm,),
        in_specs=[row(d), row(sgu.shape[1]), row(att.shape[1]),
                  pl.BlockSpec((mlen, kvm.shape[1]), lambda i: (i // tiles_per_batch, 0))]
                 + [_const_spec(c.shape) for c in consts],
        out_specs=(row(d), row(d), row(LANE), pl.BlockSpec((1, LANE), lambda i: (0, 0))),
        scratch_shapes=[pltpu.VMEM((1, LANE), F32)],
        compiler_params=_params(("arbitrary",)),
        name="post",
    )(x2d, sgu, att, kvm, *consts)


def _scatter_kernel(dest_ref, x_ref, buf_in_ref, buf_ref, sem, *, tm):
    del buf_in_ref

    def row_copy(r, slot):
        return pltpu.make_async_copy(x_ref.at[pl.ds(r, 1)],
                                     buf_ref.at[pl.ds(dest_ref[0, 0, slot * tm + r], 1)], sem)

    def start(r, carry):
        row_copy(r, 0).start()
        row_copy(r, 1).start()
        return carry

    def wait(r, carry):
        row_copy(r, 0).wait()
        row_copy(r, 1).wait()
        return carry

    lax.fori_loop(0, tm, start, 0)
    lax.fori_loop(0, tm, wait, 0)


def _scatter(dest3, xn3, buf0, tm):
    t, d = xn3.shape
    return pl.pallas_call(
        functools.partial(_scatter_kernel, tm=tm),
        out_shape=jax.ShapeDtypeStruct(buf0.shape, buf0.dtype),
        grid=(t // tm,),
        in_specs=[pl.BlockSpec((1, 1, 2 * tm), lambda i: (i, 0, 0), memory_space=pltpu.SMEM),
                  pl.BlockSpec((tm, d), lambda i: (i, 0)),
                  pl.BlockSpec(memory_space=pl.ANY)],
        out_specs=pl.BlockSpec(memory_space=pl.ANY),
        scratch_shapes=[pltpu.SemaphoreType.DMA],
        input_output_aliases={2: 0},
        compiler_params=_params(("arbitrary",)),
        name="scatter",
    )(dest3, xn3, buf0)


def _expert_kernel(be_ref, first_ref, nxt_ref, nused_ref, x_ref, wg_hbm, wu_hbm, wd_hbm, y_ref,
                   wg_st, wu_st, wd_st, wg_bf, wu_bf, wd_bf, sem):
    j = pl.program_id(0)

    def fetch(e):
        return (pltpu.make_async_copy(wg_hbm.at[e], wg_st, sem.at[0]),
                pltpu.make_async_copy(wu_hbm.at[e], wu_st, sem.at[1]),
                pltpu.make_async_copy(wd_hbm.at[e], wd_st, sem.at[2]))

    @pl.when(j == 0)
    def _():
        for cp in fetch(be_ref[0]):
            cp.start()

    @pl.when(first_ref[j] == 1)
    def _():
        for cp in fetch(be_ref[j]):
            cp.wait()
        wg_bf[...] = wg_st[...].astype(BF16)
        wu_bf[...] = wu_st[...].astype(BF16)
        wd_bf[...] = wd_st[...].astype(BF16)

        @pl.when(nxt_ref[j] >= 0)
        def _():
            for cp in fetch(nxt_ref[j]):
                cp.start()

    @pl.when(j < nused_ref[0])
    def _():
        xb = x_ref[...].astype(BF16)
        hid = jax.nn.silu(_dot(xb, wg_bf[...])) * _dot(xb, wu_bf[...])
        y_ref[...] = _dot(hid.astype(BF16), wd_bf[...])

    @pl.when(j >= nused_ref[0])
    def _():
        y_ref[...] = jnp.zeros_like(y_ref)


def _expert(blk_expert, first, nxt, nused, buf, w_gate, w_up, w_down):
    n, d = buf.shape
    eh = w_gate.shape[2]
    nblk = n // MOE_BLOCK
    hbm = pl.BlockSpec(memory_space=pl.ANY)
    grid_spec = pltpu.PrefetchScalarGridSpec(
        num_scalar_prefetch=4,
        grid=(nblk,),
        in_specs=[pl.BlockSpec((MOE_BLOCK, d), lambda j, *_: (j, 0)), hbm, hbm, hbm],
        out_specs=pl.BlockSpec((MOE_BLOCK, d), lambda j, *_: (j, 0)),
        scratch_shapes=[pltpu.VMEM((d, eh), F32), pltpu.VMEM((d, eh), F32), pltpu.VMEM((eh, d), F32),
                        pltpu.VMEM((d, eh), BF16), pltpu.VMEM((d, eh), BF16), pltpu.VMEM((eh, d), BF16),
                        pltpu.SemaphoreType.DMA((3,))],
    )
    return pl.pallas_call(
        _expert_kernel,
        out_shape=jax.ShapeDtypeStruct((n, d), F32),
        grid_spec=grid_spec,
        compiler_params=_params(("arbitrary",)),
        name="expert",
    )(blk_expert, first, nxt, nused, buf, w_gate, w_up, w_down)


def _combine_kernel(dest_ref, h2_ref, route_ref, g_ref, ybuf_ref, o_ref, rows_scr, sem, *, tm):
    def row_copy(r, slot):
        return pltpu.make_async_copy(ybuf_ref.at[pl.ds(dest_ref[0, 0, slot * tm + r], 1)],
                                     rows_scr.at[slot, pl.ds(r, 1)], sem)

    def start(r, carry):
        row_copy(r, 0).start()
        row_copy(r, 1).start()
        return carry

    def wait(r, carry):
        row_copy(r, 0).wait()
        row_copy(r, 1).wait()
        return carry

    lax.fori_loop(0, tm, start, 0)
    lax.fori_loop(0, tm, wait, 0)
    route = route_ref[...]
    y = route[:, 2:3] * rows_scr[0] + route[:, 3:4] * rows_scr[1]
    o_ref[...] = _rms(h2_ref[...] + y, g_ref[...])


def _combine(dest3, h2, route, g, ybuf, tm):
    t, d = h2.shape
    return pl.pallas_call(
        functools.partial(_combine_kernel, tm=tm),
        out_shape=jax.ShapeDtypeStruct((t, d), F32),
        grid=(t // tm,),
        in_specs=[pl.BlockSpec((1, 1, 2 * tm), lambda i: (i, 0, 0), memory_space=pltpu.SMEM),
                  pl.BlockSpec((tm, d), lambda i: (i, 0)),
                  pl.BlockSpec((tm, LANE), lambda i: (i, 0)),
                  _const_spec(g.shape),
                  pl.BlockSpec(memory_space=pl.ANY)],
        out_specs=pl.BlockSpec((tm, d), lambda i: (i, 0)),
        scratch_shapes=[pltpu.VMEM((2, tm, d), F32), pltpu.SemaphoreType.DMA],
        compiler_params=_params(("arbitrary",)),
        name="combine",
    )(dest3, h2, route, g, ybuf)


def kernel(x, mem, norm_mix, w_in, w_gate, b_gate, sgu_norm, w_spatial, b_spatial, w_branch_a,
           w_branch_b, rel_bias, w_o, norm_x, norm_mem, w_xq, w_xkv, w_xo, norm_ffn,
           w_router_group, b_router_group, w_router_expert, b_router_expert,
           w_e_gate, w_e_up, w_e_down, norm_final):
    batch, seq, d = x.shape
    assert norm_mix.shape[0] == 1, "one layer"
    assert seq % MOBA_BLOCK == 0 and d % LANE == 0
    t = batch * seq
    tm_in, tm_post, tm_moe = 512, 512, 256
    row = lambda a: a.reshape(1, -1).astype(F32)
    bf = lambda a: a.astype(BF16)

    x2d = x.reshape(t, d)
    kvm = _memkv(mem.reshape(-1, d), row(norm_mem[0]), bf(w_xkv[0]))

    width = w_in.shape[2] // 5
    bsp_full = jnp.repeat(b_spatial[0].T, width // SGU_GROUPS, axis=1)
    sgu, q, k, v, kmean = _inproj(x2d, row(norm_mix[0]), bf(w_in[0]), row(sgu_norm[0]),
                                  w_spatial[0], bsp_full, tm_in)

    rel_t = rel_bias.T.astype(F32)
    own, prev = _relbias(rel_t, MOBA_BLOCK)
    att = _attn(rel_t[:, NUM_BUCKETS - 1], q, k, v,
                kmean.reshape(batch, seq // MOBA_BLOCK, width), own, prev, batch, seq)

    lane_pad = LANE - N_GROUPS - N_EXPERTS
    w_router = jnp.concatenate([w_router_group[0], w_router_expert[0], jnp.zeros((d, lane_pad), F32)], axis=1)
    b_router = jnp.concatenate([b_router_group[0], b_router_expert[0], jnp.zeros((lane_pad,), F32)])
    wr_hi = bf(w_router)
    wr_lo = bf(w_router - wr_hi.astype(F32))
    consts = [row(norm_mix[0]), bf(w_gate[0]), row(b_gate[0]), bf(w_branch_a[0]), bf(w_branch_b[0]),
              bf(w_o[0]), row(norm_x[0]), bf(w_xq[0]), bf(w_xo[0]), row(norm_ffn[0]),
              jnp.concatenate([wr_hi, wr_hi, wr_lo], axis=0), row(b_router)]
    h2, xn3, route, cnt = _post(x2d, sgu, att, kvm, consts, tm_post, seq)

    counts = cnt[0, :N_EXPERTS].astype(jnp.int32)
    padded = (counts + MOE_BLOCK - 1) // MOE_BLOCK * MOE_BLOCK
    pad_ends = jnp.cumsum(padded)
    pad_starts = pad_ends - padded
    route_t = route[:, :8].T
    dest = [pad_starts[route_t[s].astype(jnp.int32)] + route_t[4 + s].astype(jnp.int32) for s in range(2)]
    nblk = -(-(2 * t) // MOE_BLOCK) + N_EXPERTS
    blk_start = jnp.arange(nblk, dtype=jnp.int32) * MOE_BLOCK
    blk_expert = jnp.minimum(
        jnp.sum((pad_ends[None, :] <= blk_start[:, None]).astype(jnp.int32), axis=1), N_EXPERTS - 1)
    nused = (pad_ends[-1:] // MOE_BLOCK).astype(jnp.int32)
    blk_idx = jnp.arange(nblk, dtype=jnp.int32)
    prev_expert = jnp.concatenate([jnp.full((1,), -1, jnp.int32), blk_expert[:-1]])
    first = ((blk_idx < nused[0]) & (blk_expert != prev_expert)).astype(jnp.int32)
    nxt_idx = blk_idx + (padded // MOE_BLOCK)[blk_expert]
    nxt = jnp.where((first == 1) & (nxt_idx < nused[0]),
                    blk_expert[jnp.minimum(nxt_idx, nblk - 1)], -1).astype(jnp.int32)
    dest3 = jnp.concatenate([dst.reshape(t // tm_moe, 1, tm_moe) for dst in dest], axis=2)

    buf = _scatter(dest3, xn3, jnp.zeros((nblk * MOE_BLOCK, d), F32), tm_moe)
    ybuf = _expert(blk_expert, first, nxt, nused, buf, w_e_gate[0], w_e_up[0], w_e_down[0])
    out = _combine(dest3, h2, route, row(norm_final), ybuf, tm_moe)
    return out.reshape(batch, seq, d)
```

```python
import functools
import math

import numpy as np
import jax
import jax.numpy as jnp
from jax import lax
from jax.experimental import pallas as pl
from jax.experimental.pallas import tpu as pltpu

F32 = jnp.float32
BF16 = jnp.bfloat16

EPS = 1e-6
NEG = -1e30
LOG2E = math.log2(math.e)
LANE = 128

SGU_GROUPS = 8
CHUNK = 128
ATT_HEADS = 8
ATT_HEAD_DIM = 128
MOBA_BLOCK = 256
MOBA_TOPK = 3
NUM_BUCKETS = 32
MAX_DISTANCE = 128
X_HEADS = 4
X_HEAD_DIM = 128
N_GROUPS = 8
EXPERTS_PER_GROUP = 8
N_EXPERTS = N_GROUPS * EXPERTS_PER_GROUP
MOE_BLOCK = 256

VMEM_LIMIT = 56 * 1024 * 1024

_NT = (((1,), (1,)), ((), ()))


def _rms(x, g):
    return x * lax.rsqrt(jnp.mean(x * x, axis=-1, keepdims=True) + EPS) * g


def _gelu(x):
    c = math.sqrt(2.0 / math.pi)
    return x * (0.5 * (1.0 + jnp.tanh(c * (x + 0.044715 * (x * x * x)))))


def _dot(a, b):
    return jnp.dot(a, b, preferred_element_type=F32)


def _const_spec(shape):
    nd = len(shape)
    return pl.BlockSpec(shape, lambda *_: (0,) * nd, pipeline_mode=pl.Buffered(1))


def _params(sem):
    return pltpu.CompilerParams(dimension_semantics=sem, vmem_limit_bytes=VMEM_LIMIT)


def _memkv_kernel(mem_ref, g_ref, w_ref, o_ref):
    mn = _rms(mem_ref[...], g_ref[...]).astype(BF16)
    o_ref[...] = _dot(mn, w_ref[...]).astype(BF16)


def _memkv(mem2d, g, w_bf):
    n, d = mem2d.shape
    tm = 256
    return pl.pallas_call(
        _memkv_kernel,
        out_shape=jax.ShapeDtypeStruct((n, w_bf.shape[1]), BF16),
        grid=(n // tm,),
        in_specs=[pl.BlockSpec((tm, d), lambda i: (i, 0)),
                  _const_spec(g.shape), _const_spec(w_bf.shape)],
        out_specs=pl.BlockSpec((tm, w_bf.shape[1]), lambda i: (i, 0)),
        compiler_params=_params(("parallel",)),
        name="memkv",
    )(mem2d, g, w_bf)


def _inproj_kernel(x_ref, g_ref, win_ref, sg_ref, wsp_ref, bsp_ref,
                   sgu_ref, q_ref, k_ref, v_ref, km_ref, *, tm, width, scale):
    xn = _rms(x_ref[...], g_ref[...]).astype(BF16)
    u = _gelu(_dot(xn, win_ref[:, 0:width]))
    vv = _gelu(_dot(xn, win_ref[:, width:2 * width]))
    vn = _rms(vv, sg_ref[...]).astype(BF16)
    gd = width // SGU_GROUPS
    row = lax.broadcasted_iota(jnp.int32, (CHUNK, CHUNK), 0)
    col = lax.broadcasted_iota(jnp.int32, (CHUNK, CHUNK), 1)
    causal = col <= row
    wsp = [jnp.where(causal, wsp_ref[g], 0.0).astype(BF16) for g in range(SGU_GROUPS)]
    bias = bsp_ref[...]
    for c in range(tm // CHUNK):
        rs = slice(c * CHUNK, (c + 1) * CHUNK)
        mixed = jnp.concatenate(
            [_dot(wsp[g], vn[rs, g * gd:(g + 1) * gd]) for g in range(SGU_GROUPS)], axis=1)
        sgu_ref[rs, :] = (u[rs, :] * (mixed + bias)).astype(BF16)
    q_ref[...] = (_dot(xn, win_ref[:, 2 * width:3 * width]) * scale).astype(BF16)
    k = _dot(xn, win_ref[:, 3 * width:4 * width])
    k_ref[...] = k.astype(BF16)
    for j in range(tm // MOBA_BLOCK):
        km_ref[0, j:j + 1, :] = jnp.mean(k[j * MOBA_BLOCK:(j + 1) * MOBA_BLOCK, :], axis=0, keepdims=True)
    v_ref[...] = _dot(xn, win_ref[:, 4 * width:5 * width]).astype(BF16)


def _inproj(x2d, g, win_bf, sg, wsp, bsp_full, tm):
    t, d = x2d.shape
    width = win_bf.shape[1] // 5
    nkm = tm // MOBA_BLOCK
    row_spec = pl.BlockSpec((tm, width), lambda i: (i, 0))
    act = jax.ShapeDtypeStruct((t, width), BF16)
    return pl.pallas_call(
        functools.partial(_inproj_kernel, tm=tm, width=width, scale=ATT_HEAD_DIM ** -0.5 * LOG2E),
        out_shape=(act, act, act, act, jax.ShapeDtypeStruct((t // tm, nkm, width), F32)),
        grid=(t // tm,),
        in_specs=[pl.BlockSpec((tm, d), lambda i: (i, 0)),
                  _const_spec(g.shape), _const_spec(win_bf.shape), _const_spec(sg.shape),
                  _const_spec(wsp.shape), _const_spec(bsp_full.shape)],
        out_specs=(row_spec, row_spec, row_spec, row_spec,
                   pl.BlockSpec((1, nkm, width), lambda i: (i, 0, 0))),
        compiler_params=_params(("parallel",)),
        name="inproj",
    )(x2d, g, win_bf, sg, wsp, bsp_full)


def _bucket_uppers():
    d = np.arange(0, 4 * MAX_DISTANCE)
    max_exact = NUM_BUCKETS // 2
    nf = np.maximum(d, 1).astype(np.float32)
    large = max_exact + (np.log(nf / max_exact) / math.log(MAX_DISTANCE / max_exact)
                         * (NUM_BUCKETS - max_exact)).astype(np.int32)
    bucket = np.where(d < max_exact, d, np.minimum(large, NUM_BUCKETS - 1))
    return [int(d[bucket > b].min()) for b in range(NUM_BUCKETS - 1)]


def _relbias_kernel(rb_ref, own_ref, prev_ref, *, tq, uppers):
    h = pl.program_id(0)
    key = lax.broadcasted_iota(jnp.int32, (tq, tq), 0)
    qry = lax.broadcasted_iota(jnp.int32, (tq, tq), 1)

    def table(dist):
        val = jnp.full((tq, tq), rb_ref[h, NUM_BUCKETS - 1] * LOG2E, F32)
        for b in range(NUM_BUCKETS - 2, -1, -1):
            val = jnp.where(dist < uppers[b], rb_ref[h, b] * LOG2E, val)
        return val

    d = qry - key
    own_ref[0] = jnp.where(d >= 0, table(d), NEG)
    prev_ref[0] = table(d + tq)


def _relbias(rel_bias_t, tq):
    nh = rel_bias_t.shape[0]
    tile = jax.ShapeDtypeStruct((nh, tq, tq), F32)
    spec = pl.BlockSpec((1, tq, tq), lambda h: (h, 0, 0))
    return pl.pallas_call(
        functools.partial(_relbias_kernel, tq=tq, uppers=_bucket_uppers()),
        out_shape=(tile, tile),
        grid=(nh,),
        in_specs=[pl.BlockSpec(memory_space=pltpu.SMEM)],
        out_specs=(spec, spec),
        compiler_params=_params(("parallel",)),
        name="relbias",
    )(rel_bias_t)


def _attn_kernel(c31_ref, q_ref, k_ref, v_ref, km_ref, own_ref, prev_ref, o_ref,
                 vt_scr, add_scr, s_scr, acc_scr, *, tq, nb, hg):
    hd = ATT_HEAD_DIM
    heads = range(hg)
    cols = [slice(g * hd, (g + 1) * hd) for g in heads]
    c31 = [c31_ref[pl.program_id(1) * hg + g] * LOG2E for g in heads]
    for g in heads:
        for n in range(nb):
            vt_scr[g, n // 2, :, (n % 2) * tq:(n % 2 + 1) * tq] = (
                v_ref[n * tq:(n + 1) * tq, cols[g]].astype(F32).T.astype(BF16))

    def part_max(s):
        return jnp.max(s.reshape(s.shape[0] // 8, 8, tq), axis=0)

    def q_tile(cur, carry):
        rows = pl.ds(pl.multiple_of(cur * tq, tq), tq)
        q = [q_ref[rows, cols[g]] for g in heads]
        blk = lax.broadcasted_iota(jnp.int32, (nb, tq), 0)
        past = blk < cur

        negm_prev = []
        for g in heads:
            gate = lax.dot_general(km_ref[0, :, cols[g]], q[g].astype(F32), _NT,
                                   precision=lax.Precision.HIGHEST, preferred_element_type=F32)
            gv = jnp.where(past, gate, NEG)
            sel = jnp.zeros((nb, tq), F32)
            for _ in range(MOBA_TOPK):
                mx = jnp.max(gv, axis=0, keepdims=True)
                first = jnp.min(jnp.where(gv == mx, blk, nb), axis=0, keepdims=True)
                hit = blk == first
                sel = jnp.where(hit & past, 1.0, sel)
                gv = jnp.where(hit, -jnp.inf, gv)
            negm = (sel - 1.0) * (-NEG)
            add_scr[g] = jnp.where(blk < cur - 1, negm + c31[g], NEG)
            negm_prev.append(jnp.sum(jnp.where(blk == cur - 1, negm, 0.0), axis=0, keepdims=True))

        def far_pair(i, mx8):
            out = []
            for g in heads:
                kb = k_ref[pl.ds(pl.multiple_of(i * (2 * tq), 2 * tq), 2 * tq), cols[g]]
                s2 = lax.dot_general(kb, q[g], _NT, preferred_element_type=F32)
                s_lo = s2[:tq] + add_scr[g, pl.ds(2 * i, 1), :]
                s_hi = s2[tq:] + add_scr[g, pl.ds(2 * i + 1, 1), :]
                s_scr[g, i, :tq, :] = s_lo
                s_scr[g, i, tq:, :] = s_hi
                out.append(jnp.maximum(mx8[g], jnp.maximum(part_max(s_lo), part_max(s_hi))))
            return tuple(out)

        nfar = jnp.maximum(cur - 1, 0)
        mx8 = lax.fori_loop(0, (nfar + 1) // 2, far_pair,
                            tuple(jnp.full((8, tq), NEG, F32) for _ in heads))

        def one_block(n, add_tiles, mx8):
            out = []
            for g in heads:
                kb = k_ref[pl.ds(pl.multiple_of(n * tq, tq), tq), cols[g]]
                s = lax.dot_general(kb, q[g], _NT, preferred_element_type=F32) + add_tiles[g]
                s_scr[g, n // 2, pl.ds(pl.multiple_of((n % 2) * tq, tq), tq), :] = s
                out.append(jnp.maximum(mx8[g], part_max(s)))
            return tuple(out)

        mx8 = lax.cond(cur >= 1,
                       lambda a: one_block(cur - 1, [prev_ref[g] + negm_prev[g] for g in heads], a),
                       lambda a: a, mx8)
        mx8 = one_block(cur, [own_ref[g] for g in heads], mx8)

        @pl.when(cur % 2 == 0)
        def _():
            for g in heads:
                s_scr[g, cur // 2, tq:, :] = jnp.full((tq, tq), NEG, F32)

        m = [jnp.max(mx8[g], axis=0, keepdims=True) for g in heads]
        acc_scr[...] = jnp.zeros_like(acc_scr)

        def pv_pair(i, l8):
            out = []
            for g in heads:
                p = jnp.exp2(s_scr[g, i] - m[g])
                acc_scr[g] += _dot(vt_scr[g, i], p.astype(BF16))
                out.append(l8[g] + jnp.sum(p.reshape(2 * tq // 8, 8, tq), axis=0))
            return tuple(out)

        l8 = lax.fori_loop(0, cur // 2 + 1, pv_pair, tuple(jnp.zeros((8, tq), F32) for _ in heads))
        for g in heads:
            l = jnp.sum(l8[g], axis=0, keepdims=True)
            o_ref[rows, cols[g]] = (acc_scr[g] / l).T.astype(BF16)
        return carry

    lax.fori_loop(0, nb, q_tile, 0)


def _attn(c31, q, k, v, kmean, own, prev, batch, seq):
    t, width = q.shape
    tq = MOBA_BLOCK
    nb = seq // tq
    hd = ATT_HEAD_DIM
    hg = 4
    nh = width // hd
    assert nb % 2 == 0 and nh % hg == 0
    seq_spec = pl.BlockSpec((seq, hg * hd), lambda b, h: (b, h), pipeline_mode=pl.Buffered(1))
    tile_spec = pl.BlockSpec((hg, tq, tq), lambda b, h: (h, 0, 0))
    return pl.pallas_call(
        functools.partial(_attn_kernel, tq=tq, nb=nb, hg=hg),
        out_shape=jax.ShapeDtypeStruct((t, width), BF16),
        grid=(batch, nh // hg),
        in_specs=[pl.BlockSpec(memory_space=pltpu.SMEM),
                  seq_spec, seq_spec, seq_spec,
                  pl.BlockSpec((1, nb, hg * hd), lambda b, h: (b, 0, h)),
                  tile_spec, tile_spec],
        out_specs=seq_spec,
        scratch_shapes=[pltpu.VMEM((hg, nb // 2, hd, 2 * tq), BF16),
                        pltpu.VMEM((hg, nb, tq), F32),
                        pltpu.VMEM((hg, nb // 2, 2 * tq, tq), F32),
                        pltpu.VMEM((hg, hd, tq), F32)],
        compiler_params=_params(("parallel", "parallel")),
        name="attn",
    )(c31, q, k, v, kmean, own, prev)


def _post_kernel(x_ref, sgu_ref, att_ref, kvm_ref, nmix_ref, wg_ref, bg_ref, wa_ref, wb_ref,
                 wo_ref, nx_ref, wxq_ref, wxo_ref, nffn_ref, wr_ref, br_ref,
                 h2_ref, xn3_ref, route_ref, cnt_ref, carry_scr, *, tm, ts, d, xw):
    @pl.when(pl.program_id(0) == 0)
    def _():
        carry_scr[...] = jnp.zeros_like(carry_scr)

    lane = lax.broadcasted_iota(jnp.int32, (ts, LANE), 1)
    r = lax.broadcasted_iota(jnp.int32, (ts, ts), 0)
    c = lax.broadcasted_iota(jnp.int32, (ts, ts), 1)
    tri = jnp.where(c < r, 1.0, 0.0).astype(BF16)

    def sub_tile(rows):
        x = x_ref[rows, :]
        xn = _rms(x, nmix_ref[...]).astype(BF16)
        gates = jax.nn.sigmoid(_dot(xn, wg_ref[...]) + bg_ref[...])
        ya = _dot(sgu_ref[rows, :], wa_ref[...])
        yb = _dot(att_ref[rows, :], wb_ref[...])
        merged = (gates[:, :d] * ya + gates[:, d:] * yb).astype(BF16)
        h1 = x + _dot(merged, wo_ref[...])

        xn2 = _rms(h1, nx_ref[...]).astype(BF16)
        qx = (_dot(xn2, wxq_ref[...]) * (X_HEAD_DIM ** -0.5)).astype(BF16)
        outs = []
        for hh in range(X_HEADS):
            cs = slice(hh * X_HEAD_DIM, (hh + 1) * X_HEAD_DIM)
            vs = slice(xw + hh * X_HEAD_DIM, xw + (hh + 1) * X_HEAD_DIM)
            s = lax.dot_general(qx[:, cs], kvm_ref[:, cs], _NT, preferred_element_type=F32)
            p = jnp.exp(s - jnp.max(s, axis=-1, keepdims=True))
            o = _dot(p.astype(BF16), kvm_ref[:, vs]) / jnp.sum(p, axis=-1, keepdims=True)
            outs.append(o.astype(BF16))
        h2 = h1 + _dot(jnp.concatenate(outs, axis=1), wxo_ref[...])
        h2_ref[rows, :] = h2

        xn3 = _rms(h2, nffn_ref[...])
        xn3_ref[rows, :] = xn3
        x_hi = xn3.astype(BF16)
        x_lo = (xn3 - x_hi.astype(F32)).astype(BF16)
        logits = _dot(jnp.concatenate([x_hi, x_lo, x_hi], axis=1), wr_ref[...]) + br_ref[...]
        is_g = lane < N_GROUPS
        gl = jnp.where(is_g, logits, -jnp.inf)
        gmax = jnp.max(gl, axis=-1, keepdims=True)
        gsel = jnp.min(jnp.where(gl == gmax, lane, LANE), axis=-1, keepdims=True)
        pg = 1.0 / jnp.sum(jnp.exp(gl - gmax), axis=-1, keepdims=True)
        eidx = lane - N_GROUPS
        in_group = (eidx >= 0) & (eidx < N_EXPERTS) & ((eidx // EXPERTS_PER_GROUP) == gsel)
        el = jnp.where(in_group, logits, -jnp.inf)
        m1 = jnp.max(el, axis=-1, keepdims=True)
        i1 = jnp.min(jnp.where(el == m1, lane, LANE), axis=-1, keepdims=True)
        el2 = jnp.where(lane == i1, -jnp.inf, el)
        m2 = jnp.max(el2, axis=-1, keepdims=True)
        i2 = jnp.min(jnp.where(el2 == m2, lane, LANE), axis=-1, keepdims=True)
        e2 = jnp.exp(m2 - m1)
        w1 = pg / (1.0 + e2)
        w2 = pg * e2 / (1.0 + e2)
        eid1 = i1 - N_GROUPS
        eid2 = i2 - N_GROUPS

        oh1f = jnp.where(lane == eid1, 1.0, 0.0)
        oh2f = jnp.where(lane == eid2, 1.0, 0.0)
        c1 = _dot(tri, oh1f.astype(BF16))
        c2 = _dot(tri, oh2f.astype(BF16))
        tot1 = jnp.sum(oh1f, axis=0, keepdims=True)
        tot2 = jnp.sum(oh2f, axis=0, keepdims=True)
        return eid1, eid2, w1, w2, oh1f, oh2f, c1, c2, tot1, tot2

    parts = [sub_tile(slice(j * ts, (j + 1) * ts)) for j in range(tm // ts)]

    carry = carry_scr[...]
    for j, (eid1, eid2, w1, w2, oh1f, oh2f, c1, c2, tot1, tot2) in enumerate(parts):
        r1 = jnp.sum(oh1f * (c1 + carry), axis=-1, keepdims=True)
        r2 = jnp.sum(oh2f * (c2 + carry + tot1), axis=-1, keepdims=True)
        carry = carry + tot1 + tot2
        route = jnp.where(lane == 0, eid1.astype(F32), 0.0)
        route = jnp.where(lane == 1, eid2.astype(F32), route)
        route = jnp.where(lane == 2, w1, route)
        route = jnp.where(lane == 3, w2, route)
        route = jnp.where(lane == 4, r1, route)
        route = jnp.where(lane == 5, r2, route)
        route_ref[j * ts:(j + 1) * ts, :] = route
    carry_scr[...] = carry
    cnt_ref[...] = carry


def _post(x2d, sgu, att, kvm, consts, tm, seq):
    t, d = x2d.shape
    mlen = kvm.shape[0] // (t // seq)
    xw = kvm.shape[1] // 2
    tiles_per_batch = seq // tm
    row = lambda w: pl.BlockSpec((tm, w), lambda i: (i, 0))
    return pl.pallas_call(
        functools.partial(_post_kernel, tm=tm, ts=min(tm, 256), d=d, xw=xw),
        out_shape=(jax.ShapeDtypeStruct((t, d), F32), jax.ShapeDtypeStruct((t, d), F32),
                   jax.ShapeDtypeStruct((t, LANE), F32), jax.ShapeDtypeStruct((1, LANE), F32)),
        grid=(t // tm,),
        in_specs=[row(d), row(sgu.shape[1]), row(att.shape[1]),
                  pl.BlockSpec((mlen, kvm.shape[1]), lambda i: (i // tiles_per_batch, 0))]
                 + [_const_spec(c.shape) for c in consts],
        out_specs=(row(d), row(d), row(LANE), pl.BlockSpec((1, LANE), lambda i: (0, 0))),
        scratch_shapes=[pltpu.VMEM((1, LANE), F32)],
        compiler_params=_params(("arbitrary",)),
        name="post",
    )(x2d, sgu, att, kvm, *consts)


def _pad_pair_id(row, t):
    return 2 * t + (row & (2 * MOE_BLOCK - 1))


def _invert_kernel(dest_ref, inv_ref, *, tm, t, n):
    i = pl.program_id(0)

    @pl.when(i == 0)
    def _():
        def fill(k, carry):
            inv_ref[k] = _pad_pair_id(k, t)
            return carry
        lax.fori_loop(0, n, fill, 0, unroll=8)

    def put(r, carry):
        for slot in range(2):
            inv_ref[dest_ref[0, 0, slot * tm + r]] = slot * t + i * tm + r
        return carry

    lax.fori_loop(0, tm, put, 0, unroll=8)


def _invert(dest3, tm, t, n):
    return pl.pallas_call(
        functools.partial(_invert_kernel, tm=tm, t=t, n=n),
        out_shape=jax.ShapeDtypeStruct((n,), jnp.int32),
        grid=(t // tm,),
        in_specs=[pl.BlockSpec((1, 1, 2 * tm), lambda i: (i, 0, 0), memory_space=pltpu.SMEM)],
        out_specs=pl.BlockSpec(memory_space=pltpu.SMEM),
        compiler_params=_params(("arbitrary",)),
        name="invert",
    )(dest3)


def _expert_kernel(be_ref, first_ref, nxt_ref, ids_a_ref, ids_b_ref,
                   xn_hbm, wg_hbm, wu_hbm, wd_hbm, out_hbm,
                   xbuf, ybuf, wg_st, wu_st, wd_st, wg_bf, wu_bf, wd_bf, wsem, gsem, ssem, *, t):
    g = pl.program_id(0)
    rows = MOE_BLOCK

    def fetch(e):
        return (pltpu.make_async_copy(wg_hbm.at[e], wg_st, wsem.at[0]),
                pltpu.make_async_copy(wu_hbm.at[e], wu_st, wsem.at[1]),
                pltpu.make_async_copy(wd_hbm.at[e], wd_st, wsem.at[2]))

    def token_of(p):
        if t & (t - 1) == 0:
            return p & (t - 1)
        return jnp.where(p >= 2 * t, p - 2 * t, jnp.where(p >= t, p - t, p))

    def gather_row(ids_ref, off, r, slot):
        tok = token_of(ids_ref[0, 0, off + r])
        return pltpu.make_async_copy(xn_hbm.at[pl.ds(tok, 1)], xbuf.at[slot, pl.ds(r, 1)], gsem.at[slot])

    def gather_all(slot):
        return pltpu.make_async_copy(xn_hbm.at[pl.ds(0, rows)], xbuf.at[slot], gsem.at[slot])

    def scatter_all(slot):
        return pltpu.make_async_copy(ybuf.at[slot], out_hbm.at[pl.ds(0, rows)], ssem.at[slot])

    def switch_weights(m):
        @pl.when(first_ref[m] == 1)
        def _():
            for cp in fetch(be_ref[m]):
                cp.wait()
            wg_bf[...] = wg_st[...].astype(BF16)
            wu_bf[...] = wu_st[...].astype(BF16)
            wd_bf[...] = wd_st[...].astype(BF16)

            @pl.when(nxt_ref[m] >= 0)
            def _():
                for cp in fetch(nxt_ref[m]):
                    cp.start()

    def phase(slot, next_ids, next_off, prev_ids, prev_off):
        other = 1 - slot
        for r in range(rows):
            gather_row(next_ids, next_off, r, other).start()
        for r in range(rows):
            dst = prev_ids[0, 0, prev_off + r]
            pltpu.make_async_copy(ybuf.at[other, pl.ds(r, 1)], out_hbm.at[pl.ds(dst, 1)],
                                  ssem.at[other]).start(priority=r % 2)
        xb = xbuf[slot].astype(BF16)
        hid = jax.nn.silu(_dot(xb, wg_bf[...])) * _dot(xb, wu_bf[...])
        ybuf[slot] = _dot(hid.astype(BF16), wd_bf[...])

    @pl.when(g == 0)
    def _():
        for cp in fetch(be_ref[0]):
            cp.start()

        def first_block(r, carry):
            gather_row(ids_a_ref, rows, r, 0).start()
            return carry
        lax.fori_loop(0, rows, first_block, 0)
        ybuf[...] = jnp.zeros_like(ybuf)

    switch_weights(2 * g)
    gather_all(0).wait()

    @pl.when(g >= 1)
    def _():
        scatter_all(0).wait()
    phase(0, ids_b_ref, 0, ids_a_ref, 0)

    switch_weights(2 * g + 1)
    gather_all(1).wait()
    scatter_all(1).wait()
    phase(1, ids_b_ref, rows, ids_a_ref, rows)

    @pl.when(g == pl.num_programs(0) - 1)
    def _():
        gather_all(0).wait()
        scatter_all(0).wait()


def _expert(blk_expert, first, nxt, ids, xn3, w_gate, w_up, w_down):
    t, d = xn3.shape
    eh = w_gate.shape[2]
    steps = ids.shape[0] - 1
    assert blk_expert.shape[0] == 2 * steps
    hbm = pl.BlockSpec(memory_space=pl.ANY)
    idx_block = lambda f: pl.BlockSpec((1, 1, 2 * MOE_BLOCK), f, memory_space=pltpu.SMEM)
    grid_spec = pltpu.PrefetchScalarGridSpec(
        num_scalar_prefetch=3,
        grid=(steps,),
        in_specs=[idx_block(lambda g, *_: (g, 0, 0)), idx_block(lambda g, *_: (g + 1, 0, 0)),
                  hbm, hbm, hbm, hbm],
        out_specs=hbm,
        scratch_shapes=[pltpu.VMEM((2, MOE_BLOCK, d), F32), pltpu.VMEM((2, MOE_BLOCK, d), F32),
                        pltpu.VMEM((d, eh), F32), pltpu.VMEM((d, eh), F32), pltpu.VMEM((eh, d), F32),
                        pltpu.VMEM((d, eh), BF16), pltpu.VMEM((d, eh), BF16), pltpu.VMEM((eh, d), BF16),
                        pltpu.SemaphoreType.DMA((3,)), pltpu.SemaphoreType.DMA((2,)),
                        pltpu.SemaphoreType.DMA((2,))],
    )
    return pl.pallas_call(
        functools.partial(_expert_kernel, t=t),
        out_shape=jax.ShapeDtypeStruct((2 * t + 2 * MOE_BLOCK, d), F32),
        grid_spec=grid_spec,
        compiler_params=_params(("arbitrary",)),
        name="expert",
    )(blk_expert, first, nxt, ids, ids, xn3, w_gate, w_up, w_down)


def _combine_kernel(h2_ref, y0_ref, y1_ref, route_ref, g_ref, o_ref):
    route = route_ref[...]
    y = route[:, 2:3] * y0_ref[...] + route[:, 3:4] * y1_ref[...]
    o_ref[...] = _rms(h2_ref[...] + y, g_ref[...])


def _combine(h2, ypairs, route, g, tm):
    t, d = h2.shape
    assert ypairs.shape[0] % tm == 0
    return pl.pallas_call(
        _combine_kernel,
        out_shape=jax.ShapeDtypeStruct((t, d), F32),
        grid=(t // tm,),
        in_specs=[pl.BlockSpec((tm, d), lambda i: (i, 0)),
                  pl.BlockSpec((tm, d), lambda i: (i, 0)),
                  pl.BlockSpec((tm, d), lambda i: (i + t // tm, 0)),
                  pl.BlockSpec((tm, LANE), lambda i: (i, 0)),
                  _const_spec(g.shape)],
        out_specs=pl.BlockSpec((tm, d), lambda i: (i, 0)),
        compiler_params=_params(("parallel",)),
        name="combine",
    )(h2, ypairs, ypairs, route, g)


def kernel(x, mem, norm_mix, w_in, w_gate, b_gate, sgu_norm, w_spatial, b_spatial, w_branch_a,
           w_branch_b, rel_bias, w_o, norm_x, norm_mem, w_xq, w_xkv, w_xo, norm_ffn,
           w_router_group, b_router_group, w_router_expert, b_router_expert,
           w_e_gate, w_e_up, w_e_down, norm_final):
    batch, seq, d = x.shape
    assert norm_mix.shape[0] == 1, "one layer"
    assert seq % MOBA_BLOCK == 0 and d % LANE == 0
    t = batch * seq
    tm_in, tm_post, tm_moe = 512, 512, 256
    row = lambda a: a.reshape(1, -1).astype(F32)
    bf = lambda a: a.astype(BF16)

    x2d = x.reshape(t, d)
    kvm = _memkv(mem.reshape(-1, d), row(norm_mem[0]), bf(w_xkv[0]))

    width = w_in.shape[2] // 5
    bsp_full = jnp.repeat(b_spatial[0].T, width // SGU_GROUPS, axis=1)
    sgu, q, k, v, kmean = _inproj(x2d, row(norm_mix[0]), bf(w_in[0]), row(sgu_norm[0]),
                                  w_spatial[0], bsp_full, tm_in)

    rel_t = rel_bias.T.astype(F32)
    own, prev = _relbias(rel_t, MOBA_BLOCK)
    att = _attn(rel_t[:, NUM_BUCKETS - 1], q, k, v,
                kmean.reshape(batch, seq // MOBA_BLOCK, width), own, prev, batch, seq)

    lane_pad = LANE - N_GROUPS - N_EXPERTS
    w_router = jnp.concatenate([w_router_group[0], w_router_expert[0], jnp.zeros((d, lane_pad), F32)], axis=1)
    b_router = jnp.concatenate([b_router_group[0], b_router_expert[0], jnp.zeros((lane_pad,), F32)])
    wr_hi = bf(w_router)
    wr_lo = bf(w_router - wr_hi.astype(F32))
    consts = [row(norm_mix[0]), bf(w_gate[0]), row(b_gate[0]), bf(w_branch_a[0]), bf(w_branch_b[0]),
              bf(w_o[0]), row(norm_x[0]), bf(w_xq[0]), bf(w_xo[0]), row(norm_ffn[0]),
              jnp.concatenate([wr_hi, wr_hi, wr_lo], axis=0), row(b_router)]
    h2, xn3, route, cnt = _post(x2d, sgu, att, kvm, consts, tm_post, seq)

    counts = cnt[0, :N_EXPERTS].astype(jnp.int32)
    padded = (counts + MOE_BLOCK - 1) // MOE_BLOCK * MOE_BLOCK
    pad_ends = jnp.cumsum(padded)
    pad_starts = pad_ends - padded
    route_t = route[:, :8].T
    dest = [pad_starts[route_t[s].astype(jnp.int32)] + route_t[4 + s].astype(jnp.int32) for s in range(2)]
    nblk = -(-(2 * t) // MOE_BLOCK) + N_EXPERTS
    assert nblk % 2 == 0
    nphase = nblk + 2
    blk_idx = jnp.arange(nphase, dtype=jnp.int32)
    blk_expert = jnp.minimum(
        jnp.sum((pad_ends[None, :] <= blk_idx[:, None] * MOE_BLOCK).astype(jnp.int32), axis=1),
        N_EXPERTS - 1)
    nused = pad_ends[-1] // MOE_BLOCK
    prev_expert = jnp.concatenate([jnp.full((1,), -1, jnp.int32), blk_expert[:-1]])
    first = ((blk_idx < nused) & (blk_expert != prev_expert)).astype(jnp.int32)
    nxt_idx = blk_idx + (padded // MOE_BLOCK)[blk_expert]
    nxt = jnp.where((first == 1) & (nxt_idx < nused),
                    blk_expert[jnp.minimum(nxt_idx, nphase - 1)], -1).astype(jnp.int32)
    dest3 = jnp.concatenate([dst.reshape(t // tm_moe, 1, tm_moe) for dst in dest], axis=2)

    inv = _invert(dest3, tm_moe, t, nblk * MOE_BLOCK)
    dummy = lambda m: _pad_pair_id(m * MOE_BLOCK + jnp.arange(MOE_BLOCK, dtype=jnp.int32), t)
    ids = jnp.concatenate([dummy(-1), inv] + [dummy(nblk + m) for m in range(3)])
    ypairs = _expert(blk_expert, first, nxt, ids.reshape(-1, 1, 2 * MOE_BLOCK), xn3,
                     w_e_gate[0], w_e_up[0], w_e_down[0])
    out = _combine(h2, ypairs, route, row(norm_final), tm_post)
    return out.reshape(batch, seq, d)
```

```python
import functools
import math

import numpy as np
import jax
import jax.numpy as jnp
from jax import lax
from jax.experimental import pallas as pl
from jax.experimental.pallas import tpu as pltpu

F32 = jnp.float32
BF16 = jnp.bfloat16

EPS = 1e-6
NEG = -1e30
LOG2E = math.log2(math.e)
LANE = 128
ONES_ROWS = 16

SGU_GROUPS = 8
CHUNK = 128
ATT_HEADS = 8
ATT_HEAD_DIM = 128
MOBA_BLOCK = 256
MOBA_TOPK = 3
NUM_BUCKETS = 32
MAX_DISTANCE = 128
X_HEADS = 4
X_HEAD_DIM = 128
N_GROUPS = 8
EXPERTS_PER_GROUP = 8
N_EXPERTS = N_GROUPS * EXPERTS_PER_GROUP
MOE_BLOCK = 256

VMEM_LIMIT = 56 * 1024 * 1024

_NT = (((1,), (1,)), ((), ()))


def _rms(x, g):
    return x * lax.rsqrt(jnp.mean(x * x, axis=-1, keepdims=True) + EPS) * g


def _gelu(x):
    c = math.sqrt(2.0 / math.pi)
    return x * (0.5 * (1.0 + jnp.tanh(c * (x + 0.044715 * (x * x * x)))))


def _dot(a, b):
    return jnp.dot(a, b, preferred_element_type=F32)


def _const_spec(shape):
    nd = len(shape)
    return pl.BlockSpec(shape, lambda *_: (0,) * nd, pipeline_mode=pl.Buffered(1))


def _params(sem):
    return pltpu.CompilerParams(dimension_semantics=sem, vmem_limit_bytes=VMEM_LIMIT)


def _memkv_kernel(mem_ref, g_ref, w_ref, o_ref):
    mn = _rms(mem_ref[...], g_ref[...]).astype(BF16)
    o_ref[...] = _dot(mn, w_ref[...]).astype(BF16)


def _memkv(mem2d, g, w_bf):
    n, d = mem2d.shape
    tm = 256
    return pl.pallas_call(
        _memkv_kernel,
        out_shape=jax.ShapeDtypeStruct((n, w_bf.shape[1]), BF16),
        grid=(n // tm,),
        in_specs=[pl.BlockSpec((tm, d), lambda i: (i, 0)),
                  _const_spec(g.shape), _const_spec(w_bf.shape)],
        out_specs=pl.BlockSpec((tm, w_bf.shape[1]), lambda i: (i, 0)),
        compiler_params=_params(("parallel",)),
        name="memkv",
    )(mem2d, g, w_bf)


def _inproj_kernel(x_ref, g_ref, win_ref, sg_ref, wsp_ref, bsp_ref,
                   sgu_ref, q_ref, k_ref, v_ref, km_ref, *, tm, width, scale):
    xn = _rms(x_ref[...], g_ref[...]).astype(BF16)
    u = _gelu(_dot(xn, win_ref[:, 0:width]))
    vv = _gelu(_dot(xn, win_ref[:, width:2 * width]))
    vn = _rms(vv, sg_ref[...]).astype(BF16)
    gd = width // SGU_GROUPS
    row = lax.broadcasted_iota(jnp.int32, (CHUNK, CHUNK), 0)
    col = lax.broadcasted_iota(jnp.int32, (CHUNK, CHUNK), 1)
    causal = col <= row
    wsp = [jnp.where(causal, wsp_ref[g], 0.0).astype(BF16) for g in range(SGU_GROUPS)]
    bias = bsp_ref[...]
    for c in range(tm // CHUNK):
        rs = slice(c * CHUNK, (c + 1) * CHUNK)
        mixed = jnp.concatenate(
            [_dot(wsp[g], vn[rs, g * gd:(g + 1) * gd]) for g in range(SGU_GROUPS)], axis=1)
        sgu_ref[rs, :] = (u[rs, :] * (mixed + bias)).astype(BF16)
    q_ref[...] = (_dot(xn, win_ref[:, 2 * width:3 * width]) * scale).astype(BF16)
    k = _dot(xn, win_ref[:, 3 * width:4 * width])
    k_ref[...] = k.astype(BF16)
    for j in range(tm // MOBA_BLOCK):
        km_ref[0, j:j + 1, :] = jnp.mean(k[j * MOBA_BLOCK:(j + 1) * MOBA_BLOCK, :], axis=0, keepdims=True)
    v_ref[...] = _dot(xn, win_ref[:, 4 * width:5 * width]).astype(BF16)


def _inproj(x2d, g, win_bf, sg, wsp, bsp_full, tm):
    t, d = x2d.shape
    width = win_bf.shape[1] // 5
    nkm = tm // MOBA_BLOCK
    row_spec = pl.BlockSpec((tm, width), lambda i: (i, 0))
    act = jax.ShapeDtypeStruct((t, width), BF16)
    return pl.pallas_call(
        functools.partial(_inproj_kernel, tm=tm, width=width, scale=ATT_HEAD_DIM ** -0.5 * LOG2E),
        out_shape=(act, act, act, act, jax.ShapeDtypeStruct((t // tm, nkm, width), F32)),
        grid=(t // tm,),
        in_specs=[pl.BlockSpec((tm, d), lambda i: (i, 0)),
                  _const_spec(g.shape), _const_spec(win_bf.shape), _const_spec(sg.shape),
                  _const_spec(wsp.shape), _const_spec(bsp_full.shape)],
        out_specs=(row_spec, row_spec, row_spec, row_spec,
                   pl.BlockSpec((1, nkm, width), lambda i: (i, 0, 0))),
        compiler_params=_params(("parallel",)),
        name="inproj",
    )(x2d, g, win_bf, sg, wsp, bsp_full)


def _bucket_uppers():
    d = np.arange(0, 4 * MAX_DISTANCE)
    max_exact = NUM_BUCKETS // 2
    nf = np.maximum(d, 1).astype(np.float32)
    large = max_exact + (np.log(nf / max_exact) / math.log(MAX_DISTANCE / max_exact)
                         * (NUM_BUCKETS - max_exact)).astype(np.int32)
    bucket = np.where(d < max_exact, d, np.minimum(large, NUM_BUCKETS - 1))
    return [int(d[bucket > b].min()) for b in range(NUM_BUCKETS - 1)]


def _relbias_kernel(rb_ref, own_ref, prev_ref, *, tq, uppers):
    h = pl.program_id(0)
    key = lax.broadcasted_iota(jnp.int32, (tq, tq), 0)
    qry = lax.broadcasted_iota(jnp.int32, (tq, tq), 1)

    def table(dist):
        val = jnp.full((tq, tq), rb_ref[h, NUM_BUCKETS - 1] * LOG2E, F32)
        for b in range(NUM_BUCKETS - 2, -1, -1):
            val = jnp.where(dist < uppers[b], rb_ref[h, b] * LOG2E, val)
        return val

    d = qry - key
    own_ref[0] = jnp.where(d >= 0, table(d), NEG)
    prev_ref[0] = table(d + tq)


def _relbias(rel_bias_t, tq):
    nh = rel_bias_t.shape[0]
    tile = jax.ShapeDtypeStruct((nh, tq, tq), F32)
    spec = pl.BlockSpec((1, tq, tq), lambda h: (h, 0, 0))
    return pl.pallas_call(
        functools.partial(_relbias_kernel, tq=tq, uppers=_bucket_uppers()),
        out_shape=(tile, tile),
        grid=(nh,),
        in_specs=[pl.BlockSpec(memory_space=pltpu.SMEM)],
        out_specs=(spec, spec),
        compiler_params=_params(("parallel",)),
        name="relbias",
    )(rel_bias_t)


def _attn_kernel(c31_ref, q_ref, k_ref, v_ref, km_ref, own_ref, prev_ref, o_ref,
                 vt_scr, add_scr, s_scr, acc_scr, *, tq, nb, hg):
    hd = ATT_HEAD_DIM
    heads = range(hg)
    cols = [slice(g * hd, (g + 1) * hd) for g in heads]
    c31 = [c31_ref[pl.program_id(1) * hg + g] * LOG2E for g in heads]
    for g in heads:
        for n in range(nb):
            vt_scr[g, n // 2, :hd, (n % 2) * tq:(n % 2 + 1) * tq] = (
                v_ref[n * tq:(n + 1) * tq, cols[g]].astype(F32).T.astype(BF16))
        vt_scr[g, :, hd:, :] = jnp.ones((nb // 2, ONES_ROWS, 2 * tq), BF16)

    km2 = []
    for g in heads:
        km = km_ref[0, :, cols[g]]
        km_hi = km.astype(BF16)
        km2.append(jnp.concatenate([km_hi, (km - km_hi.astype(F32)).astype(BF16)], axis=1))

    def part_max(s):
        return jnp.max(s.reshape(s.shape[0] // 8, 8, tq), axis=0)

    def q_tile(cur, carry):
        rows = pl.ds(pl.multiple_of(cur * tq, tq), tq)
        q = [q_ref[rows, cols[g]] for g in heads]
        blk = lax.broadcasted_iota(jnp.int32, (nb, tq), 0)
        past = blk < cur

        negm_prev = []
        for g in heads:
            gate = lax.dot_general(km2[g], jnp.concatenate([q[g], q[g]], axis=1), _NT,
                                   preferred_element_type=F32)
            gv = jnp.where(past, gate, NEG)
            sel = jnp.zeros((nb, tq), F32)
            for _ in range(MOBA_TOPK):
                mx = jnp.max(gv, axis=0, keepdims=True)
                first = jnp.min(jnp.where(gv == mx, blk, nb), axis=0, keepdims=True)
                hit = blk == first
                sel = jnp.where(hit & past, 1.0, sel)
                gv = jnp.where(hit, -jnp.inf, gv)
            negm = (sel - 1.0) * (-NEG)
            add_scr[g] = jnp.where(blk < cur - 1, negm + c31[g], NEG)
            negm_prev.append(jnp.sum(jnp.where(blk == cur - 1, negm, 0.0), axis=0, keepdims=True))

        def far_pair(i, mx8):
            out = []
            for g in heads:
                kb = k_ref[pl.ds(pl.multiple_of(i * (2 * tq), 2 * tq), 2 * tq), cols[g]]
                s2 = lax.dot_general(kb, q[g], _NT, preferred_element_type=F32)
                s_lo = s2[:tq] + add_scr[g, pl.ds(2 * i, 1), :]
                s_hi = s2[tq:] + add_scr[g, pl.ds(2 * i + 1, 1), :]
                s_scr[g, i, :tq, :] = s_lo
                s_scr[g, i, tq:, :] = s_hi
                out.append(jnp.maximum(mx8[g], jnp.maximum(part_max(s_lo), part_max(s_hi))))
            return tuple(out)

        nfar = jnp.maximum(cur - 1, 0)
        mx8 = lax.fori_loop(0, (nfar + 1) // 2, far_pair,
                            tuple(jnp.full((8, tq), NEG, F32) for _ in heads))

        def one_block(n, add_tiles, mx8):
            out = []
            for g in heads:
                kb = k_ref[pl.ds(pl.multiple_of(n * tq, tq), tq), cols[g]]
                s = lax.dot_general(kb, q[g], _NT, preferred_element_type=F32) + add_tiles[g]
                s_scr[g, n // 2, pl.ds(pl.multiple_of((n % 2) * tq, tq), tq), :] = s
                out.append(jnp.maximum(mx8[g], part_max(s)))
            return tuple(out)

        mx8 = lax.cond(cur >= 1,
                       lambda a: one_block(cur - 1, [prev_ref[g] + negm_prev[g] for g in heads], a),
                       lambda a: a, mx8)
        mx8 = one_block(cur, [own_ref[g] for g in heads], mx8)

        @pl.when(cur % 2 == 0)
        def _():
            for g in heads:
                s_scr[g, cur // 2, tq:, :] = jnp.full((tq, tq), NEG, F32)

        m = [jnp.max(mx8[g], axis=0, keepdims=True) for g in heads]
        acc_scr[...] = jnp.zeros_like(acc_scr)

        def pv_pair(i, c):
            for g in heads:
                p = jnp.exp2((s_scr[g, i] - m[g]).astype(BF16))
                acc_scr[g] += _dot(vt_scr[g, i], p)
            return c

        lax.fori_loop(0, cur // 2 + 1, pv_pair, 0)
        for g in heads:
            acc = acc_scr[g]
            o_ref[rows, cols[g]] = (acc[:hd] / acc[hd:hd + 1]).T.astype(BF16)
        return carry

    lax.fori_loop(0, nb, q_tile, 0)


def _attn(c31, q, k, v, kmean, own, prev, batch, seq):
    t, width = q.shape
    tq = MOBA_BLOCK
    nb = seq // tq
    hd = ATT_HEAD_DIM
    hg = 4
    nh = width // hd
    assert nb % 2 == 0 and nh % hg == 0
    seq_spec = pl.BlockSpec((seq, hg * hd), lambda b, h: (b, h), pipeline_mode=pl.Buffered(1))
    tile_spec = pl.BlockSpec((hg, tq, tq), lambda b, h: (h, 0, 0))
    return pl.pallas_call(
        functools.partial(_attn_kernel, tq=tq, nb=nb, hg=hg),
        out_shape=jax.ShapeDtypeStruct((t, width), BF16),
        grid=(batch, nh // hg),
        in_specs=[pl.BlockSpec(memory_space=pltpu.SMEM),
                  seq_spec, seq_spec, seq_spec,
                  pl.BlockSpec((1, nb, hg * hd), lambda b, h: (b, 0, h)),
                  tile_spec, tile_spec],
        out_specs=seq_spec,
        scratch_shapes=[pltpu.VMEM((hg, nb // 2, hd + ONES_ROWS, 2 * tq), BF16),
                        pltpu.VMEM((hg, nb, tq), F32),
                        pltpu.VMEM((hg, nb // 2, 2 * tq, tq), F32),
                        pltpu.VMEM((hg, hd + ONES_ROWS, tq), F32)],
        compiler_params=_params(("parallel", "parallel")),
        name="attn",
    )(c31, q, k, v, kmean, own, prev)


def _post_kernel(x_ref, sgu_ref, att_ref, kvm_ref, nmix_ref, wg_ref, bg_ref, wa_ref, wb_ref,
                 wo_ref, nx_ref, wxq_ref, wxo_ref, nffn_ref, wr_ref, br_ref,
                 h2_ref, xn3_ref, route_ref, cnt_ref, carry_scr, *, tm, ts, d, xw):
    @pl.when(pl.program_id(0) == 0)
    def _():
        carry_scr[...] = jnp.zeros_like(carry_scr)

    lane = lax.broadcasted_iota(jnp.int32, (ts, LANE), 1)
    r = lax.broadcasted_iota(jnp.int32, (ts, ts), 0)
    c = lax.broadcasted_iota(jnp.int32, (ts, ts), 1)
    tri = jnp.where(c < r, 1.0, 0.0).astype(BF16)

    def sub_tile(rows):
        x = x_ref[rows, :]
        xn = _rms(x, nmix_ref[...]).astype(BF16)
        gates = jax.nn.sigmoid(_dot(xn, wg_ref[...]) + bg_ref[...])
        ya = _dot(sgu_ref[rows, :], wa_ref[...])
        yb = _dot(att_ref[rows, :], wb_ref[...])
        merged = (gates[:, :d] * ya + gates[:, d:] * yb).astype(BF16)
        h1 = x + _dot(merged, wo_ref[...])

        xn2 = _rms(h1, nx_ref[...]).astype(BF16)
        qx = (_dot(xn2, wxq_ref[...]) * (X_HEAD_DIM ** -0.5)).astype(BF16)
        outs = []
        for hh in range(X_HEADS):
            cs = slice(hh * X_HEAD_DIM, (hh + 1) * X_HEAD_DIM)
            vs = slice(xw + hh * X_HEAD_DIM, xw + (hh + 1) * X_HEAD_DIM)
            s = lax.dot_general(qx[:, cs], kvm_ref[:, cs], _NT, preferred_element_type=F32)
            p = jnp.exp(s - jnp.max(s, axis=-1, keepdims=True))
            o = _dot(p.astype(BF16), kvm_ref[:, vs]) / jnp.sum(p, axis=-1, keepdims=True)
            outs.append(o.astype(BF16))
        h2 = h1 + _dot(jnp.concatenate(outs, axis=1), wxo_ref[...])
        h2_ref[rows, :] = h2

        xn3 = _rms(h2, nffn_ref[...])
        xn3_ref[rows, :] = xn3
        x_hi = xn3.astype(BF16)
        x_lo = (xn3 - x_hi.astype(F32)).astype(BF16)
        logits = _dot(jnp.concatenate([x_hi, x_lo, x_hi], axis=1), wr_ref[...]) + br_ref[...]
        is_g = lane < N_GROUPS
        gl = jnp.where(is_g, logits, -jnp.inf)
        gmax = jnp.max(gl, axis=-1, keepdims=True)
        gsel = jnp.min(jnp.where(gl == gmax, lane, LANE), axis=-1, keepdims=True)
        pg = 1.0 / jnp.sum(jnp.exp(gl - gmax), axis=-1, keepdims=True)
        eidx = lane - N_GROUPS
        in_group = (eidx >= 0) & (eidx < N_EXPERTS) & ((eidx // EXPERTS_PER_GROUP) == gsel)
        el = jnp.where(in_group, logits, -jnp.inf)
        m1 = jnp.max(el, axis=-1, keepdims=True)
        i1 = jnp.min(jnp.where(el == m1, lane, LANE), axis=-1, keepdims=True)
        el2 = jnp.where(lane == i1, -jnp.inf, el)
        m2 = jnp.max(el2, axis=-1, keepdims=True)
        i2 = jnp.min(jnp.where(el2 == m2, lane, LANE), axis=-1, keepdims=True)
        e2 = jnp.exp(m2 - m1)
        w1 = pg / (1.0 + e2)
        w2 = pg * e2 / (1.0 + e2)
        eid1 = i1 - N_GROUPS
        eid2 = i2 - N_GROUPS

        oh1f = jnp.where(lane == eid1, 1.0, 0.0)
        oh2f = jnp.where(lane == eid2, 1.0, 0.0)
        c1 = _dot(tri, oh1f.astype(BF16))
        c2 = _dot(tri, oh2f.astype(BF16))
        tot1 = jnp.sum(oh1f, axis=0, keepdims=True)
        tot2 = jnp.sum(oh2f, axis=0, keepdims=True)
        return eid1, eid2, w1, w2, oh1f, oh2f, c1, c2, tot1, tot2

    parts = [sub_tile(slice(j * ts, (j + 1) * ts)) for j in range(tm // ts)]

    carry = carry_scr[...]
    for j, (eid1, eid2, w1, w2, oh1f, oh2f, c1, c2, tot1, tot2) in enumerate(parts):
        r1 = jnp.sum(oh1f * (c1 + carry), axis=-1, keepdims=True)
        r2 = jnp.sum(oh2f * (c2 + carry + tot1), axis=-1, keepdims=True)
        carry = carry + tot1 + tot2
        route = jnp.where(lane == 0, eid1.astype(F32), 0.0)
        route = jnp.where(lane == 1, eid2.astype(F32), route)
        route = jnp.where(lane == 2, w1, route)
        route = jnp.where(lane == 3, w2, route)
        route = jnp.where(lane == 4, r1, route)
        route = jnp.where(lane == 5, r2, route)
        route_ref[j * ts:(j + 1) * ts, :] = route
    carry_scr[...] = carry
    cnt_ref[...] = carry


def _post(x2d, sgu, att, kvm, consts, tm, seq):
    t, d = x2d.shape
    mlen = kvm.shape[0] // (t // seq)
    xw = kvm.shape[1] // 2
    tiles_per_batch = seq // tm
    row = lambda w: pl.BlockSpec((tm, w), lambda i: (i, 0))
    return pl.pallas_call(
        functools.partial(_post_kernel, tm=tm, ts=min(tm, 256), d=d, xw=xw),
        out_shape=(jax.ShapeDtypeStruct((t, d), F32), jax.ShapeDtypeStruct((t, d), F32),
                   jax.ShapeDtypeStruct((t, LANE), F32), jax.ShapeDtypeStruct((1, LANE), F32)),
        grid=(t // tm,),
        in_specs=[row(d), row(sgu.shape[1]), row(att.shape[1]),
                  pl.BlockSpec((mlen, kvm.shape[1]), lambda i: (i // tiles_per_batch, 0))]
                 + [_const_spec(c.shape) for c in consts],
        out_specs=(row(d), row(d), row(LANE), pl.BlockSpec((1, LANE), lambda i: (0, 0))),
        scratch_shapes=[pltpu.VMEM((1, LANE), F32)],
        compiler_params=_params(("arbitrary",)),
        name="post",
    )(x2d, sgu, att, kvm, *consts)


def _pad_pair_id(row, t):
    return 2 * t + (row & (2 * MOE_BLOCK - 1))


def _invert_kernel(start_ref, count_ref, end_ref, er_ref, inv_ref, *, tm, t, n):
    i = pl.program_id(0)

    def fill(lo, hi):
        def body(k, carry):
            inv_ref[k] = _pad_pair_id(k, t)
            return carry
        lax.fori_loop(lo, hi, body, 0)

    @pl.when(i == 0)
    def _():
        for e in range(N_EXPERTS):
            fill(start_ref[e] + count_ref[e], end_ref[e])
        fill(end_ref[N_EXPERTS - 1], n)

    def put(r, carry):
        for slot in range(2):
            row = start_ref[er_ref[0, 0, slot * tm + r]] + er_ref[0, 0, (2 + slot) * tm + r]
            inv_ref[row] = slot * t + i * tm + r
        return carry

    lax.fori_loop(0, tm, put, 0, unroll=8)


def _invert(pad_starts, counts, pad_ends, er3, tm, t, n):
    grid_spec = pltpu.PrefetchScalarGridSpec(
        num_scalar_prefetch=3,
        grid=(t // tm,),
        in_specs=[pl.BlockSpec((1, 1, 4 * tm), lambda i, *_: (i, 0, 0), memory_space=pltpu.SMEM)],
        out_specs=pl.BlockSpec(memory_space=pltpu.SMEM),
    )
    return pl.pallas_call(
        functools.partial(_invert_kernel, tm=tm, t=t, n=n),
        out_shape=jax.ShapeDtypeStruct((n,), jnp.int32),
        grid_spec=grid_spec,
        compiler_params=_params(("arbitrary",)),
        name="invert",
    )(pad_starts, counts, pad_ends, er3)


def _expert_kernel(be_ref, first_ref, nxt_ref, ids_a_ref, ids_b_ref,
                   xn_hbm, wg_hbm, wu_hbm, wd_hbm, out_hbm,
                   xbuf, ybuf, wg_st, wu_st, wd_st, wg_bf, wu_bf, wd_bf, wsem, gsem, ssem, *, t):
    g = pl.program_id(0)
    rows = MOE_BLOCK

    def fetch(e):
        return (pltpu.make_async_copy(wg_hbm.at[e], wg_st, wsem.at[0]),
                pltpu.make_async_copy(wu_hbm.at[e], wu_st, wsem.at[1]),
                pltpu.make_async_copy(wd_hbm.at[e], wd_st, wsem.at[2]))

    def token_of(p):
        if t & (t - 1) == 0:
            return p & (t - 1)
        return jnp.where(p >= 2 * t, p - 2 * t, jnp.where(p >= t, p - t, p))

    def gather_row(ids_ref, off, r, slot):
        tok = token_of(ids_ref[0, 0, off + r])
        return pltpu.make_async_copy(xn_hbm.at[pl.ds(tok, 1)], xbuf.at[slot, pl.ds(r, 1)], gsem.at[slot])

    def gather_all(slot):
        return pltpu.make_async_copy(xn_hbm.at[pl.ds(0, rows)], xbuf.at[slot], gsem.at[slot])

    def scatter_all(slot):
        return pltpu.make_async_copy(ybuf.at[slot], out_hbm.at[pl.ds(0, rows)], ssem.at[slot])

    def switch_weights(m):
        @pl.when(first_ref[m] == 1)
        def _():
            for cp in fetch(be_ref[m]):
                cp.wait()
            wg_bf[...] = wg_st[...].astype(BF16)
            wu_bf[...] = wu_st[...].astype(BF16)
            wd_bf[...] = wd_st[...].astype(BF16)

            @pl.when(nxt_ref[m] >= 0)
            def _():
                for cp in fetch(nxt_ref[m]):
                    cp.start()

    def phase(slot, next_ids, next_off, prev_ids, prev_off):
        other = 1 - slot
        for r in range(rows):
            gather_row(next_ids, next_off, r, other).start()
        for r in range(rows):
            dst = prev_ids[0, 0, prev_off + r]
            pltpu.make_async_copy(ybuf.at[other, pl.ds(r, 1)], out_hbm.at[pl.ds(dst, 1)],
                                  ssem.at[other]).start(priority=r % 2)
        xb = xbuf[slot].astype(BF16)
        hid = jax.nn.silu(_dot(xb, wg_bf[...])) * _dot(xb, wu_bf[...])
        ybuf[slot] = _dot(hid.astype(BF16), wd_bf[...])

    @pl.when(g == 0)
    def _():
        for cp in fetch(be_ref[0]):
            cp.start()

        def first_block(r, carry):
            gather_row(ids_a_ref, rows, r, 0).start()
            return carry
        lax.fori_loop(0, rows, first_block, 0)
        ybuf[...] = jnp.zeros_like(ybuf)

    switch_weights(2 * g)
    gather_all(0).wait()

    @pl.when(g >= 1)
    def _():
        scatter_all(0).wait()
    phase(0, ids_b_ref, 0, ids_a_ref, 0)

    switch_weights(2 * g + 1)
    gather_all(1).wait()
    scatter_all(1).wait()
    phase(1, ids_b_ref, rows, ids_a_ref, rows)

    @pl.when(g == pl.num_programs(0) - 1)
    def _():
        gather_all(0).wait()
        scatter_all(0).wait()


def _expert(blk_expert, first, nxt, ids, xn3, w_gate, w_up, w_down):
    t, d = xn3.shape
    eh = w_gate.shape[2]
    steps = ids.shape[0] - 1
    assert blk_expert.shape[0] == 2 * steps
    hbm = pl.BlockSpec(memory_space=pl.ANY)
    idx_block = lambda f: pl.BlockSpec((1, 1, 2 * MOE_BLOCK), f, memory_space=pltpu.SMEM)
    grid_spec = pltpu.PrefetchScalarGridSpec(
        num_scalar_prefetch=3,
        grid=(steps,),
        in_specs=[idx_block(lambda g, *_: (g, 0, 0)), idx_block(lambda g, *_: (g + 1, 0, 0)),
                  hbm, hbm, hbm, hbm],
        out_specs=hbm,
        scratch_shapes=[pltpu.VMEM((2, MOE_BLOCK, d), F32), pltpu.VMEM((2, MOE_BLOCK, d), F32),
                        pltpu.VMEM((d, eh), F32), pltpu.VMEM((d, eh), F32), pltpu.VMEM((eh, d), F32),
                        pltpu.VMEM((d, eh), BF16), pltpu.VMEM((d, eh), BF16), pltpu.VMEM((eh, d), BF16),
                        pltpu.SemaphoreType.DMA((3,)), pltpu.SemaphoreType.DMA((2,)),
                        pltpu.SemaphoreType.DMA((2,))],
    )
    return pl.pallas_call(
        functools.partial(_expert_kernel, t=t),
        out_shape=jax.ShapeDtypeStruct((2 * t + 2 * MOE_BLOCK, d), F32),
        grid_spec=grid_spec,
        compiler_params=_params(("arbitrary",)),
        name="expert",
    )(blk_expert, first, nxt, ids, ids, xn3, w_gate, w_up, w_down)


def _combine_kernel(h2_ref, y0_ref, y1_ref, route_ref, g_ref, o_ref):
    route = route_ref[...]
    y = route[:, 2:3] * y0_ref[...] + route[:, 3:4] * y1_ref[...]
    o_ref[...] = _rms(h2_ref[...] + y, g_ref[...])


def _combine(h2, ypairs, route, g, tm):
    t, d = h2.shape
    assert ypairs.shape[0] % tm == 0
    return pl.pallas_call(
        _combine_kernel,
        out_shape=jax.ShapeDtypeStruct((t, d), F32),
        grid=(t // tm,),
        in_specs=[pl.BlockSpec((tm, d), lambda i: (i, 0)),
                  pl.BlockSpec((tm, d), lambda i: (i, 0)),
                  pl.BlockSpec((tm, d), lambda i: (i + t // tm, 0)),
                  pl.BlockSpec((tm, LANE), lambda i: (i, 0)),
                  _const_spec(g.shape)],
        out_specs=pl.BlockSpec((tm, d), lambda i: (i, 0)),
        compiler_params=_params(("parallel",)),
        name="combine",
    )(h2, ypairs, ypairs, route, g)


def kernel(x, mem, norm_mix, w_in, w_gate, b_gate, sgu_norm, w_spatial, b_spatial, w_branch_a,
           w_branch_b, rel_bias, w_o, norm_x, norm_mem, w_xq, w_xkv, w_xo, norm_ffn,
           w_router_group, b_router_group, w_router_expert, b_router_expert,
           w_e_gate, w_e_up, w_e_down, norm_final):
    batch, seq, d = x.shape
    assert norm_mix.shape[0] == 1, "one layer"
    assert seq % MOBA_BLOCK == 0 and d % LANE == 0
    t = batch * seq
    tm_in, tm_post, tm_moe = 512, 512, 256
    row = lambda a: a.reshape(1, -1).astype(F32)
    bf = lambda a: a.astype(BF16)

    x2d = x.reshape(t, d)
    kvm = _memkv(mem.reshape(-1, d), row(norm_mem[0]), bf(w_xkv[0]))

    width = w_in.shape[2] // 5
    bsp_full = jnp.repeat(b_spatial[0].T, width // SGU_GROUPS, axis=1)
    sgu, q, k, v, kmean = _inproj(x2d, row(norm_mix[0]), bf(w_in[0]), row(sgu_norm[0]),
                                  w_spatial[0], bsp_full, tm_in)

    rel_t = rel_bias.T.astype(F32)
    own, prev = _relbias(rel_t, MOBA_BLOCK)
    att = _attn(rel_t[:, NUM_BUCKETS - 1], q, k, v,
                kmean.reshape(batch, seq // MOBA_BLOCK, width), own, prev, batch, seq)

    lane_pad = LANE - N_GROUPS - N_EXPERTS
    w_router = jnp.concatenate([w_router_group[0], w_router_expert[0], jnp.zeros((d, lane_pad), F32)], axis=1)
    b_router = jnp.concatenate([b_router_group[0], b_router_expert[0], jnp.zeros((lane_pad,), F32)])
    wr_hi = bf(w_router)
    wr_lo = bf(w_router - wr_hi.astype(F32))
    consts = [row(norm_mix[0]), bf(w_gate[0]), row(b_gate[0]), bf(w_branch_a[0]), bf(w_branch_b[0]),
              bf(w_o[0]), row(norm_x[0]), bf(w_xq[0]), bf(w_xo[0]), row(norm_ffn[0]),
              jnp.concatenate([wr_hi, wr_hi, wr_lo], axis=0), row(b_router)]
    h2, xn3, route, cnt = _post(x2d, sgu, att, kvm, consts, tm_post, seq)

    counts = cnt[0, :N_EXPERTS].astype(jnp.int32)
    padded = (counts + MOE_BLOCK - 1) // MOE_BLOCK * MOE_BLOCK
    pad_ends = jnp.cumsum(padded)
    pad_starts = pad_ends - padded
    nblk = -(-(2 * t) // MOE_BLOCK) + N_EXPERTS
    assert nblk % 2 == 0
    nphase = nblk + 2
    blk_idx = jnp.arange(nphase, dtype=jnp.int32)
    blk_expert = jnp.minimum(
        jnp.sum((pad_ends[None, :] <= blk_idx[:, None] * MOE_BLOCK).astype(jnp.int32), axis=1),
        N_EXPERTS - 1)
    nused = pad_ends[-1] // MOE_BLOCK
    prev_expert = jnp.concatenate([jnp.full((1,), -1, jnp.int32), blk_expert[:-1]])
    first = ((blk_idx < nused) & (blk_expert != prev_expert)).astype(jnp.int32)
    nxt_idx = blk_idx + (padded // MOE_BLOCK)[blk_expert]
    nxt = jnp.where((first == 1) & (nxt_idx < nused),
                    blk_expert[jnp.minimum(nxt_idx, nphase - 1)], -1).astype(jnp.int32)

    er = jnp.concatenate([route[:, 0:2], route[:, 4:6]], axis=1).T.astype(jnp.int32)
    er3 = er.reshape(4, t // tm_moe, tm_moe).transpose(1, 0, 2).reshape(t // tm_moe, 1, 4 * tm_moe)
    inv = _invert(pad_starts, counts, pad_ends, er3, tm_moe, t, nblk * MOE_BLOCK)
    dummy = lambda m: _pad_pair_id(m * MOE_BLOCK + jnp.arange(MOE_BLOCK, dtype=jnp.int32), t)
    ids = jnp.concatenate([dummy(-1), inv] + [dummy(nblk + m) for m in range(3)])
    ypairs = _expert(blk_expert, first, nxt, ids.reshape(-1, 1, 2 * MOE_BLOCK), xn3,
                     w_e_gate[0], w_e_up[0], w_e_down[0])
    out = _combine(h2, ypairs, route, row(norm_final), tm_post)
    return out.reshape(batch, seq, d)
```

```python
import functools
import math

import numpy as np
import jax
import jax.numpy as jnp
from jax import lax
from jax.experimental import pallas as pl
from jax.experimental.pallas import tpu as pltpu

F32 = jnp.float32
BF16 = jnp.bfloat16

EPS = 1e-6
NEG = -1e30
LOG2E = math.log2(math.e)
LANE = 128
ONES_ROWS = 16
RANK_BITS = 20

SGU_GROUPS = 8
CHUNK = 128
ATT_HEADS = 8
ATT_HEAD_DIM = 128
MOBA_BLOCK = 256
MOBA_TOPK = 3
NUM_BUCKETS = 32
MAX_DISTANCE = 128
X_HEADS = 4
X_HEAD_DIM = 128
N_GROUPS = 8
EXPERTS_PER_GROUP = 8
N_EXPERTS = N_GROUPS * EXPERTS_PER_GROUP
MOE_BLOCK = 256

VMEM_LIMIT = 56 * 1024 * 1024

_NT = (((1,), (1,)), ((), ()))


def _rms(x, g):
    return x * lax.rsqrt(jnp.mean(x * x, axis=-1, keepdims=True) + EPS) * g


def _gelu(x):
    c = math.sqrt(2.0 / math.pi)
    return x * (0.5 * (1.0 + jnp.tanh(c * (x + 0.044715 * (x * x * x)))))


def _dot(a, b):
    return jnp.dot(a, b, preferred_element_type=F32)


def _store_tile_rows(ref, first_row, val, lead=()):
    n, d = val.shape
    k = d // LANE
    for j in range(k):
        ref[lead + (pl.ds(first_row * k + j, n, stride=k), slice(None))] = val[:, j * LANE:(j + 1) * LANE]


def _load_tile_rows(ref, n, k, lead=()):
    return jnp.concatenate([ref[lead + (pl.ds(j, n, stride=k), slice(None))] for j in range(k)], axis=1)


def _const_spec(shape):
    nd = len(shape)
    return pl.BlockSpec(shape, lambda *_: (0,) * nd, pipeline_mode=pl.Buffered(1))


def _params(sem):
    return pltpu.CompilerParams(dimension_semantics=sem, vmem_limit_bytes=VMEM_LIMIT)


def _memkv_kernel(mem_ref, g_ref, w_ref, o_ref):
    mn = _rms(mem_ref[...], g_ref[...]).astype(BF16)
    o_ref[...] = _dot(mn, w_ref[...]).astype(BF16)


def _memkv(mem2d, g, w_bf):
    n, d = mem2d.shape
    tm = 256
    return pl.pallas_call(
        _memkv_kernel,
        out_shape=jax.ShapeDtypeStruct((n, w_bf.shape[1]), BF16),
        grid=(n // tm,),
        in_specs=[pl.BlockSpec((tm, d), lambda i: (i, 0)),
                  _const_spec(g.shape), _const_spec(w_bf.shape)],
        out_specs=pl.BlockSpec((tm, w_bf.shape[1]), lambda i: (i, 0)),
        compiler_params=_params(("parallel",)),
        name="memkv",
    )(mem2d, g, w_bf)


def _inproj_kernel(x_ref, g_ref, win_ref, sg_ref, wsp_ref, bsp_ref,
                   sgu_ref, q_ref, k_ref, v_ref, km_ref, *, tm, width, scale):
    xn = _rms(x_ref[...], g_ref[...]).astype(BF16)
    u = _gelu(_dot(xn, win_ref[:, 0:width]))
    vv = _gelu(_dot(xn, win_ref[:, width:2 * width]))
    vn = _rms(vv, sg_ref[...]).astype(BF16)
    gd = width // SGU_GROUPS
    row = lax.broadcasted_iota(jnp.int32, (CHUNK, CHUNK), 0)
    col = lax.broadcasted_iota(jnp.int32, (CHUNK, CHUNK), 1)
    causal = col <= row
    wsp = [jnp.where(causal, wsp_ref[g], 0.0).astype(BF16) for g in range(SGU_GROUPS)]
    bias = bsp_ref[...]
    for c in range(tm // CHUNK):
        rs = slice(c * CHUNK, (c + 1) * CHUNK)
        mixed = jnp.concatenate(
            [_dot(wsp[g], vn[rs, g * gd:(g + 1) * gd]) for g in range(SGU_GROUPS)], axis=1)
        sgu_ref[rs, :] = (u[rs, :] * (mixed + bias)).astype(BF16)
    q_ref[...] = (_dot(xn, win_ref[:, 2 * width:3 * width]) * scale).astype(BF16)
    k = _dot(xn, win_ref[:, 3 * width:4 * width])
    k_ref[...] = k.astype(BF16)
    for j in range(tm // MOBA_BLOCK):
        km_ref[0, j:j + 1, :] = jnp.mean(k[j * MOBA_BLOCK:(j + 1) * MOBA_BLOCK, :], axis=0, keepdims=True)
    v_ref[...] = _dot(xn, win_ref[:, 4 * width:5 * width]).astype(BF16)


def _inproj(x2d, g, win_bf, sg, wsp, bsp_full, tm):
    t, d = x2d.shape
    width = win_bf.shape[1] // 5
    nkm = tm // MOBA_BLOCK
    row_spec = pl.BlockSpec((tm, width), lambda i: (i, 0))
    act = jax.ShapeDtypeStruct((t, width), BF16)
    return pl.pallas_call(
        functools.partial(_inproj_kernel, tm=tm, width=width, scale=ATT_HEAD_DIM ** -0.5 * LOG2E),
        out_shape=(act, act, act, act, jax.ShapeDtypeStruct((t // tm, nkm, width), F32)),
        grid=(t // tm,),
        in_specs=[pl.BlockSpec((tm, d), lambda i: (i, 0)),
                  _const_spec(g.shape), _const_spec(win_bf.shape), _const_spec(sg.shape),
                  _const_spec(wsp.shape), _const_spec(bsp_full.shape)],
        out_specs=(row_spec, row_spec, row_spec, row_spec,
                   pl.BlockSpec((1, nkm, width), lambda i: (i, 0, 0))),
        compiler_params=_params(("parallel",)),
        name="inproj",
    )(x2d, g, win_bf, sg, wsp, bsp_full)


def _bucket_uppers():
    d = np.arange(0, 4 * MAX_DISTANCE)
    max_exact = NUM_BUCKETS // 2
    nf = np.maximum(d, 1).astype(np.float32)
    large = max_exact + (np.log(nf / max_exact) / math.log(MAX_DISTANCE / max_exact)
                         * (NUM_BUCKETS - max_exact)).astype(np.int32)
    bucket = np.where(d < max_exact, d, np.minimum(large, NUM_BUCKETS - 1))
    return [int(d[bucket > b].min()) for b in range(NUM_BUCKETS - 1)]


def _relbias_kernel(rb_ref, own_ref, prev_ref, *, tq, uppers):
    h = pl.program_id(0)
    key = lax.broadcasted_iota(jnp.int32, (tq, tq), 0)
    qry = lax.broadcasted_iota(jnp.int32, (tq, tq), 1)

    def table(dist):
        val = jnp.full((tq, tq), rb_ref[h, NUM_BUCKETS - 1] * LOG2E, F32)
        for b in range(NUM_BUCKETS - 2, -1, -1):
            val = jnp.where(dist < uppers[b], rb_ref[h, b] * LOG2E, val)
        return val

    d = qry - key
    own_ref[0] = jnp.where(d >= 0, table(d), NEG)
    prev_ref[0] = table(d + tq)


def _relbias(rel_bias_t, tq):
    nh = rel_bias_t.shape[0]
    tile = jax.ShapeDtypeStruct((nh, tq, tq), F32)
    spec = pl.BlockSpec((1, tq, tq), lambda h: (h, 0, 0))
    return pl.pallas_call(
        functools.partial(_relbias_kernel, tq=tq, uppers=_bucket_uppers()),
        out_shape=(tile, tile),
        grid=(nh,),
        in_specs=[pl.BlockSpec(memory_space=pltpu.SMEM)],
        out_specs=(spec, spec),
        compiler_params=_params(("parallel",)),
        name="relbias",
    )(rel_bias_t)


def _attn_kernel(c31_ref, q_ref, k_ref, v_ref, km_ref, own_ref, prev_ref, o_ref,
                 vt_scr, add_scr, s_scr, acc_scr, *, tq, nb, hg):
    hd = ATT_HEAD_DIM
    heads = range(hg)
    cols = [slice(g * hd, (g + 1) * hd) for g in heads]
    c31 = [c31_ref[pl.program_id(1) * hg + g] * LOG2E for g in heads]
    for g in heads:
        for n in range(nb):
            vt_scr[g, n // 2, :hd, (n % 2) * tq:(n % 2 + 1) * tq] = (
                v_ref[n * tq:(n + 1) * tq, cols[g]].astype(F32).T.astype(BF16))
        vt_scr[g, :, hd:, :] = jnp.ones((nb // 2, ONES_ROWS, 2 * tq), BF16)

    km2 = []
    for g in heads:
        km = km_ref[0, :, cols[g]]
        km_hi = km.astype(BF16)
        km2.append(jnp.concatenate([km_hi, (km - km_hi.astype(F32)).astype(BF16)], axis=1))

    def part_max(s):
        return jnp.max(s.reshape(s.shape[0] // 8, 8, tq), axis=0)

    def q_tile(cur, carry):
        rows = pl.ds(pl.multiple_of(cur * tq, tq), tq)
        q = [q_ref[rows, cols[g]] for g in heads]
        blk = lax.broadcasted_iota(jnp.int32, (nb, tq), 0)
        past = blk < cur

        negm_prev = []
        for g in heads:
            gate = lax.dot_general(km2[g], jnp.concatenate([q[g], q[g]], axis=1), _NT,
                                   preferred_element_type=F32)
            gv = jnp.where(past, gate, NEG)
            sel = jnp.zeros((nb, tq), F32)
            for _ in range(MOBA_TOPK):
                mx = jnp.max(gv, axis=0, keepdims=True)
                first = jnp.min(jnp.where(gv == mx, blk, nb), axis=0, keepdims=True)
                hit = blk == first
                sel = jnp.where(hit & past, 1.0, sel)
                gv = jnp.where(hit, -jnp.inf, gv)
            negm = (sel - 1.0) * (-NEG)
            add_scr[g] = jnp.where(blk < cur - 1, negm + c31[g], NEG)
            negm_prev.append(jnp.sum(jnp.where(blk == cur - 1, negm, 0.0), axis=0, keepdims=True))

        def far_pair(i, mx8):
            out = []
            for g in heads:
                kb = k_ref[pl.ds(pl.multiple_of(i * (2 * tq), 2 * tq), 2 * tq), cols[g]]
                s2 = lax.dot_general(kb, q[g], _NT, preferred_element_type=F32)
                s_lo = s2[:tq] + add_scr[g, pl.ds(2 * i, 1), :]
                s_hi = s2[tq:] + add_scr[g, pl.ds(2 * i + 1, 1), :]
                s_scr[g, i, :tq, :] = s_lo
                s_scr[g, i, tq:, :] = s_hi
                out.append(jnp.maximum(mx8[g], jnp.maximum(part_max(s_lo), part_max(s_hi))))
            return tuple(out)

        nfar = jnp.maximum(cur - 1, 0)
        mx8 = lax.fori_loop(0, (nfar + 1) // 2, far_pair,
                            tuple(jnp.full((8, tq), NEG, F32) for _ in heads))

        def one_block(n, add_tiles, mx8):
            out = []
            for g in heads:
                kb = k_ref[pl.ds(pl.multiple_of(n * tq, tq), tq), cols[g]]
                s = lax.dot_general(kb, q[g], _NT, preferred_element_type=F32) + add_tiles[g]
                s_scr[g, n // 2, pl.ds(pl.multiple_of((n % 2) * tq, tq), tq), :] = s
                out.append(jnp.maximum(mx8[g], part_max(s)))
            return tuple(out)

        mx8 = lax.cond(cur >= 1,
                       lambda a: one_block(cur - 1, [prev_ref[g] + negm_prev[g] for g in heads], a),
                       lambda a: a, mx8)
        mx8 = one_block(cur, [own_ref[g] for g in heads], mx8)

        @pl.when(cur % 2 == 0)
        def _():
            for g in heads:
                s_scr[g, cur // 2, tq:, :] = jnp.full((tq, tq), NEG, F32)

        m = [jnp.max(mx8[g], axis=0, keepdims=True) for g in heads]
        acc_scr[...] = jnp.zeros_like(acc_scr)

        def pv_pair(i, c):
            for g in heads:
                p = jnp.exp2((s_scr[g, i] - m[g]).astype(BF16))
                acc_scr[g] += _dot(vt_scr[g, i], p)
            return c

        lax.fori_loop(0, cur // 2 + 1, pv_pair, 0)
        for g in heads:
            acc = acc_scr[g]
            o_ref[rows, cols[g]] = (acc[:hd] / acc[hd:hd + 1]).T.astype(BF16)
        return carry

    lax.fori_loop(0, nb, q_tile, 0)


def _attn(c31, q, k, v, kmean, own, prev, batch, seq):
    t, width = q.shape
    tq = MOBA_BLOCK
    nb = seq // tq
    hd = ATT_HEAD_DIM
    hg = 4
    nh = width // hd
    assert nb % 2 == 0 and nh % hg == 0
    seq_spec = pl.BlockSpec((seq, hg * hd), lambda b, h: (b, h), pipeline_mode=pl.Buffered(1))
    tile_spec = pl.BlockSpec((hg, tq, tq), lambda b, h: (h, 0, 0))
    return pl.pallas_call(
        functools.partial(_attn_kernel, tq=tq, nb=nb, hg=hg),
        out_shape=jax.ShapeDtypeStruct((t, width), BF16),
        grid=(batch, nh // hg),
        in_specs=[pl.BlockSpec(memory_space=pltpu.SMEM),
                  seq_spec, seq_spec, seq_spec,
                  pl.BlockSpec((1, nb, hg * hd), lambda b, h: (b, 0, h)),
                  tile_spec, tile_spec],
        out_specs=seq_spec,
        scratch_shapes=[pltpu.VMEM((hg, nb // 2, hd + ONES_ROWS, 2 * tq), BF16),
                        pltpu.VMEM((hg, nb, tq), F32),
                        pltpu.VMEM((hg, nb // 2, 2 * tq, tq), F32),
                        pltpu.VMEM((hg, hd + ONES_ROWS, tq), F32)],
        compiler_params=_params(("parallel", "parallel")),
        name="attn",
    )(c31, q, k, v, kmean, own, prev)


def _post_kernel(x_ref, sgu_ref, att_ref, kvm_ref, nmix_ref, wg_ref, bg_ref, wa_ref, wb_ref,
                 wo_ref, nx_ref, wxq_ref, wxo_ref, nffn_ref, wr_ref, br_ref,
                 h2_ref, xn3_ref, route_ref, cnt_ref, carry_scr, *, tm, ts, d, xw):
    @pl.when(pl.program_id(0) == 0)
    def _():
        carry_scr[...] = jnp.zeros_like(carry_scr)

    lane = lax.broadcasted_iota(jnp.int32, (ts, LANE), 1)
    r = lax.broadcasted_iota(jnp.int32, (ts, ts), 0)
    c = lax.broadcasted_iota(jnp.int32, (ts, ts), 1)
    tri = jnp.where(c < r, 1.0, 0.0).astype(BF16)

    def sub_tile(rows):
        x = x_ref[rows, :]
        xn = _rms(x, nmix_ref[...]).astype(BF16)
        gates = jax.nn.sigmoid(_dot(xn, wg_ref[...]) + bg_ref[...])
        ya = _dot(sgu_ref[rows, :], wa_ref[...])
        yb = _dot(att_ref[rows, :], wb_ref[...])
        merged = (gates[:, :d] * ya + gates[:, d:] * yb).astype(BF16)
        h1 = x + _dot(merged, wo_ref[...])

        xn2 = _rms(h1, nx_ref[...]).astype(BF16)
        qx = (_dot(xn2, wxq_ref[...]) * (X_HEAD_DIM ** -0.5)).astype(BF16)
        outs = []
        for hh in range(X_HEADS):
            cs = slice(hh * X_HEAD_DIM, (hh + 1) * X_HEAD_DIM)
            vs = slice(xw + hh * X_HEAD_DIM, xw + (hh + 1) * X_HEAD_DIM)
            s = lax.dot_general(qx[:, cs], kvm_ref[:, cs], _NT, preferred_element_type=F32)
            p = jnp.exp(s - jnp.max(s, axis=-1, keepdims=True))
            o = _dot(p.astype(BF16), kvm_ref[:, vs]) / jnp.sum(p, axis=-1, keepdims=True)
            outs.append(o.astype(BF16))
        h2 = h1 + _dot(jnp.concatenate(outs, axis=1), wxo_ref[...])
        h2_ref[rows, :] = h2

        xn3 = _rms(h2, nffn_ref[...])
        _store_tile_rows(xn3_ref, rows.start, xn3)
        x_hi = xn3.astype(BF16)
        x_lo = (xn3 - x_hi.astype(F32)).astype(BF16)
        logits = _dot(jnp.concatenate([x_hi, x_lo, x_hi], axis=1), wr_ref[...]) + br_ref[...]
        is_g = lane < N_GROUPS
        gl = jnp.where(is_g, logits, -jnp.inf)
        gmax = jnp.max(gl, axis=-1, keepdims=True)
        gsel = jnp.min(jnp.where(gl == gmax, lane, LANE), axis=-1, keepdims=True)
        pg = 1.0 / jnp.sum(jnp.exp(gl - gmax), axis=-1, keepdims=True)
        eidx = lane - N_GROUPS
        in_group = (eidx >= 0) & (eidx < N_EXPERTS) & ((eidx // EXPERTS_PER_GROUP) == gsel)
        el = jnp.where(in_group, logits, -jnp.inf)
        m1 = jnp.max(el, axis=-1, keepdims=True)
        i1 = jnp.min(jnp.where(el == m1, lane, LANE), axis=-1, keepdims=True)
        el2 = jnp.where(lane == i1, -jnp.inf, el)
        m2 = jnp.max(el2, axis=-1, keepdims=True)
        i2 = jnp.min(jnp.where(el2 == m2, lane, LANE), axis=-1, keepdims=True)
        e2 = jnp.exp(m2 - m1)
        w1 = pg / (1.0 + e2)
        w2 = pg * e2 / (1.0 + e2)
        eid1 = i1 - N_GROUPS
        eid2 = i2 - N_GROUPS

        oh1f = jnp.where(lane == eid1, 1.0, 0.0)
        oh2f = jnp.where(lane == eid2, 1.0, 0.0)
        c1 = _dot(tri, oh1f.astype(BF16))
        c2 = _dot(tri, oh2f.astype(BF16))
        tot1 = jnp.sum(oh1f, axis=0, keepdims=True)
        tot2 = jnp.sum(oh2f, axis=0, keepdims=True)
        return eid1, eid2, w1, w2, oh1f, oh2f, c1, c2, tot1, tot2

    parts = [sub_tile(slice(j * ts, (j + 1) * ts)) for j in range(tm // ts)]

    carry = carry_scr[...]
    for j, (eid1, eid2, w1, w2, oh1f, oh2f, c1, c2, tot1, tot2) in enumerate(parts):
        r1 = jnp.sum(oh1f * (c1 + carry), axis=-1, keepdims=True)
        r2 = jnp.sum(oh2f * (c2 + carry + tot1), axis=-1, keepdims=True)
        carry = carry + tot1 + tot2
        route = jnp.where(lane == 0, eid1.astype(F32), 0.0)
        route = jnp.where(lane == 1, eid2.astype(F32), route)
        route = jnp.where(lane == 2, w1, route)
        route = jnp.where(lane == 3, w2, route)
        route = jnp.where(lane == 4, r1, route)
        route = jnp.where(lane == 5, r2, route)
        route_ref[j * ts:(j + 1) * ts, :] = route
    carry_scr[...] = carry
    cnt_ref[...] = carry


def _post(x2d, sgu, att, kvm, consts, tm, seq):
    t, d = x2d.shape
    mlen = kvm.shape[0] // (t // seq)
    xw = kvm.shape[1] // 2
    tiles_per_batch = seq // tm
    row = lambda w: pl.BlockSpec((tm, w), lambda i: (i, 0))
    return pl.pallas_call(
        functools.partial(_post_kernel, tm=tm, ts=min(tm, 256), d=d, xw=xw),
        out_shape=(jax.ShapeDtypeStruct((t, d), F32), jax.ShapeDtypeStruct((t * d // LANE, LANE), F32),
                   jax.ShapeDtypeStruct((t, LANE), F32), jax.ShapeDtypeStruct((1, LANE), F32)),
        grid=(t // tm,),
        in_specs=[row(d), row(sgu.shape[1]), row(att.shape[1]),
                  pl.BlockSpec((mlen, kvm.shape[1]), lambda i: (i // tiles_per_batch, 0))]
                 + [_const_spec(c.shape) for c in consts],
        out_specs=(row(d), pl.BlockSpec((tm * d // LANE, LANE), lambda i: (i, 0)), row(LANE),
                   pl.BlockSpec((1, LANE), lambda i: (0, 0))),
        scratch_shapes=[pltpu.VMEM((1, LANE), F32)],
        compiler_params=_params(("arbitrary",)),
        name="post",
    )(x2d, sgu, att, kvm, *consts)


def _pad_pair_id(row, t):
    return 2 * t + (row & (2 * MOE_BLOCK - 1))


def _invert_kernel(start_ref, count_ref, end_ref, er_ref, inv_ref, *, tm, t, n):
    i = pl.program_id(0)

    def fill(lo, hi):
        def body(k, carry):
            inv_ref[k] = _pad_pair_id(k, t)
            return carry
        lax.fori_loop(lo, hi, body, 0)

    @pl.when(i == 0)
    def _():
        for e in range(N_EXPERTS):
            fill(start_ref[e] + count_ref[e], end_ref[e])
        fill(end_ref[N_EXPERTS - 1], n)

    def put(r, carry):
        for slot in range(2):
            code = er_ref[0, 0, slot * tm + r]
            row = start_ref[code >> RANK_BITS] + (code & ((1 << RANK_BITS) - 1))
            inv_ref[row] = slot * t + i * tm + r
        return carry

    lax.fori_loop(0, tm, put, 0, unroll=8)


def _invert(pad_starts, counts, pad_ends, er3, tm, t, n):
    grid_spec = pltpu.PrefetchScalarGridSpec(
        num_scalar_prefetch=3,
        grid=(t // tm,),
        in_specs=[pl.BlockSpec((1, 1, 2 * tm), lambda i, *_: (i, 0, 0), memory_space=pltpu.SMEM)],
        out_specs=pl.BlockSpec(memory_space=pltpu.SMEM),
    )
    return pl.pallas_call(
        functools.partial(_invert_kernel, tm=tm, t=t, n=n),
        out_shape=jax.ShapeDtypeStruct((n,), jnp.int32),
        grid_spec=grid_spec,
        compiler_params=_params(("arbitrary",)),
        name="invert",
    )(pad_starts, counts, pad_ends, er3)


def _expert_kernel(be_ref, first_ref, nxt_ref, ids_a_ref, ids_b_ref,
                   xn_hbm, wg_hbm, wu_hbm, wd_hbm, out_hbm,
                   xbuf, ybuf, wg_st, wu_st, wd_st, wg_bf, wu_bf, wd_bf, wsem, gsem, ssem, *, t):
    g = pl.program_id(0)
    rows = MOE_BLOCK
    k = xbuf.shape[1] // rows

    def fetch(e):
        return (pltpu.make_async_copy(wg_hbm.at[e], wg_st, wsem.at[0]),
                pltpu.make_async_copy(wu_hbm.at[e], wu_st, wsem.at[1]),
                pltpu.make_async_copy(wd_hbm.at[e], wd_st, wsem.at[2]))

    def token_of(p):
        if t & (t - 1) == 0:
            return p & (t - 1)
        return jnp.where(p >= 2 * t, p - 2 * t, jnp.where(p >= t, p - t, p))

    def gather_row(ids_ref, off, r, slot):
        tok = token_of(ids_ref[0, 0, off + r])
        return pltpu.make_async_copy(xn_hbm.at[pl.ds(pl.multiple_of(tok * k, k), k)],
                                     xbuf.at[slot, pl.ds(r * k, k)], gsem.at[slot])

    def gather_all(slot):
        return pltpu.make_async_copy(xn_hbm.at[pl.ds(0, rows * k)], xbuf.at[slot], gsem.at[slot])

    def scatter_all(slot):
        return pltpu.make_async_copy(ybuf.at[slot], out_hbm.at[pl.ds(0, rows * k)], ssem.at[slot])

    def switch_weights(m):
        @pl.when(first_ref[m] == 1)
        def _():
            for cp in fetch(be_ref[m]):
                cp.wait()
            wg_bf[...] = wg_st[...].astype(BF16)
            wu_bf[...] = wu_st[...].astype(BF16)
            wd_bf[...] = wd_st[...].astype(BF16)

            @pl.when(nxt_ref[m] >= 0)
            def _():
                for cp in fetch(nxt_ref[m]):
                    cp.start()

    def phase(slot, next_ids, next_off, prev_ids, prev_off):
        other = 1 - slot
        for r in range(rows):
            gather_row(next_ids, next_off, r, other).start()
        for r in range(rows):
            dst = prev_ids[0, 0, prev_off + r]
            pltpu.make_async_copy(ybuf.at[other, pl.ds(r * k, k)],
                                  out_hbm.at[pl.ds(pl.multiple_of(dst * k, k), k)],
                                  ssem.at[other]).start(priority=r % 2)
        xb = _load_tile_rows(xbuf, rows, k, lead=(slot,)).astype(BF16)
        hid = jax.nn.silu(_dot(xb, wg_bf[...])) * _dot(xb, wu_bf[...])
        _store_tile_rows(ybuf, 0, _dot(hid.astype(BF16), wd_bf[...]), lead=(slot,))

    @pl.when(g == 0)
    def _():
        for cp in fetch(be_ref[0]):
            cp.start()

        def first_block(r, carry):
            gather_row(ids_a_ref, rows, r, 0).start()
            return carry
        lax.fori_loop(0, rows, first_block, 0)
        ybuf[...] = jnp.zeros_like(ybuf)

    switch_weights(2 * g)
    gather_all(0).wait()

    @pl.when(g >= 1)
    def _():
        scatter_all(0).wait()
    phase(0, ids_b_ref, 0, ids_a_ref, 0)

    switch_weights(2 * g + 1)
    gather_all(1).wait()
    scatter_all(1).wait()
    phase(1, ids_b_ref, rows, ids_a_ref, rows)

    @pl.when(g == pl.num_programs(0) - 1)
    def _():
        gather_all(0).wait()
        scatter_all(0).wait()


def _expert(blk_expert, first, nxt, ids, xn3, w_gate, w_up, w_down):
    d, eh = w_gate.shape[1:]
    k = d // LANE
    t = xn3.shape[0] // k
    steps = ids.shape[0] - 1
    assert blk_expert.shape[0] == 2 * steps
    hbm = pl.BlockSpec(memory_space=pl.ANY)
    idx_block = lambda f: pl.BlockSpec((1, 1, 2 * MOE_BLOCK), f, memory_space=pltpu.SMEM)
    grid_spec = pltpu.PrefetchScalarGridSpec(
        num_scalar_prefetch=3,
        grid=(steps,),
        in_specs=[idx_block(lambda g, *_: (g, 0, 0)), idx_block(lambda g, *_: (g + 1, 0, 0)),
                  hbm, hbm, hbm, hbm],
        out_specs=hbm,
        scratch_shapes=[pltpu.VMEM((2, MOE_BLOCK * k, LANE), F32), pltpu.VMEM((2, MOE_BLOCK * k, LANE), F32),
                        pltpu.VMEM((d, eh), F32), pltpu.VMEM((d, eh), F32), pltpu.VMEM((eh, d), F32),
                        pltpu.VMEM((d, eh), BF16), pltpu.VMEM((d, eh), BF16), pltpu.VMEM((eh, d), BF16),
                        pltpu.SemaphoreType.DMA((3,)), pltpu.SemaphoreType.DMA((2,)),
                        pltpu.SemaphoreType.DMA((2,))],
    )
    return pl.pallas_call(
        functools.partial(_expert_kernel, t=t),
        out_shape=jax.ShapeDtypeStruct(((2 * t + 2 * MOE_BLOCK) * k, LANE), F32),
        grid_spec=grid_spec,
        compiler_params=_params(("arbitrary",)),
        name="expert",
    )(blk_expert, first, nxt, ids, ids, xn3, w_gate, w_up, w_down)


def _combine_kernel(h2_ref, y0_ref, y1_ref, route_ref, g_ref, o_ref):
    tm, d = h2_ref.shape
    route = route_ref[...]
    y0 = _load_tile_rows(y0_ref, tm, d // LANE)
    y1 = _load_tile_rows(y1_ref, tm, d // LANE)
    y = route[:, 2:3] * y0 + route[:, 3:4] * y1
    o_ref[...] = _rms(h2_ref[...] + y, g_ref[...])


def _combine(h2, ypairs, route, g, tm):
    t, d = h2.shape
    k = d // LANE
    assert ypairs.shape[0] % (tm * k) == 0
    return pl.pallas_call(
        _combine_kernel,
        out_shape=jax.ShapeDtypeStruct((t, d), F32),
        grid=(t // tm,),
        in_specs=[pl.BlockSpec((tm, d), lambda i: (i, 0)),
                  pl.BlockSpec((tm * k, LANE), lambda i: (i, 0)),
                  pl.BlockSpec((tm * k, LANE), lambda i: (i + t // tm, 0)),
                  pl.BlockSpec((tm, LANE), lambda i: (i, 0)),
                  _const_spec(g.shape)],
        out_specs=pl.BlockSpec((tm, d), lambda i: (i, 0)),
        compiler_params=_params(("parallel",)),
        name="combine",
    )(h2, ypairs, ypairs, route, g)


def kernel(x, mem, norm_mix, w_in, w_gate, b_gate, sgu_norm, w_spatial, b_spatial, w_branch_a,
           w_branch_b, rel_bias, w_o, norm_x, norm_mem, w_xq, w_xkv, w_xo, norm_ffn,
           w_router_group, b_router_group, w_router_expert, b_router_expert,
           w_e_gate, w_e_up, w_e_down, norm_final):
    batch, seq, d = x.shape
    assert norm_mix.shape[0] == 1, "one layer"
    assert seq % MOBA_BLOCK == 0 and d % LANE == 0
    t = batch * seq
    tm_in, tm_post, tm_moe = 512, 512, 256
    row = lambda a: a.reshape(1, -1).astype(F32)
    bf = lambda a: a.astype(BF16)

    x2d = x.reshape(t, d)
    kvm = _memkv(mem.reshape(-1, d), row(norm_mem[0]), bf(w_xkv[0]))

    width = w_in.shape[2] // 5
    bsp_full = jnp.repeat(b_spatial[0].T, width // SGU_GROUPS, axis=1)
    sgu, q, k, v, kmean = _inproj(x2d, row(norm_mix[0]), bf(w_in[0]), row(sgu_norm[0]),
                                  w_spatial[0], bsp_full, tm_in)

    rel_t = rel_bias.T.astype(F32)
    own, prev = _relbias(rel_t, MOBA_BLOCK)
    att = _attn(rel_t[:, NUM_BUCKETS - 1], q, k, v,
                kmean.reshape(batch, seq // MOBA_BLOCK, width), own, prev, batch, seq)

    lane_pad = LANE - N_GROUPS - N_EXPERTS
    w_router = jnp.concatenate([w_router_group[0], w_router_expert[0], jnp.zeros((d, lane_pad), F32)], axis=1)
    b_router = jnp.concatenate([b_router_group[0], b_router_expert[0], jnp.zeros((lane_pad,), F32)])
    wr_hi = bf(w_router)
    wr_lo = bf(w_router - wr_hi.astype(F32))
    consts = [row(norm_mix[0]), bf(w_gate[0]), row(b_gate[0]), bf(w_branch_a[0]), bf(w_branch_b[0]),
              bf(w_o[0]), row(norm_x[0]), bf(w_xq[0]), bf(w_xo[0]), row(norm_ffn[0]),
              jnp.concatenate([wr_hi, wr_hi, wr_lo], axis=0), row(b_router)]
    h2, xn3, route, cnt = _post(x2d, sgu, att, kvm, consts, tm_post, seq)

    counts = cnt[0, :N_EXPERTS].astype(jnp.int32)
    padded = (counts + MOE_BLOCK - 1) // MOE_BLOCK * MOE_BLOCK
    pad_ends = jnp.cumsum(padded)
    pad_starts = pad_ends - padded
    nblk = -(-(2 * t) // MOE_BLOCK) + N_EXPERTS
    assert nblk % 2 == 0
    nphase = nblk + 2
    blk_idx = jnp.arange(nphase, dtype=jnp.int32)
    blk_expert = jnp.minimum(
        jnp.sum((pad_ends[None, :] <= blk_idx[:, None] * MOE_BLOCK).astype(jnp.int32), axis=1),
        N_EXPERTS - 1)
    nused = pad_ends[-1] // MOE_BLOCK
    prev_expert = jnp.concatenate([jnp.full((1,), -1, jnp.int32), blk_expert[:-1]])
    first = ((blk_idx < nused) & (blk_expert != prev_expert)).astype(jnp.int32)
    nxt_idx = blk_idx + (padded // MOE_BLOCK)[blk_expert]
    nxt = jnp.where((first == 1) & (nxt_idx < nused),
                    blk_expert[jnp.minimum(nxt_idx, nphase - 1)], -1).astype(jnp.int32)

    assert 2 * t <= 1 << RANK_BITS
    er = jnp.concatenate([route[:, 0:2], route[:, 4:6]], axis=1).T.astype(jnp.int32)
    code = (er[0:2] << RANK_BITS) | er[2:4]
    er3 = code.reshape(2, t // tm_moe, tm_moe).transpose(1, 0, 2).reshape(t // tm_moe, 1, 2 * tm_moe)
    inv = _invert(pad_starts, counts, pad_ends, er3, tm_moe, t, nblk * MOE_BLOCK)
    dummy = lambda m: _pad_pair_id(m * MOE_BLOCK + jnp.arange(MOE_BLOCK, dtype=jnp.int32), t)
    ids = jnp.concatenate([dummy(-1), inv] + [dummy(nblk + m) for m in range(3)])
    ypairs = _expert(blk_expert, first, nxt, ids.reshape(-1, 1, 2 * MOE_BLOCK), xn3,
                     w_e_gate[0], w_e_up[0], w_e_down[0])
    out = _combine(h2, ypairs, route, row(norm_final), tm_post)
    return out.reshape(batch, seq, d)
```

```python
import functools
import math

import numpy as np
import jax
import jax.numpy as jnp
from jax import lax
from jax.experimental import pallas as pl
from jax.experimental.pallas import tpu as pltpu

F32 = jnp.float32
BF16 = jnp.bfloat16

EPS = 1e-6
NEG = -1e30
LOG2E = math.log2(math.e)
LANE = 128
ONES_ROWS = 16
RANK_BITS = 20

SGU_GROUPS = 8
CHUNK = 128
ATT_HEADS = 8
ATT_HEAD_DIM = 128
MOBA_BLOCK = 256
MOBA_TOPK = 3
NUM_BUCKETS = 32
MAX_DISTANCE = 128
X_HEADS = 4
X_HEAD_DIM = 128
N_GROUPS = 8
EXPERTS_PER_GROUP = 8
N_EXPERTS = N_GROUPS * EXPERTS_PER_GROUP
MOE_BLOCK = 256

VMEM_LIMIT = 56 * 1024 * 1024

_NT = (((1,), (1,)), ((), ()))


def _rms(x, g):
    return x * lax.rsqrt(jnp.mean(x * x, axis=-1, keepdims=True) + EPS) * g


def _gelu(x):
    c = math.sqrt(2.0 / math.pi)
    return x * (0.5 * (1.0 + jnp.tanh(c * (x + 0.044715 * (x * x * x)))))


def _dot(a, b):
    return jnp.dot(a, b, preferred_element_type=F32)


def _store_tile_rows(ref, first_row, val, lead=()):
    n, d = val.shape
    k = d // LANE
    for j in range(k):
        ref[lead + (pl.ds(first_row * k + j, n, stride=k), slice(None))] = val[:, j * LANE:(j + 1) * LANE]


def _load_tile_rows(ref, n, k, lead=()):
    return jnp.concatenate([ref[lead + (pl.ds(j, n, stride=k), slice(None))] for j in range(k)], axis=1)


def _const_spec(shape):
    nd = len(shape)
    return pl.BlockSpec(shape, lambda *_: (0,) * nd, pipeline_mode=pl.Buffered(1))


def _params(sem):
    return pltpu.CompilerParams(dimension_semantics=sem, vmem_limit_bytes=VMEM_LIMIT)


def _memkv_kernel(mem_ref, g_ref, w_ref, o_ref):
    mn = _rms(mem_ref[...], g_ref[...]).astype(BF16)
    o_ref[...] = _dot(mn, w_ref[...]).astype(BF16)


def _memkv(mem2d, g, w_bf):
    n, d = mem2d.shape
    tm = 256
    return pl.pallas_call(
        _memkv_kernel,
        out_shape=jax.ShapeDtypeStruct((n, w_bf.shape[1]), BF16),
        grid=(n // tm,),
        in_specs=[pl.BlockSpec((tm, d), lambda i: (i, 0)),
                  _const_spec(g.shape), _const_spec(w_bf.shape)],
        out_specs=pl.BlockSpec((tm, w_bf.shape[1]), lambda i: (i, 0)),
        compiler_params=_params(("parallel",)),
        name="memkv",
    )(mem2d, g, w_bf)


def _inproj_kernel(x_ref, g_ref, win_ref, sg_ref, wsp_ref, bsp_ref,
                   sgu_ref, q_ref, k_ref, v_ref, km_ref, *, tm, width, scale):
    xn = _rms(x_ref[...], g_ref[...]).astype(BF16)
    u = _gelu(_dot(xn, win_ref[:, 0:width]))
    vv = _gelu(_dot(xn, win_ref[:, width:2 * width]))
    vn = _rms(vv, sg_ref[...]).astype(BF16)
    gd = width // SGU_GROUPS
    row = lax.broadcasted_iota(jnp.int32, (CHUNK, CHUNK), 0)
    col = lax.broadcasted_iota(jnp.int32, (CHUNK, CHUNK), 1)
    causal = col <= row
    wsp = [jnp.where(causal, wsp_ref[g], 0.0).astype(BF16) for g in range(SGU_GROUPS)]
    bias = bsp_ref[...]
    for c in range(tm // CHUNK):
        rs = slice(c * CHUNK, (c + 1) * CHUNK)
        mixed = jnp.concatenate(
            [_dot(wsp[g], vn[rs, g * gd:(g + 1) * gd]) for g in range(SGU_GROUPS)], axis=1)
        sgu_ref[rs, :] = (u[rs, :] * (mixed + bias)).astype(BF16)
    q_ref[...] = (_dot(xn, win_ref[:, 2 * width:3 * width]) * scale).astype(BF16)
    k = _dot(xn, win_ref[:, 3 * width:4 * width])
    k_ref[...] = k.astype(BF16)
    for j in range(tm // MOBA_BLOCK):
        km_ref[0, j:j + 1, :] = jnp.mean(k[j * MOBA_BLOCK:(j + 1) * MOBA_BLOCK, :], axis=0, keepdims=True)
    v_ref[...] = _dot(xn, win_ref[:, 4 * width:5 * width]).astype(BF16)


def _inproj(x2d, g, win_bf, sg, wsp, bsp_full, tm):
    t, d = x2d.shape
    width = win_bf.shape[1] // 5
    nkm = tm // MOBA_BLOCK
    row_spec = pl.BlockSpec((tm, width), lambda i: (i, 0))
    act = jax.ShapeDtypeStruct((t, width), BF16)
    return pl.pallas_call(
        functools.partial(_inproj_kernel, tm=tm, width=width, scale=ATT_HEAD_DIM ** -0.5 * LOG2E),
        out_shape=(act, act, act, act, jax.ShapeDtypeStruct((t // tm, nkm, width), F32)),
        grid=(t // tm,),
        in_specs=[pl.BlockSpec((tm, d), lambda i: (i, 0)),
                  _const_spec(g.shape), _const_spec(win_bf.shape), _const_spec(sg.shape),
                  _const_spec(wsp.shape), _const_spec(bsp_full.shape)],
        out_specs=(row_spec, row_spec, row_spec, row_spec,
                   pl.BlockSpec((1, nkm, width), lambda i: (i, 0, 0))),
        compiler_params=_params(("parallel",)),
        name="inproj",
    )(x2d, g, win_bf, sg, wsp, bsp_full)


def _bucket_uppers():
    d = np.arange(0, 4 * MAX_DISTANCE)
    max_exact = NUM_BUCKETS // 2
    nf = np.maximum(d, 1).astype(np.float32)
    large = max_exact + (np.log(nf / max_exact) / math.log(MAX_DISTANCE / max_exact)
                         * (NUM_BUCKETS - max_exact)).astype(np.int32)
    bucket = np.where(d < max_exact, d, np.minimum(large, NUM_BUCKETS - 1))
    return [int(d[bucket > b].min()) for b in range(NUM_BUCKETS - 1)]


def _relbias_kernel(rb_ref, own_ref, prev_ref, *, tq, uppers):
    h = pl.program_id(0)
    key = lax.broadcasted_iota(jnp.int32, (tq, tq), 0)
    qry = lax.broadcasted_iota(jnp.int32, (tq, tq), 1)

    def table(dist):
        val = jnp.full((tq, tq), rb_ref[h, NUM_BUCKETS - 1] * LOG2E, F32)
        for b in range(NUM_BUCKETS - 2, -1, -1):
            val = jnp.where(dist < uppers[b], rb_ref[h, b] * LOG2E, val)
        return val

    d = qry - key
    own_ref[0] = jnp.where(d >= 0, table(d), NEG)
    prev_ref[0] = table(d + tq)


def _relbias(rel_bias_t, tq):
    nh = rel_bias_t.shape[0]
    tile = jax.ShapeDtypeStruct((nh, tq, tq), F32)
    spec = pl.BlockSpec((1, tq, tq), lambda h: (h, 0, 0))
    return pl.pallas_call(
        functools.partial(_relbias_kernel, tq=tq, uppers=_bucket_uppers()),
        out_shape=(tile, tile),
        grid=(nh,),
        in_specs=[pl.BlockSpec(memory_space=pltpu.SMEM)],
        out_specs=(spec, spec),
        compiler_params=_params(("parallel",)),
        name="relbias",
    )(rel_bias_t)


def _attn_kernel(c31_ref, q_ref, k_ref, v_ref, km_ref, own_ref, prev_ref, o_ref,
                 vt_scr, add_scr, s_scr, acc_scr, *, tq, nb, hg):
    hd = ATT_HEAD_DIM
    heads = range(hg)
    cols = [slice(g * hd, (g + 1) * hd) for g in heads]
    c31 = [c31_ref[pl.program_id(1) * hg + g] * LOG2E for g in heads]
    for g in heads:
        for n in range(nb):
            vt_scr[g, n // 2, :hd, (n % 2) * tq:(n % 2 + 1) * tq] = (
                v_ref[n * tq:(n + 1) * tq, cols[g]].astype(F32).T.astype(BF16))
        vt_scr[g, :, hd:, :] = jnp.ones((nb // 2, ONES_ROWS, 2 * tq), BF16)

    km2 = []
    for g in heads:
        km = km_ref[0, :, cols[g]]
        km_hi = km.astype(BF16)
        km2.append(jnp.concatenate([km_hi, (km - km_hi.astype(F32)).astype(BF16)], axis=1))

    def part_max(s):
        return jnp.max(s.reshape(s.shape[0] // 8, 8, tq), axis=0)

    def q_tile(cur, carry):
        rows = pl.ds(pl.multiple_of(cur * tq, tq), tq)
        q = [q_ref[rows, cols[g]] for g in heads]
        blk = lax.broadcasted_iota(jnp.int32, (nb, tq), 0)
        past = blk < cur

        negm_prev = []
        for g in heads:
            gate = lax.dot_general(km2[g], jnp.concatenate([q[g], q[g]], axis=1), _NT,
                                   preferred_element_type=F32)
            gv = jnp.where(past, gate, NEG)
            sel = jnp.zeros((nb, tq), F32)
            for _ in range(MOBA_TOPK):
                mx = jnp.max(gv, axis=0, keepdims=True)
                first = jnp.min(jnp.where(gv == mx, blk, nb), axis=0, keepdims=True)
                hit = blk == first
                sel = jnp.where(hit & past, 1.0, sel)
                gv = jnp.where(hit, -jnp.inf, gv)
            negm = (sel - 1.0) * (-NEG)
            add_scr[g] = jnp.where(blk < cur - 1, negm + c31[g], NEG)
            negm_prev.append(jnp.sum(jnp.where(blk == cur - 1, negm, 0.0), axis=0, keepdims=True))

        def far_pair(i, mx8):
            out = []
            for g in heads:
                kb = k_ref[pl.ds(pl.multiple_of(i * (2 * tq), 2 * tq), 2 * tq), cols[g]]
                s2 = lax.dot_general(kb, q[g], _NT, preferred_element_type=F32)
                s_lo = s2[:tq] + add_scr[g, pl.ds(2 * i, 1), :]
                s_hi = s2[tq:] + add_scr[g, pl.ds(2 * i + 1, 1), :]
                s_scr[g, i, :tq, :] = s_lo
                s_scr[g, i, tq:, :] = s_hi
                out.append(jnp.maximum(mx8[g], jnp.maximum(part_max(s_lo), part_max(s_hi))))
            return tuple(out)

        nfar = jnp.maximum(cur - 1, 0)
        mx8 = lax.fori_loop(0, (nfar + 1) // 2, far_pair,
                            tuple(jnp.full((8, tq), NEG, F32) for _ in heads))

        def one_block(n, add_tiles, mx8):
            out = []
            for g in heads:
                kb = k_ref[pl.ds(pl.multiple_of(n * tq, tq), tq), cols[g]]
                s = lax.dot_general(kb, q[g], _NT, preferred_element_type=F32) + add_tiles[g]
                s_scr[g, n // 2, pl.ds(pl.multiple_of((n % 2) * tq, tq), tq), :] = s
                out.append(jnp.maximum(mx8[g], part_max(s)))
            return tuple(out)

        mx8 = lax.cond(cur >= 1,
                       lambda a: one_block(cur - 1, [prev_ref[g] + negm_prev[g] for g in heads], a),
                       lambda a: a, mx8)
        mx8 = one_block(cur, [own_ref[g] for g in heads], mx8)

        @pl.when(cur % 2 == 0)
        def _():
            for g in heads:
                s_scr[g, cur // 2, tq:, :] = jnp.full((tq, tq), NEG, F32)

        m = [jnp.max(mx8[g], axis=0, keepdims=True) for g in heads]
        acc_scr[...] = jnp.zeros_like(acc_scr)

        def pv_pair(i, c):
            for g in heads:
                p = jnp.exp2((s_scr[g, i] - m[g]).astype(BF16))
                acc_scr[g] += _dot(vt_scr[g, i], p)
            return c

        lax.fori_loop(0, cur // 2 + 1, pv_pair, 0)
        for g in heads:
            acc = acc_scr[g]
            o_ref[rows, cols[g]] = (acc[:hd] / acc[hd:hd + 1]).T.astype(BF16)
        return carry

    lax.fori_loop(0, nb, q_tile, 0)


def _attn(c31, q, k, v, kmean, own, prev, batch, seq):
    t, width = q.shape
    tq = MOBA_BLOCK
    nb = seq // tq
    hd = ATT_HEAD_DIM
    hg = 4
    nh = width // hd
    assert nb % 2 == 0 and nh % hg == 0
    seq_spec = pl.BlockSpec((seq, hg * hd), lambda b, h: (b, h), pipeline_mode=pl.Buffered(1))
    tile_spec = pl.BlockSpec((hg, tq, tq), lambda b, h: (h, 0, 0))
    return pl.pallas_call(
        functools.partial(_attn_kernel, tq=tq, nb=nb, hg=hg),
        out_shape=jax.ShapeDtypeStruct((t, width), BF16),
        grid=(batch, nh // hg),
        in_specs=[pl.BlockSpec(memory_space=pltpu.SMEM),
                  seq_spec, seq_spec, seq_spec,
                  pl.BlockSpec((1, nb, hg * hd), lambda b, h: (b, 0, h)),
                  tile_spec, tile_spec],
        out_specs=seq_spec,
        scratch_shapes=[pltpu.VMEM((hg, nb // 2, hd + ONES_ROWS, 2 * tq), BF16),
                        pltpu.VMEM((hg, nb, tq), F32),
                        pltpu.VMEM((hg, nb // 2, 2 * tq, tq), F32),
                        pltpu.VMEM((hg, hd + ONES_ROWS, tq), F32)],
        compiler_params=_params(("parallel", "parallel")),
        name="attn",
    )(c31, q, k, v, kmean, own, prev)


def _post_kernel(x_ref, sgu_ref, att_ref, kvm_ref, nmix_ref, wg_ref, bg_ref, wa_ref, wb_ref,
                 wo_ref, nx_ref, wxq_ref, wxo_ref, nffn_ref, wr_ref, br_ref,
                 h2_ref, xn3_ref, route_ref, cnt_ref, carry_scr, *, tm, ts, d, xw):
    @pl.when(pl.program_id(0) == 0)
    def _():
        carry_scr[...] = jnp.zeros_like(carry_scr)

    lane = lax.broadcasted_iota(jnp.int32, (ts, LANE), 1)
    r = lax.broadcasted_iota(jnp.int32, (ts, ts), 0)
    c = lax.broadcasted_iota(jnp.int32, (ts, ts), 1)
    tri = jnp.where(c < r, 1.0, 0.0).astype(BF16)

    def sub_tile(rows):
        x = x_ref[rows, :]
        xn = _rms(x, nmix_ref[...]).astype(BF16)
        gates = jax.nn.sigmoid(_dot(xn, wg_ref[...]) + bg_ref[...])
        ya = _dot(sgu_ref[rows, :], wa_ref[...])
        yb = _dot(att_ref[rows, :], wb_ref[...])
        merged = (gates[:, :d] * ya + gates[:, d:] * yb).astype(BF16)
        h1 = x + _dot(merged, wo_ref[...])

        xn2 = _rms(h1, nx_ref[...]).astype(BF16)
        qx = (_dot(xn2, wxq_ref[...]) * (X_HEAD_DIM ** -0.5)).astype(BF16)
        outs = []
        for hh in range(X_HEADS):
            cs = slice(hh * X_HEAD_DIM, (hh + 1) * X_HEAD_DIM)
            vs = slice(xw + hh * X_HEAD_DIM, xw + (hh + 1) * X_HEAD_DIM)
            s = lax.dot_general(qx[:, cs], kvm_ref[:, cs], _NT, preferred_element_type=F32)
            p = jnp.exp(s - jnp.max(s, axis=-1, keepdims=True))
            o = _dot(p.astype(BF16), kvm_ref[:, vs]) / jnp.sum(p, axis=-1, keepdims=True)
            outs.append(o.astype(BF16))
        h2 = h1 + _dot(jnp.concatenate(outs, axis=1), wxo_ref[...])
        h2_ref[rows, :] = h2

        xn3 = _rms(h2, nffn_ref[...])
        _store_tile_rows(xn3_ref, rows.start, xn3)
        x_hi = xn3.astype(BF16)
        x_lo = (xn3 - x_hi.astype(F32)).astype(BF16)
        logits = _dot(jnp.concatenate([x_hi, x_lo, x_hi], axis=1), wr_ref[...]) + br_ref[...]
        is_g = lane < N_GROUPS
        gl = jnp.where(is_g, logits, -jnp.inf)
        gmax = jnp.max(gl, axis=-1, keepdims=True)
        gsel = jnp.min(jnp.where(gl == gmax, lane, LANE), axis=-1, keepdims=True)
        pg = 1.0 / jnp.sum(jnp.exp(gl - gmax), axis=-1, keepdims=True)
        eidx = lane - N_GROUPS
        in_group = (eidx >= 0) & (eidx < N_EXPERTS) & ((eidx // EXPERTS_PER_GROUP) == gsel)
        el = jnp.where(in_group, logits, -jnp.inf)
        m1 = jnp.max(el, axis=-1, keepdims=True)
        i1 = jnp.min(jnp.where(el == m1, lane, LANE), axis=-1, keepdims=True)
        el2 = jnp.where(lane == i1, -jnp.inf, el)
        m2 = jnp.max(el2, axis=-1, keepdims=True)
        i2 = jnp.min(jnp.where(el2 == m2, lane, LANE), axis=-1, keepdims=True)
        e2 = jnp.exp(m2 - m1)
        w1 = pg / (1.0 + e2)
        w2 = pg * e2 / (1.0 + e2)
        eid1 = i1 - N_GROUPS
        eid2 = i2 - N_GROUPS

        oh1f = jnp.where(lane == eid1, 1.0, 0.0)
        oh2f = jnp.where(lane == eid2, 1.0, 0.0)
        c1 = _dot(tri, oh1f.astype(BF16))
        c2 = _dot(tri, oh2f.astype(BF16))
        tot1 = jnp.sum(oh1f, axis=0, keepdims=True)
        tot2 = jnp.sum(oh2f, axis=0, keepdims=True)
        return eid1, eid2, w1, w2, oh1f, oh2f, c1, c2, tot1, tot2

    parts = [sub_tile(slice(j * ts, (j + 1) * ts)) for j in range(tm // ts)]

    carry = carry_scr[...]
    for j, (eid1, eid2, w1, w2, oh1f, oh2f, c1, c2, tot1, tot2) in enumerate(parts):
        r1 = jnp.sum(oh1f * (c1 + carry), axis=-1, keepdims=True)
        r2 = jnp.sum(oh2f * (c2 + carry + tot1), axis=-1, keepdims=True)
        carry = carry + tot1 + tot2
        route = jnp.where(lane == 0, eid1.astype(F32), 0.0)
        route = jnp.where(lane == 1, eid2.astype(F32), route)
        route = jnp.where(lane == 2, w1, route)
        route = jnp.where(lane == 3, w2, route)
        route = jnp.where(lane == 4, r1, route)
        route = jnp.where(lane == 5, r2, route)
        route_ref[j * ts:(j + 1) * ts, :] = route
    carry_scr[...] = carry
    cnt_ref[...] = carry


def _post(x2d, sgu, att, kvm, consts, tm, seq):
    t, d = x2d.shape
    mlen = kvm.shape[0] // (t // seq)
    xw = kvm.shape[1] // 2
    tiles_per_batch = seq // tm
    row = lambda w: pl.BlockSpec((tm, w), lambda i: (i, 0))
    return pl.pallas_call(
        functools.partial(_post_kernel, tm=tm, ts=min(tm, 256), d=d, xw=xw),
        out_shape=(jax.ShapeDtypeStruct((t, d), F32), jax.ShapeDtypeStruct((t * d // LANE, LANE), F32),
                   jax.ShapeDtypeStruct((t, LANE), F32), jax.ShapeDtypeStruct((1, LANE), F32)),
        grid=(t // tm,),
        in_specs=[row(d), row(sgu.shape[1]), row(att.shape[1]),
                  pl.BlockSpec((mlen, kvm.shape[1]), lambda i: (i // tiles_per_batch, 0))]
                 + [_const_spec(c.shape) for c in consts],
        out_specs=(row(d), pl.BlockSpec((tm * d // LANE, LANE), lambda i: (i, 0)), row(LANE),
                   pl.BlockSpec((1, LANE), lambda i: (0, 0))),
        scratch_shapes=[pltpu.VMEM((1, LANE), F32)],
        compiler_params=_params(("arbitrary",)),
        name="post",
    )(x2d, sgu, att, kvm, *consts)


def _pad_pair_id(row, t):
    return 2 * t + (row & (2 * MOE_BLOCK - 1))


def _invert_kernel(start_ref, count_ref, end_ref, er_ref, inv_ref, *, tm, t, n):
    i = pl.program_id(0)

    def fill(lo, hi):
        def body(k, carry):
            inv_ref[k] = _pad_pair_id(k, t)
            return carry
        lax.fori_loop(lo, hi, body, 0)

    @pl.when(i == 0)
    def _():
        for e in range(N_EXPERTS):
            fill(start_ref[e] + count_ref[e], end_ref[e])
        fill(end_ref[N_EXPERTS - 1], n)

    def put(r, carry):
        for slot in range(2):
            code = er_ref[0, 0, slot * tm + r]
            row = start_ref[code >> RANK_BITS] + (code & ((1 << RANK_BITS) - 1))
            inv_ref[row] = slot * t + i * tm + r
        return carry

    lax.fori_loop(0, tm, put, 0, unroll=8)


def _invert(pad_starts, counts, pad_ends, er3, tm, t, n):
    grid_spec = pltpu.PrefetchScalarGridSpec(
        num_scalar_prefetch=3,
        grid=(t // tm,),
        in_specs=[pl.BlockSpec((1, 1, 2 * tm), lambda i, *_: (i, 0, 0), memory_space=pltpu.SMEM)],
        out_specs=pl.BlockSpec(memory_space=pltpu.SMEM),
    )
    return pl.pallas_call(
        functools.partial(_invert_kernel, tm=tm, t=t, n=n),
        out_shape=jax.ShapeDtypeStruct((n,), jnp.int32),
        grid_spec=grid_spec,
        compiler_params=_params(("arbitrary",)),
        name="invert",
    )(pad_starts, counts, pad_ends, er3)


def _expert_kernel(be_ref, first_ref, nxt_ref, ids_a_ref, ids_b_ref,
                   xn_hbm, wg_hbm, wu_hbm, wd_hbm, out_hbm,
                   xbuf, ybuf, wg_st, wu_st, wd_st, wg_bf, wu_bf, wd_bf, wsem, gsem, ssem, *, t):
    g = pl.program_id(0)
    rows = MOE_BLOCK
    k = xbuf.shape[1] // rows

    def fetch(e):
        return (pltpu.make_async_copy(wg_hbm.at[e], wg_st, wsem.at[0]),
                pltpu.make_async_copy(wu_hbm.at[e], wu_st, wsem.at[1]),
                pltpu.make_async_copy(wd_hbm.at[e], wd_st, wsem.at[2]))

    def token_of(p):
        if t & (t - 1) == 0:
            return p & (t - 1)
        return jnp.where(p >= 2 * t, p - 2 * t, jnp.where(p >= t, p - t, p))

    def gather_row(ids_ref, off, r, slot):
        tok = token_of(ids_ref[0, 0, off + r])
        return pltpu.make_async_copy(xn_hbm.at[pl.ds(pl.multiple_of(tok * k, k), k)],
                                     xbuf.at[slot, pl.ds(r * k, k)], gsem.at[slot])

    def gather_all(slot):
        return pltpu.make_async_copy(xn_hbm.at[pl.ds(0, rows * k)], xbuf.at[slot], gsem.at[slot])

    def scatter_all(slot):
        return pltpu.make_async_copy(ybuf.at[slot], out_hbm.at[pl.ds(0, rows * k)], ssem.at[slot])

    def switch_weights(m):
        @pl.when(first_ref[m] == 1)
        def _():
            for cp in fetch(be_ref[m]):
                cp.wait()
            wg_bf[...] = wg_st[...].astype(BF16)
            wu_bf[...] = wu_st[...].astype(BF16)
            wd_bf[...] = wd_st[...].astype(BF16)

            @pl.when(nxt_ref[m] >= 0)
            def _():
                for cp in fetch(nxt_ref[m]):
                    cp.start()

    def phase(slot, next_ids, next_off, prev_ids, prev_off):
        other = 1 - slot
        for r in range(rows):
            gather_row(next_ids, next_off, r, other).start(priority=r % 2)
        for r in range(rows):
            dst = prev_ids[0, 0, prev_off + r]
            pltpu.make_async_copy(ybuf.at[other, pl.ds(r * k, k)],
                                  out_hbm.at[pl.ds(pl.multiple_of(dst * k, k), k)],
                                  ssem.at[other]).start(priority=r % 2)
        xb = _load_tile_rows(xbuf, rows, k, lead=(slot,)).astype(BF16)
        hid = jax.nn.silu(_dot(xb, wg_bf[...])) * _dot(xb, wu_bf[...])
        _store_tile_rows(ybuf, 0, _dot(hid.astype(BF16), wd_bf[...]), lead=(slot,))

    @pl.when(g == 0)
    def _():
        for cp in fetch(be_ref[0]):
            cp.start()

        def first_block(r, carry):
            gather_row(ids_a_ref, rows, r, 0).start()
            return carry
        lax.fori_loop(0, rows, first_block, 0)
        ybuf[...] = jnp.zeros_like(ybuf)

    switch_weights(2 * g)
    gather_all(0).wait()

    @pl.when(g >= 1)
    def _():
        scatter_all(0).wait()
    phase(0, ids_b_ref, 0, ids_a_ref, 0)

    switch_weights(2 * g + 1)
    gather_all(1).wait()
    scatter_all(1).wait()
    phase(1, ids_b_ref, rows, ids_a_ref, rows)

    @pl.when(g == pl.num_programs(0) - 1)
    def _():
        gather_all(0).wait()
        scatter_all(0).wait()


def _expert(blk_expert, first, nxt, ids, xn3, w_gate, w_up, w_down):
    d, eh = w_gate.shape[1:]
    k = d // LANE
    t = xn3.shape[0] // k
    steps = ids.shape[0] - 1
    assert blk_expert.shape[0] == 2 * steps
    hbm = pl.BlockSpec(memory_space=pl.ANY)
    idx_block = lambda f: pl.BlockSpec((1, 1, 2 * MOE_BLOCK), f, memory_space=pltpu.SMEM)
    grid_spec = pltpu.PrefetchScalarGridSpec(
        num_scalar_prefetch=3,
        grid=(steps,),
        in_specs=[idx_block(lambda g, *_: (g, 0, 0)), idx_block(lambda g, *_: (g + 1, 0, 0)),
                  hbm, hbm, hbm, hbm],
        out_specs=hbm,
        scratch_shapes=[pltpu.VMEM((2, MOE_BLOCK * k, LANE), F32), pltpu.VMEM((2, MOE_BLOCK * k, LANE), F32),
                        pltpu.VMEM((d, eh), F32), pltpu.VMEM((d, eh), F32), pltpu.VMEM((eh, d), F32),
                        pltpu.VMEM((d, eh), BF16), pltpu.VMEM((d, eh), BF16), pltpu.VMEM((eh, d), BF16),
                        pltpu.SemaphoreType.DMA((3,)), pltpu.SemaphoreType.DMA((2,)),
                        pltpu.SemaphoreType.DMA((2,))],
    )
    return pl.pallas_call(
        functools.partial(_expert_kernel, t=t),
        out_shape=jax.ShapeDtypeStruct(((2 * t + 2 * MOE_BLOCK) * k, LANE), F32),
        grid_spec=grid_spec,
        compiler_params=_params(("arbitrary",)),
        name="expert",
    )(blk_expert, first, nxt, ids, ids, xn3, w_gate, w_up, w_down)


def _combine_kernel(h2_ref, y0_ref, y1_ref, route_ref, g_ref, o_ref):
    tm, d = h2_ref.shape
    route = route_ref[...]
    y0 = _load_tile_rows(y0_ref, tm, d // LANE)
    y1 = _load_tile_rows(y1_ref, tm, d // LANE)
    y = route[:, 2:3] * y0 + route[:, 3:4] * y1
    o_ref[...] = _rms(h2_ref[...] + y, g_ref[...])


def _combine(h2, ypairs, route, g, tm):
    t, d = h2.shape
    k = d // LANE
    assert ypairs.shape[0] % (tm * k) == 0
    return pl.pallas_call(
        _combine_kernel,
        out_shape=jax.ShapeDtypeStruct((t, d), F32),
        grid=(t // tm,),
        in_specs=[pl.BlockSpec((tm, d), lambda i: (i, 0)),
                  pl.BlockSpec((tm * k, LANE), lambda i: (i, 0)),
                  pl.BlockSpec((tm * k, LANE), lambda i: (i + t // tm, 0)),
                  pl.BlockSpec((tm, LANE), lambda i: (i, 0)),
                  _const_spec(g.shape)],
        out_specs=pl.BlockSpec((tm, d), lambda i: (i, 0)),
        compiler_params=_params(("parallel",)),
        name="combine",
    )(h2, ypairs, ypairs, route, g)


def kernel(x, mem, norm_mix, w_in, w_gate, b_gate, sgu_norm, w_spatial, b_spatial, w_branch_a,
           w_branch_b, rel_bias, w_o, norm_x, norm_mem, w_xq, w_xkv, w_xo, norm_ffn,
           w_router_group, b_router_group, w_router_expert, b_router_expert,
           w_e_gate, w_e_up, w_e_down, norm_final):
    batch, seq, d = x.shape
    assert norm_mix.shape[0] == 1, "one layer"
    assert seq % MOBA_BLOCK == 0 and d % LANE == 0
    t = batch * seq
    tm_in, tm_post, tm_moe = 512, 512, 1024
    row = lambda a: a.reshape(1, -1).astype(F32)
    bf = lambda a: a.astype(BF16)

    x2d = x.reshape(t, d)
    kvm = _memkv(mem.reshape(-1, d), row(norm_mem[0]), bf(w_xkv[0]))

    width = w_in.shape[2] // 5
    bsp_full = jnp.repeat(b_spatial[0].T, width // SGU_GROUPS, axis=1)
    sgu, q, k, v, kmean = _inproj(x2d, row(norm_mix[0]), bf(w_in[0]), row(sgu_norm[0]),
                                  w_spatial[0], bsp_full, tm_in)

    rel_t = rel_bias.T.astype(F32)
    own, prev = _relbias(rel_t, MOBA_BLOCK)
    att = _attn(rel_t[:, NUM_BUCKETS - 1], q, k, v,
                kmean.reshape(batch, seq // MOBA_BLOCK, width), own, prev, batch, seq)

    lane_pad = LANE - N_GROUPS - N_EXPERTS
    w_router = jnp.concatenate([w_router_group[0], w_router_expert[0], jnp.zeros((d, lane_pad), F32)], axis=1)
    b_router = jnp.concatenate([b_router_group[0], b_router_expert[0], jnp.zeros((lane_pad,), F32)])
    wr_hi = bf(w_router)
    wr_lo = bf(w_router - wr_hi.astype(F32))
    consts = [row(norm_mix[0]), bf(w_gate[0]), row(b_gate[0]), bf(w_branch_a[0]), bf(w_branch_b[0]),
              bf(w_o[0]), row(norm_x[0]), bf(w_xq[0]), bf(w_xo[0]), row(norm_ffn[0]),
              jnp.concatenate([wr_hi, wr_hi, wr_lo], axis=0), row(b_router)]
    h2, xn3, route, cnt = _post(x2d, sgu, att, kvm, consts, tm_post, seq)

    counts = cnt[0, :N_EXPERTS].astype(jnp.int32)
    padded = (counts + MOE_BLOCK - 1) // MOE_BLOCK * MOE_BLOCK
    pad_ends = jnp.cumsum(padded)
    pad_starts = pad_ends - padded
    nblk = -(-(2 * t) // MOE_BLOCK) + N_EXPERTS
    assert nblk % 2 == 0
    nphase = nblk + 2
    blk_idx = jnp.arange(nphase, dtype=jnp.int32)
    blk_expert = jnp.minimum(
        jnp.sum((pad_ends[None, :] <= blk_idx[:, None] * MOE_BLOCK).astype(jnp.int32), axis=1),
        N_EXPERTS - 1)
    nused = pad_ends[-1] // MOE_BLOCK
    prev_expert = jnp.concatenate([jnp.full((1,), -1, jnp.int32), blk_expert[:-1]])
    first = ((blk_idx < nused) & (blk_expert != prev_expert)).astype(jnp.int32)
    eidx = jnp.arange(N_EXPERTS, dtype=jnp.int32)
    later = (eidx[None, :] > eidx[:, None]) & (padded[None, :] > 0)
    next_expert = jnp.min(jnp.where(later, eidx[None, :], N_EXPERTS), axis=1)
    next_expert = jnp.where(next_expert == N_EXPERTS, -1, next_expert)
    onehot = (blk_expert[:, None] == eidx[None, :]).astype(jnp.int32)
    nxt = jnp.where(first == 1, jnp.sum(onehot * next_expert[None, :], axis=1), -1).astype(jnp.int32)

    assert 2 * t <= 1 << RANK_BITS
    er = jnp.concatenate([route[:, 0:2], route[:, 4:6]], axis=1).T.astype(jnp.int32)
    code = (er[0:2] << RANK_BITS) | er[2:4]
    er3 = code.reshape(2, t // tm_moe, tm_moe).transpose(1, 0, 2).reshape(t // tm_moe, 1, 2 * tm_moe)
    inv = _invert(pad_starts, counts, pad_ends, er3, tm_moe, t, nblk * MOE_BLOCK)
    dummy = lambda m: _pad_pair_id(m * MOE_BLOCK + jnp.arange(MOE_BLOCK, dtype=jnp.int32), t)
    ids = jnp.concatenate([dummy(-1), inv] + [dummy(nblk + m) for m in range(3)])
    ypairs = _expert(blk_expert, first, nxt, ids.reshape(-1, 1, 2 * MOE_BLOCK), xn3,
                     w_e_gate[0], w_e_up[0], w_e_down[0])
    out = _combine(h2, ypairs, route, row(norm_final), tm_post)
    return out.reshape(batch, seq, d)
```

```python
import functools
import math

import numpy as np
import jax
import jax.numpy as jnp
from jax import lax
from jax.experimental import pallas as pl
from jax.experimental.pallas import tpu as pltpu
from jax.experimental.pallas import tpu_sc as plsc

F32 = jnp.float32
BF16 = jnp.bfloat16

EPS = 1e-6
NEG = -1e30
LOG2E = math.log2(math.e)
LANE = 128
ONES_ROWS = 16
RANK_BITS = 20
SC_WINDOW = 128

SGU_GROUPS = 8
CHUNK = 128
ATT_HEADS = 8
ATT_HEAD_DIM = 128
MOBA_BLOCK = 256
MOBA_TOPK = 3
NUM_BUCKETS = 32
MAX_DISTANCE = 128
X_HEADS = 4
X_HEAD_DIM = 128
N_GROUPS = 8
EXPERTS_PER_GROUP = 8
N_EXPERTS = N_GROUPS * EXPERTS_PER_GROUP
MOE_BLOCK = 256

VMEM_LIMIT = 56 * 1024 * 1024

_NT = (((1,), (1,)), ((), ()))


def _rms(x, g):
    return x * lax.rsqrt(jnp.mean(x * x, axis=-1, keepdims=True) + EPS) * g


def _gelu(x):
    c = math.sqrt(2.0 / math.pi)
    return x * (0.5 * (1.0 + jnp.tanh(c * (x + 0.044715 * (x * x * x)))))


def _dot(a, b):
    return jnp.dot(a, b, preferred_element_type=F32)


def _store_tile_rows(ref, first_row, val, lead=()):
    n, d = val.shape
    k = d // LANE
    for j in range(k):
        ref[lead + (pl.ds(first_row * k + j, n, stride=k), slice(None))] = val[:, j * LANE:(j + 1) * LANE]


def _load_tile_rows(ref, n, k, lead=()):
    return jnp.concatenate([ref[lead + (pl.ds(j, n, stride=k), slice(None))] for j in range(k)], axis=1)


def _const_spec(shape):
    nd = len(shape)
    return pl.BlockSpec(shape, lambda *_: (0,) * nd, pipeline_mode=pl.Buffered(1))


def _params(sem):
    return pltpu.CompilerParams(dimension_semantics=sem, vmem_limit_bytes=VMEM_LIMIT)


def _memkv_kernel(mem_ref, g_ref, w_ref, o_ref):
    mn = _rms(mem_ref[...], g_ref[...]).astype(BF16)
    o_ref[...] = _dot(mn, w_ref[...]).astype(BF16)


def _memkv(mem2d, g, w_bf):
    n, d = mem2d.shape
    tm = 256
    return pl.pallas_call(
        _memkv_kernel,
        out_shape=jax.ShapeDtypeStruct((n, w_bf.shape[1]), BF16),
        grid=(n // tm,),
        in_specs=[pl.BlockSpec((tm, d), lambda i: (i, 0)),
                  _const_spec(g.shape), _const_spec(w_bf.shape)],
        out_specs=pl.BlockSpec((tm, w_bf.shape[1]), lambda i: (i, 0)),
        compiler_params=_params(("parallel",)),
        name="memkv",
    )(mem2d, g, w_bf)


def _inproj_kernel(x_ref, g_ref, win_ref, sg_ref, wsp_ref, bsp_ref,
                   sgu_ref, q_ref, k_ref, v_ref, km_ref, *, tm, width, scale):
    xn = _rms(x_ref[...], g_ref[...]).astype(BF16)
    u = _gelu(_dot(xn, win_ref[:, 0:width]))
    vv = _gelu(_dot(xn, win_ref[:, width:2 * width]))
    vn = _rms(vv, sg_ref[...]).astype(BF16)
    gd = width // SGU_GROUPS
    row = lax.broadcasted_iota(jnp.int32, (CHUNK, CHUNK), 0)
    col = lax.broadcasted_iota(jnp.int32, (CHUNK, CHUNK), 1)
    causal = col <= row
    wsp = [jnp.where(causal, wsp_ref[g], 0.0).astype(BF16) for g in range(SGU_GROUPS)]
    bias = bsp_ref[...]
    for c in range(tm // CHUNK):
        rs = slice(c * CHUNK, (c + 1) * CHUNK)
        mixed = jnp.concatenate(
            [_dot(wsp[g], vn[rs, g * gd:(g + 1) * gd]) for g in range(SGU_GROUPS)], axis=1)
        sgu_ref[rs, :] = (u[rs, :] * (mixed + bias)).astype(BF16)
    q_ref[...] = (_dot(xn, win_ref[:, 2 * width:3 * width]) * scale).astype(BF16)
    k = _dot(xn, win_ref[:, 3 * width:4 * width])
    k_ref[...] = k.astype(BF16)
    for j in range(tm // MOBA_BLOCK):
        km_ref[0, j:j + 1, :] = jnp.mean(k[j * MOBA_BLOCK:(j + 1) * MOBA_BLOCK, :], axis=0, keepdims=True)
    v_ref[...] = _dot(xn, win_ref[:, 4 * width:5 * width]).astype(BF16)


def _inproj(x2d, g, win_bf, sg, wsp, bsp_full, tm):
    t, d = x2d.shape
    width = win_bf.shape[1] // 5
    nkm = tm // MOBA_BLOCK
    row_spec = pl.BlockSpec((tm, width), lambda i: (i, 0))
    act = jax.ShapeDtypeStruct((t, width), BF16)
    return pl.pallas_call(
        functools.partial(_inproj_kernel, tm=tm, width=width, scale=ATT_HEAD_DIM ** -0.5 * LOG2E),
        out_shape=(act, act, act, act, jax.ShapeDtypeStruct((t // tm, nkm, width), F32)),
        grid=(t // tm,),
        in_specs=[pl.BlockSpec((tm, d), lambda i: (i, 0)),
                  _const_spec(g.shape), _const_spec(win_bf.shape), _const_spec(sg.shape),
                  _const_spec(wsp.shape), _const_spec(bsp_full.shape)],
        out_specs=(row_spec, row_spec, row_spec, row_spec,
                   pl.BlockSpec((1, nkm, width), lambda i: (i, 0, 0))),
        compiler_params=_params(("parallel",)),
        name="inproj",
    )(x2d, g, win_bf, sg, wsp, bsp_full)


def _bucket_uppers():
    d = np.arange(0, 4 * MAX_DISTANCE)
    max_exact = NUM_BUCKETS // 2
    nf = np.maximum(d, 1).astype(np.float32)
    large = max_exact + (np.log(nf / max_exact) / math.log(MAX_DISTANCE / max_exact)
                         * (NUM_BUCKETS - max_exact)).astype(np.int32)
    bucket = np.where(d < max_exact, d, np.minimum(large, NUM_BUCKETS - 1))
    return [int(d[bucket > b].min()) for b in range(NUM_BUCKETS - 1)]


def _relbias_kernel(rb_ref, own_ref, prev_ref, *, tq, uppers):
    h = pl.program_id(0)
    key = lax.broadcasted_iota(jnp.int32, (tq, tq), 0)
    qry = lax.broadcasted_iota(jnp.int32, (tq, tq), 1)

    def table(dist):
        val = jnp.full((tq, tq), rb_ref[h, NUM_BUCKETS - 1] * LOG2E, F32)
        for b in range(NUM_BUCKETS - 2, -1, -1):
            val = jnp.where(dist < uppers[b], rb_ref[h, b] * LOG2E, val)
        return val

    d = qry - key
    own_ref[0] = jnp.where(d >= 0, table(d), NEG)
    prev_ref[0] = table(d + tq)


def _relbias(rel_bias_t, tq):
    nh = rel_bias_t.shape[0]
    tile = jax.ShapeDtypeStruct((nh, tq, tq), F32)
    spec = pl.BlockSpec((1, tq, tq), lambda h: (h, 0, 0))
    return pl.pallas_call(
        functools.partial(_relbias_kernel, tq=tq, uppers=_bucket_uppers()),
        out_shape=(tile, tile),
        grid=(nh,),
        in_specs=[pl.BlockSpec(memory_space=pltpu.SMEM)],
        out_specs=(spec, spec),
        compiler_params=_params(("parallel",)),
        name="relbias",
    )(rel_bias_t)


def _attn_kernel(c31_ref, q_ref, k_ref, v_ref, km_ref, own_ref, prev_ref, o_ref,
                 vt_scr, add_scr, s_scr, acc_scr, *, tq, nb, hg):
    hd = ATT_HEAD_DIM
    heads = range(hg)
    cols = [slice(g * hd, (g + 1) * hd) for g in heads]
    c31 = [c31_ref[pl.program_id(1) * hg + g] * LOG2E for g in heads]
    for g in heads:
        for n in range(nb):
            vt_scr[g, n // 2, :hd, (n % 2) * tq:(n % 2 + 1) * tq] = (
                v_ref[n * tq:(n + 1) * tq, cols[g]].astype(F32).T.astype(BF16))
        vt_scr[g, :, hd:, :] = jnp.ones((nb // 2, ONES_ROWS, 2 * tq), BF16)

    km2 = []
    for g in heads:
        km = km_ref[0, :, cols[g]]
        km_hi = km.astype(BF16)
        km2.append(jnp.concatenate([km_hi, (km - km_hi.astype(F32)).astype(BF16)], axis=1))

    def part_max(s):
        return jnp.max(s.reshape(s.shape[0] // 8, 8, tq), axis=0)

    def q_tile(cur, carry):
        rows = pl.ds(pl.multiple_of(cur * tq, tq), tq)
        q = [q_ref[rows, cols[g]] for g in heads]
        blk = lax.broadcasted_iota(jnp.int32, (nb, tq), 0)
        past = blk < cur

        negm_prev = []
        for g in heads:
            gate = lax.dot_general(km2[g], jnp.concatenate([q[g], q[g]], axis=1), _NT,
                                   preferred_element_type=F32)
            gv = jnp.where(past, gate, NEG)
            sel = jnp.zeros((nb, tq), F32)
            for _ in range(MOBA_TOPK):
                mx = jnp.max(gv, axis=0, keepdims=True)
                first = jnp.min(jnp.where(gv == mx, blk, nb), axis=0, keepdims=True)
                hit = blk == first
                sel = jnp.where(hit & past, 1.0, sel)
                gv = jnp.where(hit, -jnp.inf, gv)
            negm = (sel - 1.0) * (-NEG)
            add_scr[g] = jnp.where(blk < cur - 1, negm + c31[g], NEG)
            negm_prev.append(jnp.sum(jnp.where(blk == cur - 1, negm, 0.0), axis=0, keepdims=True))

        def far_pair(i, mx8):
            out = []
            for g in heads:
                kb = k_ref[pl.ds(pl.multiple_of(i * (2 * tq), 2 * tq), 2 * tq), cols[g]]
                s2 = lax.dot_general(kb, q[g], _NT, preferred_element_type=F32)
                s_lo = s2[:tq] + add_scr[g, pl.ds(2 * i, 1), :]
                s_hi = s2[tq:] + add_scr[g, pl.ds(2 * i + 1, 1), :]
                s_scr[g, i, :tq, :] = s_lo
                s_scr[g, i, tq:, :] = s_hi
                out.append(jnp.maximum(mx8[g], jnp.maximum(part_max(s_lo), part_max(s_hi))))
            return tuple(out)

        nfar = jnp.maximum(cur - 1, 0)
        mx8 = lax.fori_loop(0, (nfar + 1) // 2, far_pair,
                            tuple(jnp.full((8, tq), NEG, F32) for _ in heads))

        def one_block(n, add_tiles, mx8):
            out = []
            for g in heads:
                kb = k_ref[pl.ds(pl.multiple_of(n * tq, tq), tq), cols[g]]
                s = lax.dot_general(kb, q[g], _NT, preferred_element_type=F32) + add_tiles[g]
                s_scr[g, n // 2, pl.ds(pl.multiple_of((n % 2) * tq, tq), tq), :] = s
                out.append(jnp.maximum(mx8[g], part_max(s)))
            return tuple(out)

        mx8 = lax.cond(cur >= 1,
                       lambda a: one_block(cur - 1, [prev_ref[g] + negm_prev[g] for g in heads], a),
                       lambda a: a, mx8)
        mx8 = one_block(cur, [own_ref[g] for g in heads], mx8)

        @pl.when(cur % 2 == 0)
        def _():
            for g in heads:
                s_scr[g, cur // 2, tq:, :] = jnp.full((tq, tq), NEG, F32)

        m = [jnp.max(mx8[g], axis=0, keepdims=True) for g in heads]
        acc_scr[...] = jnp.zeros_like(acc_scr)

        def pv_pair(i, c):
            for g in heads:
                p = jnp.exp2((s_scr[g, i] - m[g]).astype(BF16))
                acc_scr[g] += _dot(vt_scr[g, i], p)
            return c

        lax.fori_loop(0, cur // 2 + 1, pv_pair, 0)
        for g in heads:
            acc = acc_scr[g]
            o_ref[rows, cols[g]] = (acc[:hd] / acc[hd:hd + 1]).T.astype(BF16)
        return carry

    lax.fori_loop(0, nb, q_tile, 0)


def _attn(c31, q, k, v, kmean, own, prev, batch, seq):
    t, width = q.shape
    tq = MOBA_BLOCK
    nb = seq // tq
    hd = ATT_HEAD_DIM
    hg = 4
    nh = width // hd
    assert nb % 2 == 0 and nh % hg == 0
    seq_spec = pl.BlockSpec((seq, hg * hd), lambda b, h: (b, h), pipeline_mode=pl.Buffered(1))
    tile_spec = pl.BlockSpec((hg, tq, tq), lambda b, h: (h, 0, 0))
    return pl.pallas_call(
        functools.partial(_attn_kernel, tq=tq, nb=nb, hg=hg),
        out_shape=jax.ShapeDtypeStruct((t, width), BF16),
        grid=(batch, nh // hg),
        in_specs=[pl.BlockSpec(memory_space=pltpu.SMEM),
                  seq_spec, seq_spec, seq_spec,
                  pl.BlockSpec((1, nb, hg * hd), lambda b, h: (b, 0, h)),
                  tile_spec, tile_spec],
        out_specs=seq_spec,
        scratch_shapes=[pltpu.VMEM((hg, nb // 2, hd + ONES_ROWS, 2 * tq), BF16),
                        pltpu.VMEM((hg, nb, tq), F32),
                        pltpu.VMEM((hg, nb // 2, 2 * tq, tq), F32),
                        pltpu.VMEM((hg, hd + ONES_ROWS, tq), F32)],
        compiler_params=_params(("parallel", "parallel")),
        name="attn",
    )(c31, q, k, v, kmean, own, prev)


def _post_kernel(x_ref, sgu_ref, att_ref, kvm_ref, nmix_ref, wg_ref, bg_ref, wa_ref, wb_ref,
                 wo_ref, nx_ref, wxq_ref, wxo_ref, nffn_ref, wr_ref, br_ref,
                 h2_ref, xn3_ref, route_ref, cnt_ref, carry_scr, *, tm, ts, d, xw):
    @pl.when(pl.program_id(0) == 0)
    def _():
        carry_scr[...] = jnp.zeros_like(carry_scr)

    lane = lax.broadcasted_iota(jnp.int32, (ts, LANE), 1)
    r = lax.broadcasted_iota(jnp.int32, (ts, ts), 0)
    c = lax.broadcasted_iota(jnp.int32, (ts, ts), 1)
    tri = jnp.where(c < r, 1.0, 0.0).astype(BF16)

    def sub_tile(rows):
        x = x_ref[rows, :]
        xn = _rms(x, nmix_ref[...]).astype(BF16)
        gates = jax.nn.sigmoid(_dot(xn, wg_ref[...]) + bg_ref[...])
        ya = _dot(sgu_ref[rows, :], wa_ref[...])
        yb = _dot(att_ref[rows, :], wb_ref[...])
        merged = (gates[:, :d] * ya + gates[:, d:] * yb).astype(BF16)
        h1 = x + _dot(merged, wo_ref[...])

        xn2 = _rms(h1, nx_ref[...]).astype(BF16)
        qx = (_dot(xn2, wxq_ref[...]) * (X_HEAD_DIM ** -0.5)).astype(BF16)
        outs = []
        for hh in range(X_HEADS):
            cs = slice(hh * X_HEAD_DIM, (hh + 1) * X_HEAD_DIM)
            vs = slice(xw + hh * X_HEAD_DIM, xw + (hh + 1) * X_HEAD_DIM)
            s = lax.dot_general(qx[:, cs], kvm_ref[:, cs], _NT, preferred_element_type=F32)
            p = jnp.exp(s - jnp.max(s, axis=-1, keepdims=True))
            o = _dot(p.astype(BF16), kvm_ref[:, vs]) / jnp.sum(p, axis=-1, keepdims=True)
            outs.append(o.astype(BF16))
        h2 = h1 + _dot(jnp.concatenate(outs, axis=1), wxo_ref[...])
        h2_ref[rows, :] = h2

        xn3 = _rms(h2, nffn_ref[...])
        _store_tile_rows(xn3_ref, rows.start, xn3)
        x_hi = xn3.astype(BF16)
        x_lo = (xn3 - x_hi.astype(F32)).astype(BF16)
        logits = _dot(jnp.concatenate([x_hi, x_lo, x_hi], axis=1), wr_ref[...]) + br_ref[...]
        is_g = lane < N_GROUPS
        gl = jnp.where(is_g, logits, -jnp.inf)
        gmax = jnp.max(gl, axis=-1, keepdims=True)
        gsel = jnp.min(jnp.where(gl == gmax, lane, LANE), axis=-1, keepdims=True)
        pg = 1.0 / jnp.sum(jnp.exp(gl - gmax), axis=-1, keepdims=True)
        eidx = lane - N_GROUPS
        in_group = (eidx >= 0) & (eidx < N_EXPERTS) & ((eidx // EXPERTS_PER_GROUP) == gsel)
        el = jnp.where(in_group, logits, -jnp.inf)
        m1 = jnp.max(el, axis=-1, keepdims=True)
        i1 = jnp.min(jnp.where(el == m1, lane, LANE), axis=-1, keepdims=True)
        el2 = jnp.where(lane == i1, -jnp.inf, el)
        m2 = jnp.max(el2, axis=-1, keepdims=True)
        i2 = jnp.min(jnp.where(el2 == m2, lane, LANE), axis=-1, keepdims=True)
        e2 = jnp.exp(m2 - m1)
        w1 = pg / (1.0 + e2)
        w2 = pg * e2 / (1.0 + e2)
        eid1 = i1 - N_GROUPS
        eid2 = i2 - N_GROUPS

        oh1f = jnp.where(lane == eid1, 1.0, 0.0)
        oh2f = jnp.where(lane == eid2, 1.0, 0.0)
        c1 = _dot(tri, oh1f.astype(BF16))
        c2 = _dot(tri, oh2f.astype(BF16))
        tot1 = jnp.sum(oh1f, axis=0, keepdims=True)
        tot2 = jnp.sum(oh2f, axis=0, keepdims=True)
        return eid1, eid2, w1, w2, oh1f, oh2f, c1, c2, tot1, tot2

    parts = [sub_tile(slice(j * ts, (j + 1) * ts)) for j in range(tm // ts)]

    carry = carry_scr[...]
    for j, (eid1, eid2, w1, w2, oh1f, oh2f, c1, c2, tot1, tot2) in enumerate(parts):
        r1 = jnp.sum(oh1f * (c1 + carry), axis=-1, keepdims=True)
        r2 = jnp.sum(oh2f * (c2 + carry + tot1), axis=-1, keepdims=True)
        carry = carry + tot1 + tot2
        route = jnp.where(lane == 0, eid1.astype(F32), 0.0)
        route = jnp.where(lane == 1, eid2.astype(F32), route)
        route = jnp.where(lane == 2, w1, route)
        route = jnp.where(lane == 3, w2, route)
        route = jnp.where(lane == 4, r1, route)
        route = jnp.where(lane == 5, r2, route)
        route_ref[j * ts:(j + 1) * ts, :] = route
    carry_scr[...] = carry
    cnt_ref[...] = carry


def _post(x2d, sgu, att, kvm, consts, tm, seq):
    t, d = x2d.shape
    mlen = kvm.shape[0] // (t // seq)
    xw = kvm.shape[1] // 2
    tiles_per_batch = seq // tm
    row = lambda w: pl.BlockSpec((tm, w), lambda i: (i, 0))
    return pl.pallas_call(
        functools.partial(_post_kernel, tm=tm, ts=min(tm, 256), d=d, xw=xw),
        out_shape=(jax.ShapeDtypeStruct((t, d), F32), jax.ShapeDtypeStruct((t * d // LANE, LANE), F32),
                   jax.ShapeDtypeStruct((t, LANE), F32), jax.ShapeDtypeStruct((1, LANE), F32)),
        grid=(t // tm,),
        in_specs=[row(d), row(sgu.shape[1]), row(att.shape[1]),
                  pl.BlockSpec((mlen, kvm.shape[1]), lambda i: (i // tiles_per_batch, 0))]
                 + [_const_spec(c.shape) for c in consts],
        out_specs=(row(d), pl.BlockSpec((tm * d // LANE, LANE), lambda i: (i, 0)), row(LANE),
                   pl.BlockSpec((1, LANE), lambda i: (0, 0))),
        scratch_shapes=[pltpu.VMEM((1, LANE), F32)],
        compiler_params=_params(("arbitrary",)),
        name="post",
    )(x2d, sgu, att, kvm, *consts)


def _pad_pair_id(row, t):
    return 2 * t + (row & (2 * MOE_BLOCK - 1))


def _invert_kernel(start_ref, count_ref, end_ref, er_ref, inv_ref, *, tm, t, n):
    i = pl.program_id(0)

    def fill(lo, hi):
        def body(k, carry):
            inv_ref[k] = _pad_pair_id(k, t)
            return carry
        lax.fori_loop(lo, hi, body, 0)

    @pl.when(i == 0)
    def _():
        for e in range(N_EXPERTS):
            fill(start_ref[e] + count_ref[e], end_ref[e])
        fill(end_ref[N_EXPERTS - 1], n)

    def put(r, carry):
        for slot in range(2):
            code = er_ref[0, 0, slot * tm + r]
            row = start_ref[code >> RANK_BITS] + (code & ((1 << RANK_BITS) - 1))
            inv_ref[row] = slot * t + i * tm + r
        return carry

    lax.fori_loop(0, tm, put, 0, unroll=8)


def _invert(pad_starts, counts, pad_ends, er3, tm, t, n):
    grid_spec = pltpu.PrefetchScalarGridSpec(
        num_scalar_prefetch=3,
        grid=(t // tm,),
        in_specs=[pl.BlockSpec((1, 1, 2 * tm), lambda i, *_: (i, 0, 0), memory_space=pltpu.SMEM)],
        out_specs=pl.BlockSpec(memory_space=pltpu.SMEM),
    )
    return pl.pallas_call(
        functools.partial(_invert_kernel, tm=tm, t=t, n=n),
        out_shape=jax.ShapeDtypeStruct((n,), jnp.int32),
        grid_spec=grid_spec,
        compiler_params=_params(("arbitrary",)),
        name="invert",
    )(pad_starts, counts, pad_ends, er3)


def _sc_invert(pad_starts, code, n):
    info = plsc.get_sparse_core_info()
    nw = info.num_cores * info.num_subcores
    lanes = info.num_lanes
    npairs = code.shape[0]
    chunk = npairs // nw
    nwin = chunk // SC_WINDOW
    assert chunk * nw == npairs and nwin * SC_WINDOW == chunk
    vals = jnp.broadcast_to(jnp.arange(npairs, dtype=jnp.int32)[:, None], (npairs, lanes))

    def body(start_hbm, code_hbm, vals_hbm, o_hbm, start_v, code_v, dest_v, vals_v):
        wid = lax.axis_index("s") * info.num_cores + lax.axis_index("c")
        base = pl.multiple_of(wid * chunk, chunk)
        pltpu.sync_copy(start_hbm, start_v)
        pltpu.sync_copy(code_hbm.at[pl.ds(base, chunk)], code_v)
        pltpu.sync_copy(vals_hbm.at[pl.ds(base, chunk)], vals_v)

        @pl.loop(0, nwin)
        def _(j):
            for l in range(0, SC_WINDOW, lanes):
                c = code_v[pl.ds(j * SC_WINDOW + l, lanes)]
                seg = plsc.load_gather(start_v, [lax.shift_right_logical(c, RANK_BITS)])
                dest_v[j, pl.ds(l, lanes)] = seg + (c & ((1 << RANK_BITS) - 1))
            pltpu.sync_copy(vals_v.at[pl.ds(j * SC_WINDOW, SC_WINDOW)], o_hbm.at[dest_v.at[j]])

    return pl.kernel(
        body,
        out_type=jax.ShapeDtypeStruct((n, lanes), jnp.int32),
        mesh=plsc.VectorSubcoreMesh(core_axis_name="c", subcore_axis_name="s"),
        scratch_types=[pltpu.VMEM((N_EXPERTS,), jnp.int32), pltpu.VMEM((chunk,), jnp.int32),
                       pltpu.VMEM((nwin, SC_WINDOW), jnp.int32), pltpu.VMEM((chunk, lanes), jnp.int32)],
        compiler_params=pltpu.CompilerParams(needs_layout_passes=False, use_tc_tiling_on_sc=False),
        name="sc_invert",
    )(pad_starts, code, vals)


def _expert_kernel(be_ref, first_ref, nxt_ref, ids_a_ref, ids_b_ref,
                   xn_hbm, wg_hbm, wu_hbm, wd_hbm, out_hbm,
                   xbuf, ybuf, wg_st, wu_st, wd_st, wg_bf, wu_bf, wd_bf, wsem, gsem, ssem, *, t):
    g = pl.program_id(0)
    rows = MOE_BLOCK
    k = xbuf.shape[1] // rows

    def fetch(e):
        return (pltpu.make_async_copy(wg_hbm.at[e], wg_st, wsem.at[0]),
                pltpu.make_async_copy(wu_hbm.at[e], wu_st, wsem.at[1]),
                pltpu.make_async_copy(wd_hbm.at[e], wd_st, wsem.at[2]))

    def token_of(p):
        if t & (t - 1) == 0:
            return p & (t - 1)
        return jnp.where(p >= 2 * t, p - 2 * t, jnp.where(p >= t, p - t, p))

    def gather_row(ids_ref, off, r, slot):
        tok = token_of(ids_ref[0, 0, off + r])
        return pltpu.make_async_copy(xn_hbm.at[pl.ds(pl.multiple_of(tok * k, k), k)],
                                     xbuf.at[slot, pl.ds(r * k, k)], gsem.at[slot])

    def gather_all(slot):
        return pltpu.make_async_copy(xn_hbm.at[pl.ds(0, rows * k)], xbuf.at[slot], gsem.at[slot])

    def scatter_all(slot):
        return pltpu.make_async_copy(ybuf.at[slot], out_hbm.at[pl.ds(0, rows * k)], ssem.at[slot])

    def switch_weights(m):
        @pl.when(first_ref[m] == 1)
        def _():
            for cp in fetch(be_ref[m]):
                cp.wait()
            wg_bf[...] = wg_st[...].astype(BF16)
            wu_bf[...] = wu_st[...].astype(BF16)
            wd_bf[...] = wd_st[...].astype(BF16)

            @pl.when(nxt_ref[m] >= 0)
            def _():
                for cp in fetch(nxt_ref[m]):
                    cp.start()

    def phase(slot, next_ids, next_off, prev_ids, prev_off):
        other = 1 - slot
        for r in range(rows):
            gather_row(next_ids, next_off, r, other).start(priority=r % 2)
        for r in range(rows):
            dst = prev_ids[0, 0, prev_off + r]
            pltpu.make_async_copy(ybuf.at[other, pl.ds(r * k, k)],
                                  out_hbm.at[pl.ds(pl.multiple_of(dst * k, k), k)],
                                  ssem.at[other]).start(priority=r % 2)
        xb = _load_tile_rows(xbuf, rows, k, lead=(slot,)).astype(BF16)
        hid = jax.nn.silu(_dot(xb, wg_bf[...])) * _dot(xb, wu_bf[...])
        _store_tile_rows(ybuf, 0, _dot(hid.astype(BF16), wd_bf[...]), lead=(slot,))

    @pl.when(g == 0)
    def _():
        for cp in fetch(be_ref[0]):
            cp.start()

        def first_block(r, carry):
            gather_row(ids_a_ref, rows, r, 0).start()
            return carry
        lax.fori_loop(0, rows, first_block, 0)
        ybuf[...] = jnp.zeros_like(ybuf)

    switch_weights(2 * g)
    gather_all(0).wait()

    @pl.when(g >= 1)
    def _():
        scatter_all(0).wait()
    phase(0, ids_b_ref, 0, ids_a_ref, 0)

    switch_weights(2 * g + 1)
    gather_all(1).wait()
    scatter_all(1).wait()
    phase(1, ids_b_ref, rows, ids_a_ref, rows)

    @pl.when(g == pl.num_programs(0) - 1)
    def _():
        gather_all(0).wait()
        scatter_all(0).wait()


def _expert(blk_expert, first, nxt, ids, xn3, w_gate, w_up, w_down):
    d, eh = w_gate.shape[1:]
    k = d // LANE
    t = xn3.shape[0] // k
    steps = ids.shape[0] - 1
    assert blk_expert.shape[0] == 2 * steps
    hbm = pl.BlockSpec(memory_space=pl.ANY)
    idx_block = lambda f: pl.BlockSpec((1, 1, 2 * MOE_BLOCK), f, memory_space=pltpu.SMEM)
    grid_spec = pltpu.PrefetchScalarGridSpec(
        num_scalar_prefetch=3,
        grid=(steps,),
        in_specs=[idx_block(lambda g, *_: (g, 0, 0)), idx_block(lambda g, *_: (g + 1, 0, 0)),
                  hbm, hbm, hbm, hbm],
        out_specs=hbm,
        scratch_shapes=[pltpu.VMEM((2, MOE_BLOCK * k, LANE), F32), pltpu.VMEM((2, MOE_BLOCK * k, LANE), F32),
                        pltpu.VMEM((d, eh), F32), pltpu.VMEM((d, eh), F32), pltpu.VMEM((eh, d), F32),
                        pltpu.VMEM((d, eh), BF16), pltpu.VMEM((d, eh), BF16), pltpu.VMEM((eh, d), BF16),
                        pltpu.SemaphoreType.DMA((3,)), pltpu.SemaphoreType.DMA((2,)),
                        pltpu.SemaphoreType.DMA((2,))],
    )
    return pl.pallas_call(
        functools.partial(_expert_kernel, t=t),
        out_shape=jax.ShapeDtypeStruct(((2 * t + 2 * MOE_BLOCK) * k, LANE), F32),
        grid_spec=grid_spec,
        compiler_params=_params(("arbitrary",)),
        name="expert",
    )(blk_expert, first, nxt, ids, ids, xn3, w_gate, w_up, w_down)


def _combine_kernel(h2_ref, y0_ref, y1_ref, route_ref, g_ref, o_ref):
    tm, d = h2_ref.shape
    route = route_ref[...]
    y0 = _load_tile_rows(y0_ref, tm, d // LANE)
    y1 = _load_tile_rows(y1_ref, tm, d // LANE)
    y = route[:, 2:3] * y0 + route[:, 3:4] * y1
    o_ref[...] = _rms(h2_ref[...] + y, g_ref[...])


def _combine(h2, ypairs, route, g, tm):
    t, d = h2.shape
    k = d // LANE
    assert ypairs.shape[0] % (tm * k) == 0
    return pl.pallas_call(
        _combine_kernel,
        out_shape=jax.ShapeDtypeStruct((t, d), F32),
        grid=(t // tm,),
        in_specs=[pl.BlockSpec((tm, d), lambda i: (i, 0)),
                  pl.BlockSpec((tm * k, LANE), lambda i: (i, 0)),
                  pl.BlockSpec((tm * k, LANE), lambda i: (i + t // tm, 0)),
                  pl.BlockSpec((tm, LANE), lambda i: (i, 0)),
                  _const_spec(g.shape)],
        out_specs=pl.BlockSpec((tm, d), lambda i: (i, 0)),
        compiler_params=_params(("parallel",)),
        name="combine",
    )(h2, ypairs, ypairs, route, g)


def kernel(x, mem, norm_mix, w_in, w_gate, b_gate, sgu_norm, w_spatial, b_spatial, w_branch_a,
           w_branch_b, rel_bias, w_o, norm_x, norm_mem, w_xq, w_xkv, w_xo, norm_ffn,
           w_router_group, b_router_group, w_router_expert, b_router_expert,
           w_e_gate, w_e_up, w_e_down, norm_final):
    batch, seq, d = x.shape
    assert norm_mix.shape[0] == 1, "one layer"
    assert seq % MOBA_BLOCK == 0 and d % LANE == 0
    t = batch * seq
    tm_in, tm_post, tm_moe = 512, 512, 1024
    row = lambda a: a.reshape(1, -1).astype(F32)
    bf = lambda a: a.astype(BF16)

    x2d = x.reshape(t, d)
    kvm = _memkv(mem.reshape(-1, d), row(norm_mem[0]), bf(w_xkv[0]))

    width = w_in.shape[2] // 5
    bsp_full = jnp.repeat(b_spatial[0].T, width // SGU_GROUPS, axis=1)
    sgu, q, k, v, kmean = _inproj(x2d, row(norm_mix[0]), bf(w_in[0]), row(sgu_norm[0]),
                                  w_spatial[0], bsp_full, tm_in)

    rel_t = rel_bias.T.astype(F32)
    own, prev = _relbias(rel_t, MOBA_BLOCK)
    att = _attn(rel_t[:, NUM_BUCKETS - 1], q, k, v,
                kmean.reshape(batch, seq // MOBA_BLOCK, width), own, prev, batch, seq)

    lane_pad = LANE - N_GROUPS - N_EXPERTS
    w_router = jnp.concatenate([w_router_group[0], w_router_expert[0], jnp.zeros((d, lane_pad), F32)], axis=1)
    b_router = jnp.concatenate([b_router_group[0], b_router_expert[0], jnp.zeros((lane_pad,), F32)])
    wr_hi = bf(w_router)
    wr_lo = bf(w_router - wr_hi.astype(F32))
    consts = [row(norm_mix[0]), bf(w_gate[0]), row(b_gate[0]), bf(w_branch_a[0]), bf(w_branch_b[0]),
              bf(w_o[0]), row(norm_x[0]), bf(w_xq[0]), bf(w_xo[0]), row(norm_ffn[0]),
              jnp.concatenate([wr_hi, wr_hi, wr_lo], axis=0), row(b_router)]
    h2, xn3, route, cnt = _post(x2d, sgu, att, kvm, consts, tm_post, seq)

    counts = cnt[0, :N_EXPERTS].astype(jnp.int32)
    padded = (counts + MOE_BLOCK - 1) // MOE_BLOCK * MOE_BLOCK
    pad_ends = jnp.cumsum(padded)
    pad_starts = pad_ends - padded
    nblk = -(-(2 * t) // MOE_BLOCK) + N_EXPERTS
    assert nblk % 2 == 0
    nphase = nblk + 2
    blk_idx = jnp.arange(nphase, dtype=jnp.int32)
    blk_expert = jnp.minimum(
        jnp.sum((pad_ends[None, :] <= blk_idx[:, None] * MOE_BLOCK).astype(jnp.int32), axis=1),
        N_EXPERTS - 1)
    nused = pad_ends[-1] // MOE_BLOCK
    prev_expert = jnp.concatenate([jnp.full((1,), -1, jnp.int32), blk_expert[:-1]])
    first = ((blk_idx < nused) & (blk_expert != prev_expert)).astype(jnp.int32)
    eidx = jnp.arange(N_EXPERTS, dtype=jnp.int32)
    later = (eidx[None, :] > eidx[:, None]) & (padded[None, :] > 0)
    next_expert = jnp.min(jnp.where(later, eidx[None, :], N_EXPERTS), axis=1)
    next_expert = jnp.where(next_expert == N_EXPERTS, -1, next_expert)
    onehot = (blk_expert[:, None] == eidx[None, :]).astype(jnp.int32)
    nxt = jnp.where(first == 1, jnp.sum(onehot * next_expert[None, :], axis=1), -1).astype(jnp.int32)
    seg_left = jnp.sum(onehot * (pad_starts + counts)[None, :], axis=1) - blk_idx * MOE_BLOCK
    nvalid = jnp.where(blk_idx < nused, jnp.clip(seg_left, 0, MOE_BLOCK), 0).astype(jnp.int32)[:nblk]

    assert 2 * t <= 1 << RANK_BITS
    er = jnp.concatenate([route[:, 0:2], route[:, 4:6]], axis=1).T.astype(jnp.int32)
    code = ((er[0:2] << RANK_BITS) | er[2:4]).reshape(2 * t)
    inv = _sc_invert(pad_starts, code, nblk * MOE_BLOCK)[:, 0].reshape(nblk, MOE_BLOCK)
    row_in_blk = jnp.arange(MOE_BLOCK, dtype=jnp.int32)[None, :]
    pad_ids = _pad_pair_id(jnp.arange(nblk, dtype=jnp.int32)[:, None] * MOE_BLOCK + row_in_blk, t)
    inv = jnp.where(row_in_blk < nvalid[:, None], inv, pad_ids).reshape(nblk * MOE_BLOCK)
    dummy = lambda m: _pad_pair_id(m * MOE_BLOCK + jnp.arange(MOE_BLOCK, dtype=jnp.int32), t)
    ids = jnp.concatenate([dummy(-1), inv] + [dummy(nblk + m) for m in range(3)])
    ypairs = _expert(blk_expert, first, nxt, ids.reshape(-1, 1, 2 * MOE_BLOCK), xn3,
                     w_e_gate[0], w_e_up[0], w_e_down[0])
    out = _combine(h2, ypairs, route, row(norm_final), tm_post)
    return out.reshape(batch, seq, d)
```

```python
import functools
import math

import numpy as np
import jax
import jax.numpy as jnp
from jax import lax
from jax.experimental import pallas as pl
from jax.experimental.pallas import tpu as pltpu
from jax.experimental.pallas import tpu_sc as plsc

F32 = jnp.float32
BF16 = jnp.bfloat16

EPS = 1e-6
NEG = -1e30
LOG2E = math.log2(math.e)
LANE = 128
ONES_ROWS = 16
RANK_BITS = 20
SC_WINDOW = 128

SGU_GROUPS = 8
CHUNK = 128
ATT_HEADS = 8
ATT_HEAD_DIM = 128
MOBA_BLOCK = 256
MOBA_TOPK = 3
NUM_BUCKETS = 32
MAX_DISTANCE = 128
X_HEADS = 4
X_HEAD_DIM = 128
N_GROUPS = 8
EXPERTS_PER_GROUP = 8
N_EXPERTS = N_GROUPS * EXPERTS_PER_GROUP
MOE_BLOCK = 256

VMEM_LIMIT = 56 * 1024 * 1024

_NT = (((1,), (1,)), ((), ()))


def _rms(x, g):
    return x * lax.rsqrt(jnp.mean(x * x, axis=-1, keepdims=True) + EPS) * g


def _gelu(x):
    c = math.sqrt(2.0 / math.pi)
    return x * (0.5 * (1.0 + jnp.tanh(c * (x + 0.044715 * (x * x * x)))))


def _dot(a, b):
    return jnp.dot(a, b, preferred_element_type=F32)


def _store_tile_rows(ref, first_row, val, lead=()):
    n, d = val.shape
    k = d // LANE
    for j in range(k):
        ref[lead + (pl.ds(first_row * k + j, n, stride=k), slice(None))] = val[:, j * LANE:(j + 1) * LANE]


def _load_tile_rows(ref, n, k, lead=()):
    return jnp.concatenate([ref[lead + (pl.ds(j, n, stride=k), slice(None))] for j in range(k)], axis=1)


def _const_spec(shape):
    nd = len(shape)
    return pl.BlockSpec(shape, lambda *_: (0,) * nd, pipeline_mode=pl.Buffered(1))


def _params(sem):
    return pltpu.CompilerParams(dimension_semantics=sem, vmem_limit_bytes=VMEM_LIMIT)


def _memkv_kernel(mem_ref, g_ref, w_ref, o_ref):
    mn = _rms(mem_ref[...], g_ref[...]).astype(BF16)
    o_ref[...] = _dot(mn, w_ref[...]).astype(BF16)


def _memkv(mem2d, g, w_bf):
    n, d = mem2d.shape
    tm = 256
    return pl.pallas_call(
        _memkv_kernel,
        out_shape=jax.ShapeDtypeStruct((n, w_bf.shape[1]), BF16),
        grid=(n // tm,),
        in_specs=[pl.BlockSpec((tm, d), lambda i: (i, 0)),
                  _const_spec(g.shape), _const_spec(w_bf.shape)],
        out_specs=pl.BlockSpec((tm, w_bf.shape[1]), lambda i: (i, 0)),
        compiler_params=_params(("parallel",)),
        name="memkv",
    )(mem2d, g, w_bf)


def _inproj_kernel(x_ref, g_ref, win_ref, sg_ref, wsp_ref, bsp_ref,
                   sgu_ref, q_ref, k_ref, v_ref, km_ref, *, tm, width, scale):
    xn = _rms(x_ref[...], g_ref[...]).astype(BF16)
    u = _gelu(_dot(xn, win_ref[:, 0:width]))
    vv = _gelu(_dot(xn, win_ref[:, width:2 * width]))
    vn = _rms(vv, sg_ref[...]).astype(BF16)
    gd = width // SGU_GROUPS
    row = lax.broadcasted_iota(jnp.int32, (CHUNK, CHUNK), 0)
    col = lax.broadcasted_iota(jnp.int32, (CHUNK, CHUNK), 1)
    causal = col <= row
    wsp = [jnp.where(causal, wsp_ref[g], 0.0).astype(BF16) for g in range(SGU_GROUPS)]
    bias = bsp_ref[...]
    for c in range(tm // CHUNK):
        rs = slice(c * CHUNK, (c + 1) * CHUNK)
        mixed = jnp.concatenate(
            [_dot(wsp[g], vn[rs, g * gd:(g + 1) * gd]) for g in range(SGU_GROUPS)], axis=1)
        sgu_ref[rs, :] = (u[rs, :] * (mixed + bias)).astype(BF16)
    q_ref[...] = (_dot(xn, win_ref[:, 2 * width:3 * width]) * scale).astype(BF16)
    k = _dot(xn, win_ref[:, 3 * width:4 * width])
    k_ref[...] = k.astype(BF16)
    for j in range(tm // MOBA_BLOCK):
        km_ref[0, j:j + 1, :] = jnp.mean(k[j * MOBA_BLOCK:(j + 1) * MOBA_BLOCK, :], axis=0, keepdims=True)
    v_ref[...] = _dot(xn, win_ref[:, 4 * width:5 * width]).astype(BF16)


def _inproj(x2d, g, win_bf, sg, wsp, bsp_full, tm):
    t, d = x2d.shape
    width = win_bf.shape[1] // 5
    nkm = tm // MOBA_BLOCK
    row_spec = pl.BlockSpec((tm, width), lambda i: (i, 0))
    act = jax.ShapeDtypeStruct((t, width), BF16)
    return pl.pallas_call(
        functools.partial(_inproj_kernel, tm=tm, width=width, scale=ATT_HEAD_DIM ** -0.5 * LOG2E),
        out_shape=(act, act, act, act, jax.ShapeDtypeStruct((t // tm, nkm, width), F32)),
        grid=(t // tm,),
        in_specs=[pl.BlockSpec((tm, d), lambda i: (i, 0)),
                  _const_spec(g.shape), _const_spec(win_bf.shape), _const_spec(sg.shape),
                  _const_spec(wsp.shape), _const_spec(bsp_full.shape)],
        out_specs=(row_spec, row_spec, row_spec, row_spec,
                   pl.BlockSpec((1, nkm, width), lambda i: (i, 0, 0))),
        compiler_params=_params(("parallel",)),
        name="inproj",
    )(x2d, g, win_bf, sg, wsp, bsp_full)


def _bucket_uppers():
    d = np.arange(0, 4 * MAX_DISTANCE)
    max_exact = NUM_BUCKETS // 2
    nf = np.maximum(d, 1).astype(np.float32)
    large = max_exact + (np.log(nf / max_exact) / math.log(MAX_DISTANCE / max_exact)
                         * (NUM_BUCKETS - max_exact)).astype(np.int32)
    bucket = np.where(d < max_exact, d, np.minimum(large, NUM_BUCKETS - 1))
    return [int(d[bucket > b].min()) for b in range(NUM_BUCKETS - 1)]


def _relbias_kernel(rb_ref, own_ref, prev_ref, *, tq, uppers):
    h = pl.program_id(0)
    key = lax.broadcasted_iota(jnp.int32, (tq, tq), 0)
    qry = lax.broadcasted_iota(jnp.int32, (tq, tq), 1)

    def table(dist):
        val = jnp.full((tq, tq), rb_ref[h, NUM_BUCKETS - 1] * LOG2E, F32)
        for b in range(NUM_BUCKETS - 2, -1, -1):
            val = jnp.where(dist < uppers[b], rb_ref[h, b] * LOG2E, val)
        return val

    d = qry - key
    own_ref[0] = jnp.where(d >= 0, table(d), NEG)
    prev_ref[0] = table(d + tq)


def _relbias(rel_bias_t, tq):
    nh = rel_bias_t.shape[0]
    tile = jax.ShapeDtypeStruct((nh, tq, tq), F32)
    spec = pl.BlockSpec((1, tq, tq), lambda h: (h, 0, 0))
    return pl.pallas_call(
        functools.partial(_relbias_kernel, tq=tq, uppers=_bucket_uppers()),
        out_shape=(tile, tile),
        grid=(nh,),
        in_specs=[pl.BlockSpec(memory_space=pltpu.SMEM)],
        out_specs=(spec, spec),
        compiler_params=_params(("parallel",)),
        name="relbias",
    )(rel_bias_t)


def _attn_kernel(c31_ref, q_ref, k_ref, v_ref, km_ref, own_ref, prev_ref, o_ref,
                 vt_scr, add_scr, s_scr, acc_scr, *, tq, nb, hg):
    hd = ATT_HEAD_DIM
    heads = range(hg)
    cols = [slice(g * hd, (g + 1) * hd) for g in heads]
    c31 = [c31_ref[pl.program_id(1) * hg + g] * LOG2E for g in heads]
    for g in heads:
        for n in range(nb):
            vt_scr[g, n // 2, :hd, (n % 2) * tq:(n % 2 + 1) * tq] = (
                v_ref[n * tq:(n + 1) * tq, cols[g]].astype(F32).T.astype(BF16))
        vt_scr[g, :, hd:, :] = jnp.ones((nb // 2, ONES_ROWS, 2 * tq), BF16)

    km2 = []
    for g in heads:
        km = km_ref[0, :, cols[g]]
        km_hi = km.astype(BF16)
        km2.append(jnp.concatenate([km_hi, (km - km_hi.astype(F32)).astype(BF16)], axis=1))

    def part_max(s):
        return jnp.max(s.reshape(s.shape[0] // 8, 8, tq), axis=0)

    def q_tile(cur, carry):
        rows = pl.ds(pl.multiple_of(cur * tq, tq), tq)
        q = [q_ref[rows, cols[g]] for g in heads]
        blk = lax.broadcasted_iota(jnp.int32, (nb, tq), 0)
        past = blk < cur

        negm_prev = []
        for g in heads:
            gate = lax.dot_general(km2[g], jnp.concatenate([q[g], q[g]], axis=1), _NT,
                                   preferred_element_type=F32)
            gv = jnp.where(past, gate, NEG)
            sel = jnp.zeros((nb, tq), F32)
            for _ in range(MOBA_TOPK):
                mx = jnp.max(gv, axis=0, keepdims=True)
                first = jnp.min(jnp.where(gv == mx, blk, nb), axis=0, keepdims=True)
                hit = blk == first
                sel = jnp.where(hit & past, 1.0, sel)
                gv = jnp.where(hit, -jnp.inf, gv)
            negm = (sel - 1.0) * (-NEG)
            add_scr[g] = jnp.where(blk < cur - 1, negm + c31[g], NEG)
            negm_prev.append(jnp.sum(jnp.where(blk == cur - 1, negm, 0.0), axis=0, keepdims=True))

        def far_pair(i, mx8):
            out = []
            for g in heads:
                kb = k_ref[pl.ds(pl.multiple_of(i * (2 * tq), 2 * tq), 2 * tq), cols[g]]
                s2 = lax.dot_general(kb, q[g], _NT, preferred_element_type=F32)
                s_lo = s2[:tq] + add_scr[g, pl.ds(2 * i, 1), :]
                s_hi = s2[tq:] + add_scr[g, pl.ds(2 * i + 1, 1), :]
                s_scr[g, i, :tq, :] = s_lo
                s_scr[g, i, tq:, :] = s_hi
                out.append(jnp.maximum(mx8[g], jnp.maximum(part_max(s_lo), part_max(s_hi))))
            return tuple(out)

        nfar = jnp.maximum(cur - 1, 0)
        mx8 = lax.fori_loop(0, (nfar + 1) // 2, far_pair,
                            tuple(jnp.full((8, tq), NEG, F32) for _ in heads))

        def one_block(n, add_tiles, mx8):
            out = []
            for g in heads:
                kb = k_ref[pl.ds(pl.multiple_of(n * tq, tq), tq), cols[g]]
                s = lax.dot_general(kb, q[g], _NT, preferred_element_type=F32) + add_tiles[g]
                s_scr[g, n // 2, pl.ds(pl.multiple_of((n % 2) * tq, tq), tq), :] = s
                out.append(jnp.maximum(mx8[g], part_max(s)))
            return tuple(out)

        mx8 = lax.cond(cur >= 1,
                       lambda a: one_block(cur - 1, [prev_ref[g] + negm_prev[g] for g in heads], a),
                       lambda a: a, mx8)
        mx8 = one_block(cur, [own_ref[g] for g in heads], mx8)

        @pl.when(cur % 2 == 0)
        def _():
            for g in heads:
                s_scr[g, cur // 2, tq:, :] = jnp.full((tq, tq), NEG, F32)

        m = [jnp.max(mx8[g], axis=0, keepdims=True) for g in heads]
        acc_scr[...] = jnp.zeros_like(acc_scr)

        def pv_pair(i, c):
            for g in heads:
                p = jnp.exp2((s_scr[g, i] - m[g]).astype(BF16))
                acc_scr[g] += _dot(vt_scr[g, i], p)
            return c

        lax.fori_loop(0, cur // 2 + 1, pv_pair, 0)
        for g in heads:
            acc = acc_scr[g]
            o_ref[rows, cols[g]] = (acc[:hd] / acc[hd:hd + 1]).T.astype(BF16)
        return carry

    lax.fori_loop(0, nb, q_tile, 0)


def _attn(c31, q, k, v, kmean, own, prev, batch, seq):
    t, width = q.shape
    tq = MOBA_BLOCK
    nb = seq // tq
    hd = ATT_HEAD_DIM
    hg = 4
    nh = width // hd
    assert nb % 2 == 0 and nh % hg == 0
    seq_spec = pl.BlockSpec((seq, hg * hd), lambda b, h: (b, h), pipeline_mode=pl.Buffered(1))
    seq_in_spec = pl.BlockSpec((seq, hg * hd), lambda b, h: (b, h))
    tile_spec = pl.BlockSpec((hg, tq, tq), lambda b, h: (h, 0, 0))
    return pl.pallas_call(
        functools.partial(_attn_kernel, tq=tq, nb=nb, hg=hg),
        out_shape=jax.ShapeDtypeStruct((t, width), BF16),
        grid=(batch, nh // hg),
        in_specs=[pl.BlockSpec(memory_space=pltpu.SMEM),
                  seq_in_spec, seq_in_spec, seq_in_spec,
                  pl.BlockSpec((1, nb, hg * hd), lambda b, h: (b, 0, h)),
                  tile_spec, tile_spec],
        out_specs=seq_spec,
        scratch_shapes=[pltpu.VMEM((hg, nb // 2, hd + ONES_ROWS, 2 * tq), BF16),
                        pltpu.VMEM((hg, nb, tq), F32),
                        pltpu.VMEM((hg, nb // 2, 2 * tq, tq), F32),
                        pltpu.VMEM((hg, hd + ONES_ROWS, tq), F32)],
        compiler_params=_params(("parallel", "parallel")),
        name="attn",
    )(c31, q, k, v, kmean, own, prev)


def _post_kernel(x_ref, sgu_ref, att_ref, kvm_ref, nmix_ref, wg_ref, bg_ref, wa_ref, wb_ref,
                 wo_ref, nx_ref, wxq_ref, wxo_ref, nffn_ref, wr_ref, br_ref,
                 h2_ref, xn3_ref, route_ref, cnt_ref, carry_scr, *, tm, ts, d, xw):
    @pl.when(pl.program_id(0) == 0)
    def _():
        carry_scr[...] = jnp.zeros_like(carry_scr)

    lane = lax.broadcasted_iota(jnp.int32, (ts, LANE), 1)
    r = lax.broadcasted_iota(jnp.int32, (ts, ts), 0)
    c = lax.broadcasted_iota(jnp.int32, (ts, ts), 1)
    tri = jnp.where(c < r, 1.0, 0.0).astype(BF16)

    def sub_tile(rows):
        x = x_ref[rows, :]
        xn = _rms(x, nmix_ref[...]).astype(BF16)
        gates = jax.nn.sigmoid(_dot(xn, wg_ref[...]) + bg_ref[...])
        ya = _dot(sgu_ref[rows, :], wa_ref[...])
        yb = _dot(att_ref[rows, :], wb_ref[...])
        merged = (gates[:, :d] * ya + gates[:, d:] * yb).astype(BF16)
        h1 = x + _dot(merged, wo_ref[...])

        xn2 = _rms(h1, nx_ref[...]).astype(BF16)
        qx = (_dot(xn2, wxq_ref[...]) * (X_HEAD_DIM ** -0.5)).astype(BF16)
        outs = []
        for hh in range(X_HEADS):
            cs = slice(hh * X_HEAD_DIM, (hh + 1) * X_HEAD_DIM)
            vs = slice(xw + hh * X_HEAD_DIM, xw + (hh + 1) * X_HEAD_DIM)
            s = lax.dot_general(qx[:, cs], kvm_ref[:, cs], _NT, preferred_element_type=F32)
            p = jnp.exp(s - jnp.max(s, axis=-1, keepdims=True))
            o = _dot(p.astype(BF16), kvm_ref[:, vs]) / jnp.sum(p, axis=-1, keepdims=True)
            outs.append(o.astype(BF16))
        h2 = h1 + _dot(jnp.concatenate(outs, axis=1), wxo_ref[...])
        h2_ref[rows, :] = h2

        xn3 = _rms(h2, nffn_ref[...])
        _store_tile_rows(xn3_ref, rows.start, xn3)
        x_hi = xn3.astype(BF16)
        x_lo = (xn3 - x_hi.astype(F32)).astype(BF16)
        logits = _dot(jnp.concatenate([x_hi, x_lo, x_hi], axis=1), wr_ref[...]) + br_ref[...]
        is_g = lane < N_GROUPS
        gl = jnp.where(is_g, logits, -jnp.inf)
        gmax = jnp.max(gl, axis=-1, keepdims=True)
        gsel = jnp.min(jnp.where(gl == gmax, lane, LANE), axis=-1, keepdims=True)
        pg = 1.0 / jnp.sum(jnp.exp(gl - gmax), axis=-1, keepdims=True)
        eidx = lane - N_GROUPS
        in_group = (eidx >= 0) & (eidx < N_EXPERTS) & ((eidx // EXPERTS_PER_GROUP) == gsel)
        el = jnp.where(in_group, logits, -jnp.inf)
        m1 = jnp.max(el, axis=-1, keepdims=True)
        i1 = jnp.min(jnp.where(el == m1, lane, LANE), axis=-1, keepdims=True)
        el2 = jnp.where(lane == i1, -jnp.inf, el)
        m2 = jnp.max(el2, axis=-1, keepdims=True)
        i2 = jnp.min(jnp.where(el2 == m2, lane, LANE), axis=-1, keepdims=True)
        e2 = jnp.exp(m2 - m1)
        w1 = pg / (1.0 + e2)
        w2 = pg * e2 / (1.0 + e2)
        eid1 = i1 - N_GROUPS
        eid2 = i2 - N_GROUPS

        oh1f = jnp.where(lane == eid1, 1.0, 0.0)
        oh2f = jnp.where(lane == eid2, 1.0, 0.0)
        c1 = _dot(tri, oh1f.astype(BF16))
        c2 = _dot(tri, oh2f.astype(BF16))
        tot1 = jnp.sum(oh1f, axis=0, keepdims=True)
        tot2 = jnp.sum(oh2f, axis=0, keepdims=True)
        return eid1, eid2, w1, w2, oh1f, oh2f, c1, c2, tot1, tot2

    parts = [sub_tile(slice(j * ts, (j + 1) * ts)) for j in range(tm // ts)]

    carry = carry_scr[...]
    for j, (eid1, eid2, w1, w2, oh1f, oh2f, c1, c2, tot1, tot2) in enumerate(parts):
        r1 = jnp.sum(oh1f * (c1 + carry), axis=-1, keepdims=True)
        r2 = jnp.sum(oh2f * (c2 + carry + tot1), axis=-1, keepdims=True)
        carry = carry + tot1 + tot2
        route = jnp.where(lane == 0, eid1.astype(F32), 0.0)
        route = jnp.where(lane == 1, eid2.astype(F32), route)
        route = jnp.where(lane == 2, w1, route)
        route = jnp.where(lane == 3, w2, route)
        route = jnp.where(lane == 4, r1, route)
        route = jnp.where(lane == 5, r2, route)
        route_ref[j * ts:(j + 1) * ts, :] = route
    carry_scr[...] = carry
    cnt_ref[...] = carry


def _post(x2d, sgu, att, kvm, consts, tm, seq):
    t, d = x2d.shape
    mlen = kvm.shape[0] // (t // seq)
    xw = kvm.shape[1] // 2
    tiles_per_batch = seq // tm
    row = lambda w: pl.BlockSpec((tm, w), lambda i: (i, 0))
    return pl.pallas_call(
        functools.partial(_post_kernel, tm=tm, ts=min(tm, 256), d=d, xw=xw),
        out_shape=(jax.ShapeDtypeStruct((t, d), F32), jax.ShapeDtypeStruct((t * d // LANE, LANE), F32),
                   jax.ShapeDtypeStruct((t, LANE), F32), jax.ShapeDtypeStruct((1, LANE), F32)),
        grid=(t // tm,),
        in_specs=[row(d), row(sgu.shape[1]), row(att.shape[1]),
                  pl.BlockSpec((mlen, kvm.shape[1]), lambda i: (i // tiles_per_batch, 0))]
                 + [_const_spec(c.shape) for c in consts],
        out_specs=(row(d), pl.BlockSpec((tm * d // LANE, LANE), lambda i: (i, 0)), row(LANE),
                   pl.BlockSpec((1, LANE), lambda i: (0, 0))),
        scratch_shapes=[pltpu.VMEM((1, LANE), F32)],
        compiler_params=_params(("arbitrary",)),
        name="post",
    )(x2d, sgu, att, kvm, *consts)


def _pad_pair_id(row, t):
    return 2 * t + (row & (2 * MOE_BLOCK - 1))


def _invert_kernel(start_ref, count_ref, end_ref, er_ref, inv_ref, *, tm, t, n):
    i = pl.program_id(0)

    def fill(lo, hi):
        def body(k, carry):
            inv_ref[k] = _pad_pair_id(k, t)
            return carry
        lax.fori_loop(lo, hi, body, 0)

    @pl.when(i == 0)
    def _():
        for e in range(N_EXPERTS):
            fill(start_ref[e] + count_ref[e], end_ref[e])
        fill(end_ref[N_EXPERTS - 1], n)

    def put(r, carry):
        for slot in range(2):
            code = er_ref[0, 0, slot * tm + r]
            row = start_ref[code >> RANK_BITS] + (code & ((1 << RANK_BITS) - 1))
            inv_ref[row] = slot * t + i * tm + r
        return carry

    lax.fori_loop(0, tm, put, 0, unroll=8)


def _invert(pad_starts, counts, pad_ends, er3, tm, t, n):
    grid_spec = pltpu.PrefetchScalarGridSpec(
        num_scalar_prefetch=3,
        grid=(t // tm,),
        in_specs=[pl.BlockSpec((1, 1, 2 * tm), lambda i, *_: (i, 0, 0), memory_space=pltpu.SMEM)],
        out_specs=pl.BlockSpec(memory_space=pltpu.SMEM),
    )
    return pl.pallas_call(
        functools.partial(_invert_kernel, tm=tm, t=t, n=n),
        out_shape=jax.ShapeDtypeStruct((n,), jnp.int32),
        grid_spec=grid_spec,
        compiler_params=_params(("arbitrary",)),
        name="invert",
    )(pad_starts, counts, pad_ends, er3)


def _sc_invert(pad_starts, code, n):
    info = plsc.get_sparse_core_info()
    nw = info.num_cores * info.num_subcores
    lanes = info.num_lanes
    npairs = code.shape[0]
    chunk = npairs // nw
    nwin = chunk // SC_WINDOW
    assert chunk * nw == npairs and nwin * SC_WINDOW == chunk
    vals = jnp.broadcast_to(jnp.arange(npairs, dtype=jnp.int32)[:, None], (npairs, lanes))

    def body(start_hbm, code_hbm, vals_hbm, o_hbm, start_v, code_v, dest_v, vals_v):
        wid = lax.axis_index("s") * info.num_cores + lax.axis_index("c")
        base = pl.multiple_of(wid * chunk, chunk)
        pltpu.sync_copy(start_hbm, start_v)
        pltpu.sync_copy(code_hbm.at[pl.ds(base, chunk)], code_v)
        pltpu.sync_copy(vals_hbm.at[pl.ds(base, chunk)], vals_v)

        @pl.loop(0, nwin)
        def _(j):
            for l in range(0, SC_WINDOW, lanes):
                c = code_v[pl.ds(j * SC_WINDOW + l, lanes)]
                seg = plsc.load_gather(start_v, [lax.shift_right_logical(c, RANK_BITS)])
                dest_v[j, pl.ds(l, lanes)] = seg + (c & ((1 << RANK_BITS) - 1))
            pltpu.sync_copy(vals_v.at[pl.ds(j * SC_WINDOW, SC_WINDOW)], o_hbm.at[dest_v.at[j]])

    return pl.kernel(
        body,
        out_type=jax.ShapeDtypeStruct((n, lanes), jnp.int32),
        mesh=plsc.VectorSubcoreMesh(core_axis_name="c", subcore_axis_name="s"),
        scratch_types=[pltpu.VMEM((N_EXPERTS,), jnp.int32), pltpu.VMEM((chunk,), jnp.int32),
                       pltpu.VMEM((nwin, SC_WINDOW), jnp.int32), pltpu.VMEM((chunk, lanes), jnp.int32)],
        compiler_params=pltpu.CompilerParams(needs_layout_passes=False, use_tc_tiling_on_sc=False),
        name="sc_invert",
    )(pad_starts, code, vals)


def _expert_kernel(be_ref, first_ref, nxt_ref, nused_ref, ids_a_ref, ids_b_ref,
                   xn_hbm, wg_hbm, wu_hbm, wd_hbm, out_hbm,
                   xbuf, ybuf, wg_st, wu_st, wd_st, wg_bf, wu_bf, wd_bf, wsem, gsem, ssem, *, t):
    g = pl.program_id(0)
    rows = MOE_BLOCK
    k = xbuf.shape[1] // rows

    def fetch(e):
        return (pltpu.make_async_copy(wg_hbm.at[e], wg_st, wsem.at[0]),
                pltpu.make_async_copy(wu_hbm.at[e], wu_st, wsem.at[1]),
                pltpu.make_async_copy(wd_hbm.at[e], wd_st, wsem.at[2]))

    def token_of(p):
        if t & (t - 1) == 0:
            return p & (t - 1)
        return jnp.where(p >= 2 * t, p - 2 * t, jnp.where(p >= t, p - t, p))

    def gather_row(ids_ref, off, r, slot):
        tok = token_of(ids_ref[0, 0, off + r])
        return pltpu.make_async_copy(xn_hbm.at[pl.ds(pl.multiple_of(tok * k, k), k)],
                                     xbuf.at[slot, pl.ds(r * k, k)], gsem.at[slot])

    def gather_all(slot):
        return pltpu.make_async_copy(xn_hbm.at[pl.ds(0, rows * k)], xbuf.at[slot], gsem.at[slot])

    def scatter_all(slot):
        return pltpu.make_async_copy(ybuf.at[slot], out_hbm.at[pl.ds(0, rows * k)], ssem.at[slot])

    def switch_weights(m):
        @pl.when(first_ref[m] == 1)
        def _():
            for cp in fetch(be_ref[m]):
                cp.wait()
            wg_bf[...] = wg_st[...].astype(BF16)
            wu_bf[...] = wu_st[...].astype(BF16)
            wd_bf[...] = wd_st[...].astype(BF16)

            @pl.when(nxt_ref[m] >= 0)
            def _():
                for cp in fetch(nxt_ref[m]):
                    cp.start()

    def phase(slot, next_ids, next_off, prev_ids, prev_off):
        other = 1 - slot
        for r in range(rows):
            gather_row(next_ids, next_off, r, other).start(priority=r % 2)
        for r in range(rows):
            dst = prev_ids[0, 0, prev_off + r]
            pltpu.make_async_copy(ybuf.at[other, pl.ds(r * k, k)],
                                  out_hbm.at[pl.ds(pl.multiple_of(dst * k, k), k)],
                                  ssem.at[other]).start(priority=r % 2)
        xb = _load_tile_rows(xbuf, rows, k, lead=(slot,)).astype(BF16)
        hid = jax.nn.silu(_dot(xb, wg_bf[...])) * _dot(xb, wu_bf[...])
        _store_tile_rows(ybuf, 0, _dot(hid.astype(BF16), wd_bf[...]), lead=(slot,))

    @pl.when(g == 0)
    def _():
        for cp in fetch(be_ref[0]):
            cp.start()

        def first_block(r, carry):
            gather_row(ids_a_ref, rows, r, 0).start()
            return carry
        lax.fori_loop(0, rows, first_block, 0)
        ybuf[...] = jnp.zeros_like(ybuf)
        for half in range(2):
            cp = pltpu.make_async_copy(
                ybuf.at[half], out_hbm.at[pl.ds((2 * t + half * rows) * k, rows * k)], ssem.at[half])
            cp.start()
            cp.wait()

    nused = nused_ref[0]

    def run_phase(slot, m, off):
        @pl.when(m <= nused)
        def _():
            switch_weights(m)
            gather_all(slot).wait()

            @pl.when(m >= 1)
            def _():
                scatter_all(slot).wait()
            phase(slot, ids_b_ref, off, ids_a_ref, off)

            @pl.when(m == nused)
            def _():
                gather_all(1 - slot).wait()
                scatter_all(1 - slot).wait()

    run_phase(0, 2 * g, 0)
    run_phase(1, 2 * g + 1, rows)


def _expert(blk_expert, first, nxt, nused, ids, xn3, w_gate, w_up, w_down):
    d, eh = w_gate.shape[1:]
    k = d // LANE
    t = xn3.shape[0] // k
    steps = ids.shape[0] - 1
    assert blk_expert.shape[0] == 2 * steps
    hbm = pl.BlockSpec(memory_space=pl.ANY)
    idx_block = lambda f: pl.BlockSpec((1, 1, 2 * MOE_BLOCK), f, memory_space=pltpu.SMEM)
    grid_spec = pltpu.PrefetchScalarGridSpec(
        num_scalar_prefetch=4,
        grid=(steps,),
        in_specs=[idx_block(lambda g, *_: (g, 0, 0)), idx_block(lambda g, *_: (g + 1, 0, 0)),
                  hbm, hbm, hbm, hbm],
        out_specs=hbm,
        scratch_shapes=[pltpu.VMEM((2, MOE_BLOCK * k, LANE), F32), pltpu.VMEM((2, MOE_BLOCK * k, LANE), F32),
                        pltpu.VMEM((d, eh), F32), pltpu.VMEM((d, eh), F32), pltpu.VMEM((eh, d), F32),
                        pltpu.VMEM((d, eh), BF16), pltpu.VMEM((d, eh), BF16), pltpu.VMEM((eh, d), BF16),
                        pltpu.SemaphoreType.DMA((3,)), pltpu.SemaphoreType.DMA((2,)),
                        pltpu.SemaphoreType.DMA((2,))],
    )
    return pl.pallas_call(
        functools.partial(_expert_kernel, t=t),
        out_shape=jax.ShapeDtypeStruct(((2 * t + 2 * MOE_BLOCK) * k, LANE), F32),
        grid_spec=grid_spec,
        compiler_params=_params(("arbitrary",)),
        name="expert",
    )(blk_expert, first, nxt, nused, ids, ids, xn3, w_gate, w_up, w_down)


def _combine_kernel(h2_ref, y0_ref, y1_ref, route_ref, g_ref, o_ref):
    tm, d = h2_ref.shape
    route = route_ref[...]
    y0 = _load_tile_rows(y0_ref, tm, d // LANE)
    y1 = _load_tile_rows(y1_ref, tm, d // LANE)
    y = route[:, 2:3] * y0 + route[:, 3:4] * y1
    o_ref[...] = _rms(h2_ref[...] + y, g_ref[...])


def _combine(h2, ypairs, route, g, tm):
    t, d = h2.shape
    k = d // LANE
    assert ypairs.shape[0] % (tm * k) == 0
    return pl.pallas_call(
        _combine_kernel,
        out_shape=jax.ShapeDtypeStruct((t, d), F32),
        grid=(t // tm,),
        in_specs=[pl.BlockSpec((tm, d), lambda i: (i, 0)),
                  pl.BlockSpec((tm * k, LANE), lambda i: (i, 0)),
                  pl.BlockSpec((tm * k, LANE), lambda i: (i + t // tm, 0)),
                  pl.BlockSpec((tm, LANE), lambda i: (i, 0)),
                  _const_spec(g.shape)],
        out_specs=pl.BlockSpec((tm, d), lambda i: (i, 0)),
        compiler_params=_params(("parallel",)),
        name="combine",
    )(h2, ypairs, ypairs, route, g)


def kernel(x, mem, norm_mix, w_in, w_gate, b_gate, sgu_norm, w_spatial, b_spatial, w_branch_a,
           w_branch_b, rel_bias, w_o, norm_x, norm_mem, w_xq, w_xkv, w_xo, norm_ffn,
           w_router_group, b_router_group, w_router_expert, b_router_expert,
           w_e_gate, w_e_up, w_e_down, norm_final):
    batch, seq, d = x.shape
    assert norm_mix.shape[0] == 1, "one layer"
    assert seq % MOBA_BLOCK == 0 and d % LANE == 0
    t = batch * seq
    tm_in, tm_post, tm_moe = 512, 512, 1024
    row = lambda a: a.reshape(1, -1).astype(F32)
    bf = lambda a: a.astype(BF16)

    x2d = x.reshape(t, d)
    kvm = _memkv(mem.reshape(-1, d), row(norm_mem[0]), bf(w_xkv[0]))

    width = w_in.shape[2] // 5
    bsp_full = jnp.repeat(b_spatial[0].T, width // SGU_GROUPS, axis=1)
    sgu, q, k, v, kmean = _inproj(x2d, row(norm_mix[0]), bf(w_in[0]), row(sgu_norm[0]),
                                  w_spatial[0], bsp_full, tm_in)

    rel_t = rel_bias.T.astype(F32)
    own, prev = _relbias(rel_t, MOBA_BLOCK)
    att = _attn(rel_t[:, NUM_BUCKETS - 1], q, k, v,
                kmean.reshape(batch, seq // MOBA_BLOCK, width), own, prev, batch, seq)

    lane_pad = LANE - N_GROUPS - N_EXPERTS
    w_router = jnp.concatenate([w_router_group[0], w_router_expert[0], jnp.zeros((d, lane_pad), F32)], axis=1)
    b_router = jnp.concatenate([b_router_group[0], b_router_expert[0], jnp.zeros((lane_pad,), F32)])
    wr_hi = bf(w_router)
    wr_lo = bf(w_router - wr_hi.astype(F32))
    consts = [row(norm_mix[0]), bf(w_gate[0]), row(b_gate[0]), bf(w_branch_a[0]), bf(w_branch_b[0]),
              bf(w_o[0]), row(norm_x[0]), bf(w_xq[0]), bf(w_xo[0]), row(norm_ffn[0]),
              jnp.concatenate([wr_hi, wr_hi, wr_lo], axis=0), row(b_router)]
    h2, xn3, route, cnt = _post(x2d, sgu, att, kvm, consts, tm_post, seq)

    counts = cnt[0, :N_EXPERTS].astype(jnp.int32)
    padded = (counts + MOE_BLOCK - 1) // MOE_BLOCK * MOE_BLOCK
    pad_ends = jnp.cumsum(padded)
    pad_starts = pad_ends - padded
    nblk = -(-(2 * t) // MOE_BLOCK) + N_EXPERTS
    assert nblk % 2 == 0
    nphase = nblk + 2
    blk_idx = jnp.arange(nphase, dtype=jnp.int32)
    blk_expert = jnp.minimum(
        jnp.sum((pad_ends[None, :] <= blk_idx[:, None] * MOE_BLOCK).astype(jnp.int32), axis=1),
        N_EXPERTS - 1)
    nused = pad_ends[-1] // MOE_BLOCK
    prev_expert = jnp.concatenate([jnp.full((1,), -1, jnp.int32), blk_expert[:-1]])
    first = ((blk_idx < nused) & (blk_expert != prev_expert)).astype(jnp.int32)
    eidx = jnp.arange(N_EXPERTS, dtype=jnp.int32)
    later = (eidx[None, :] > eidx[:, None]) & (padded[None, :] > 0)
    next_expert = jnp.min(jnp.where(later, eidx[None, :], N_EXPERTS), axis=1)
    next_expert = jnp.where(next_expert == N_EXPERTS, -1, next_expert)
    onehot = (blk_expert[:, None] == eidx[None, :]).astype(jnp.int32)
    nxt = jnp.where(first == 1, jnp.sum(onehot * next_expert[None, :], axis=1), -1).astype(jnp.int32)
    seg_left = jnp.sum(onehot * (pad_starts + counts)[None, :], axis=1) - blk_idx * MOE_BLOCK
    nvalid = jnp.where(blk_idx < nused, jnp.clip(seg_left, 0, MOE_BLOCK), 0).astype(jnp.int32)[:nblk]

    assert 2 * t <= 1 << RANK_BITS
    er = jnp.concatenate([route[:, 0:2], route[:, 4:6]], axis=1).T.astype(jnp.int32)
    code = ((er[0:2] << RANK_BITS) | er[2:4]).reshape(2 * t)
    inv = _sc_invert(pad_starts, code, nblk * MOE_BLOCK)[:, 0].reshape(nblk, MOE_BLOCK)
    row_in_blk = jnp.arange(MOE_BLOCK, dtype=jnp.int32)[None, :]
    pad_ids = _pad_pair_id(jnp.arange(nblk, dtype=jnp.int32)[:, None] * MOE_BLOCK + row_in_blk, t)
    inv = jnp.where(row_in_blk < nvalid[:, None], inv, pad_ids).reshape(nblk * MOE_BLOCK)
    dummy = lambda m: _pad_pair_id(m * MOE_BLOCK + jnp.arange(MOE_BLOCK, dtype=jnp.int32), t)
    ids = jnp.concatenate([dummy(-1), inv] + [dummy(nblk + m) for m in range(3)])
    ypairs = _expert(blk_expert, first, nxt, nused.reshape(1), ids.reshape(-1, 1, 2 * MOE_BLOCK), xn3,
                     w_e_gate[0], w_e_up[0], w_e_down[0])
    out = _combine(h2, ypairs, route, row(norm_final), tm_post)
    return out.reshape(batch, seq, d)
```

```python
import functools
import math

import numpy as np
import jax
import jax.numpy as jnp
from jax import lax
from jax.experimental import pallas as pl
from jax.experimental.pallas import tpu as pltpu
from jax.experimental.pallas import tpu_sc as plsc

F32 = jnp.float32
BF16 = jnp.bfloat16

EPS = 1e-6
NEG = -1e30
LOG2E = math.log2(math.e)
LANE = 128
ONES_ROWS = 16
RANK_BITS = 20
SC_WINDOW = 128

SGU_GROUPS = 8
CHUNK = 128
ATT_HEADS = 8
ATT_HEAD_DIM = 128
MOBA_BLOCK = 256
MOBA_TOPK = 3
NUM_BUCKETS = 32
MAX_DISTANCE = 128
X_HEADS = 4
X_HEAD_DIM = 128
N_GROUPS = 8
EXPERTS_PER_GROUP = 8
N_EXPERTS = N_GROUPS * EXPERTS_PER_GROUP
MOE_BLOCK = 256

VMEM_LIMIT = 56 * 1024 * 1024

_NT = (((1,), (1,)), ((), ()))


def _rms(x, g):
    return x * lax.rsqrt(jnp.mean(x * x, axis=-1, keepdims=True) + EPS) * g


def _gelu(x):
    c = math.sqrt(2.0 / math.pi)
    return x * (0.5 * (1.0 + jnp.tanh(c * (x + 0.044715 * (x * x * x)))))


def _dot(a, b):
    return jnp.dot(a, b, preferred_element_type=F32)


def _store_tile_rows(ref, first_row, val, lead=()):
    n, d = val.shape
    k = d // LANE
    for j in range(k):
        ref[lead + (pl.ds(first_row * k + j, n, stride=k), slice(None))] = val[:, j * LANE:(j + 1) * LANE]


def _load_tile_rows(ref, n, k, lead=()):
    return jnp.concatenate([ref[lead + (pl.ds(j, n, stride=k), slice(None))] for j in range(k)], axis=1)


def _const_spec(shape):
    nd = len(shape)
    return pl.BlockSpec(shape, lambda *_: (0,) * nd, pipeline_mode=pl.Buffered(1))


def _params(sem):
    return pltpu.CompilerParams(dimension_semantics=sem, vmem_limit_bytes=VMEM_LIMIT)


def _memkv_kernel(mem_ref, g_ref, w_ref, o_ref):
    mn = _rms(mem_ref[...], g_ref[...]).astype(BF16)
    o_ref[...] = _dot(mn, w_ref[...]).astype(BF16)


def _memkv(mem2d, g, w_bf):
    n, d = mem2d.shape
    tm = 256
    return pl.pallas_call(
        _memkv_kernel,
        out_shape=jax.ShapeDtypeStruct((n, w_bf.shape[1]), BF16),
        grid=(n // tm,),
        in_specs=[pl.BlockSpec((tm, d), lambda i: (i, 0)),
                  _const_spec(g.shape), _const_spec(w_bf.shape)],
        out_specs=pl.BlockSpec((tm, w_bf.shape[1]), lambda i: (i, 0)),
        compiler_params=_params(("parallel",)),
        name="memkv",
    )(mem2d, g, w_bf)


def _inproj_kernel(x_ref, g_ref, win_ref, sg_ref, wsp_ref, bsp_ref,
                   sgu_ref, q_ref, k_ref, v_ref, km_ref, *, tm, width, scale):
    xn = _rms(x_ref[...], g_ref[...]).astype(BF16)
    u = _gelu(_dot(xn, win_ref[:, 0:width]))
    vv = _gelu(_dot(xn, win_ref[:, width:2 * width]))
    vn = _rms(vv, sg_ref[...]).astype(BF16)
    gd = width // SGU_GROUPS
    row = lax.broadcasted_iota(jnp.int32, (CHUNK, CHUNK), 0)
    col = lax.broadcasted_iota(jnp.int32, (CHUNK, CHUNK), 1)
    causal = col <= row
    wsp = [jnp.where(causal, wsp_ref[g], 0.0).astype(BF16) for g in range(SGU_GROUPS)]
    bias = bsp_ref[...]
    for c in range(tm // CHUNK):
        rs = slice(c * CHUNK, (c + 1) * CHUNK)
        mixed = jnp.concatenate(
            [_dot(wsp[g], vn[rs, g * gd:(g + 1) * gd]) for g in range(SGU_GROUPS)], axis=1)
        sgu_ref[rs, :] = (u[rs, :] * (mixed + bias)).astype(BF16)
    q_ref[...] = (_dot(xn, win_ref[:, 2 * width:3 * width]) * scale).astype(BF16)
    k = _dot(xn, win_ref[:, 3 * width:4 * width])
    k_ref[...] = k.astype(BF16)
    for j in range(tm // MOBA_BLOCK):
        km_ref[0, j:j + 1, :] = jnp.mean(k[j * MOBA_BLOCK:(j + 1) * MOBA_BLOCK, :], axis=0, keepdims=True)
    v_ref[...] = _dot(xn, win_ref[:, 4 * width:5 * width]).astype(BF16)


def _inproj(x2d, g, win_bf, sg, wsp, bsp_full, tm):
    t, d = x2d.shape
    width = win_bf.shape[1] // 5
    nkm = tm // MOBA_BLOCK
    row_spec = pl.BlockSpec((tm, width), lambda i: (i, 0))
    act = jax.ShapeDtypeStruct((t, width), BF16)
    return pl.pallas_call(
        functools.partial(_inproj_kernel, tm=tm, width=width, scale=ATT_HEAD_DIM ** -0.5 * LOG2E),
        out_shape=(act, act, act, act, jax.ShapeDtypeStruct((t // tm, nkm, width), F32)),
        grid=(t // tm,),
        in_specs=[pl.BlockSpec((tm, d), lambda i: (i, 0)),
                  _const_spec(g.shape), _const_spec(win_bf.shape), _const_spec(sg.shape),
                  _const_spec(wsp.shape), _const_spec(bsp_full.shape)],
        out_specs=(row_spec, row_spec, row_spec, row_spec,
                   pl.BlockSpec((1, nkm, width), lambda i: (i, 0, 0))),
        compiler_params=_params(("parallel",)),
        name="inproj",
    )(x2d, g, win_bf, sg, wsp, bsp_full)


def _bucket_uppers():
    d = np.arange(0, 4 * MAX_DISTANCE)
    max_exact = NUM_BUCKETS // 2
    nf = np.maximum(d, 1).astype(np.float32)
    large = max_exact + (np.log(nf / max_exact) / math.log(MAX_DISTANCE / max_exact)
                         * (NUM_BUCKETS - max_exact)).astype(np.int32)
    bucket = np.where(d < max_exact, d, np.minimum(large, NUM_BUCKETS - 1))
    return [int(d[bucket > b].min()) for b in range(NUM_BUCKETS - 1)]


def _relbias_kernel(rb_ref, own_ref, prev_ref, *, tq, uppers):
    h = pl.program_id(0)
    key = lax.broadcasted_iota(jnp.int32, (tq, tq), 0)
    qry = lax.broadcasted_iota(jnp.int32, (tq, tq), 1)

    def table(dist):
        val = jnp.full((tq, tq), rb_ref[h, NUM_BUCKETS - 1] * LOG2E, F32)
        for b in range(NUM_BUCKETS - 2, -1, -1):
            val = jnp.where(dist < uppers[b], rb_ref[h, b] * LOG2E, val)
        return val

    d = qry - key
    own_ref[0] = jnp.where(d >= 0, table(d), NEG)
    prev_ref[0] = table(d + tq)


def _relbias(rel_bias_t, tq):
    nh = rel_bias_t.shape[0]
    tile = jax.ShapeDtypeStruct((nh, tq, tq), F32)
    spec = pl.BlockSpec((1, tq, tq), lambda h: (h, 0, 0))
    return pl.pallas_call(
        functools.partial(_relbias_kernel, tq=tq, uppers=_bucket_uppers()),
        out_shape=(tile, tile),
        grid=(nh,),
        in_specs=[pl.BlockSpec(memory_space=pltpu.SMEM)],
        out_specs=(spec, spec),
        compiler_params=_params(("parallel",)),
        name="relbias",
    )(rel_bias_t)


def _attn_kernel(c31_ref, q_ref, k_ref, v_ref, km_ref, own_ref, prev_ref, o_ref,
                 vt_scr, add_scr, s_scr, p_scr, acc_scr, *, tq, nb, hg):
    hd = ATT_HEAD_DIM
    heads = range(hg)
    cols = [slice(g * hd, (g + 1) * hd) for g in heads]
    c31 = [c31_ref[pl.program_id(1) * hg + g] * LOG2E for g in heads]
    for g in heads:
        for n in range(nb):
            vt_scr[g, n // 2, :hd, (n % 2) * tq:(n % 2 + 1) * tq] = (
                v_ref[n * tq:(n + 1) * tq, cols[g]].astype(F32).T.astype(BF16))
        vt_scr[g, :, hd:, :] = jnp.ones((nb // 2, ONES_ROWS, 2 * tq), BF16)

    km2 = []
    for g in heads:
        km = km_ref[0, :, cols[g]]
        km_hi = km.astype(BF16)
        km2.append(jnp.concatenate([km_hi, (km - km_hi.astype(F32)).astype(BF16)], axis=1))

    def part_max(s):
        return jnp.max(s.reshape(s.shape[0] // 8, 8, tq), axis=0)

    def q_tile(cur, carry):
        rows = pl.ds(pl.multiple_of(cur * tq, tq), tq)
        q = [q_ref[rows, cols[g]] for g in heads]
        blk = lax.broadcasted_iota(jnp.int32, (nb, tq), 0)
        past = blk < cur

        negm_prev = []
        for g in heads:
            gate = lax.dot_general(km2[g], jnp.concatenate([q[g], q[g]], axis=1), _NT,
                                   preferred_element_type=F32)
            gv = jnp.where(past, gate, NEG)
            sel = jnp.zeros((nb, tq), F32)
            for _ in range(MOBA_TOPK):
                mx = jnp.max(gv, axis=0, keepdims=True)
                first = jnp.min(jnp.where(gv == mx, blk, nb), axis=0, keepdims=True)
                hit = blk == first
                sel = jnp.where(hit & past, 1.0, sel)
                gv = jnp.where(hit, -jnp.inf, gv)
            negm = (sel - 1.0) * (-NEG)
            add_scr[g] = jnp.where(blk < cur - 1, negm + c31[g], NEG)
            negm_prev.append(jnp.sum(jnp.where(blk == cur - 1, negm, 0.0), axis=0, keepdims=True))

        def far_pair(i, mx8):
            out = []
            for g in heads:
                kb = k_ref[pl.ds(pl.multiple_of(i * (2 * tq), 2 * tq), 2 * tq), cols[g]]
                s2 = lax.dot_general(kb, q[g], _NT, preferred_element_type=F32)
                s_lo = s2[:tq] + add_scr[g, pl.ds(2 * i, 1), :]
                s_hi = s2[tq:] + add_scr[g, pl.ds(2 * i + 1, 1), :]
                s_scr[g, i, :tq, :] = s_lo
                s_scr[g, i, tq:, :] = s_hi
                out.append(jnp.maximum(mx8[g], jnp.maximum(part_max(s_lo), part_max(s_hi))))
            return tuple(out)

        nfar = jnp.maximum(cur - 1, 0)
        mx8 = lax.fori_loop(0, (nfar + 1) // 2, far_pair,
                            tuple(jnp.full((8, tq), NEG, F32) for _ in heads))

        def one_block(n, add_tiles, mx8):
            out = []
            for g in heads:
                kb = k_ref[pl.ds(pl.multiple_of(n * tq, tq), tq), cols[g]]
                s = lax.dot_general(kb, q[g], _NT, preferred_element_type=F32) + add_tiles[g]
                s_scr[g, n // 2, pl.ds(pl.multiple_of((n % 2) * tq, tq), tq), :] = s
                out.append(jnp.maximum(mx8[g], part_max(s)))
            return tuple(out)

        mx8 = lax.cond(cur >= 1,
                       lambda a: one_block(cur - 1, [prev_ref[g] + negm_prev[g] for g in heads], a),
                       lambda a: a, mx8)
        mx8 = one_block(cur, [own_ref[g] for g in heads], mx8)

        @pl.when(cur % 2 == 0)
        def _():
            for g in heads:
                s_scr[g, cur // 2, tq:, :] = jnp.full((tq, tq), NEG, F32)

        m = [jnp.max(mx8[g], axis=0, keepdims=True) for g in heads]
        acc_scr[...] = jnp.zeros_like(acc_scr)

        def exp_pair(i, buf):
            for g in heads:
                p_scr[g, buf] = jnp.exp2((s_scr[g, i] - m[g]).astype(BF16))

        def pv_pair(i, buf):
            for g in heads:
                acc_scr[g] += _dot(vt_scr[g, i], p_scr[g, buf])

        def pv_step(j, c):
            pv_pair(2 * j, 0)
            exp_pair(2 * j + 1, 1)
            pv_pair(2 * j + 1, 1)
            exp_pair(2 * j + 2, 0)
            return c

        last = cur // 2
        exp_pair(0, 0)
        lax.fori_loop(0, last // 2, pv_step, 0)
        pv_pair(2 * (last // 2), 0)

        @pl.when(last % 2 == 1)
        def _():
            exp_pair(last, 1)
            pv_pair(last, 1)
        for g in heads:
            acc = acc_scr[g]
            o_ref[rows, cols[g]] = (acc[:hd] / acc[hd:hd + 1]).T.astype(BF16)
        return carry

    lax.fori_loop(0, nb, q_tile, 0)


def _attn(c31, q, k, v, kmean, own, prev, batch, seq):
    t, width = q.shape
    tq = MOBA_BLOCK
    nb = seq // tq
    hd = ATT_HEAD_DIM
    hg = 4
    nh = width // hd
    assert nb % 2 == 0 and nh % hg == 0
    seq_spec = pl.BlockSpec((seq, hg * hd), lambda b, h: (b, h), pipeline_mode=pl.Buffered(1))
    seq_in_spec = pl.BlockSpec((seq, hg * hd), lambda b, h: (b, h))
    tile_spec = pl.BlockSpec((hg, tq, tq), lambda b, h: (h, 0, 0))
    return pl.pallas_call(
        functools.partial(_attn_kernel, tq=tq, nb=nb, hg=hg),
        out_shape=jax.ShapeDtypeStruct((t, width), BF16),
        grid=(batch, nh // hg),
        in_specs=[pl.BlockSpec(memory_space=pltpu.SMEM),
                  seq_in_spec, seq_in_spec, seq_in_spec,
                  pl.BlockSpec((1, nb, hg * hd), lambda b, h: (b, 0, h)),
                  tile_spec, tile_spec],
        out_specs=seq_spec,
        scratch_shapes=[pltpu.VMEM((hg, nb // 2, hd + ONES_ROWS, 2 * tq), BF16),
                        pltpu.VMEM((hg, nb, tq), F32),
                        pltpu.VMEM((hg, nb // 2, 2 * tq, tq), F32),
                        pltpu.VMEM((hg, 2, 2 * tq, tq), BF16),
                        pltpu.VMEM((hg, hd + ONES_ROWS, tq), F32)],
        compiler_params=_params(("parallel", "parallel")),
        name="attn",
    )(c31, q, k, v, kmean, own, prev)


def _post_kernel(x_ref, sgu_ref, att_ref, kvm_ref, nmix_ref, wg_ref, bg_ref, wa_ref, wb_ref,
                 wo_ref, nx_ref, wxq_ref, wxo_ref, nffn_ref, wr_ref, br_ref,
                 h2_ref, xn3_ref, route_ref, cnt_ref, carry_scr, *, tm, ts, d, xw):
    @pl.when(pl.program_id(0) == 0)
    def _():
        carry_scr[...] = jnp.zeros_like(carry_scr)

    lane = lax.broadcasted_iota(jnp.int32, (ts, LANE), 1)
    r = lax.broadcasted_iota(jnp.int32, (ts, ts), 0)
    c = lax.broadcasted_iota(jnp.int32, (ts, ts), 1)
    tri = jnp.where(c < r, 1.0, 0.0).astype(BF16)

    def sub_tile(rows):
        x = x_ref[rows, :]
        xn = _rms(x, nmix_ref[...]).astype(BF16)
        gates = jax.nn.sigmoid(_dot(xn, wg_ref[...]) + bg_ref[...])
        ya = _dot(sgu_ref[rows, :], wa_ref[...])
        yb = _dot(att_ref[rows, :], wb_ref[...])
        merged = (gates[:, :d] * ya + gates[:, d:] * yb).astype(BF16)
        h1 = x + _dot(merged, wo_ref[...])

        xn2 = _rms(h1, nx_ref[...]).astype(BF16)
        qx = (_dot(xn2, wxq_ref[...]) * (X_HEAD_DIM ** -0.5)).astype(BF16)
        outs = []
        for hh in range(X_HEADS):
            cs = slice(hh * X_HEAD_DIM, (hh + 1) * X_HEAD_DIM)
            vs = slice(xw + hh * X_HEAD_DIM, xw + (hh + 1) * X_HEAD_DIM)
            s = lax.dot_general(qx[:, cs], kvm_ref[:, cs], _NT, preferred_element_type=F32)
            p = jnp.exp(s - jnp.max(s, axis=-1, keepdims=True))
            o = _dot(p.astype(BF16), kvm_ref[:, vs]) / jnp.sum(p, axis=-1, keepdims=True)
            outs.append(o.astype(BF16))
        h2 = h1 + _dot(jnp.concatenate(outs, axis=1), wxo_ref[...])
        h2_ref[rows, :] = h2

        xn3 = _rms(h2, nffn_ref[...])
        _store_tile_rows(xn3_ref, rows.start, xn3)
        x_hi = xn3.astype(BF16)
        x_lo = (xn3 - x_hi.astype(F32)).astype(BF16)
        logits = _dot(jnp.concatenate([x_hi, x_lo, x_hi], axis=1), wr_ref[...]) + br_ref[...]
        is_g = lane < N_GROUPS
        gl = jnp.where(is_g, logits, -jnp.inf)
        gmax = jnp.max(gl, axis=-1, keepdims=True)
        gsel = jnp.min(jnp.where(gl == gmax, lane, LANE), axis=-1, keepdims=True)
        pg = 1.0 / jnp.sum(jnp.exp(gl - gmax), axis=-1, keepdims=True)
        eidx = lane - N_GROUPS
        in_group = (eidx >= 0) & (eidx < N_EXPERTS) & ((eidx // EXPERTS_PER_GROUP) == gsel)
        el = jnp.where(in_group, logits, -jnp.inf)
        m1 = jnp.max(el, axis=-1, keepdims=True)
        i1 = jnp.min(jnp.where(el == m1, lane, LANE), axis=-1, keepdims=True)
        el2 = jnp.where(lane == i1, -jnp.inf, el)
        m2 = jnp.max(el2, axis=-1, keepdims=True)
        i2 = jnp.min(jnp.where(el2 == m2, lane, LANE), axis=-1, keepdims=True)
        e2 = jnp.exp(m2 - m1)
        w1 = pg / (1.0 + e2)
        w2 = pg * e2 / (1.0 + e2)
        eid1 = i1 - N_GROUPS
        eid2 = i2 - N_GROUPS

        oh1f = jnp.where(lane == eid1, 1.0, 0.0)
        oh2f = jnp.where(lane == eid2, 1.0, 0.0)
        c1 = _dot(tri, oh1f.astype(BF16))
        c2 = _dot(tri, oh2f.astype(BF16))
        tot1 = jnp.sum(oh1f, axis=0, keepdims=True)
        tot2 = jnp.sum(oh2f, axis=0, keepdims=True)
        return eid1, eid2, w1, w2, oh1f, oh2f, c1, c2, tot1, tot2

    parts = [sub_tile(slice(j * ts, (j + 1) * ts)) for j in range(tm // ts)]

    carry = carry_scr[...]
    for j, (eid1, eid2, w1, w2, oh1f, oh2f, c1, c2, tot1, tot2) in enumerate(parts):
        r1 = jnp.sum(oh1f * (c1 + carry), axis=-1, keepdims=True)
        r2 = jnp.sum(oh2f * (c2 + carry + tot1), axis=-1, keepdims=True)
        carry = carry + tot1 + tot2
        route = jnp.where(lane == 0, eid1.astype(F32), 0.0)
        route = jnp.where(lane == 1, eid2.astype(F32), route)
        route = jnp.where(lane == 2, w1, route)
        route = jnp.where(lane == 3, w2, route)
        route = jnp.where(lane == 4, r1, route)
        route = jnp.where(lane == 5, r2, route)
        route_ref[j * ts:(j + 1) * ts, :] = route
    carry_scr[...] = carry
    cnt_ref[...] = carry


def _post(x2d, sgu, att, kvm, consts, tm, seq):
    t, d = x2d.shape
    mlen = kvm.shape[0] // (t // seq)
    xw = kvm.shape[1] // 2
    tiles_per_batch = seq // tm
    row = lambda w: pl.BlockSpec((tm, w), lambda i: (i, 0))
    return pl.pallas_call(
        functools.partial(_post_kernel, tm=tm, ts=min(tm, 256), d=d, xw=xw),
        out_shape=(jax.ShapeDtypeStruct((t, d), F32), jax.ShapeDtypeStruct((t * d // LANE, LANE), F32),
                   jax.ShapeDtypeStruct((t, LANE), F32), jax.ShapeDtypeStruct((1, LANE), F32)),
        grid=(t // tm,),
        in_specs=[row(d), row(sgu.shape[1]), row(att.shape[1]),
                  pl.BlockSpec((mlen, kvm.shape[1]), lambda i: (i // tiles_per_batch, 0))]
                 + [_const_spec(c.shape) for c in consts],
        out_specs=(row(d), pl.BlockSpec((tm * d // LANE, LANE), lambda i: (i, 0)), row(LANE),
                   pl.BlockSpec((1, LANE), lambda i: (0, 0))),
        scratch_shapes=[pltpu.VMEM((1, LANE), F32)],
        compiler_params=_params(("arbitrary",)),
        name="post",
    )(x2d, sgu, att, kvm, *consts)


def _pad_pair_id(row, t):
    return 2 * t + (row & (2 * MOE_BLOCK - 1))


def _invert_kernel(start_ref, count_ref, end_ref, er_ref, inv_ref, *, tm, t, n):
    i = pl.program_id(0)

    def fill(lo, hi):
        def body(k, carry):
            inv_ref[k] = _pad_pair_id(k, t)
            return carry
        lax.fori_loop(lo, hi, body, 0)

    @pl.when(i == 0)
    def _():
        for e in range(N_EXPERTS):
            fill(start_ref[e] + count_ref[e], end_ref[e])
        fill(end_ref[N_EXPERTS - 1], n)

    def put(r, carry):
        for slot in range(2):
            code = er_ref[0, 0, slot * tm + r]
            row = start_ref[code >> RANK_BITS] + (code & ((1 << RANK_BITS) - 1))
            inv_ref[row] = slot * t + i * tm + r
        return carry

    lax.fori_loop(0, tm, put, 0, unroll=8)


def _invert(pad_starts, counts, pad_ends, er3, tm, t, n):
    grid_spec = pltpu.PrefetchScalarGridSpec(
        num_scalar_prefetch=3,
        grid=(t // tm,),
        in_specs=[pl.BlockSpec((1, 1, 2 * tm), lambda i, *_: (i, 0, 0), memory_space=pltpu.SMEM)],
        out_specs=pl.BlockSpec(memory_space=pltpu.SMEM),
    )
    return pl.pallas_call(
        functools.partial(_invert_kernel, tm=tm, t=t, n=n),
        out_shape=jax.ShapeDtypeStruct((n,), jnp.int32),
        grid_spec=grid_spec,
        compiler_params=_params(("arbitrary",)),
        name="invert",
    )(pad_starts, counts, pad_ends, er3)


def _sc_invert(pad_starts, code, n):
    info = plsc.get_sparse_core_info()
    nw = info.num_cores * info.num_subcores
    lanes = info.num_lanes
    npairs = code.shape[0]
    chunk = npairs // nw
    nwin = chunk // SC_WINDOW
    assert chunk * nw == npairs and nwin * SC_WINDOW == chunk
    vals = jnp.broadcast_to(jnp.arange(npairs, dtype=jnp.int32)[:, None], (npairs, lanes))

    def body(start_hbm, code_hbm, vals_hbm, o_hbm, start_v, code_v, dest_v, vals_v):
        wid = lax.axis_index("s") * info.num_cores + lax.axis_index("c")
        base = pl.multiple_of(wid * chunk, chunk)
        pltpu.sync_copy(start_hbm, start_v)
        pltpu.sync_copy(code_hbm.at[pl.ds(base, chunk)], code_v)
        pltpu.sync_copy(vals_hbm.at[pl.ds(base, chunk)], vals_v)

        @pl.loop(0, nwin)
        def _(j):
            for l in range(0, SC_WINDOW, lanes):
                c = code_v[pl.ds(j * SC_WINDOW + l, lanes)]
                seg = plsc.load_gather(start_v, [lax.shift_right_logical(c, RANK_BITS)])
                dest_v[j, pl.ds(l, lanes)] = seg + (c & ((1 << RANK_BITS) - 1))
            pltpu.sync_copy(vals_v.at[pl.ds(j * SC_WINDOW, SC_WINDOW)], o_hbm.at[dest_v.at[j]])

    return pl.kernel(
        body,
        out_type=jax.ShapeDtypeStruct((n, lanes), jnp.int32),
        mesh=plsc.VectorSubcoreMesh(core_axis_name="c", subcore_axis_name="s"),
        scratch_types=[pltpu.VMEM((N_EXPERTS,), jnp.int32), pltpu.VMEM((chunk,), jnp.int32),
                       pltpu.VMEM((nwin, SC_WINDOW), jnp.int32), pltpu.VMEM((chunk, lanes), jnp.int32)],
        compiler_params=pltpu.CompilerParams(needs_layout_passes=False, use_tc_tiling_on_sc=False),
        name="sc_invert",
    )(pad_starts, code, vals)


def _expert_kernel(be_ref, first_ref, nxt_ref, nused_ref, ids_a_ref, ids_b_ref,
                   xn_hbm, wg_hbm, wu_hbm, wd_hbm, out_hbm,
                   xbuf, ybuf, wg_st, wu_st, wd_st, wg_bf, wu_bf, wd_bf, wsem, gsem, ssem, *, t):
    g = pl.program_id(0)
    rows = MOE_BLOCK
    k = xbuf.shape[1] // rows

    def fetch(e):
        return (pltpu.make_async_copy(wg_hbm.at[e], wg_st, wsem.at[0]),
                pltpu.make_async_copy(wu_hbm.at[e], wu_st, wsem.at[1]),
                pltpu.make_async_copy(wd_hbm.at[e], wd_st, wsem.at[2]))

    def token_of(p):
        if t & (t - 1) == 0:
            return p & (t - 1)
        return jnp.where(p >= 2 * t, p - 2 * t, jnp.where(p >= t, p - t, p))

    def gather_row(ids_ref, off, r, slot):
        tok = token_of(ids_ref[0, 0, off + r])
        return pltpu.make_async_copy(xn_hbm.at[pl.ds(pl.multiple_of(tok * k, k), k)],
                                     xbuf.at[slot, pl.ds(r * k, k)], gsem.at[slot])

    def gather_all(slot):
        return pltpu.make_async_copy(xn_hbm.at[pl.ds(0, rows * k)], xbuf.at[slot], gsem.at[slot])

    def scatter_all(slot):
        return pltpu.make_async_copy(ybuf.at[slot], out_hbm.at[pl.ds(0, rows * k)], ssem.at[slot])

    def switch_weights(m):
        @pl.when(first_ref[m] == 1)
        def _():
            for cp in fetch(be_ref[m]):
                cp.wait()
            wg_bf[...] = wg_st[...].astype(BF16)
            wu_bf[...] = wu_st[...].astype(BF16)
            wd_bf[...] = wd_st[...].astype(BF16)

            @pl.when(nxt_ref[m] >= 0)
            def _():
                for cp in fetch(nxt_ref[m]):
                    cp.start()

    def phase(slot, next_ids, next_off, prev_ids, prev_off):
        other = 1 - slot
        for r in range(rows):
            gather_row(next_ids, next_off, r, other).start(priority=r % 2)
        for r in range(rows):
            dst = prev_ids[0, 0, prev_off + r]
            pltpu.make_async_copy(ybuf.at[other, pl.ds(r * k, k)],
                                  out_hbm.at[pl.ds(pl.multiple_of(dst * k, k), k)],
                                  ssem.at[other]).start(priority=r % 2)
        xb = _load_tile_rows(xbuf, rows, k, lead=(slot,)).astype(BF16)
        hid = jax.nn.silu(_dot(xb, wg_bf[...])) * _dot(xb, wu_bf[...])
        _store_tile_rows(ybuf, 0, _dot(hid.astype(BF16), wd_bf[...]), lead=(slot,))

    @pl.when(g == 0)
    def _():
        for cp in fetch(be_ref[0]):
            cp.start()

        def first_block(r, carry):
            gather_row(ids_a_ref, rows, r, 0).start()
            return carry
        lax.fori_loop(0, rows, first_block, 0)
        ybuf[...] = jnp.zeros_like(ybuf)
        for half in range(2):
            cp = pltpu.make_async_copy(
                ybuf.at[half], out_hbm.at[pl.ds((2 * t + half * rows) * k, rows * k)], ssem.at[half])
            cp.start()
            cp.wait()

    nused = nused_ref[0]

    def run_phase(slot, m, off):
        @pl.when(m <= nused)
        def _():
            switch_weights(m)
            gather_all(slot).wait()

            @pl.when(m >= 1)
            def _():
                scatter_all(slot).wait()
            phase(slot, ids_b_ref, off, ids_a_ref, off)

            @pl.when(m == nused)
            def _():
                gather_all(1 - slot).wait()
                scatter_all(1 - slot).wait()

    run_phase(0, 2 * g, 0)
    run_phase(1, 2 * g + 1, rows)


def _expert(blk_expert, first, nxt, nused, ids, xn3, w_gate, w_up, w_down):
    d, eh = w_gate.shape[1:]
    k = d // LANE
    t = xn3.shape[0] // k
    steps = ids.shape[0] - 1
    assert blk_expert.shape[0] == 2 * steps
    hbm = pl.BlockSpec(memory_space=pl.ANY)
    idx_block = lambda f: pl.BlockSpec((1, 1, 2 * MOE_BLOCK), f, memory_space=pltpu.SMEM)
    grid_spec = pltpu.PrefetchScalarGridSpec(
        num_scalar_prefetch=4,
        grid=(steps,),
        in_specs=[idx_block(lambda g, *_: (g, 0, 0)), idx_block(lambda g, *_: (g + 1, 0, 0)),
                  hbm, hbm, hbm, hbm],
        out_specs=hbm,
        scratch_shapes=[pltpu.VMEM((2, MOE_BLOCK * k, LANE), F32), pltpu.VMEM((2, MOE_BLOCK * k, LANE), F32),
                        pltpu.VMEM((d, eh), F32), pltpu.VMEM((d, eh), F32), pltpu.VMEM((eh, d), F32),
                        pltpu.VMEM((d, eh), BF16), pltpu.VMEM((d, eh), BF16), pltpu.VMEM((eh, d), BF16),
                        pltpu.SemaphoreType.DMA((3,)), pltpu.SemaphoreType.DMA((2,)),
                        pltpu.SemaphoreType.DMA((2,))],
    )
    return pl.pallas_call(
        functools.partial(_expert_kernel, t=t),
        out_shape=jax.ShapeDtypeStruct(((2 * t + 2 * MOE_BLOCK) * k, LANE), F32),
        grid_spec=grid_spec,
        compiler_params=_params(("arbitrary",)),
        name="expert",
    )(blk_expert, first, nxt, nused, ids, ids, xn3, w_gate, w_up, w_down)


def _combine_kernel(h2_ref, y0_ref, y1_ref, route_ref, g_ref, o_ref):
    tm, d = h2_ref.shape
    route = route_ref[...]
    y0 = _load_tile_rows(y0_ref, tm, d // LANE)
    y1 = _load_tile_rows(y1_ref, tm, d // LANE)
    y = route[:, 2:3] * y0 + route[:, 3:4] * y1
    o_ref[...] = _rms(h2_ref[...] + y, g_ref[...])


def _combine(h2, ypairs, route, g, tm):
    t, d = h2.shape
    k = d // LANE
    assert ypairs.shape[0] % (tm * k) == 0
    return pl.pallas_call(
        _combine_kernel,
        out_shape=jax.ShapeDtypeStruct((t, d), F32),
        grid=(t // tm,),
        in_specs=[pl.BlockSpec((tm, d), lambda i: (i, 0)),
                  pl.BlockSpec((tm * k, LANE), lambda i: (i, 0)),
                  pl.BlockSpec((tm * k, LANE), lambda i: (i + t // tm, 0)),
                  pl.BlockSpec((tm, LANE), lambda i: (i, 0)),
                  _const_spec(g.shape)],
        out_specs=pl.BlockSpec((tm, d), lambda i: (i, 0)),
        compiler_params=_params(("parallel",)),
        name="combine",
    )(h2, ypairs, ypairs, route, g)


def kernel(x, mem, norm_mix, w_in, w_gate, b_gate, sgu_norm, w_spatial, b_spatial, w_branch_a,
           w_branch_b, rel_bias, w_o, norm_x, norm_mem, w_xq, w_xkv, w_xo, norm_ffn,
           w_router_group, b_router_group, w_router_expert, b_router_expert,
           w_e_gate, w_e_up, w_e_down, norm_final):
    batch, seq, d = x.shape
    assert norm_mix.shape[0] == 1, "one layer"
    assert seq % MOBA_BLOCK == 0 and d % LANE == 0
    t = batch * seq
    tm_in, tm_post, tm_moe = 512, 512, 1024
    row = lambda a: a.reshape(1, -1).astype(F32)
    bf = lambda a: a.astype(BF16)

    x2d = x.reshape(t, d)
    kvm = _memkv(mem.reshape(-1, d), row(norm_mem[0]), bf(w_xkv[0]))

    width = w_in.shape[2] // 5
    bsp_full = jnp.repeat(b_spatial[0].T, width // SGU_GROUPS, axis=1)
    sgu, q, k, v, kmean = _inproj(x2d, row(norm_mix[0]), bf(w_in[0]), row(sgu_norm[0]),
                                  w_spatial[0], bsp_full, tm_in)

    rel_t = rel_bias.T.astype(F32)
    own, prev = _relbias(rel_t, MOBA_BLOCK)
    att = _attn(rel_t[:, NUM_BUCKETS - 1], q, k, v,
                kmean.reshape(batch, seq // MOBA_BLOCK, width), own, prev, batch, seq)

    lane_pad = LANE - N_GROUPS - N_EXPERTS
    w_router = jnp.concatenate([w_router_group[0], w_router_expert[0], jnp.zeros((d, lane_pad), F32)], axis=1)
    b_router = jnp.concatenate([b_router_group[0], b_router_expert[0], jnp.zeros((lane_pad,), F32)])
    wr_hi = bf(w_router)
    wr_lo = bf(w_router - wr_hi.astype(F32))
    consts = [row(norm_mix[0]), bf(w_gate[0]), row(b_gate[0]), bf(w_branch_a[0]), bf(w_branch_b[0]),
              bf(w_o[0]), row(norm_x[0]), bf(w_xq[0]), bf(w_xo[0]), row(norm_ffn[0]),
              jnp.concatenate([wr_hi, wr_hi, wr_lo], axis=0), row(b_router)]
    h2, xn3, route, cnt = _post(x2d, sgu, att, kvm, consts, tm_post, seq)

    counts = cnt[0, :N_EXPERTS].astype(jnp.int32)
    padded = (counts + MOE_BLOCK - 1) // MOE_BLOCK * MOE_BLOCK
    pad_ends = jnp.cumsum(padded)
    pad_starts = pad_ends - padded
    nblk = -(-(2 * t) // MOE_BLOCK) + N_EXPERTS
    assert nblk % 2 == 0
    nphase = nblk + 2
    blk_idx = jnp.arange(nphase, dtype=jnp.int32)
    blk_expert = jnp.minimum(
        jnp.sum((pad_ends[None, :] <= blk_idx[:, None] * MOE_BLOCK).astype(jnp.int32), axis=1),
        N_EXPERTS - 1)
    nused = pad_ends[-1] // MOE_BLOCK
    prev_expert = jnp.concatenate([jnp.full((1,), -1, jnp.int32), blk_expert[:-1]])
    first = ((blk_idx < nused) & (blk_expert != prev_expert)).astype(jnp.int32)
    eidx = jnp.arange(N_EXPERTS, dtype=jnp.int32)
    later = (eidx[None, :] > eidx[:, None]) & (padded[None, :] > 0)
    next_expert = jnp.min(jnp.where(later, eidx[None, :], N_EXPERTS), axis=1)
    next_expert = jnp.where(next_expert == N_EXPERTS, -1, next_expert)
    onehot = (blk_expert[:, None] == eidx[None, :]).astype(jnp.int32)
    nxt = jnp.where(first == 1, jnp.sum(onehot * next_expert[None, :], axis=1), -1).astype(jnp.int32)
    seg_left = jnp.sum(onehot * (pad_starts + counts)[None, :], axis=1) - blk_idx * MOE_BLOCK
    nvalid = jnp.where(blk_idx < nused, jnp.clip(seg_left, 0, MOE_BLOCK), 0).astype(jnp.int32)[:nblk]

    assert 2 * t <= 1 << RANK_BITS
    er = jnp.concatenate([route[:, 0:2], route[:, 4:6]], axis=1).T.astype(jnp.int32)
    code = ((er[0:2] << RANK_BITS) | er[2:4]).reshape(2 * t)
    inv = _sc_invert(pad_starts, code, nblk * MOE_BLOCK)[:, 0].reshape(nblk, MOE_BLOCK)
    row_in_blk = jnp.arange(MOE_BLOCK, dtype=jnp.int32)[None, :]
    pad_ids = _pad_pair_id(jnp.arange(nblk, dtype=jnp.int32)[:, None] * MOE_BLOCK + row_in_blk, t)
    inv = jnp.where(row_in_blk < nvalid[:, None], inv, pad_ids).reshape(nblk * MOE_BLOCK)
    dummy = lambda m: _pad_pair_id(m * MOE_BLOCK + jnp.arange(MOE_BLOCK, dtype=jnp.int32), t)
    ids = jnp.concatenate([dummy(-1), inv] + [dummy(nblk + m) for m in range(3)])
    ypairs = _expert(blk_expert, first, nxt, nused.reshape(1), ids.reshape(-1, 1, 2 * MOE_BLOCK), xn3,
                     w_e_gate[0], w_e_up[0], w_e_down[0])
    out = _combine(h2, ypairs, route, row(norm_final), tm_post)
    return out.reshape(batch, seq, d)
```

```python
import functools
import math

import numpy as np
import jax
import jax.numpy as jnp
from jax import lax
from jax.experimental import pallas as pl
from jax.experimental.pallas import tpu as pltpu
from jax.experimental.pallas import tpu_sc as plsc

F32 = jnp.float32
BF16 = jnp.bfloat16

EPS = 1e-6
NEG = -1e30
LOG2E = math.log2(math.e)
LANE = 128
ONES_ROWS = 16
RANK_BITS = 20
SC_WINDOW = 128

SGU_GROUPS = 8
CHUNK = 128
ATT_HEADS = 8
ATT_HEAD_DIM = 128
MOBA_BLOCK = 256
MOBA_TOPK = 3
NUM_BUCKETS = 32
MAX_DISTANCE = 128
X_HEADS = 4
X_HEAD_DIM = 128
N_GROUPS = 8
EXPERTS_PER_GROUP = 8
N_EXPERTS = N_GROUPS * EXPERTS_PER_GROUP
MOE_BLOCK = 256

VMEM_LIMIT = 56 * 1024 * 1024

_NT = (((1,), (1,)), ((), ()))


def _rms(x, g):
    return x * lax.rsqrt(jnp.mean(x * x, axis=-1, keepdims=True) + EPS) * g


def _gelu(x):
    c = math.sqrt(2.0 / math.pi)
    return x * (0.5 * (1.0 + jnp.tanh(c * (x + 0.044715 * (x * x * x)))))


def _dot(a, b):
    return jnp.dot(a, b, preferred_element_type=F32)


def _store_tile_rows(ref, first_row, val, lead=()):
    n, d = val.shape
    k = d // LANE
    for j in range(k):
        ref[lead + (pl.ds(first_row * k + j, n, stride=k), slice(None))] = val[:, j * LANE:(j + 1) * LANE]


def _load_tile_rows(ref, n, k, lead=()):
    return jnp.concatenate([ref[lead + (pl.ds(j, n, stride=k), slice(None))] for j in range(k)], axis=1)


def _const_spec(shape):
    nd = len(shape)
    return pl.BlockSpec(shape, lambda *_: (0,) * nd, pipeline_mode=pl.Buffered(1))


def _params(sem):
    return pltpu.CompilerParams(dimension_semantics=sem, vmem_limit_bytes=VMEM_LIMIT)


def _memkv_kernel(mem_ref, g_ref, w_ref, o_ref):
    mn = _rms(mem_ref[...], g_ref[...]).astype(BF16)
    o_ref[...] = _dot(mn, w_ref[...]).astype(BF16)


def _memkv(mem2d, g, w_bf):
    n, d = mem2d.shape
    tm = 256
    return pl.pallas_call(
        _memkv_kernel,
        out_shape=jax.ShapeDtypeStruct((n, w_bf.shape[1]), BF16),
        grid=(n // tm,),
        in_specs=[pl.BlockSpec((tm, d), lambda i: (i, 0)),
                  _const_spec(g.shape), _const_spec(w_bf.shape)],
        out_specs=pl.BlockSpec((tm, w_bf.shape[1]), lambda i: (i, 0)),
        compiler_params=_params(("parallel",)),
        name="memkv",
    )(mem2d, g, w_bf)


def _inproj_kernel(x_ref, g_ref, win_ref, sg_ref, wsp_ref, bsp_ref,
                   sgu_ref, q_ref, k_ref, v_ref, km_ref, *, tm, width, scale):
    xn = _rms(x_ref[...], g_ref[...]).astype(BF16)
    u = _gelu(_dot(xn, win_ref[:, 0:width]))
    vv = _gelu(_dot(xn, win_ref[:, width:2 * width]))
    vn = _rms(vv, sg_ref[...]).astype(BF16)
    gd = width // SGU_GROUPS
    row = lax.broadcasted_iota(jnp.int32, (CHUNK, CHUNK), 0)
    col = lax.broadcasted_iota(jnp.int32, (CHUNK, CHUNK), 1)
    causal = col <= row
    wsp = [jnp.where(causal, wsp_ref[g], 0.0).astype(BF16) for g in range(SGU_GROUPS)]
    bias = bsp_ref[...]
    for c in range(tm // CHUNK):
        rs = slice(c * CHUNK, (c + 1) * CHUNK)
        mixed = jnp.concatenate(
            [_dot(wsp[g], vn[rs, g * gd:(g + 1) * gd]) for g in range(SGU_GROUPS)], axis=1)
        sgu_ref[rs, :] = (u[rs, :] * (mixed + bias)).astype(BF16)
    q_ref[...] = (_dot(xn, win_ref[:, 2 * width:3 * width]) * scale).astype(BF16)
    k = _dot(xn, win_ref[:, 3 * width:4 * width])
    k_ref[...] = k.astype(BF16)
    for j in range(tm // MOBA_BLOCK):
        km_ref[0, j:j + 1, :] = jnp.mean(k[j * MOBA_BLOCK:(j + 1) * MOBA_BLOCK, :], axis=0, keepdims=True)
    v_ref[...] = _dot(xn, win_ref[:, 4 * width:5 * width]).astype(BF16)


def _inproj(x2d, g, win_bf, sg, wsp, bsp_full, tm):
    t, d = x2d.shape
    width = win_bf.shape[1] // 5
    nkm = tm // MOBA_BLOCK
    row_spec = pl.BlockSpec((tm, width), lambda i: (i, 0))
    act = jax.ShapeDtypeStruct((t, width), BF16)
    return pl.pallas_call(
        functools.partial(_inproj_kernel, tm=tm, width=width, scale=ATT_HEAD_DIM ** -0.5 * LOG2E),
        out_shape=(act, act, act, act, jax.ShapeDtypeStruct((t // tm, nkm, width), F32)),
        grid=(t // tm,),
        in_specs=[pl.BlockSpec((tm, d), lambda i: (i, 0)),
                  _const_spec(g.shape), _const_spec(win_bf.shape), _const_spec(sg.shape),
                  _const_spec(wsp.shape), _const_spec(bsp_full.shape)],
        out_specs=(row_spec, row_spec, row_spec, row_spec,
                   pl.BlockSpec((1, nkm, width), lambda i: (i, 0, 0))),
        compiler_params=_params(("parallel",)),
        name="inproj",
    )(x2d, g, win_bf, sg, wsp, bsp_full)


def _bucket_uppers():
    d = np.arange(0, 4 * MAX_DISTANCE)
    max_exact = NUM_BUCKETS // 2
    nf = np.maximum(d, 1).astype(np.float32)
    large = max_exact + (np.log(nf / max_exact) / math.log(MAX_DISTANCE / max_exact)
                         * (NUM_BUCKETS - max_exact)).astype(np.int32)
    bucket = np.where(d < max_exact, d, np.minimum(large, NUM_BUCKETS - 1))
    return [int(d[bucket > b].min()) for b in range(NUM_BUCKETS - 1)]


def _relbias_kernel(rb_ref, own_ref, prev_ref, *, tq, uppers):
    h = pl.program_id(0)
    key = lax.broadcasted_iota(jnp.int32, (tq, tq), 0)
    qry = lax.broadcasted_iota(jnp.int32, (tq, tq), 1)

    def table(dist):
        val = jnp.full((tq, tq), rb_ref[h, NUM_BUCKETS - 1] * LOG2E, F32)
        for b in range(NUM_BUCKETS - 2, -1, -1):
            val = jnp.where(dist < uppers[b], rb_ref[h, b] * LOG2E, val)
        return val

    d = qry - key
    own_ref[0] = jnp.where(d >= 0, table(d), NEG)
    prev_ref[0] = table(d + tq)


def _relbias(rel_bias_t, tq):
    nh = rel_bias_t.shape[0]
    tile = jax.ShapeDtypeStruct((nh, tq, tq), F32)
    spec = pl.BlockSpec((1, tq, tq), lambda h: (h, 0, 0))
    return pl.pallas_call(
        functools.partial(_relbias_kernel, tq=tq, uppers=_bucket_uppers()),
        out_shape=(tile, tile),
        grid=(nh,),
        in_specs=[pl.BlockSpec(memory_space=pltpu.SMEM)],
        out_specs=(spec, spec),
        compiler_params=_params(("parallel",)),
        name="relbias",
    )(rel_bias_t)


def _attn_kernel(c31_ref, q_ref, k_ref, v_ref, km_ref, own_ref, prev_ref, o_ref,
                 vt_scr, add_scr, s_scr, acc_scr, *, tq, nb, hg):
    hd = ATT_HEAD_DIM
    heads = range(hg)
    cols = [slice(g * hd, (g + 1) * hd) for g in heads]
    c31 = [c31_ref[pl.program_id(1) * hg + g] * LOG2E for g in heads]
    for g in heads:
        for n in range(nb):
            vt_scr[g, n // 2, :hd, (n % 2) * tq:(n % 2 + 1) * tq] = (
                v_ref[n * tq:(n + 1) * tq, cols[g]].astype(F32).T.astype(BF16))
        vt_scr[g, :, hd:, :] = jnp.ones((nb // 2, ONES_ROWS, 2 * tq), BF16)

    km2 = []
    for g in heads:
        km = km_ref[0, :, cols[g]]
        km_hi = km.astype(BF16)
        km2.append(jnp.concatenate([km_hi, (km - km_hi.astype(F32)).astype(BF16)], axis=1))

    def part_max(s):
        return jnp.max(s.reshape(s.shape[0] // 8, 8, tq), axis=0)

    def q_tile(cur, carry):
        rows = pl.ds(pl.multiple_of(cur * tq, tq), tq)
        q = [q_ref[rows, cols[g]] for g in heads]
        blk = lax.broadcasted_iota(jnp.int32, (nb, tq), 0)
        past = blk < cur

        negm_prev = []
        for g in heads:
            gate = lax.dot_general(km2[g], jnp.concatenate([q[g], q[g]], axis=1), _NT,
                                   preferred_element_type=F32)
            gv = jnp.where(past, gate, NEG)
            sel = jnp.zeros((nb, tq), F32)
            for _ in range(MOBA_TOPK):
                mx = jnp.max(gv, axis=0, keepdims=True)
                first = jnp.min(jnp.where(gv == mx, blk, nb), axis=0, keepdims=True)
                hit = blk == first
                sel = jnp.where(hit & past, 1.0, sel)
                gv = jnp.where(hit, -jnp.inf, gv)
            negm = (sel - 1.0) * (-NEG)
            add_scr[g] = jnp.where(blk < cur - 1, negm + c31[g], NEG)
            negm_prev.append(jnp.sum(jnp.where(blk == cur - 1, negm, 0.0), axis=0, keepdims=True))

        def far_pair(i, mx8):
            out = []
            for g in heads:
                kb = k_ref[pl.ds(pl.multiple_of(i * (2 * tq), 2 * tq), 2 * tq), cols[g]]
                s2 = lax.dot_general(kb, q[g], _NT, preferred_element_type=F32)
                s_lo = s2[:tq] + add_scr[g, pl.ds(2 * i, 1), :]
                s_hi = s2[tq:] + add_scr[g, pl.ds(2 * i + 1, 1), :]
                s_scr[g, i, :tq, :] = s_lo
                s_scr[g, i, tq:, :] = s_hi
                out.append(jnp.maximum(mx8[g], jnp.maximum(part_max(s_lo), part_max(s_hi))))
            return tuple(out)

        nfar = jnp.maximum(cur - 1, 0)
        mx8 = lax.fori_loop(0, (nfar + 1) // 2, far_pair,
                            tuple(jnp.full((8, tq), NEG, F32) for _ in heads))

        def one_block(n, add_tiles, mx8):
            out = []
            for g in heads:
                kb = k_ref[pl.ds(pl.multiple_of(n * tq, tq), tq), cols[g]]
                s = lax.dot_general(kb, q[g], _NT, preferred_element_type=F32) + add_tiles[g]
                s_scr[g, n // 2, pl.ds(pl.multiple_of((n % 2) * tq, tq), tq), :] = s
                out.append(jnp.maximum(mx8[g], part_max(s)))
            return tuple(out)

        mx8 = lax.cond(cur >= 1,
                       lambda a: one_block(cur - 1, [prev_ref[g] + negm_prev[g] for g in heads], a),
                       lambda a: a, mx8)
        mx8 = one_block(cur, [own_ref[g] for g in heads], mx8)

        @pl.when(cur % 2 == 0)
        def _():
            for g in heads:
                s_scr[g, cur // 2, tq:, :] = jnp.full((tq, tq), NEG, F32)

        m = [jnp.max(mx8[g], axis=0, keepdims=True) for g in heads]
        acc_scr[...] = jnp.zeros_like(acc_scr)

        def pv_pair(i, c):
            for g in heads:
                p = jnp.exp2((s_scr[g, i] - m[g]).astype(BF16))
                acc_scr[g] += _dot(vt_scr[g, i], p)
            return c

        lax.fori_loop(0, cur // 2 + 1, pv_pair, 0)
        for g in heads:
            acc = acc_scr[g]
            o_ref[rows, cols[g]] = (acc[:hd] / acc[hd:hd + 1]).T.astype(BF16)
        return carry

    lax.fori_loop(0, nb, q_tile, 0)


def _attn(c31, q, k, v, kmean, own, prev, batch, seq):
    t, width = q.shape
    tq = MOBA_BLOCK
    nb = seq // tq
    hd = ATT_HEAD_DIM
    hg = 4
    nh = width // hd
    assert nb % 2 == 0 and nh % hg == 0
    seq_spec = pl.BlockSpec((seq, hg * hd), lambda b, h: (b, h), pipeline_mode=pl.Buffered(1))
    seq_in_spec = pl.BlockSpec((seq, hg * hd), lambda b, h: (b, h))
    tile_spec = pl.BlockSpec((hg, tq, tq), lambda b, h: (h, 0, 0))
    return pl.pallas_call(
        functools.partial(_attn_kernel, tq=tq, nb=nb, hg=hg),
        out_shape=jax.ShapeDtypeStruct((t, width), BF16),
        grid=(batch, nh // hg),
        in_specs=[pl.BlockSpec(memory_space=pltpu.SMEM),
                  seq_in_spec, seq_in_spec, seq_in_spec,
                  pl.BlockSpec((1, nb, hg * hd), lambda b, h: (b, 0, h)),
                  tile_spec, tile_spec],
        out_specs=seq_spec,
        scratch_shapes=[pltpu.VMEM((hg, nb // 2, hd + ONES_ROWS, 2 * tq), BF16),
                        pltpu.VMEM((hg, nb, tq), F32),
                        pltpu.VMEM((hg, nb // 2, 2 * tq, tq), F32),
                        pltpu.VMEM((hg, hd + ONES_ROWS, tq), F32)],
        compiler_params=_params(("parallel", "parallel")),
        name="attn",
    )(c31, q, k, v, kmean, own, prev)


def _post_kernel(x_ref, sgu_ref, att_ref, kvm_ref, nmix_ref, wg_ref, bg_ref, wa_ref, wb_ref,
                 wo_ref, nx_ref, wxq_ref, wxo_ref, nffn_ref, wr_ref, br_ref,
                 h2_ref, xn3_ref, route_ref, cnt_ref, carry_scr, *, tm, ts, d, xw):
    @pl.when(pl.program_id(0) == 0)
    def _():
        carry_scr[...] = jnp.zeros_like(carry_scr)

    lane = lax.broadcasted_iota(jnp.int32, (ts, LANE), 1)
    r = lax.broadcasted_iota(jnp.int32, (ts, ts), 0)
    c = lax.broadcasted_iota(jnp.int32, (ts, ts), 1)
    tri = jnp.where(c < r, 1.0, 0.0).astype(BF16)

    def sub_tile(rows):
        x = x_ref[rows, :]
        xn = _rms(x, nmix_ref[...]).astype(BF16)
        gates = jax.nn.sigmoid(_dot(xn, wg_ref[...]) + bg_ref[...])
        ya = _dot(sgu_ref[rows, :], wa_ref[...])
        yb = _dot(att_ref[rows, :], wb_ref[...])
        merged = (gates[:, :d] * ya + gates[:, d:] * yb).astype(BF16)
        h1 = x + _dot(merged, wo_ref[...])

        xn2 = _rms(h1, nx_ref[...]).astype(BF16)
        qx = (_dot(xn2, wxq_ref[...]) * (X_HEAD_DIM ** -0.5)).astype(BF16)
        outs = []
        for hh in range(X_HEADS):
            cs = slice(hh * X_HEAD_DIM, (hh + 1) * X_HEAD_DIM)
            vs = slice(xw + hh * X_HEAD_DIM, xw + (hh + 1) * X_HEAD_DIM)
            s = lax.dot_general(qx[:, cs], kvm_ref[:, cs], _NT, preferred_element_type=F32)
            p = jnp.exp(s - jnp.max(s, axis=-1, keepdims=True))
            o = _dot(p.astype(BF16), kvm_ref[:, vs]) / jnp.sum(p, axis=-1, keepdims=True)
            outs.append(o.astype(BF16))
        h2 = h1 + _dot(jnp.concatenate(outs, axis=1), wxo_ref[...])
        h2_ref[rows, :] = h2

        xn3 = _rms(h2, nffn_ref[...])
        _store_tile_rows(xn3_ref, rows.start, xn3)
        x_hi = xn3.astype(BF16)
        x_lo = (xn3 - x_hi.astype(F32)).astype(BF16)
        terms = _dot(jnp.concatenate([x_hi, x_lo], axis=0), wr_ref[...])
        logits = ((terms[:ts, :LANE] + terms[ts:, :LANE]) + (terms[:ts, LANE:] + terms[ts:, LANE:])
                  + br_ref[...])
        is_g = lane < N_GROUPS
        gl = jnp.where(is_g, logits, -jnp.inf)
        gmax = jnp.max(gl, axis=-1, keepdims=True)
        gsel = jnp.min(jnp.where(gl == gmax, lane, LANE), axis=-1, keepdims=True)
        pg = 1.0 / jnp.sum(jnp.exp(gl - gmax), axis=-1, keepdims=True)
        eidx = lane - N_GROUPS
        in_group = (eidx >= 0) & (eidx < N_EXPERTS) & ((eidx // EXPERTS_PER_GROUP) == gsel)
        el = jnp.where(in_group, logits, -jnp.inf)
        m1 = jnp.max(el, axis=-1, keepdims=True)
        i1 = jnp.min(jnp.where(el == m1, lane, LANE), axis=-1, keepdims=True)
        el2 = jnp.where(lane == i1, -jnp.inf, el)
        m2 = jnp.max(el2, axis=-1, keepdims=True)
        i2 = jnp.min(jnp.where(el2 == m2, lane, LANE), axis=-1, keepdims=True)
        e2 = jnp.exp(m2 - m1)
        w1 = pg / (1.0 + e2)
        w2 = pg * e2 / (1.0 + e2)
        eid1 = i1 - N_GROUPS
        eid2 = i2 - N_GROUPS

        oh1f = jnp.where(lane == eid1, 1.0, 0.0)
        oh2f = jnp.where(lane == eid2, 1.0, 0.0)
        c1 = _dot(tri, oh1f.astype(BF16))
        c2 = _dot(tri, oh2f.astype(BF16))
        tot1 = jnp.sum(oh1f, axis=0, keepdims=True)
        tot2 = jnp.sum(oh2f, axis=0, keepdims=True)
        return eid1, eid2, w1, w2, oh1f, oh2f, c1, c2, tot1, tot2

    parts = [sub_tile(slice(j * ts, (j + 1) * ts)) for j in range(tm // ts)]

    carry = carry_scr[...]
    for j, (eid1, eid2, w1, w2, oh1f, oh2f, c1, c2, tot1, tot2) in enumerate(parts):
        r1 = jnp.sum(oh1f * (c1 + carry), axis=-1, keepdims=True)
        r2 = jnp.sum(oh2f * (c2 + carry + tot1), axis=-1, keepdims=True)
        carry = carry + tot1 + tot2
        route = jnp.where(lane == 0, eid1.astype(F32), 0.0)
        route = jnp.where(lane == 1, eid2.astype(F32), route)
        route = jnp.where(lane == 2, w1, route)
        route = jnp.where(lane == 3, w2, route)
        route = jnp.where(lane == 4, r1, route)
        route = jnp.where(lane == 5, r2, route)
        route_ref[j * ts:(j + 1) * ts, :] = route
    carry_scr[...] = carry
    cnt_ref[...] = carry


def _post(x2d, sgu, att, kvm, consts, tm, seq):
    t, d = x2d.shape
    mlen = kvm.shape[0] // (t // seq)
    xw = kvm.shape[1] // 2
    tiles_per_batch = seq // tm
    row = lambda w: pl.BlockSpec((tm, w), lambda i: (i, 0))
    return pl.pallas_call(
        functools.partial(_post_kernel, tm=tm, ts=min(tm, 256), d=d, xw=xw),
        out_shape=(jax.ShapeDtypeStruct((t, d), F32), jax.ShapeDtypeStruct((t * d // LANE, LANE), F32),
                   jax.ShapeDtypeStruct((t, LANE), F32), jax.ShapeDtypeStruct((1, LANE), F32)),
        grid=(t // tm,),
        in_specs=[row(d), row(sgu.shape[1]), row(att.shape[1]),
                  pl.BlockSpec((mlen, kvm.shape[1]), lambda i: (i // tiles_per_batch, 0))]
                 + [_const_spec(c.shape) for c in consts],
        out_specs=(row(d), pl.BlockSpec((tm * d // LANE, LANE), lambda i: (i, 0)), row(LANE),
                   pl.BlockSpec((1, LANE), lambda i: (0, 0))),
        scratch_shapes=[pltpu.VMEM((1, LANE), F32)],
        compiler_params=_params(("arbitrary",)),
        name="post",
    )(x2d, sgu, att, kvm, *consts)


def _pad_pair_id(row, t):
    return 2 * t + (row & (2 * MOE_BLOCK - 1))


def _sc_invert(pad_starts, code, n):
    info = plsc.get_sparse_core_info()
    nw = info.num_cores * info.num_subcores
    lanes = info.num_lanes
    npairs = code.shape[0]
    chunk = npairs // nw
    nwin = chunk // SC_WINDOW
    assert chunk * nw == npairs and nwin * SC_WINDOW == chunk
    vals = jnp.broadcast_to(jnp.arange(npairs, dtype=jnp.int32)[:, None], (npairs, lanes))

    def body(start_hbm, code_hbm, vals_hbm, o_hbm, start_v, code_v, dest_v, vals_v):
        wid = lax.axis_index("s") * info.num_cores + lax.axis_index("c")
        base = pl.multiple_of(wid * chunk, chunk)
        pltpu.sync_copy(start_hbm, start_v)
        pltpu.sync_copy(code_hbm.at[pl.ds(base, chunk)], code_v)
        pltpu.sync_copy(vals_hbm.at[pl.ds(base, chunk)], vals_v)

        @pl.loop(0, nwin)
        def _(j):
            for l in range(0, SC_WINDOW, lanes):
                c = code_v[pl.ds(j * SC_WINDOW + l, lanes)]
                seg = plsc.load_gather(start_v, [lax.shift_right_logical(c, RANK_BITS)])
                dest_v[j, pl.ds(l, lanes)] = seg + (c & ((1 << RANK_BITS) - 1))
            pltpu.sync_copy(vals_v.at[pl.ds(j * SC_WINDOW, SC_WINDOW)], o_hbm.at[dest_v.at[j]])

    return pl.kernel(
        body,
        out_type=jax.ShapeDtypeStruct((n, lanes), jnp.int32),
        mesh=plsc.VectorSubcoreMesh(core_axis_name="c", subcore_axis_name="s"),
        scratch_types=[pltpu.VMEM((N_EXPERTS,), jnp.int32), pltpu.VMEM((chunk,), jnp.int32),
                       pltpu.VMEM((nwin, SC_WINDOW), jnp.int32), pltpu.VMEM((chunk, lanes), jnp.int32)],
        compiler_params=pltpu.CompilerParams(needs_layout_passes=False, use_tc_tiling_on_sc=False),
        name="sc_invert",
    )(pad_starts, code, vals)


def _expert_kernel(be_ref, first_ref, nxt_ref, nused_ref, ids_a_ref, ids_b_ref,
                   xn_hbm, wg_hbm, wu_hbm, wd_hbm, out_hbm,
                   xbuf, ybuf, wg_st, wu_st, wd_st, wg_bf, wu_bf, wd_bf, wsem, gsem, ssem, *, t):
    g = pl.program_id(0)
    rows = MOE_BLOCK
    k = xbuf.shape[1] // rows

    def fetch(e):
        return (pltpu.make_async_copy(wg_hbm.at[e], wg_st, wsem.at[0]),
                pltpu.make_async_copy(wu_hbm.at[e], wu_st, wsem.at[1]),
                pltpu.make_async_copy(wd_hbm.at[e], wd_st, wsem.at[2]))

    def token_of(p):
        if t & (t - 1) == 0:
            return p & (t - 1)
        return jnp.where(p >= 2 * t, p - 2 * t, jnp.where(p >= t, p - t, p))

    def gather_row(ids_ref, off, r, slot):
        tok = token_of(ids_ref[0, 0, off + r])
        return pltpu.make_async_copy(xn_hbm.at[pl.ds(pl.multiple_of(tok * k, k), k)],
                                     xbuf.at[slot, pl.ds(r * k, k)], gsem.at[slot])

    def gather_all(slot):
        return pltpu.make_async_copy(xn_hbm.at[pl.ds(0, rows * k)], xbuf.at[slot], gsem.at[slot])

    def scatter_all(slot):
        return pltpu.make_async_copy(ybuf.at[slot], out_hbm.at[pl.ds(0, rows * k)], ssem.at[slot])

    def switch_weights(m):
        @pl.when(first_ref[m] == 1)
        def _():
            for cp in fetch(be_ref[m]):
                cp.wait()
            wg_bf[...] = wg_st[...].astype(BF16)
            wu_bf[...] = wu_st[...].astype(BF16)
            wd_bf[...] = wd_st[...].astype(BF16)

            @pl.when(nxt_ref[m] >= 0)
            def _():
                for cp in fetch(nxt_ref[m]):
                    cp.start()

    def phase(slot, next_ids, next_off, prev_ids, prev_off):
        other = 1 - slot
        for r in range(rows):
            gather_row(next_ids, next_off, r, other).start(priority=r % 2)
        for r in range(rows):
            dst = prev_ids[0, 0, prev_off + r]
            pltpu.make_async_copy(ybuf.at[other, pl.ds(r * k, k)],
                                  out_hbm.at[pl.ds(pl.multiple_of(dst * k, k), k)],
                                  ssem.at[other]).start(priority=r % 2)
        xb = _load_tile_rows(xbuf, rows, k, lead=(slot,)).astype(BF16)
        hid = jax.nn.silu(_dot(xb, wg_bf[...])) * _dot(xb, wu_bf[...])
        _store_tile_rows(ybuf, 0, _dot(hid.astype(BF16), wd_bf[...]), lead=(slot,))

    @pl.when(g == 0)
    def _():
        for cp in fetch(be_ref[0]):
            cp.start()

        def first_block(r, carry):
            gather_row(ids_a_ref, rows, r, 0).start()
            return carry
        lax.fori_loop(0, rows, first_block, 0)
        ybuf[...] = jnp.zeros_like(ybuf)
        for half in range(2):
            cp = pltpu.make_async_copy(
                ybuf.at[half], out_hbm.at[pl.ds((2 * t + half * rows) * k, rows * k)], ssem.at[half])
            cp.start()
            cp.wait()

    nused = nused_ref[0]

    def run_phase(slot, m, off):
        @pl.when(m <= nused)
        def _():
            switch_weights(m)
            gather_all(slot).wait()

            @pl.when(m >= 1)
            def _():
                scatter_all(slot).wait()
            phase(slot, ids_b_ref, off, ids_a_ref, off)

            @pl.when(m == nused)
            def _():
                gather_all(1 - slot).wait()
                scatter_all(1 - slot).wait()

    run_phase(0, 2 * g, 0)
    run_phase(1, 2 * g + 1, rows)


def _expert(blk_expert, first, nxt, nused, ids, xn3, w_gate, w_up, w_down):
    d, eh = w_gate.shape[1:]
    k = d // LANE
    t = xn3.shape[0] // k
    steps = ids.shape[0] - 1
    assert blk_expert.shape[0] == 2 * steps
    hbm = pl.BlockSpec(memory_space=pl.ANY)
    idx_block = lambda f: pl.BlockSpec((1, 1, 2 * MOE_BLOCK), f, memory_space=pltpu.SMEM)
    grid_spec = pltpu.PrefetchScalarGridSpec(
        num_scalar_prefetch=4,
        grid=(steps,),
        in_specs=[idx_block(lambda g, *_: (g, 0, 0)), idx_block(lambda g, *_: (g + 1, 0, 0)),
                  hbm, hbm, hbm, hbm],
        out_specs=hbm,
        scratch_shapes=[pltpu.VMEM((2, MOE_BLOCK * k, LANE), F32), pltpu.VMEM((2, MOE_BLOCK * k, LANE), F32),
                        pltpu.VMEM((d, eh), F32), pltpu.VMEM((d, eh), F32), pltpu.VMEM((eh, d), F32),
                        pltpu.VMEM((d, eh), BF16), pltpu.VMEM((d, eh), BF16), pltpu.VMEM((eh, d), BF16),
                        pltpu.SemaphoreType.DMA((3,)), pltpu.SemaphoreType.DMA((2,)),
                        pltpu.SemaphoreType.DMA((2,))],
    )
    return pl.pallas_call(
        functools.partial(_expert_kernel, t=t),
        out_shape=jax.ShapeDtypeStruct(((2 * t + 2 * MOE_BLOCK) * k, LANE), F32),
        grid_spec=grid_spec,
        compiler_params=_params(("arbitrary",)),
        name="expert",
    )(blk_expert, first, nxt, nused, ids, ids, xn3, w_gate, w_up, w_down)


def _combine_kernel(h2_ref, y0_ref, y1_ref, route_ref, g_ref, o_ref):
    tm, d = h2_ref.shape
    route = route_ref[...]
    y0 = _load_tile_rows(y0_ref, tm, d // LANE)
    y1 = _load_tile_rows(y1_ref, tm, d // LANE)
    y = route[:, 2:3] * y0 + route[:, 3:4] * y1
    o_ref[...] = _rms(h2_ref[...] + y, g_ref[...])


def _combine(h2, ypairs, route, g, tm):
    t, d = h2.shape
    k = d // LANE
    assert ypairs.shape[0] % (tm * k) == 0
    return pl.pallas_call(
        _combine_kernel,
        out_shape=jax.ShapeDtypeStruct((t, d), F32),
        grid=(t // tm,),
        in_specs=[pl.BlockSpec((tm, d), lambda i: (i, 0)),
                  pl.BlockSpec((tm * k, LANE), lambda i: (i, 0)),
                  pl.BlockSpec((tm * k, LANE), lambda i: (i + t // tm, 0)),
                  pl.BlockSpec((tm, LANE), lambda i: (i, 0)),
                  _const_spec(g.shape)],
        out_specs=pl.BlockSpec((tm, d), lambda i: (i, 0)),
        compiler_params=_params(("parallel",)),
        name="combine",
    )(h2, ypairs, ypairs, route, g)


def kernel(x, mem, norm_mix, w_in, w_gate, b_gate, sgu_norm, w_spatial, b_spatial, w_branch_a,
           w_branch_b, rel_bias, w_o, norm_x, norm_mem, w_xq, w_xkv, w_xo, norm_ffn,
           w_router_group, b_router_group, w_router_expert, b_router_expert,
           w_e_gate, w_e_up, w_e_down, norm_final):
    batch, seq, d = x.shape
    assert norm_mix.shape[0] == 1, "one layer"
    assert seq % MOBA_BLOCK == 0 and d % LANE == 0
    t = batch * seq
    tm_in, tm_post = 512, 512
    row = lambda a: a.reshape(1, -1).astype(F32)
    bf = lambda a: a.astype(BF16)

    x2d = x.reshape(t, d)
    kvm = _memkv(mem.reshape(-1, d), row(norm_mem[0]), bf(w_xkv[0]))

    width = w_in.shape[2] // 5
    bsp_full = jnp.repeat(b_spatial[0].T, width // SGU_GROUPS, axis=1)
    sgu, q, k, v, kmean = _inproj(x2d, row(norm_mix[0]), bf(w_in[0]), row(sgu_norm[0]),
                                  w_spatial[0], bsp_full, tm_in)

    rel_t = rel_bias.T.astype(F32)
    own, prev = _relbias(rel_t, MOBA_BLOCK)
    att = _attn(rel_t[:, NUM_BUCKETS - 1], q, k, v,
                kmean.reshape(batch, seq // MOBA_BLOCK, width), own, prev, batch, seq)

    lane_pad = LANE - N_GROUPS - N_EXPERTS
    w_router = jnp.concatenate([w_router_group[0], w_router_expert[0], jnp.zeros((d, lane_pad), F32)], axis=1)
    b_router = jnp.concatenate([b_router_group[0], b_router_expert[0], jnp.zeros((lane_pad,), F32)])
    wr_hi = bf(w_router)
    wr_lo = bf(w_router - wr_hi.astype(F32))
    consts = [row(norm_mix[0]), bf(w_gate[0]), row(b_gate[0]), bf(w_branch_a[0]), bf(w_branch_b[0]),
              bf(w_o[0]), row(norm_x[0]), bf(w_xq[0]), bf(w_xo[0]), row(norm_ffn[0]),
              jnp.concatenate([wr_hi, wr_lo], axis=1), row(b_router)]
    h2, xn3, route, cnt = _post(x2d, sgu, att, kvm, consts, tm_post, seq)

    counts = cnt[0, :N_EXPERTS].astype(jnp.int32)
    padded = (counts + MOE_BLOCK - 1) // MOE_BLOCK * MOE_BLOCK
    pad_ends = jnp.cumsum(padded)
    pad_starts = pad_ends - padded
    nblk = -(-(2 * t) // MOE_BLOCK) + N_EXPERTS
    assert nblk % 2 == 0
    nphase = nblk + 2
    blk_idx = jnp.arange(nphase, dtype=jnp.int32)
    blk_expert = jnp.minimum(
        jnp.sum((pad_ends[None, :] <= blk_idx[:, None] * MOE_BLOCK).astype(jnp.int32), axis=1),
        N_EXPERTS - 1)
    nused = pad_ends[-1] // MOE_BLOCK
    prev_expert = jnp.concatenate([jnp.full((1,), -1, jnp.int32), blk_expert[:-1]])
    first = ((blk_idx < nused) & (blk_expert != prev_expert)).astype(jnp.int32)
    eidx = jnp.arange(N_EXPERTS, dtype=jnp.int32)
    later = (eidx[None, :] > eidx[:, None]) & (padded[None, :] > 0)
    next_expert = jnp.min(jnp.where(later, eidx[None, :], N_EXPERTS), axis=1)
    next_expert = jnp.where(next_expert == N_EXPERTS, -1, next_expert)
    onehot = (blk_expert[:, None] == eidx[None, :]).astype(jnp.int32)
    nxt = jnp.where(first == 1, jnp.sum(onehot * next_expert[None, :], axis=1), -1).astype(jnp.int32)
    seg_left = jnp.sum(onehot * (pad_starts + counts)[None, :], axis=1) - blk_idx * MOE_BLOCK
    nvalid = jnp.where(blk_idx < nused, jnp.clip(seg_left, 0, MOE_BLOCK), 0).astype(jnp.int32)[:nblk]

    assert 2 * t <= 1 << RANK_BITS
    er = jnp.concatenate([route[:, 0:2], route[:, 4:6]], axis=1).T.astype(jnp.int32)
    code = ((er[0:2] << RANK_BITS) | er[2:4]).reshape(2 * t)
    inv = _sc_invert(pad_starts, code, nblk * MOE_BLOCK)[:, 0].reshape(nblk, MOE_BLOCK)
    row_in_blk = jnp.arange(MOE_BLOCK, dtype=jnp.int32)[None, :]
    pad_ids = _pad_pair_id(jnp.arange(nblk, dtype=jnp.int32)[:, None] * MOE_BLOCK + row_in_blk, t)
    inv = jnp.where(row_in_blk < nvalid[:, None], inv, pad_ids).reshape(nblk * MOE_BLOCK)
    dummy = lambda m: _pad_pair_id(m * MOE_BLOCK + jnp.arange(MOE_BLOCK, dtype=jnp.int32), t)
    ids = jnp.concatenate([dummy(-1), inv] + [dummy(nblk + m) for m in range(3)])
    ypairs = _expert(blk_expert, first, nxt, nused.reshape(1), ids.reshape(-1, 1, 2 * MOE_BLOCK), xn3,
                     w_e_gate[0], w_e_up[0], w_e_down[0])
    out = _combine(h2, ypairs, route, row(norm_final), tm_post)
    return out.reshape(batch, seq, d)
```

```python
import functools
import math

import numpy as np
import jax
import jax.numpy as jnp
from jax import lax
from jax.experimental import pallas as pl
from jax.experimental.pallas import tpu as pltpu
from jax.experimental.pallas import tpu_sc as plsc

F32 = jnp.float32
BF16 = jnp.bfloat16

EPS = 1e-6
NEG = -1e30
LOG2E = math.log2(math.e)
LANE = 128
ONES_ROWS = 16
RANK_BITS = 20
SC_WINDOW = 128

SGU_GROUPS = 8
CHUNK = 128
ATT_HEADS = 8
ATT_HEAD_DIM = 128
MOBA_BLOCK = 256
MOBA_TOPK = 3
NUM_BUCKETS = 32
MAX_DISTANCE = 128
X_HEADS = 4
X_HEAD_DIM = 128
N_GROUPS = 8
EXPERTS_PER_GROUP = 8
N_EXPERTS = N_GROUPS * EXPERTS_PER_GROUP
MOE_BLOCK = 256

VMEM_LIMIT = 56 * 1024 * 1024

_NT = (((1,), (1,)), ((), ()))


def _rms(x, g):
    return x * lax.rsqrt(jnp.mean(x * x, axis=-1, keepdims=True) + EPS) * g


def _gelu(x):
    c = math.sqrt(2.0 / math.pi)
    return x * (0.5 * (1.0 + jnp.tanh(c * (x + 0.044715 * (x * x * x)))))


def _dot(a, b):
    return jnp.dot(a, b, preferred_element_type=F32)


def _store_tile_rows(ref, first_row, val, lead=()):
    n, d = val.shape
    k = d // LANE
    for j in range(k):
        ref[lead + (pl.ds(first_row * k + j, n, stride=k), slice(None))] = val[:, j * LANE:(j + 1) * LANE]


def _load_tile_rows(ref, n, k, lead=()):
    return jnp.concatenate([ref[lead + (pl.ds(j, n, stride=k), slice(None))] for j in range(k)], axis=1)


def _const_spec(shape):
    nd = len(shape)
    return pl.BlockSpec(shape, lambda *_: (0,) * nd, pipeline_mode=pl.Buffered(1))


def _params(sem):
    return pltpu.CompilerParams(dimension_semantics=sem, vmem_limit_bytes=VMEM_LIMIT)


def _memkv_kernel(mem_ref, g_ref, w_ref, o_ref):
    mn = _rms(mem_ref[...], g_ref[...]).astype(BF16)
    o_ref[...] = _dot(mn, w_ref[...]).astype(BF16)


def _memkv(mem2d, g, w_bf):
    n, d = mem2d.shape
    tm = 256
    return pl.pallas_call(
        _memkv_kernel,
        out_shape=jax.ShapeDtypeStruct((n, w_bf.shape[1]), BF16),
        grid=(n // tm,),
        in_specs=[pl.BlockSpec((tm, d), lambda i: (i, 0)),
                  _const_spec(g.shape), _const_spec(w_bf.shape)],
        out_specs=pl.BlockSpec((tm, w_bf.shape[1]), lambda i: (i, 0)),
        compiler_params=_params(("parallel",)),
        name="memkv",
    )(mem2d, g, w_bf)


def _inproj_kernel(x_ref, g_ref, win_ref, sg_ref, wsp_ref, bsp_ref,
                   sgu_ref, q_ref, k_ref, v_ref, km_ref, *, tm, width, scale):
    xn = _rms(x_ref[...], g_ref[...]).astype(BF16)
    u = _gelu(_dot(xn, win_ref[:, 0:width]))
    vv = _gelu(_dot(xn, win_ref[:, width:2 * width]))
    vn = _rms(vv, sg_ref[...]).astype(BF16)
    gd = width // SGU_GROUPS
    row = lax.broadcasted_iota(jnp.int32, (CHUNK, CHUNK), 0)
    col = lax.broadcasted_iota(jnp.int32, (CHUNK, CHUNK), 1)
    causal = col <= row
    wsp = [jnp.where(causal, wsp_ref[g], 0.0).astype(BF16) for g in range(SGU_GROUPS)]
    bias = bsp_ref[...]
    for c in range(tm // CHUNK):
        rs = slice(c * CHUNK, (c + 1) * CHUNK)
        mixed = jnp.concatenate(
            [_dot(wsp[g], vn[rs, g * gd:(g + 1) * gd]) for g in range(SGU_GROUPS)], axis=1)
        sgu_ref[rs, :] = (u[rs, :] * (mixed + bias)).astype(BF16)
    q_ref[...] = (_dot(xn, win_ref[:, 2 * width:3 * width]) * scale).astype(BF16)
    k = _dot(xn, win_ref[:, 3 * width:4 * width])
    k_ref[...] = k.astype(BF16)
    for j in range(tm // MOBA_BLOCK):
        km_ref[0, j:j + 1, :] = jnp.mean(k[j * MOBA_BLOCK:(j + 1) * MOBA_BLOCK, :], axis=0, keepdims=True)
    v_ref[...] = _dot(xn, win_ref[:, 4 * width:5 * width]).astype(BF16)


def _inproj(x2d, g, win_bf, sg, wsp, bsp_full, tm):
    t, d = x2d.shape
    width = win_bf.shape[1] // 5
    nkm = tm // MOBA_BLOCK
    row_spec = pl.BlockSpec((tm, width), lambda i: (i, 0))
    act = jax.ShapeDtypeStruct((t, width), BF16)
    return pl.pallas_call(
        functools.partial(_inproj_kernel, tm=tm, width=width, scale=ATT_HEAD_DIM ** -0.5 * LOG2E),
        out_shape=(act, act, act, act, jax.ShapeDtypeStruct((t // tm, nkm, width), F32)),
        grid=(t // tm,),
        in_specs=[pl.BlockSpec((tm, d), lambda i: (i, 0)),
                  _const_spec(g.shape), _const_spec(win_bf.shape), _const_spec(sg.shape),
                  _const_spec(wsp.shape), _const_spec(bsp_full.shape)],
        out_specs=(row_spec, row_spec, row_spec, row_spec,
                   pl.BlockSpec((1, nkm, width), lambda i: (i, 0, 0))),
        compiler_params=_params(("parallel",)),
        name="inproj",
    )(x2d, g, win_bf, sg, wsp, bsp_full)


def _bucket_uppers():
    d = np.arange(0, 4 * MAX_DISTANCE)
    max_exact = NUM_BUCKETS // 2
    nf = np.maximum(d, 1).astype(np.float32)
    large = max_exact + (np.log(nf / max_exact) / math.log(MAX_DISTANCE / max_exact)
                         * (NUM_BUCKETS - max_exact)).astype(np.int32)
    bucket = np.where(d < max_exact, d, np.minimum(large, NUM_BUCKETS - 1))
    return [int(d[bucket > b].min()) for b in range(NUM_BUCKETS - 1)]


def _relbias_kernel(rb_ref, own_ref, prev_ref, *, tq, uppers):
    h = pl.program_id(0)
    key = lax.broadcasted_iota(jnp.int32, (tq, tq), 0)
    qry = lax.broadcasted_iota(jnp.int32, (tq, tq), 1)

    def table(dist):
        val = jnp.full((tq, tq), rb_ref[h, NUM_BUCKETS - 1] * LOG2E, F32)
        for b in range(NUM_BUCKETS - 2, -1, -1):
            val = jnp.where(dist < uppers[b], rb_ref[h, b] * LOG2E, val)
        return val

    d = qry - key
    own_ref[0] = jnp.where(d >= 0, table(d), NEG)
    prev_ref[0] = table(d + tq)


def _relbias(rel_bias_t, tq):
    nh = rel_bias_t.shape[0]
    tile = jax.ShapeDtypeStruct((nh, tq, tq), F32)
    spec = pl.BlockSpec((1, tq, tq), lambda h: (h, 0, 0))
    return pl.pallas_call(
        functools.partial(_relbias_kernel, tq=tq, uppers=_bucket_uppers()),
        out_shape=(tile, tile),
        grid=(nh,),
        in_specs=[pl.BlockSpec(memory_space=pltpu.SMEM)],
        out_specs=(spec, spec),
        compiler_params=_params(("parallel",)),
        name="relbias",
    )(rel_bias_t)


def _attn_kernel(c31_ref, q_ref, k_ref, v_ref, km_ref, own_ref, prev_ref, o_ref,
                 vt_scr, add_scr, s_scr, acc_scr, *, tq, nb, hg):
    hd = ATT_HEAD_DIM
    heads = range(hg)
    cols = [slice(g * hd, (g + 1) * hd) for g in heads]
    c31 = [c31_ref[pl.program_id(1) * hg + g] * LOG2E for g in heads]
    for g in heads:
        for n in range(nb):
            vt_scr[g, n // 2, :hd, (n % 2) * tq:(n % 2 + 1) * tq] = (
                v_ref[n * tq:(n + 1) * tq, cols[g]].astype(F32).T.astype(BF16))
        vt_scr[g, :, hd:, :] = jnp.ones((nb // 2, ONES_ROWS, 2 * tq), BF16)

    km2 = []
    for g in heads:
        km = km_ref[0, :, cols[g]]
        km_hi = km.astype(BF16)
        km2.append(jnp.concatenate([km_hi, (km - km_hi.astype(F32)).astype(BF16)], axis=1))

    def part_max(s):
        return jnp.max(s.reshape(s.shape[0] // 8, 8, tq), axis=0)

    def q_tile(cur, carry):
        rows = pl.ds(pl.multiple_of(cur * tq, tq), tq)
        q = [q_ref[rows, cols[g]] for g in heads]
        blk = lax.broadcasted_iota(jnp.int32, (nb, tq), 0)
        past = blk < cur

        negm_prev = []
        for g in heads:
            gate = lax.dot_general(km2[g], jnp.concatenate([q[g], q[g]], axis=1), _NT,
                                   preferred_element_type=F32)
            gv = jnp.where(past, gate, NEG)
            sel = jnp.zeros((nb, tq), F32)
            for _ in range(MOBA_TOPK):
                mx = jnp.max(gv, axis=0, keepdims=True)
                first = jnp.min(jnp.where(gv == mx, blk, nb), axis=0, keepdims=True)
                hit = blk == first
                sel = jnp.where(hit & past, 1.0, sel)
                gv = jnp.where(hit, -jnp.inf, gv)
            negm = (sel - 1.0) * (-NEG)
            add_scr[g] = jnp.where(blk < cur - 1, negm + c31[g], NEG)
            negm_prev.append(jnp.sum(jnp.where(blk == cur - 1, negm, 0.0), axis=0, keepdims=True))

        def far_pair(i, mx8):
            out = []
            for g in heads:
                kb = k_ref[pl.ds(pl.multiple_of(i * (2 * tq), 2 * tq), 2 * tq), cols[g]]
                s2 = lax.dot_general(kb, q[g], _NT, preferred_element_type=F32)
                s_lo = s2[:tq] + add_scr[g, pl.ds(2 * i, 1), :]
                s_hi = s2[tq:] + add_scr[g, pl.ds(2 * i + 1, 1), :]
                s_scr[g, i, :tq, :] = s_lo
                s_scr[g, i, tq:, :] = s_hi
                out.append(jnp.maximum(mx8[g], jnp.maximum(part_max(s_lo), part_max(s_hi))))
            return tuple(out)

        nfar = jnp.maximum(cur - 1, 0)
        mx8 = lax.fori_loop(0, (nfar + 1) // 2, far_pair,
                            tuple(jnp.full((8, tq), NEG, F32) for _ in heads))

        def one_block(n, add_tiles, mx8):
            out = []
            for g in heads:
                kb = k_ref[pl.ds(pl.multiple_of(n * tq, tq), tq), cols[g]]
                s = lax.dot_general(kb, q[g], _NT, preferred_element_type=F32) + add_tiles[g]
                s_scr[g, n // 2, pl.ds(pl.multiple_of((n % 2) * tq, tq), tq), :] = s
                out.append(jnp.maximum(mx8[g], part_max(s)))
            return tuple(out)

        mx8 = lax.cond(cur >= 1,
                       lambda a: one_block(cur - 1, [prev_ref[g] + negm_prev[g] for g in heads], a),
                       lambda a: a, mx8)
        mx8 = one_block(cur, [own_ref[g] for g in heads], mx8)

        @pl.when(cur % 2 == 0)
        def _():
            for g in heads:
                s_scr[g, cur // 2, tq:, :] = jnp.full((tq, tq), NEG, F32)

        m = [jnp.max(mx8[g], axis=0, keepdims=True) for g in heads]
        acc_scr[...] = jnp.zeros_like(acc_scr)

        def pv_pair(i, c):
            for g in heads:
                p = jnp.exp2((s_scr[g, i] - m[g]).astype(BF16))
                acc_scr[g] += _dot(vt_scr[g, i], p)
            return c

        lax.fori_loop(0, cur // 2 + 1, pv_pair, 0)
        for g in heads:
            acc = acc_scr[g]
            o_ref[rows, cols[g]] = (acc[:hd] / acc[hd:hd + 1]).T.astype(BF16)
        return carry

    lax.fori_loop(0, nb, q_tile, 0)


def _attn(c31, q, k, v, kmean, own, prev, batch, seq):
    t, width = q.shape
    tq = MOBA_BLOCK
    nb = seq // tq
    hd = ATT_HEAD_DIM
    hg = 4
    nh = width // hd
    assert nb % 2 == 0 and nh % hg == 0
    seq_spec = pl.BlockSpec((seq, hg * hd), lambda b, h: (b, h), pipeline_mode=pl.Buffered(1))
    seq_in_spec = pl.BlockSpec((seq, hg * hd), lambda b, h: (b, h))
    tile_spec = pl.BlockSpec((hg, tq, tq), lambda b, h: (h, 0, 0))
    return pl.pallas_call(
        functools.partial(_attn_kernel, tq=tq, nb=nb, hg=hg),
        out_shape=jax.ShapeDtypeStruct((t, width), BF16),
        grid=(batch, nh // hg),
        in_specs=[pl.BlockSpec(memory_space=pltpu.SMEM),
                  seq_in_spec, seq_in_spec, seq_in_spec,
                  pl.BlockSpec((1, nb, hg * hd), lambda b, h: (b, 0, h)),
                  tile_spec, tile_spec],
        out_specs=seq_spec,
        scratch_shapes=[pltpu.VMEM((hg, nb // 2, hd + ONES_ROWS, 2 * tq), BF16),
                        pltpu.VMEM((hg, nb, tq), F32),
                        pltpu.VMEM((hg, nb // 2, 2 * tq, tq), F32),
                        pltpu.VMEM((hg, hd + ONES_ROWS, tq), F32)],
        compiler_params=_params(("parallel", "parallel")),
        name="attn",
    )(c31, q, k, v, kmean, own, prev)


def _post_kernel(x_ref, sgu_ref, att_ref, kvm_ref, nmix_ref, wg_ref, bg_ref, wa_ref, wb_ref,
                 wo_ref, nx_ref, wxq_ref, wxo_ref, nffn_ref, wr_ref, br_ref,
                 h2_ref, xn3_ref, route_ref, cnt_ref, carry_scr, *, tm, ts, d, xw):
    @pl.when(pl.program_id(0) == 0)
    def _():
        carry_scr[...] = jnp.zeros_like(carry_scr)

    lane = lax.broadcasted_iota(jnp.int32, (ts, LANE), 1)
    r = lax.broadcasted_iota(jnp.int32, (ts, ts), 0)
    c = lax.broadcasted_iota(jnp.int32, (ts, ts), 1)
    tri = jnp.where(c < r, 1.0, 0.0).astype(BF16)

    def sub_tile(rows):
        x = x_ref[rows, :]
        xn = _rms(x, nmix_ref[...]).astype(BF16)
        gates = jax.nn.sigmoid(_dot(xn, wg_ref[...]) + bg_ref[...])
        ya = _dot(sgu_ref[rows, :], wa_ref[...])
        yb = _dot(att_ref[rows, :], wb_ref[...])
        merged = (gates[:, :d] * ya + gates[:, d:] * yb).astype(BF16)
        h1 = x + _dot(merged, wo_ref[...])

        xn2 = _rms(h1, nx_ref[...]).astype(BF16)
        qx = (_dot(xn2, wxq_ref[...]) * (X_HEAD_DIM ** -0.5)).astype(BF16)
        outs = []
        for hh in range(X_HEADS):
            cs = slice(hh * X_HEAD_DIM, (hh + 1) * X_HEAD_DIM)
            vs = slice(xw + hh * X_HEAD_DIM, xw + (hh + 1) * X_HEAD_DIM)
            s = lax.dot_general(qx[:, cs], kvm_ref[:, cs], _NT, preferred_element_type=F32)
            p = jnp.exp(s - jnp.max(s, axis=-1, keepdims=True))
            o = _dot(p.astype(BF16), kvm_ref[:, vs]) / jnp.sum(p, axis=-1, keepdims=True)
            outs.append(o.astype(BF16))
        h2 = h1 + _dot(jnp.concatenate(outs, axis=1), wxo_ref[...])
        h2_ref[rows, :] = h2

        xn3 = _rms(h2, nffn_ref[...])
        _store_tile_rows(xn3_ref, rows.start, xn3)
        x_hi = xn3.astype(BF16)
        x_lo = (xn3 - x_hi.astype(F32)).astype(BF16)
        terms = _dot(jnp.concatenate([x_hi, x_lo], axis=0), wr_ref[...])
        logits = ((terms[:ts, :LANE] + terms[ts:, :LANE]) + (terms[:ts, LANE:] + terms[ts:, LANE:])
                  + br_ref[...])
        is_g = lane < N_GROUPS
        gl = jnp.where(is_g, logits, -jnp.inf)
        gmax = jnp.max(gl, axis=-1, keepdims=True)
        gsel = jnp.min(jnp.where(gl == gmax, lane, LANE), axis=-1, keepdims=True)
        pg = 1.0 / jnp.sum(jnp.exp(gl - gmax), axis=-1, keepdims=True)
        eidx = lane - N_GROUPS
        in_group = (eidx >= 0) & (eidx < N_EXPERTS) & ((eidx // EXPERTS_PER_GROUP) == gsel)
        el = jnp.where(in_group, logits, -jnp.inf)
        m1 = jnp.max(el, axis=-1, keepdims=True)
        i1 = jnp.min(jnp.where(el == m1, lane, LANE), axis=-1, keepdims=True)
        el2 = jnp.where(lane == i1, -jnp.inf, el)
        m2 = jnp.max(el2, axis=-1, keepdims=True)
        i2 = jnp.min(jnp.where(el2 == m2, lane, LANE), axis=-1, keepdims=True)
        e2 = jnp.exp(m2 - m1)
        w1 = pg / (1.0 + e2)
        w2 = pg * e2 / (1.0 + e2)
        eid1 = i1 - N_GROUPS
        eid2 = i2 - N_GROUPS

        oh1f = jnp.where(lane == eid1, 1.0, 0.0)
        oh2f = jnp.where(lane == eid2, 1.0, 0.0)
        c1 = _dot(tri, oh1f.astype(BF16))
        c2 = _dot(tri, oh2f.astype(BF16))
        tot1 = jnp.sum(oh1f, axis=0, keepdims=True)
        tot2 = jnp.sum(oh2f, axis=0, keepdims=True)
        return eid1, eid2, w1, w2, oh1f, oh2f, c1, c2, tot1, tot2

    parts = [sub_tile(slice(j * ts, (j + 1) * ts)) for j in range(tm // ts)]

    carry = carry_scr[...]
    for j, (eid1, eid2, w1, w2, oh1f, oh2f, c1, c2, tot1, tot2) in enumerate(parts):
        r1 = jnp.sum(oh1f * (c1 + carry), axis=-1, keepdims=True)
        r2 = jnp.sum(oh2f * (c2 + carry + tot1), axis=-1, keepdims=True)
        carry = carry + tot1 + tot2
        route = jnp.where(lane == 0, eid1.astype(F32), 0.0)
        route = jnp.where(lane == 1, eid2.astype(F32), route)
        route = jnp.where(lane == 2, w1, route)
        route = jnp.where(lane == 3, w2, route)
        route = jnp.where(lane == 4, r1, route)
        route = jnp.where(lane == 5, r2, route)
        route_ref[j * ts:(j + 1) * ts, :] = route
    carry_scr[...] = carry
    cnt_ref[...] = carry


def _post(x2d, sgu, att, kvm, consts, tm, seq):
    t, d = x2d.shape
    mlen = kvm.shape[0] // (t // seq)
    xw = kvm.shape[1] // 2
    tiles_per_batch = seq // tm
    row = lambda w: pl.BlockSpec((tm, w), lambda i: (i, 0))
    return pl.pallas_call(
        functools.partial(_post_kernel, tm=tm, ts=min(tm, 256), d=d, xw=xw),
        out_shape=(jax.ShapeDtypeStruct((t, d), F32), jax.ShapeDtypeStruct((t * d // LANE, LANE), F32),
                   jax.ShapeDtypeStruct((t, LANE), F32), jax.ShapeDtypeStruct((1, LANE), F32)),
        grid=(t // tm,),
        in_specs=[row(d), row(sgu.shape[1]), row(att.shape[1]),
                  pl.BlockSpec((mlen, kvm.shape[1]), lambda i: (i // tiles_per_batch, 0))]
                 + [_const_spec(c.shape) for c in consts],
        out_specs=(row(d), pl.BlockSpec((tm * d // LANE, LANE), lambda i: (i, 0)), row(LANE),
                   pl.BlockSpec((1, LANE), lambda i: (0, 0))),
        scratch_shapes=[pltpu.VMEM((1, LANE), F32)],
        compiler_params=_params(("arbitrary",)),
        name="post",
    )(x2d, sgu, att, kvm, *consts)


def _pad_pair_id(row, t):
    return 2 * t + (row & (2 * MOE_BLOCK - 1))


def _sc_invert(pad_starts, code, n):
    info = plsc.get_sparse_core_info()
    nw = info.num_cores * info.num_subcores
    lanes = info.num_lanes
    npairs = code.shape[0]
    chunk = npairs // nw
    nwin = chunk // SC_WINDOW
    assert chunk * nw == npairs and nwin * SC_WINDOW == chunk

    def body(start_hbm, code_hbm, o_hbm, start_v, code_v, dest_v, vals_v):
        wid = lax.axis_index("s") * info.num_cores + lax.axis_index("c")
        base = pl.multiple_of(wid * chunk, chunk)
        pltpu.sync_copy(start_hbm, start_v)
        pltpu.sync_copy(code_hbm.at[pl.ds(base, chunk)], code_v)
        lane = lax.iota(jnp.int32, lanes)

        @pl.loop(0, nwin)
        def _(j):
            first = base + j * SC_WINDOW
            for l in range(0, SC_WINDOW, lanes):
                c = code_v[pl.ds(j * SC_WINDOW + l, lanes)]
                seg = plsc.load_gather(start_v, [lax.shift_right_logical(c, RANK_BITS)])
                dest_v[j, pl.ds(l, lanes)] = seg + (c & ((1 << RANK_BITS) - 1))
            for r in range(SC_WINDOW):
                vals_v[r, :] = lane + (first + r)
            pltpu.sync_copy(vals_v, o_hbm.at[dest_v.at[j]])

    return pl.kernel(
        body,
        out_type=jax.ShapeDtypeStruct((n, lanes), jnp.int32),
        mesh=plsc.VectorSubcoreMesh(core_axis_name="c", subcore_axis_name="s"),
        scratch_types=[pltpu.VMEM((N_EXPERTS,), jnp.int32), pltpu.VMEM((chunk,), jnp.int32),
                       pltpu.VMEM((nwin, SC_WINDOW), jnp.int32), pltpu.VMEM((SC_WINDOW, lanes), jnp.int32)],
        compiler_params=pltpu.CompilerParams(needs_layout_passes=False, use_tc_tiling_on_sc=False),
        name="sc_invert",
    )(pad_starts, code)


def _expert_kernel(be_ref, first_ref, nxt_ref, nused_ref, ids_a_ref, ids_b_ref,
                   xn_hbm, wg_hbm, wu_hbm, wd_hbm, out_hbm,
                   xbuf, ybuf, wg_st, wu_st, wd_st, wg_bf, wu_bf, wd_bf, wsem, gsem, ssem, *, t):
    g = pl.program_id(0)
    rows = MOE_BLOCK
    k = xbuf.shape[1] // rows

    def fetch(e):
        return (pltpu.make_async_copy(wg_hbm.at[e], wg_st, wsem.at[0]),
                pltpu.make_async_copy(wu_hbm.at[e], wu_st, wsem.at[1]),
                pltpu.make_async_copy(wd_hbm.at[e], wd_st, wsem.at[2]))

    def token_of(p):
        if t & (t - 1) == 0:
            return p & (t - 1)
        return jnp.where(p >= 2 * t, p - 2 * t, jnp.where(p >= t, p - t, p))

    def gather_row(ids_ref, off, r, slot):
        tok = token_of(ids_ref[0, 0, off + r])
        return pltpu.make_async_copy(xn_hbm.at[pl.ds(pl.multiple_of(tok * k, k), k)],
                                     xbuf.at[slot, pl.ds(r * k, k)], gsem.at[slot])

    def gather_all(slot):
        return pltpu.make_async_copy(xn_hbm.at[pl.ds(0, rows * k)], xbuf.at[slot], gsem.at[slot])

    def scatter_all(slot):
        return pltpu.make_async_copy(ybuf.at[slot], out_hbm.at[pl.ds(0, rows * k)], ssem.at[slot])

    def switch_weights(m):
        @pl.when(first_ref[m] == 1)
        def _():
            for cp in fetch(be_ref[m]):
                cp.wait()
            wg_bf[...] = wg_st[...].astype(BF16)
            wu_bf[...] = wu_st[...].astype(BF16)
            wd_bf[...] = wd_st[...].astype(BF16)

            @pl.when(nxt_ref[m] >= 0)
            def _():
                for cp in fetch(nxt_ref[m]):
                    cp.start()

    def phase(slot, next_ids, next_off, prev_ids, prev_off):
        other = 1 - slot
        for r in range(rows):
            gather_row(next_ids, next_off, r, other).start(priority=r % 2)
        for r in range(rows):
            dst = prev_ids[0, 0, prev_off + r]
            pltpu.make_async_copy(ybuf.at[other, pl.ds(r * k, k)],
                                  out_hbm.at[pl.ds(pl.multiple_of(dst * k, k), k)],
                                  ssem.at[other]).start(priority=r % 2)
        xb = _load_tile_rows(xbuf, rows, k, lead=(slot,)).astype(BF16)
        hid = jax.nn.silu(_dot(xb, wg_bf[...])) * _dot(xb, wu_bf[...])
        _store_tile_rows(ybuf, 0, _dot(hid.astype(BF16), wd_bf[...]), lead=(slot,))

    @pl.when(g == 0)
    def _():
        for cp in fetch(be_ref[0]):
            cp.start()

        def first_block(r, carry):
            gather_row(ids_a_ref, rows, r, 0).start()
            return carry
        lax.fori_loop(0, rows, first_block, 0)
        ybuf[...] = jnp.zeros_like(ybuf)
        for half in range(2):
            cp = pltpu.make_async_copy(
                ybuf.at[half], out_hbm.at[pl.ds((2 * t + half * rows) * k, rows * k)], ssem.at[half])
            cp.start()
            cp.wait()

    nused = nused_ref[0]

    def run_phase(slot, m, off):
        @pl.when(m <= nused)
        def _():
            switch_weights(m)
            gather_all(slot).wait()

            @pl.when(m >= 1)
            def _():
                scatter_all(slot).wait()
            phase(slot, ids_b_ref, off, ids_a_ref, off)

            @pl.when(m == nused)
            def _():
                gather_all(1 - slot).wait()
                scatter_all(1 - slot).wait()

    run_phase(0, 2 * g, 0)
    run_phase(1, 2 * g + 1, rows)


def _expert(blk_expert, first, nxt, nused, ids, xn3, w_gate, w_up, w_down):
    d, eh = w_gate.shape[1:]
    k = d // LANE
    t = xn3.shape[0] // k
    steps = ids.shape[0] - 1
    assert blk_expert.shape[0] == 2 * steps
    hbm = pl.BlockSpec(memory_space=pl.ANY)
    idx_block = lambda f: pl.BlockSpec((1, 1, 2 * MOE_BLOCK), f, memory_space=pltpu.SMEM)
    grid_spec = pltpu.PrefetchScalarGridSpec(
        num_scalar_prefetch=4,
        grid=(steps,),
        in_specs=[idx_block(lambda g, *_: (g, 0, 0)), idx_block(lambda g, *_: (g + 1, 0, 0)),
                  hbm, hbm, hbm, hbm],
        out_specs=hbm,
        scratch_shapes=[pltpu.VMEM((2, MOE_BLOCK * k, LANE), F32), pltpu.VMEM((2, MOE_BLOCK * k, LANE), F32),
                        pltpu.VMEM((d, eh), F32), pltpu.VMEM((d, eh), F32), pltpu.VMEM((eh, d), F32),
                        pltpu.VMEM((d, eh), BF16), pltpu.VMEM((d, eh), BF16), pltpu.VMEM((eh, d), BF16),
                        pltpu.SemaphoreType.DMA((3,)), pltpu.SemaphoreType.DMA((2,)),
                        pltpu.SemaphoreType.DMA((2,))],
    )
    return pl.pallas_call(
        functools.partial(_expert_kernel, t=t),
        out_shape=jax.ShapeDtypeStruct(((2 * t + 2 * MOE_BLOCK) * k, LANE), F32),
        grid_spec=grid_spec,
        compiler_params=_params(("arbitrary",)),
        name="expert",
    )(blk_expert, first, nxt, nused, ids, ids, xn3, w_gate, w_up, w_down)


def _combine_kernel(h2_ref, y0_ref, y1_ref, route_ref, g_ref, o_ref):
    tm, d = h2_ref.shape
    route = route_ref[...]
    y0 = _load_tile_rows(y0_ref, tm, d // LANE)
    y1 = _load_tile_rows(y1_ref, tm, d // LANE)
    y = route[:, 2:3] * y0 + route[:, 3:4] * y1
    o_ref[...] = _rms(h2_ref[...] + y, g_ref[...])


def _combine(h2, ypairs, route, g, tm):
    t, d = h2.shape
    k = d // LANE
    assert ypairs.shape[0] % (tm * k) == 0
    return pl.pallas_call(
        _combine_kernel,
        out_shape=jax.ShapeDtypeStruct((t, d), F32),
        grid=(t // tm,),
        in_specs=[pl.BlockSpec((tm, d), lambda i: (i, 0)),
                  pl.BlockSpec((tm * k, LANE), lambda i: (i, 0)),
                  pl.BlockSpec((tm * k, LANE), lambda i: (i + t // tm, 0)),
                  pl.BlockSpec((tm, LANE), lambda i: (i, 0)),
                  _const_spec(g.shape)],
        out_specs=pl.BlockSpec((tm, d), lambda i: (i, 0)),
        compiler_params=_params(("parallel",)),
        name="combine",
    )(h2, ypairs, ypairs, route, g)


def kernel(x, mem, norm_mix, w_in, w_gate, b_gate, sgu_norm, w_spatial, b_spatial, w_branch_a,
           w_branch_b, rel_bias, w_o, norm_x, norm_mem, w_xq, w_xkv, w_xo, norm_ffn,
           w_router_group, b_router_group, w_router_expert, b_router_expert,
           w_e_gate, w_e_up, w_e_down, norm_final):
    batch, seq, d = x.shape
    assert norm_mix.shape[0] == 1, "one layer"
    assert seq % MOBA_BLOCK == 0 and d % LANE == 0
    t = batch * seq
    tm_in, tm_post = 512, 512
    row = lambda a: a.reshape(1, -1).astype(F32)
    bf = lambda a: a.astype(BF16)

    x2d = x.reshape(t, d)
    kvm = _memkv(mem.reshape(-1, d), row(norm_mem[0]), bf(w_xkv[0]))

    width = w_in.shape[2] // 5
    bsp_full = jnp.repeat(b_spatial[0].T, width // SGU_GROUPS, axis=1)
    sgu, q, k, v, kmean = _inproj(x2d, row(norm_mix[0]), bf(w_in[0]), row(sgu_norm[0]),
                                  w_spatial[0], bsp_full, tm_in)

    rel_t = rel_bias.T.astype(F32)
    own, prev = _relbias(rel_t, MOBA_BLOCK)
    att = _attn(rel_t[:, NUM_BUCKETS - 1], q, k, v,
                kmean.reshape(batch, seq // MOBA_BLOCK, width), own, prev, batch, seq)

    lane_pad = LANE - N_GROUPS - N_EXPERTS
    w_router = jnp.concatenate([w_router_group[0], w_router_expert[0], jnp.zeros((d, lane_pad), F32)], axis=1)
    b_router = jnp.concatenate([b_router_group[0], b_router_expert[0], jnp.zeros((lane_pad,), F32)])
    wr_hi = bf(w_router)
    wr_lo = bf(w_router - wr_hi.astype(F32))
    consts = [row(norm_mix[0]), bf(w_gate[0]), row(b_gate[0]), bf(w_branch_a[0]), bf(w_branch_b[0]),
              bf(w_o[0]), row(norm_x[0]), bf(w_xq[0]), bf(w_xo[0]), row(norm_ffn[0]),
              jnp.concatenate([wr_hi, wr_lo], axis=1), row(b_router)]
    h2, xn3, route, cnt = _post(x2d, sgu, att, kvm, consts, tm_post, seq)

    counts = cnt[0, :N_EXPERTS].astype(jnp.int32)
    padded = (counts + MOE_BLOCK - 1) // MOE_BLOCK * MOE_BLOCK
    pad_ends = jnp.cumsum(padded)
    pad_starts = pad_ends - padded
    nblk = -(-(2 * t) // MOE_BLOCK) + N_EXPERTS
    assert nblk % 2 == 0
    nphase = nblk + 2
    blk_idx = jnp.arange(nphase, dtype=jnp.int32)
    blk_expert = jnp.minimum(
        jnp.sum((pad_ends[None, :] <= blk_idx[:, None] * MOE_BLOCK).astype(jnp.int32), axis=1),
        N_EXPERTS - 1)
    nused = pad_ends[-1] // MOE_BLOCK
    prev_expert = jnp.concatenate([jnp.full((1,), -1, jnp.int32), blk_expert[:-1]])
    first = ((blk_idx < nused) & (blk_expert != prev_expert)).astype(jnp.int32)
    eidx = jnp.arange(N_EXPERTS, dtype=jnp.int32)
    later = (eidx[None, :] > eidx[:, None]) & (padded[None, :] > 0)
    next_expert = jnp.min(jnp.where(later, eidx[None, :], N_EXPERTS), axis=1)
    next_expert = jnp.where(next_expert == N_EXPERTS, -1, next_expert)
    onehot = (blk_expert[:, None] == eidx[None, :]).astype(jnp.int32)
    nxt = jnp.where(first == 1, jnp.sum(onehot * next_expert[None, :], axis=1), -1).astype(jnp.int32)
    seg_left = jnp.sum(onehot * (pad_starts + counts)[None, :], axis=1) - blk_idx * MOE_BLOCK
    nvalid = jnp.where(blk_idx < nused, jnp.clip(seg_left, 0, MOE_BLOCK), 0).astype(jnp.int32)[:nblk]

    assert 2 * t <= 1 << RANK_BITS
    er = jnp.concatenate([route[:, 0:2], route[:, 4:6]], axis=1).T.astype(jnp.int32)
    code = ((er[0:2] << RANK_BITS) | er[2:4]).reshape(2 * t)
    inv = _sc_invert(pad_starts, code, nblk * MOE_BLOCK)[:, 0].reshape(nblk, MOE_BLOCK)
    row_in_blk = jnp.arange(MOE_BLOCK, dtype=jnp.int32)[None, :]
    pad_ids = _pad_pair_id(jnp.arange(nblk, dtype=jnp.int32)[:, None] * MOE_BLOCK + row_in_blk, t)
    inv = jnp.where(row_in_blk < nvalid[:, None], inv, pad_ids).reshape(nblk * MOE_BLOCK)
    dummy = lambda m: _pad_pair_id(m * MOE_BLOCK + jnp.arange(MOE_BLOCK, dtype=jnp.int32), t)
    ids = jnp.concatenate([dummy(-1), inv] + [dummy(nblk + m) for m in range(3)])
    ypairs = _expert(blk_expert, first, nxt, nused.reshape(1), ids.reshape(-1, 1, 2 * MOE_BLOCK), xn3,
                     w_e_gate[0], w_e_up[0], w_e_down[0])
    out = _combine(h2, ypairs, route, row(norm_final), tm_post)
    return out.reshape(batch, seq, d)
```

```python
import functools
import math

import numpy as np
import jax
import jax.numpy as jnp
from jax import lax
from jax.experimental import pallas as pl
from jax.experimental.pallas import tpu as pltpu
from jax.experimental.pallas import tpu_sc as plsc

F32 = jnp.float32
BF16 = jnp.bfloat16

EPS = 1e-6
NEG = -1e30
LOG2E = math.log2(math.e)
LANE = 128
ONES_ROWS = 16
RANK_BITS = 20
SC_WINDOW = 128

SGU_GROUPS = 8
CHUNK = 128
ATT_HEADS = 8
ATT_HEAD_DIM = 128
MOBA_BLOCK = 256
MOBA_TOPK = 3
NUM_BUCKETS = 32
MAX_DISTANCE = 128
X_HEADS = 4
X_HEAD_DIM = 128
N_GROUPS = 8
EXPERTS_PER_GROUP = 8
N_EXPERTS = N_GROUPS * EXPERTS_PER_GROUP
MOE_BLOCK = 256

VMEM_LIMIT = 56 * 1024 * 1024

_NT = (((1,), (1,)), ((), ()))


def _rms(x, g):
    return x * lax.rsqrt(jnp.mean(x * x, axis=-1, keepdims=True) + EPS) * g


def _gelu(x):
    c = math.sqrt(2.0 / math.pi)
    return x * (0.5 * (1.0 + jnp.tanh(c * (x + 0.044715 * (x * x * x)))))


def _dot(a, b):
    return jnp.dot(a, b, preferred_element_type=F32)


def _store_tile_rows(ref, first_row, val, lead=()):
    n, d = val.shape
    k = d // LANE
    for j in range(k):
        ref[lead + (pl.ds(first_row * k + j, n, stride=k), slice(None))] = val[:, j * LANE:(j + 1) * LANE]


def _load_tile_rows(ref, n, k, lead=()):
    return jnp.concatenate([ref[lead + (pl.ds(j, n, stride=k), slice(None))] for j in range(k)], axis=1)


def _const_spec(shape):
    nd = len(shape)
    return pl.BlockSpec(shape, lambda *_: (0,) * nd, pipeline_mode=pl.Buffered(1))


def _params(sem):
    return pltpu.CompilerParams(dimension_semantics=sem, vmem_limit_bytes=VMEM_LIMIT)


def _memkv_kernel(mem_ref, g_ref, w_ref, o_ref):
    mn = _rms(mem_ref[...], g_ref[...]).astype(BF16)
    o_ref[...] = _dot(mn, w_ref[...]).astype(BF16)


def _memkv(mem2d, g, w_bf):
    n, d = mem2d.shape
    tm = 256
    return pl.pallas_call(
        _memkv_kernel,
        out_shape=jax.ShapeDtypeStruct((n, w_bf.shape[1]), BF16),
        grid=(n // tm,),
        in_specs=[pl.BlockSpec((tm, d), lambda i: (i, 0)),
                  _const_spec(g.shape), _const_spec(w_bf.shape)],
        out_specs=pl.BlockSpec((tm, w_bf.shape[1]), lambda i: (i, 0)),
        compiler_params=_params(("parallel",)),
        name="memkv",
    )(mem2d, g, w_bf)


def _inproj_kernel(x_ref, g_ref, win_ref, sg_ref, wsp_ref, bsp_ref,
                   sgu_ref, q_ref, k_ref, v_ref, km_ref, *, tm, width, scale):
    xn = _rms(x_ref[...], g_ref[...]).astype(BF16)
    u = _gelu(_dot(xn, win_ref[:, 0:width]))
    vv = _gelu(_dot(xn, win_ref[:, width:2 * width]))
    vn = _rms(vv, sg_ref[...]).astype(BF16)
    gd = width // SGU_GROUPS
    row = lax.broadcasted_iota(jnp.int32, (CHUNK, CHUNK), 0)
    col = lax.broadcasted_iota(jnp.int32, (CHUNK, CHUNK), 1)
    causal = col <= row
    wsp = [jnp.where(causal, wsp_ref[g], 0.0).astype(BF16) for g in range(SGU_GROUPS)]
    bias = bsp_ref[...]
    for c in range(tm // CHUNK):
        rs = slice(c * CHUNK, (c + 1) * CHUNK)
        mixed = jnp.concatenate(
            [_dot(wsp[g], vn[rs, g * gd:(g + 1) * gd]) for g in range(SGU_GROUPS)], axis=1)
        sgu_ref[rs, :] = (u[rs, :] * (mixed + bias)).astype(BF16)
    q_ref[...] = (_dot(xn, win_ref[:, 2 * width:3 * width]) * scale).astype(BF16)
    k = _dot(xn, win_ref[:, 3 * width:4 * width])
    k_ref[...] = k.astype(BF16)
    for j in range(tm // MOBA_BLOCK):
        km_ref[0, j:j + 1, :] = jnp.mean(k[j * MOBA_BLOCK:(j + 1) * MOBA_BLOCK, :], axis=0, keepdims=True)
    v_ref[...] = _dot(xn, win_ref[:, 4 * width:5 * width]).astype(BF16)


def _inproj(x2d, g, win_bf, sg, wsp, bsp_full, tm):
    t, d = x2d.shape
    width = win_bf.shape[1] // 5
    nkm = tm // MOBA_BLOCK
    row_spec = pl.BlockSpec((tm, width), lambda i: (i, 0))
    act = jax.ShapeDtypeStruct((t, width), BF16)
    return pl.pallas_call(
        functools.partial(_inproj_kernel, tm=tm, width=width, scale=ATT_HEAD_DIM ** -0.5 * LOG2E),
        out_shape=(act, act, act, act, jax.ShapeDtypeStruct((t // tm, nkm, width), F32)),
        grid=(t // tm,),
        in_specs=[pl.BlockSpec((tm, d), lambda i: (i, 0)),
                  _const_spec(g.shape), _const_spec(win_bf.shape), _const_spec(sg.shape),
                  _const_spec(wsp.shape), _const_spec(bsp_full.shape)],
        out_specs=(row_spec, row_spec, row_spec, row_spec,
                   pl.BlockSpec((1, nkm, width), lambda i: (i, 0, 0))),
        compiler_params=_params(("parallel",)),
        name="inproj",
    )(x2d, g, win_bf, sg, wsp, bsp_full)


def _bucket_uppers():
    d = np.arange(0, 4 * MAX_DISTANCE)
    max_exact = NUM_BUCKETS // 2
    nf = np.maximum(d, 1).astype(np.float32)
    large = max_exact + (np.log(nf / max_exact) / math.log(MAX_DISTANCE / max_exact)
                         * (NUM_BUCKETS - max_exact)).astype(np.int32)
    bucket = np.where(d < max_exact, d, np.minimum(large, NUM_BUCKETS - 1))
    return [int(d[bucket > b].min()) for b in range(NUM_BUCKETS - 1)]


def _relbias_kernel(rb_ref, own_ref, prev_ref, *, tq, uppers):
    h = pl.program_id(0)
    key = lax.broadcasted_iota(jnp.int32, (tq, tq), 0)
    qry = lax.broadcasted_iota(jnp.int32, (tq, tq), 1)

    def table(dist):
        val = jnp.full((tq, tq), rb_ref[h, NUM_BUCKETS - 1] * LOG2E, F32)
        for b in range(NUM_BUCKETS - 2, -1, -1):
            val = jnp.where(dist < uppers[b], rb_ref[h, b] * LOG2E, val)
        return val

    d = qry - key
    own_ref[0] = jnp.where(d >= 0, table(d), NEG)
    prev_ref[0] = table(d + tq)


def _relbias(rel_bias_t, tq):
    nh = rel_bias_t.shape[0]
    tile = jax.ShapeDtypeStruct((nh, tq, tq), F32)
    spec = pl.BlockSpec((1, tq, tq), lambda h: (h, 0, 0))
    return pl.pallas_call(
        functools.partial(_relbias_kernel, tq=tq, uppers=_bucket_uppers()),
        out_shape=(tile, tile),
        grid=(nh,),
        in_specs=[pl.BlockSpec(memory_space=pltpu.SMEM)],
        out_specs=(spec, spec),
        compiler_params=_params(("parallel",)),
        name="relbias",
    )(rel_bias_t)


def _attn_kernel(c31_ref, q_ref, k_ref, v_ref, km_ref, own_ref, prev_ref, o_ref,
                 vt_scr, add_scr, s_scr, acc_scr, *, tq, nb, hg):
    hd = ATT_HEAD_DIM
    heads = range(hg)
    cols = [slice(g * hd, (g + 1) * hd) for g in heads]
    c31 = [c31_ref[pl.program_id(1) * hg + g] * LOG2E for g in heads]
    for g in heads:
        for n in range(nb):
            vt_scr[g, n // 2, :hd, (n % 2) * tq:(n % 2 + 1) * tq] = (
                v_ref[n * tq:(n + 1) * tq, cols[g]].astype(F32).T.astype(BF16))
        vt_scr[g, :, hd:, :] = jnp.ones((nb // 2, ONES_ROWS, 2 * tq), BF16)

    km2 = []
    for g in heads:
        km = km_ref[0, :, cols[g]]
        km_hi = km.astype(BF16)
        km2.append(jnp.concatenate([km_hi, (km - km_hi.astype(F32)).astype(BF16)], axis=1))

    def part_max(s):
        return jnp.max(s.reshape(s.shape[0] // 8, 8, tq), axis=0)

    def q_tile(cur, carry):
        rows = pl.ds(pl.multiple_of(cur * tq, tq), tq)
        q = [q_ref[rows, cols[g]] for g in heads]
        blk = lax.broadcasted_iota(jnp.int32, (nb, tq), 0)
        past = blk < cur

        negm_prev = []
        for g in heads:
            gate = lax.dot_general(km2[g], jnp.concatenate([q[g], q[g]], axis=1), _NT,
                                   preferred_element_type=F32)
            gv = jnp.where(past, gate, NEG)
            sel = jnp.zeros((nb, tq), F32)
            for _ in range(MOBA_TOPK):
                mx = jnp.max(gv, axis=0, keepdims=True)
                first = jnp.min(jnp.where(gv == mx, blk, nb), axis=0, keepdims=True)
                hit = blk == first
                sel = jnp.where(hit & past, 1.0, sel)
                gv = jnp.where(hit, -jnp.inf, gv)
            negm = (sel - 1.0) * (-NEG)
            add_scr[g] = jnp.where(blk < cur - 1, negm + c31[g], NEG)
            negm_prev.append(jnp.sum(jnp.where(blk == cur - 1, negm, 0.0), axis=0, keepdims=True))

        def far_pair(i, mx8):
            out = []
            for g in heads:
                kb = k_ref[pl.ds(pl.multiple_of(i * (2 * tq), 2 * tq), 2 * tq), cols[g]]
                s2 = lax.dot_general(kb, q[g], _NT, preferred_element_type=F32)
                s_lo = s2[:tq] + add_scr[g, pl.ds(2 * i, 1), :]
                s_hi = s2[tq:] + add_scr[g, pl.ds(2 * i + 1, 1), :]
                s_scr[g, i, :tq, :] = s_lo
                s_scr[g, i, tq:, :] = s_hi
                out.append(jnp.maximum(mx8[g], jnp.maximum(part_max(s_lo), part_max(s_hi))))
            return tuple(out)

        nfar = jnp.maximum(cur - 1, 0)
        mx8 = lax.fori_loop(0, (nfar + 1) // 2, far_pair,
                            tuple(jnp.full((8, tq), NEG, F32) for _ in heads))

        def one_block(n, add_tiles, mx8):
            out = []
            for g in heads:
                kb = k_ref[pl.ds(pl.multiple_of(n * tq, tq), tq), cols[g]]
                s = lax.dot_general(kb, q[g], _NT, preferred_element_type=F32) + add_tiles[g]
                s_scr[g, n // 2, pl.ds(pl.multiple_of((n % 2) * tq, tq), tq), :] = s
                out.append(jnp.maximum(mx8[g], part_max(s)))
            return tuple(out)

        mx8 = lax.cond(cur >= 1,
                       lambda a: one_block(cur - 1, [prev_ref[g] + negm_prev[g] for g in heads], a),
                       lambda a: a, mx8)
        mx8 = one_block(cur, [own_ref[g] for g in heads], mx8)

        @pl.when(cur % 2 == 0)
        def _():
            for g in heads:
                s_scr[g, cur // 2, tq:, :] = jnp.full((tq, tq), NEG, F32)

        m = [jnp.max(mx8[g], axis=0, keepdims=True) for g in heads]
        acc_scr[...] = jnp.zeros_like(acc_scr)

        def pv_pair(i, c):
            for g in heads:
                p = jnp.exp2((s_scr[g, i] - m[g]).astype(BF16))
                acc_scr[g] += _dot(vt_scr[g, i], p)
            return c

        lax.fori_loop(0, cur // 2 + 1, pv_pair, 0)
        for g in heads:
            acc = acc_scr[g]
            o_ref[rows, cols[g]] = (acc[:hd] / acc[hd:hd + 1]).T.astype(BF16)
        return carry

    lax.fori_loop(0, nb, q_tile, 0)


def _attn(c31, q, k, v, kmean, own, prev, batch, seq):
    t, width = q.shape
    tq = MOBA_BLOCK
    nb = seq // tq
    hd = ATT_HEAD_DIM
    hg = 4
    nh = width // hd
    assert nb % 2 == 0 and nh % hg == 0
    seq_spec = pl.BlockSpec((seq, hg * hd), lambda b, h: (b, h), pipeline_mode=pl.Buffered(1))
    seq_in_spec = pl.BlockSpec((seq, hg * hd), lambda b, h: (b, h))
    tile_spec = pl.BlockSpec((hg, tq, tq), lambda b, h: (h, 0, 0))
    return pl.pallas_call(
        functools.partial(_attn_kernel, tq=tq, nb=nb, hg=hg),
        out_shape=jax.ShapeDtypeStruct((t, width), BF16),
        grid=(batch, nh // hg),
        in_specs=[pl.BlockSpec(memory_space=pltpu.SMEM),
                  seq_in_spec, seq_in_spec, seq_in_spec,
                  pl.BlockSpec((1, nb, hg * hd), lambda b, h: (b, 0, h)),
                  tile_spec, tile_spec],
        out_specs=seq_spec,
        scratch_shapes=[pltpu.VMEM((hg, nb // 2, hd + ONES_ROWS, 2 * tq), BF16),
                        pltpu.VMEM((hg, nb, tq), F32),
                        pltpu.VMEM((hg, nb // 2, 2 * tq, tq), F32),
                        pltpu.VMEM((hg, hd + ONES_ROWS, tq), F32)],
        compiler_params=_params(("parallel", "parallel")),
        name="attn",
    )(c31, q, k, v, kmean, own, prev)


def _post_kernel(x_ref, sgu_ref, att_ref, kvm_ref, nmix_ref, wg_ref, bg_ref, wa_ref, wb_ref,
                 wo_ref, nx_ref, wxq_ref, wxo_ref, nffn_ref, wr_ref, br_ref,
                 h2_ref, xn3_ref, route_ref, code_ref, cnt_ref, carry_scr, *, tm, ts, d, xw):
    @pl.when(pl.program_id(0) == 0)
    def _():
        carry_scr[...] = jnp.zeros_like(carry_scr)

    lane = lax.broadcasted_iota(jnp.int32, (ts, LANE), 1)
    r = lax.broadcasted_iota(jnp.int32, (ts, ts), 0)
    c = lax.broadcasted_iota(jnp.int32, (ts, ts), 1)
    tri = jnp.where(c < r, 1.0, 0.0).astype(BF16)

    def sub_tile(rows):
        x = x_ref[rows, :]
        xn = _rms(x, nmix_ref[...]).astype(BF16)
        gates = jax.nn.sigmoid(_dot(xn, wg_ref[...]) + bg_ref[...])
        ya = _dot(sgu_ref[rows, :], wa_ref[...])
        yb = _dot(att_ref[rows, :], wb_ref[...])
        merged = (gates[:, :d] * ya + gates[:, d:] * yb).astype(BF16)
        h1 = x + _dot(merged, wo_ref[...])

        xn2 = _rms(h1, nx_ref[...]).astype(BF16)
        qx = (_dot(xn2, wxq_ref[...]) * (X_HEAD_DIM ** -0.5)).astype(BF16)
        outs = []
        for hh in range(X_HEADS):
            cs = slice(hh * X_HEAD_DIM, (hh + 1) * X_HEAD_DIM)
            vs = slice(xw + hh * X_HEAD_DIM, xw + (hh + 1) * X_HEAD_DIM)
            s = lax.dot_general(qx[:, cs], kvm_ref[:, cs], _NT, preferred_element_type=F32)
            p = jnp.exp(s - jnp.max(s, axis=-1, keepdims=True))
            o = _dot(p.astype(BF16), kvm_ref[:, vs]) / jnp.sum(p, axis=-1, keepdims=True)
            outs.append(o.astype(BF16))
        h2 = h1 + _dot(jnp.concatenate(outs, axis=1), wxo_ref[...])
        h2_ref[rows, :] = h2

        xn3 = _rms(h2, nffn_ref[...])
        _store_tile_rows(xn3_ref, rows.start, xn3)
        x_hi = xn3.astype(BF16)
        x_lo = (xn3 - x_hi.astype(F32)).astype(BF16)
        terms = _dot(jnp.concatenate([x_hi, x_lo], axis=0), wr_ref[...])
        logits = ((terms[:ts, :LANE] + terms[ts:, :LANE]) + (terms[:ts, LANE:] + terms[ts:, LANE:])
                  + br_ref[...])
        is_g = lane < N_GROUPS
        gl = jnp.where(is_g, logits, -jnp.inf)
        gmax = jnp.max(gl, axis=-1, keepdims=True)
        gsel = jnp.min(jnp.where(gl == gmax, lane, LANE), axis=-1, keepdims=True)
        pg = 1.0 / jnp.sum(jnp.exp(gl - gmax), axis=-1, keepdims=True)
        eidx = lane - N_GROUPS
        in_group = (eidx >= 0) & (eidx < N_EXPERTS) & ((eidx // EXPERTS_PER_GROUP) == gsel)
        el = jnp.where(in_group, logits, -jnp.inf)
        m1 = jnp.max(el, axis=-1, keepdims=True)
        i1 = jnp.min(jnp.where(el == m1, lane, LANE), axis=-1, keepdims=True)
        el2 = jnp.where(lane == i1, -jnp.inf, el)
        m2 = jnp.max(el2, axis=-1, keepdims=True)
        i2 = jnp.min(jnp.where(el2 == m2, lane, LANE), axis=-1, keepdims=True)
        e2 = jnp.exp(m2 - m1)
        w1 = pg / (1.0 + e2)
        w2 = pg * e2 / (1.0 + e2)
        eid1 = i1 - N_GROUPS
        eid2 = i2 - N_GROUPS

        oh1f = jnp.where(lane == eid1, 1.0, 0.0)
        oh2f = jnp.where(lane == eid2, 1.0, 0.0)
        c1 = _dot(tri, oh1f.astype(BF16))
        c2 = _dot(tri, oh2f.astype(BF16))
        tot1 = jnp.sum(oh1f, axis=0, keepdims=True)
        tot2 = jnp.sum(oh2f, axis=0, keepdims=True)
        return eid1, eid2, w1, w2, oh1f, oh2f, c1, c2, tot1, tot2

    parts = [sub_tile(slice(j * ts, (j + 1) * ts)) for j in range(tm // ts)]

    carry = carry_scr[...]
    for j, (eid1, eid2, w1, w2, oh1f, oh2f, c1, c2, tot1, tot2) in enumerate(parts):
        r1 = jnp.sum(oh1f * (c1 + carry), axis=-1, keepdims=True)
        r2 = jnp.sum(oh2f * (c2 + carry + tot1), axis=-1, keepdims=True)
        carry = carry + tot1 + tot2
        route = jnp.where(lane == 0, eid1.astype(F32), 0.0)
        route = jnp.where(lane == 1, eid2.astype(F32), route)
        route = jnp.where(lane == 2, w1, route)
        route = jnp.where(lane == 3, w2, route)
        route = jnp.where(lane == 4, r1, route)
        route = jnp.where(lane == 5, r2, route)
        route_ref[j * ts:(j + 1) * ts, :] = route
        route_t = route.T
        for slot in range(2):
            code = ((route_t[slot:slot + 1].astype(jnp.int32) << RANK_BITS)
                    | route_t[4 + slot:5 + slot].astype(jnp.int32))
            code_ref[slot:slot + 1, j * ts:(j + 1) * ts] = code
    carry_scr[...] = carry
    cnt_ref[...] = carry


def _post(x2d, sgu, att, kvm, consts, tm, seq):
    t, d = x2d.shape
    mlen = kvm.shape[0] // (t // seq)
    xw = kvm.shape[1] // 2
    tiles_per_batch = seq // tm
    row = lambda w: pl.BlockSpec((tm, w), lambda i: (i, 0))
    return pl.pallas_call(
        functools.partial(_post_kernel, tm=tm, ts=min(tm, 256), d=d, xw=xw),
        out_shape=(jax.ShapeDtypeStruct((t, d), F32), jax.ShapeDtypeStruct((t * d // LANE, LANE), F32),
                   jax.ShapeDtypeStruct((t, LANE), F32), jax.ShapeDtypeStruct((2, t), jnp.int32),
                   jax.ShapeDtypeStruct((1, LANE), F32)),
        grid=(t // tm,),
        in_specs=[row(d), row(sgu.shape[1]), row(att.shape[1]),
                  pl.BlockSpec((mlen, kvm.shape[1]), lambda i: (i // tiles_per_batch, 0))]
                 + [_const_spec(c.shape) for c in consts],
        out_specs=(row(d), pl.BlockSpec((tm * d // LANE, LANE), lambda i: (i, 0)), row(LANE),
                   pl.BlockSpec((2, tm), lambda i: (0, i)), pl.BlockSpec((1, LANE), lambda i: (0, 0))),
        scratch_shapes=[pltpu.VMEM((1, LANE), F32)],
        compiler_params=_params(("arbitrary",)),
        name="post",
    )(x2d, sgu, att, kvm, *consts)


def _pad_pair_id(row, t):
    return 2 * t + (row & (2 * MOE_BLOCK - 1))


def _sc_invert(pad_starts, code, n):
    info = plsc.get_sparse_core_info()
    nw = info.num_cores * info.num_subcores
    lanes = info.num_lanes
    nslot, t = code.shape
    per_slot = nw // nslot
    chunk = t // per_slot
    nwin = chunk // SC_WINDOW
    assert per_slot * nslot == nw and chunk * per_slot == t and nwin * SC_WINDOW == chunk

    def body(start_hbm, code_hbm, o_hbm, start_v, code_v, dest_v, vals_v):
        wid = lax.axis_index("s") * info.num_cores + lax.axis_index("c")
        slot = wid // per_slot
        off = pl.multiple_of((wid % per_slot) * chunk, chunk)
        pltpu.sync_copy(start_hbm, start_v)
        pltpu.sync_copy(code_hbm.at[slot, pl.ds(off, chunk)], code_v)
        lane = lax.iota(jnp.int32, lanes)

        @pl.loop(0, nwin)
        def _(j):
            first = slot * t + off + j * SC_WINDOW
            for l in range(0, SC_WINDOW, lanes):
                c = code_v[pl.ds(j * SC_WINDOW + l, lanes)]
                seg = plsc.load_gather(start_v, [lax.shift_right_logical(c, RANK_BITS)])
                dest_v[j, pl.ds(l, lanes)] = seg + (c & ((1 << RANK_BITS) - 1))
            for r in range(SC_WINDOW):
                vals_v[r, :] = lane + (first + r)
            pltpu.sync_copy(vals_v, o_hbm.at[dest_v.at[j]])

    return pl.kernel(
        body,
        out_type=jax.ShapeDtypeStruct((n, lanes), jnp.int32),
        mesh=plsc.VectorSubcoreMesh(core_axis_name="c", subcore_axis_name="s"),
        scratch_types=[pltpu.VMEM((N_EXPERTS,), jnp.int32), pltpu.VMEM((chunk,), jnp.int32),
                       pltpu.VMEM((nwin, SC_WINDOW), jnp.int32), pltpu.VMEM((SC_WINDOW, lanes), jnp.int32)],
        compiler_params=pltpu.CompilerParams(needs_layout_passes=False, use_tc_tiling_on_sc=False),
        name="sc_invert",
    )(pad_starts, code)


def _expert_kernel(be_ref, first_ref, nxt_ref, nused_ref, ids_a_ref, ids_b_ref,
                   xn_hbm, wg_hbm, wu_hbm, wd_hbm, out_hbm,
                   xbuf, ybuf, wg_st, wu_st, wd_st, wg_bf, wu_bf, wd_bf, wsem, gsem, ssem, *, t):
    g = pl.program_id(0)
    rows = MOE_BLOCK
    k = xbuf.shape[1] // rows

    def fetch(e):
        return (pltpu.make_async_copy(wg_hbm.at[e], wg_st, wsem.at[0]),
                pltpu.make_async_copy(wu_hbm.at[e], wu_st, wsem.at[1]),
                pltpu.make_async_copy(wd_hbm.at[e], wd_st, wsem.at[2]))

    def token_of(p):
        if t & (t - 1) == 0:
            return p & (t - 1)
        return jnp.where(p >= 2 * t, p - 2 * t, jnp.where(p >= t, p - t, p))

    def gather_row(ids_ref, off, r, slot):
        tok = token_of(ids_ref[0, 0, off + r])
        return pltpu.make_async_copy(xn_hbm.at[pl.ds(pl.multiple_of(tok * k, k), k)],
                                     xbuf.at[slot, pl.ds(r * k, k)], gsem.at[slot])

    def gather_all(slot):
        return pltpu.make_async_copy(xn_hbm.at[pl.ds(0, rows * k)], xbuf.at[slot], gsem.at[slot])

    def scatter_all(slot):
        return pltpu.make_async_copy(ybuf.at[slot], out_hbm.at[pl.ds(0, rows * k)], ssem.at[slot])

    def switch_weights(m):
        @pl.when(first_ref[m] == 1)
        def _():
            for cp in fetch(be_ref[m]):
                cp.wait()
            wg_bf[...] = wg_st[...].astype(BF16)
            wu_bf[...] = wu_st[...].astype(BF16)
            wd_bf[...] = wd_st[...].astype(BF16)

            @pl.when(nxt_ref[m] >= 0)
            def _():
                for cp in fetch(nxt_ref[m]):
                    cp.start()

    def phase(slot, next_ids, next_off, prev_ids, prev_off):
        other = 1 - slot
        for r in range(rows):
            gather_row(next_ids, next_off, r, other).start(priority=r % 2)
        for r in range(rows):
            dst = prev_ids[0, 0, prev_off + r]
            pltpu.make_async_copy(ybuf.at[other, pl.ds(r * k, k)],
                                  out_hbm.at[pl.ds(pl.multiple_of(dst * k, k), k)],
                                  ssem.at[other]).start(priority=r % 2)
        xb = _load_tile_rows(xbuf, rows, k, lead=(slot,)).astype(BF16)
        hid = jax.nn.silu(_dot(xb, wg_bf[...])) * _dot(xb, wu_bf[...])
        _store_tile_rows(ybuf, 0, _dot(hid.astype(BF16), wd_bf[...]), lead=(slot,))

    @pl.when(g == 0)
    def _():
        for cp in fetch(be_ref[0]):
            cp.start()

        def first_block(r, carry):
            gather_row(ids_a_ref, rows, r, 0).start()
            return carry
        lax.fori_loop(0, rows, first_block, 0)
        ybuf[...] = jnp.zeros_like(ybuf)
        for half in range(2):
            cp = pltpu.make_async_copy(
                ybuf.at[half], out_hbm.at[pl.ds((2 * t + half * rows) * k, rows * k)], ssem.at[half])
            cp.start()
            cp.wait()

    nused = nused_ref[0]

    def run_phase(slot, m, off):
        @pl.when(m <= nused)
        def _():
            switch_weights(m)
            gather_all(slot).wait()

            @pl.when(m >= 1)
            def _():
                scatter_all(slot).wait()
            phase(slot, ids_b_ref, off, ids_a_ref, off)

            @pl.when(m == nused)
            def _():
                gather_all(1 - slot).wait()
                scatter_all(1 - slot).wait()

    run_phase(0, 2 * g, 0)
    run_phase(1, 2 * g + 1, rows)


def _expert(blk_expert, first, nxt, nused, ids, xn3, w_gate, w_up, w_down):
    d, eh = w_gate.shape[1:]
    k = d // LANE
    t = xn3.shape[0] // k
    steps = ids.shape[0] - 1
    assert blk_expert.shape[0] == 2 * steps
    hbm = pl.BlockSpec(memory_space=pl.ANY)
    idx_block = lambda f: pl.BlockSpec((1, 1, 2 * MOE_BLOCK), f, memory_space=pltpu.SMEM)
    grid_spec = pltpu.PrefetchScalarGridSpec(
        num_scalar_prefetch=4,
        grid=(steps,),
        in_specs=[idx_block(lambda g, *_: (g, 0, 0)), idx_block(lambda g, *_: (g + 1, 0, 0)),
                  hbm, hbm, hbm, hbm],
        out_specs=hbm,
        scratch_shapes=[pltpu.VMEM((2, MOE_BLOCK * k, LANE), F32), pltpu.VMEM((2, MOE_BLOCK * k, LANE), F32),
                        pltpu.VMEM((d, eh), F32), pltpu.VMEM((d, eh), F32), pltpu.VMEM((eh, d), F32),
                        pltpu.VMEM((d, eh), BF16), pltpu.VMEM((d, eh), BF16), pltpu.VMEM((eh, d), BF16),
                        pltpu.SemaphoreType.DMA((3,)), pltpu.SemaphoreType.DMA((2,)),
                        pltpu.SemaphoreType.DMA((2,))],
    )
    return pl.pallas_call(
        functools.partial(_expert_kernel, t=t),
        out_shape=jax.ShapeDtypeStruct(((2 * t + 2 * MOE_BLOCK) * k, LANE), F32),
        grid_spec=grid_spec,
        compiler_params=_params(("arbitrary",)),
        name="expert",
    )(blk_expert, first, nxt, nused, ids, ids, xn3, w_gate, w_up, w_down)


def _combine_kernel(h2_ref, y0_ref, y1_ref, route_ref, g_ref, o_ref):
    tm, d = h2_ref.shape
    route = route_ref[...]
    y0 = _load_tile_rows(y0_ref, tm, d // LANE)
    y1 = _load_tile_rows(y1_ref, tm, d // LANE)
    y = route[:, 2:3] * y0 + route[:, 3:4] * y1
    o_ref[...] = _rms(h2_ref[...] + y, g_ref[...])


def _combine(h2, ypairs, route, g, tm):
    t, d = h2.shape
    k = d // LANE
    assert ypairs.shape[0] % (tm * k) == 0
    return pl.pallas_call(
        _combine_kernel,
        out_shape=jax.ShapeDtypeStruct((t, d), F32),
        grid=(t // tm,),
        in_specs=[pl.BlockSpec((tm, d), lambda i: (i, 0)),
                  pl.BlockSpec((tm * k, LANE), lambda i: (i, 0)),
                  pl.BlockSpec((tm * k, LANE), lambda i: (i + t // tm, 0)),
                  pl.BlockSpec((tm, LANE), lambda i: (i, 0)),
                  _const_spec(g.shape)],
        out_specs=pl.BlockSpec((tm, d), lambda i: (i, 0)),
        compiler_params=_params(("parallel",)),
        name="combine",
    )(h2, ypairs, ypairs, route, g)


def kernel(x, mem, norm_mix, w_in, w_gate, b_gate, sgu_norm, w_spatial, b_spatial, w_branch_a,
           w_branch_b, rel_bias, w_o, norm_x, norm_mem, w_xq, w_xkv, w_xo, norm_ffn,
           w_router_group, b_router_group, w_router_expert, b_router_expert,
           w_e_gate, w_e_up, w_e_down, norm_final):
    batch, seq, d = x.shape
    assert norm_mix.shape[0] == 1, "one layer"
    assert seq % MOBA_BLOCK == 0 and d % LANE == 0
    t = batch * seq
    tm_in, tm_post = 512, 512
    row = lambda a: a.reshape(1, -1).astype(F32)
    bf = lambda a: a.astype(BF16)

    x2d = x.reshape(t, d)
    kvm = _memkv(mem.reshape(-1, d), row(norm_mem[0]), bf(w_xkv[0]))

    width = w_in.shape[2] // 5
    bsp_full = jnp.repeat(b_spatial[0].T, width // SGU_GROUPS, axis=1)
    sgu, q, k, v, kmean = _inproj(x2d, row(norm_mix[0]), bf(w_in[0]), row(sgu_norm[0]),
                                  w_spatial[0], bsp_full, tm_in)

    rel_t = rel_bias.T.astype(F32)
    own, prev = _relbias(rel_t, MOBA_BLOCK)
    att = _attn(rel_t[:, NUM_BUCKETS - 1], q, k, v,
                kmean.reshape(batch, seq // MOBA_BLOCK, width), own, prev, batch, seq)

    lane_pad = LANE - N_GROUPS - N_EXPERTS
    w_router = jnp.concatenate([w_router_group[0], w_router_expert[0], jnp.zeros((d, lane_pad), F32)], axis=1)
    b_router = jnp.concatenate([b_router_group[0], b_router_expert[0], jnp.zeros((lane_pad,), F32)])
    wr_hi = bf(w_router)
    wr_lo = bf(w_router - wr_hi.astype(F32))
    consts = [row(norm_mix[0]), bf(w_gate[0]), row(b_gate[0]), bf(w_branch_a[0]), bf(w_branch_b[0]),
              bf(w_o[0]), row(norm_x[0]), bf(w_xq[0]), bf(w_xo[0]), row(norm_ffn[0]),
              jnp.concatenate([wr_hi, wr_lo], axis=1), row(b_router)]
    assert 2 * t <= 1 << RANK_BITS
    h2, xn3, route, code, cnt = _post(x2d, sgu, att, kvm, consts, tm_post, seq)

    counts = cnt[0, :N_EXPERTS].astype(jnp.int32)
    padded = (counts + MOE_BLOCK - 1) // MOE_BLOCK * MOE_BLOCK
    pad_ends = jnp.cumsum(padded)
    pad_starts = pad_ends - padded
    nblk = -(-(2 * t) // MOE_BLOCK) + N_EXPERTS
    assert nblk % 2 == 0
    nphase = nblk + 2
    blk_idx = jnp.arange(nphase, dtype=jnp.int32)
    blk_expert = jnp.minimum(
        jnp.sum((pad_ends[None, :] <= blk_idx[:, None] * MOE_BLOCK).astype(jnp.int32), axis=1),
        N_EXPERTS - 1)
    nused = pad_ends[-1] // MOE_BLOCK
    prev_expert = jnp.concatenate([jnp.full((1,), -1, jnp.int32), blk_expert[:-1]])
    first = ((blk_idx < nused) & (blk_expert != prev_expert)).astype(jnp.int32)
    eidx = jnp.arange(N_EXPERTS, dtype=jnp.int32)
    later = (eidx[None, :] > eidx[:, None]) & (padded[None, :] > 0)
    next_expert = jnp.min(jnp.where(later, eidx[None, :], N_EXPERTS), axis=1)
    next_expert = jnp.where(next_expert == N_EXPERTS, -1, next_expert)
    onehot = (blk_expert[:, None] == eidx[None, :]).astype(jnp.int32)
    nxt = jnp.where(first == 1, jnp.sum(onehot * next_expert[None, :], axis=1), -1).astype(jnp.int32)
    seg_left = jnp.sum(onehot * (pad_starts + counts)[None, :], axis=1) - blk_idx * MOE_BLOCK
    nvalid = jnp.where(blk_idx < nused, jnp.clip(seg_left, 0, MOE_BLOCK), 0).astype(jnp.int32)[:nblk]

    inv = _sc_invert(pad_starts, code, nblk * MOE_BLOCK)[:, 0].reshape(nblk, MOE_BLOCK)
    row_in_blk = jnp.arange(MOE_BLOCK, dtype=jnp.int32)[None, :]
    pad_ids = _pad_pair_id(jnp.arange(nblk, dtype=jnp.int32)[:, None] * MOE_BLOCK + row_in_blk, t)
    inv = jnp.where(row_in_blk < nvalid[:, None], inv, pad_ids).reshape(nblk * MOE_BLOCK)
    dummy = lambda m: _pad_pair_id(m * MOE_BLOCK + jnp.arange(MOE_BLOCK, dtype=jnp.int32), t)
    ids = jnp.concatenate([dummy(-1), inv] + [dummy(nblk + m) for m in range(3)])
    ypairs = _expert(blk_expert, first, nxt, nused.reshape(1), ids.reshape(-1, 1, 2 * MOE_BLOCK), xn3,
                     w_e_gate[0], w_e_up[0], w_e_down[0])
    out = _combine(h2, ypairs, route, row(norm_final), tm_post)
    return out.reshape(batch, seq, d)
```

```python
import functools
import math

import numpy as np
import jax
import jax.numpy as jnp
from jax import lax
from jax.experimental import pallas as pl
from jax.experimental.pallas import tpu as pltpu
from jax.experimental.pallas import tpu_sc as plsc

F32 = jnp.float32
BF16 = jnp.bfloat16

EPS = 1e-6
NEG = -1e30
LOG2E = math.log2(math.e)
LANE = 128
ONES_ROWS = 16
RANK_BITS = 20
SC_WINDOW = 128

SGU_GROUPS = 8
CHUNK = 128
ATT_HEADS = 8
ATT_HEAD_DIM = 128
MOBA_BLOCK = 256
MOBA_TOPK = 3
NUM_BUCKETS = 32
MAX_DISTANCE = 128
X_HEADS = 4
X_HEAD_DIM = 128
N_GROUPS = 8
EXPERTS_PER_GROUP = 8
N_EXPERTS = N_GROUPS * EXPERTS_PER_GROUP
MOE_BLOCK = 256

VMEM_LIMIT = 56 * 1024 * 1024

_NT = (((1,), (1,)), ((), ()))


def _rms(x, g):
    return x * lax.rsqrt(jnp.mean(x * x, axis=-1, keepdims=True) + EPS) * g


def _gelu(x):
    c = math.sqrt(2.0 / math.pi)
    return x * (0.5 * (1.0 + jnp.tanh(c * (x + 0.044715 * (x * x * x)))))


def _dot(a, b):
    return jnp.dot(a, b, preferred_element_type=F32)


def _store_tile_rows(ref, first_row, val, lead=()):
    n, d = val.shape
    k = d // LANE
    for j in range(k):
        ref[lead + (pl.ds(first_row * k + j, n, stride=k), slice(None))] = val[:, j * LANE:(j + 1) * LANE]


def _load_tile_rows(ref, n, k, lead=()):
    return jnp.concatenate([ref[lead + (pl.ds(j, n, stride=k), slice(None))] for j in range(k)], axis=1)


def _const_spec(shape):
    nd = len(shape)
    return pl.BlockSpec(shape, lambda *_: (0,) * nd, pipeline_mode=pl.Buffered(1))


def _params(sem):
    return pltpu.CompilerParams(dimension_semantics=sem, vmem_limit_bytes=VMEM_LIMIT)


def _memkv_kernel(mem_ref, g_ref, w_ref, o_ref):
    mn = _rms(mem_ref[...], g_ref[...]).astype(BF16)
    o_ref[...] = _dot(mn, w_ref[...]).astype(BF16)


def _memkv(mem2d, g, w_bf):
    n, d = mem2d.shape
    tm = 256
    return pl.pallas_call(
        _memkv_kernel,
        out_shape=jax.ShapeDtypeStruct((n, w_bf.shape[1]), BF16),
        grid=(n // tm,),
        in_specs=[pl.BlockSpec((tm, d), lambda i: (i, 0)),
                  _const_spec(g.shape), _const_spec(w_bf.shape)],
        out_specs=pl.BlockSpec((tm, w_bf.shape[1]), lambda i: (i, 0)),
        compiler_params=_params(("parallel",)),
        name="memkv",
    )(mem2d, g, w_bf)


def _inproj_kernel(x_ref, g_ref, win_ref, sg_ref, wsp_ref, bsp_ref,
                   sgu_ref, q_ref, k_ref, v_ref, km_ref, *, tm, width, scale):
    xn = _rms(x_ref[...], g_ref[...]).astype(BF16)
    u = _gelu(_dot(xn, win_ref[:, 0:width]))
    vv = _gelu(_dot(xn, win_ref[:, width:2 * width]))
    vn = _rms(vv, sg_ref[...]).astype(BF16)
    gd = width // SGU_GROUPS
    row = lax.broadcasted_iota(jnp.int32, (CHUNK, CHUNK), 0)
    col = lax.broadcasted_iota(jnp.int32, (CHUNK, CHUNK), 1)
    causal = col <= row
    wsp = [jnp.where(causal, wsp_ref[g], 0.0).astype(BF16) for g in range(SGU_GROUPS)]
    bias = bsp_ref[...]
    for c in range(tm // CHUNK):
        rs = slice(c * CHUNK, (c + 1) * CHUNK)
        mixed = jnp.concatenate(
            [_dot(wsp[g], vn[rs, g * gd:(g + 1) * gd]) for g in range(SGU_GROUPS)], axis=1)
        sgu_ref[rs, :] = (u[rs, :] * (mixed + bias)).astype(BF16)
    q_ref[...] = (_dot(xn, win_ref[:, 2 * width:3 * width]) * scale).astype(BF16)
    k = _dot(xn, win_ref[:, 3 * width:4 * width])
    k_ref[...] = k.astype(BF16)
    for j in range(tm // MOBA_BLOCK):
        km_ref[0, j:j + 1, :] = jnp.mean(k[j * MOBA_BLOCK:(j + 1) * MOBA_BLOCK, :], axis=0, keepdims=True)
    v_ref[...] = _dot(xn, win_ref[:, 4 * width:5 * width]).astype(BF16)


def _inproj(x2d, g, win_bf, sg, wsp, bsp_full, tm):
    t, d = x2d.shape
    width = win_bf.shape[1] // 5
    nkm = tm // MOBA_BLOCK
    row_spec = pl.BlockSpec((tm, width), lambda i: (i, 0))
    act = jax.ShapeDtypeStruct((t, width), BF16)
    return pl.pallas_call(
        functools.partial(_inproj_kernel, tm=tm, width=width, scale=ATT_HEAD_DIM ** -0.5 * LOG2E),
        out_shape=(act, act, act, act, jax.ShapeDtypeStruct((t // tm, nkm, width), F32)),
        grid=(t // tm,),
        in_specs=[pl.BlockSpec((tm, d), lambda i: (i, 0)),
                  _const_spec(g.shape), _const_spec(win_bf.shape), _const_spec(sg.shape),
                  _const_spec(wsp.shape), _const_spec(bsp_full.shape)],
        out_specs=(row_spec, row_spec, row_spec, row_spec,
                   pl.BlockSpec((1, nkm, width), lambda i: (i, 0, 0))),
        compiler_params=_params(("parallel",)),
        name="inproj",
    )(x2d, g, win_bf, sg, wsp, bsp_full)


def _bucket_uppers():
    d = np.arange(0, 4 * MAX_DISTANCE)
    max_exact = NUM_BUCKETS // 2
    nf = np.maximum(d, 1).astype(np.float32)
    large = max_exact + (np.log(nf / max_exact) / math.log(MAX_DISTANCE / max_exact)
                         * (NUM_BUCKETS - max_exact)).astype(np.int32)
    bucket = np.where(d < max_exact, d, np.minimum(large, NUM_BUCKETS - 1))
    return [int(d[bucket > b].min()) for b in range(NUM_BUCKETS - 1)]


def _relbias_kernel(rb_ref, own_ref, prev_ref, *, tq, uppers):
    h = pl.program_id(0)
    key = lax.broadcasted_iota(jnp.int32, (tq, tq), 0)
    qry = lax.broadcasted_iota(jnp.int32, (tq, tq), 1)

    def table(dist):
        val = jnp.full((tq, tq), rb_ref[h, NUM_BUCKETS - 1] * LOG2E, F32)
        for b in range(NUM_BUCKETS - 2, -1, -1):
            val = jnp.where(dist < uppers[b], rb_ref[h, b] * LOG2E, val)
        return val

    d = qry - key
    own_ref[0] = jnp.where(d >= 0, table(d), NEG)
    prev_ref[0] = table(d + tq)


def _relbias(rel_bias_t, tq):
    nh = rel_bias_t.shape[0]
    tile = jax.ShapeDtypeStruct((nh, tq, tq), F32)
    spec = pl.BlockSpec((1, tq, tq), lambda h: (h, 0, 0))
    return pl.pallas_call(
        functools.partial(_relbias_kernel, tq=tq, uppers=_bucket_uppers()),
        out_shape=(tile, tile),
        grid=(nh,),
        in_specs=[pl.BlockSpec(memory_space=pltpu.SMEM)],
        out_specs=(spec, spec),
        compiler_params=_params(("parallel",)),
        name="relbias",
    )(rel_bias_t)


def _attn_kernel(c31_ref, q_ref, k_ref, v_ref, km_ref, own_ref, prev_ref, o_ref,
                 vt_scr, add_scr, s_scr, acc_scr, *, tq, nb, hg):
    hd = ATT_HEAD_DIM
    heads = range(hg)
    cols = [slice(g * hd, (g + 1) * hd) for g in heads]
    c31 = [c31_ref[pl.program_id(1) * hg + g] * LOG2E for g in heads]
    for g in heads:
        for n in range(nb):
            vt_scr[g, n // 2, :hd, (n % 2) * tq:(n % 2 + 1) * tq] = (
                v_ref[n * tq:(n + 1) * tq, cols[g]].astype(F32).T.astype(BF16))
        vt_scr[g, :, hd:, :] = jnp.ones((nb // 2, ONES_ROWS, 2 * tq), BF16)

    km2 = []
    for g in heads:
        km = km_ref[0, :, cols[g]]
        km_hi = km.astype(BF16)
        km2.append(jnp.concatenate([km_hi, (km - km_hi.astype(F32)).astype(BF16)], axis=1))

    def part_max(s):
        return jnp.max(s.reshape(s.shape[0] // 8, 8, tq), axis=0)

    def q_tile(cur, carry):
        rows = pl.ds(pl.multiple_of(cur * tq, tq), tq)
        q = [q_ref[rows, cols[g]] for g in heads]
        blk = lax.broadcasted_iota(jnp.int32, (nb, tq), 0)
        past = blk < cur

        negm_prev = []
        for g in heads:
            gate = lax.dot_general(km2[g], jnp.concatenate([q[g], q[g]], axis=1), _NT,
                                   preferred_element_type=F32)
            gv = jnp.where(past, gate, NEG)
            sel = jnp.zeros((nb, tq), F32)
            for _ in range(MOBA_TOPK):
                mx = jnp.max(gv, axis=0, keepdims=True)
                first = jnp.min(jnp.where(gv == mx, blk, nb), axis=0, keepdims=True)
                hit = blk == first
                sel = jnp.where(hit & past, 1.0, sel)
                gv = jnp.where(hit, -jnp.inf, gv)
            negm = (sel - 1.0) * (-NEG)
            add_scr[g] = jnp.where(blk < cur - 1, negm + c31[g], NEG)
            negm_prev.append(jnp.sum(jnp.where(blk == cur - 1, negm, 0.0), axis=0, keepdims=True))

        def far_pair(i, mx8):
            out = []
            for g in heads:
                kb = k_ref[pl.ds(pl.multiple_of(i * (2 * tq), 2 * tq), 2 * tq), cols[g]]
                s2 = lax.dot_general(kb, q[g], _NT, preferred_element_type=F32)
                s_lo = s2[:tq] + add_scr[g, pl.ds(2 * i, 1), :]
                s_hi = s2[tq:] + add_scr[g, pl.ds(2 * i + 1, 1), :]
                s_scr[g, i, :tq, :] = s_lo
                s_scr[g, i, tq:, :] = s_hi
                out.append(jnp.maximum(mx8[g], jnp.maximum(part_max(s_lo), part_max(s_hi))))
            return tuple(out)

        nfar = jnp.maximum(cur - 1, 0)
        mx8 = lax.fori_loop(0, (nfar + 1) // 2, far_pair,
                            tuple(jnp.full((8, tq), NEG, F32) for _ in heads))

        def one_block(n, add_tiles, mx8):
            out = []
            for g in heads:
                kb = k_ref[pl.ds(pl.multiple_of(n * tq, tq), tq), cols[g]]
                s = lax.dot_general(kb, q[g], _NT, preferred_element_type=F32) + add_tiles[g]
                s_scr[g, n // 2, pl.ds(pl.multiple_of((n % 2) * tq, tq), tq), :] = s
                out.append(jnp.maximum(mx8[g], part_max(s)))
            return tuple(out)

        mx8 = lax.cond(cur >= 1,
                       lambda a: one_block(cur - 1, [prev_ref[g] + negm_prev[g] for g in heads], a),
                       lambda a: a, mx8)
        mx8 = one_block(cur, [own_ref[g] for g in heads], mx8)

        @pl.when(cur % 2 == 0)
        def _():
            for g in heads:
                s_scr[g, cur // 2, tq:, :] = jnp.full((tq, tq), NEG, F32)

        m = [jnp.max(mx8[g], axis=0, keepdims=True) for g in heads]
        acc_scr[...] = jnp.zeros_like(acc_scr)

        def pv_pair(i, c):
            for g in heads:
                p = jnp.exp2((s_scr[g, i] - m[g]).astype(BF16))
                acc_scr[g] += _dot(vt_scr[g, i], p)
            return c

        lax.fori_loop(0, cur // 2 + 1, pv_pair, 0)
        for g in heads:
            acc = acc_scr[g]
            o_ref[rows, cols[g]] = (acc[:hd] / acc[hd:hd + 1]).T.astype(BF16)
        return carry

    lax.fori_loop(0, nb, q_tile, 0)


def _attn(c31, q, k, v, kmean, own, prev, batch, seq):
    t, width = q.shape
    tq = MOBA_BLOCK
    nb = seq // tq
    hd = ATT_HEAD_DIM
    hg = 4
    nh = width // hd
    assert nb % 2 == 0 and nh % hg == 0
    seq_spec = pl.BlockSpec((seq, hg * hd), lambda b, h: (b, h), pipeline_mode=pl.Buffered(1))
    seq_in_spec = pl.BlockSpec((seq, hg * hd), lambda b, h: (b, h))
    tile_spec = pl.BlockSpec((hg, tq, tq), lambda b, h: (h, 0, 0))
    return pl.pallas_call(
        functools.partial(_attn_kernel, tq=tq, nb=nb, hg=hg),
        out_shape=jax.ShapeDtypeStruct((t, width), BF16),
        grid=(batch, nh // hg),
        in_specs=[pl.BlockSpec(memory_space=pltpu.SMEM),
                  seq_in_spec, seq_in_spec, seq_in_spec,
                  pl.BlockSpec((1, nb, hg * hd), lambda b, h: (b, 0, h)),
                  tile_spec, tile_spec],
        out_specs=seq_spec,
        scratch_shapes=[pltpu.VMEM((hg, nb // 2, hd + ONES_ROWS, 2 * tq), BF16),
                        pltpu.VMEM((hg, nb, tq), F32),
                        pltpu.VMEM((hg, nb // 2, 2 * tq, tq), F32),
                        pltpu.VMEM((hg, hd + ONES_ROWS, tq), F32)],
        compiler_params=_params(("parallel", "parallel")),
        name="attn",
    )(c31, q, k, v, kmean, own, prev)


def _post_kernel(x_ref, sgu_ref, att_ref, kvm_ref, nmix_ref, wg_ref, bg_ref, wa_ref, wb_ref,
                 wo_ref, nx_ref, wxq_ref, wxo_ref, nffn_ref, wr_ref, br_ref,
                 h2_ref, xn3_ref, route_ref, code_ref, cnt_ref, carry_scr, *, tm, ts, d, xw):
    @pl.when(pl.program_id(0) == 0)
    def _():
        carry_scr[...] = jnp.zeros_like(carry_scr)

    lane = lax.broadcasted_iota(jnp.int32, (ts, LANE), 1)
    r = lax.broadcasted_iota(jnp.int32, (ts, ts), 0)
    c = lax.broadcasted_iota(jnp.int32, (ts, ts), 1)
    tri = jnp.where(c < r, 1.0, 0.0).astype(BF16)

    def sub_tile(rows):
        x = x_ref[rows, :]
        xn = _rms(x, nmix_ref[...]).astype(BF16)
        gates = jax.nn.sigmoid(_dot(xn, wg_ref[...]) + bg_ref[...])
        ya = _dot(sgu_ref[rows, :], wa_ref[...])
        yb = _dot(att_ref[rows, :], wb_ref[...])
        merged = (gates[:, :d] * ya + gates[:, d:] * yb).astype(BF16)
        h1 = x + _dot(merged, wo_ref[...])

        xn2 = _rms(h1, nx_ref[...]).astype(BF16)
        qx = (_dot(xn2, wxq_ref[...]) * (X_HEAD_DIM ** -0.5)).astype(BF16)
        outs = []
        for hh in range(X_HEADS):
            cs = slice(hh * X_HEAD_DIM, (hh + 1) * X_HEAD_DIM)
            vs = slice(xw + hh * X_HEAD_DIM, xw + (hh + 1) * X_HEAD_DIM)
            s = lax.dot_general(qx[:, cs], kvm_ref[:, cs], _NT, preferred_element_type=F32)
            p = jnp.exp(s - jnp.max(s, axis=-1, keepdims=True))
            o = _dot(p.astype(BF16), kvm_ref[:, vs]) / jnp.sum(p, axis=-1, keepdims=True)
            outs.append(o.astype(BF16))
        h2 = h1 + _dot(jnp.concatenate(outs, axis=1), wxo_ref[...])
        h2_ref[rows, :] = h2

        xn3 = _rms(h2, nffn_ref[...])
        _store_tile_rows(xn3_ref, rows.start, xn3)
        x_hi = xn3.astype(BF16)
        x_lo = (xn3 - x_hi.astype(F32)).astype(BF16)
        terms = _dot(jnp.concatenate([x_hi, x_lo], axis=0), wr_ref[...])
        logits = ((terms[:ts, :LANE] + terms[ts:, :LANE]) + (terms[:ts, LANE:] + terms[ts:, LANE:])
                  + br_ref[...])
        is_g = lane < N_GROUPS
        gl = jnp.where(is_g, logits, -jnp.inf)
        gmax = jnp.max(gl, axis=-1, keepdims=True)
        gsel = jnp.min(jnp.where(gl == gmax, lane, LANE), axis=-1, keepdims=True)
        pg = 1.0 / jnp.sum(jnp.exp(gl - gmax), axis=-1, keepdims=True)
        eidx = lane - N_GROUPS
        in_group = (eidx >= 0) & (eidx < N_EXPERTS) & ((eidx // EXPERTS_PER_GROUP) == gsel)
        el = jnp.where(in_group, logits, -jnp.inf)
        m1 = jnp.max(el, axis=-1, keepdims=True)
        i1 = jnp.min(jnp.where(el == m1, lane, LANE), axis=-1, keepdims=True)
        el2 = jnp.where(lane == i1, -jnp.inf, el)
        m2 = jnp.max(el2, axis=-1, keepdims=True)
        i2 = jnp.min(jnp.where(el2 == m2, lane, LANE), axis=-1, keepdims=True)
        e2 = jnp.exp(m2 - m1)
        w1 = pg / (1.0 + e2)
        w2 = pg * e2 / (1.0 + e2)
        eid1 = i1 - N_GROUPS
        eid2 = i2 - N_GROUPS

        oh1f = jnp.where(lane == eid1, 1.0, 0.0)
        oh2f = jnp.where(lane == eid2, 1.0, 0.0)
        c1 = _dot(tri, oh1f.astype(BF16))
        c2 = _dot(tri, oh2f.astype(BF16))
        tot1 = jnp.sum(oh1f, axis=0, keepdims=True)
        tot2 = jnp.sum(oh2f, axis=0, keepdims=True)
        return eid1, eid2, w1, w2, oh1f, oh2f, c1, c2, tot1, tot2

    parts = [sub_tile(slice(j * ts, (j + 1) * ts)) for j in range(tm // ts)]

    carry = carry_scr[...]
    for j, (eid1, eid2, w1, w2, oh1f, oh2f, c1, c2, tot1, tot2) in enumerate(parts):
        r1 = jnp.sum(oh1f * (c1 + carry), axis=-1, keepdims=True)
        r2 = jnp.sum(oh2f * (c2 + carry + tot1), axis=-1, keepdims=True)
        carry = carry + tot1 + tot2
        route = jnp.where(lane == 0, eid1.astype(F32), 0.0)
        route = jnp.where(lane == 1, eid2.astype(F32), route)
        route = jnp.where(lane == 2, w1, route)
        route = jnp.where(lane == 3, w2, route)
        route = jnp.where(lane == 4, r1, route)
        route = jnp.where(lane == 5, r2, route)
        route_ref[j * ts:(j + 1) * ts, :] = route
        route_t = route.T
        for slot in range(2):
            code = ((route_t[slot:slot + 1].astype(jnp.int32) << RANK_BITS)
                    | route_t[4 + slot:5 + slot].astype(jnp.int32))
            code_ref[slot:slot + 1, j * ts:(j + 1) * ts] = code
    carry_scr[...] = carry
    cnt_ref[...] = carry


def _post(x2d, sgu, att, kvm, consts, tm, seq):
    t, d = x2d.shape
    mlen = kvm.shape[0] // (t // seq)
    xw = kvm.shape[1] // 2
    tiles_per_batch = seq // tm
    row = lambda w: pl.BlockSpec((tm, w), lambda i: (i, 0))
    return pl.pallas_call(
        functools.partial(_post_kernel, tm=tm, ts=min(tm, 256), d=d, xw=xw),
        out_shape=(jax.ShapeDtypeStruct((t, d), F32), jax.ShapeDtypeStruct((t * d // LANE, LANE), F32),
                   jax.ShapeDtypeStruct((t, LANE), F32), jax.ShapeDtypeStruct((2, t), jnp.int32),
                   jax.ShapeDtypeStruct((1, LANE), F32)),
        grid=(t // tm,),
        in_specs=[row(d), row(sgu.shape[1]), row(att.shape[1]),
                  pl.BlockSpec((mlen, kvm.shape[1]), lambda i: (i // tiles_per_batch, 0))]
                 + [_const_spec(c.shape) for c in consts],
        out_specs=(row(d), pl.BlockSpec((tm * d // LANE, LANE), lambda i: (i, 0)), row(LANE),
                   pl.BlockSpec((2, tm), lambda i: (0, i)), pl.BlockSpec((1, LANE), lambda i: (0, 0))),
        scratch_shapes=[pltpu.VMEM((1, LANE), F32)],
        compiler_params=_params(("arbitrary",)),
        name="post",
    )(x2d, sgu, att, kvm, *consts)


def _pad_pair_id(row, t):
    return 2 * t + (row & (2 * MOE_BLOCK - 1))


def _sc_invert(pad_starts, code, n):
    info = plsc.get_sparse_core_info()
    nw = info.num_cores * info.num_subcores
    lanes = info.num_lanes
    nslot, t = code.shape
    per_slot = nw // nslot
    chunk = t // per_slot
    nwin = chunk // SC_WINDOW
    assert per_slot * nslot == nw and chunk * per_slot == t and nwin * SC_WINDOW == chunk

    def body(start_hbm, code_hbm, o_hbm, start_v, code_v, dest_v, vals_v):
        wid = lax.axis_index("s") * info.num_cores + lax.axis_index("c")
        slot = wid // per_slot
        off = pl.multiple_of((wid % per_slot) * chunk, chunk)
        pltpu.sync_copy(start_hbm, start_v)
        pltpu.sync_copy(code_hbm.at[slot, pl.ds(off, chunk)], code_v)
        lane = lax.iota(jnp.int32, lanes)

        @pl.loop(0, nwin)
        def _(j):
            first = slot * t + off + j * SC_WINDOW
            for l in range(0, SC_WINDOW, lanes):
                c = code_v[pl.ds(j * SC_WINDOW + l, lanes)]
                seg = plsc.load_gather(start_v, [lax.shift_right_logical(c, RANK_BITS)])
                dest_v[j, pl.ds(l, lanes)] = seg + (c & ((1 << RANK_BITS) - 1))
            for r in range(SC_WINDOW):
                vals_v[r, :] = lane + (first + r)
            pltpu.sync_copy(vals_v, o_hbm.at[dest_v.at[j]])

    return pl.kernel(
        body,
        out_type=jax.ShapeDtypeStruct((n, lanes), jnp.int32),
        mesh=plsc.VectorSubcoreMesh(core_axis_name="c", subcore_axis_name="s"),
        scratch_types=[pltpu.VMEM((N_EXPERTS,), jnp.int32), pltpu.VMEM((chunk,), jnp.int32),
                       pltpu.VMEM((nwin, SC_WINDOW), jnp.int32), pltpu.VMEM((SC_WINDOW, lanes), jnp.int32)],
        compiler_params=pltpu.CompilerParams(needs_layout_passes=False, use_tc_tiling_on_sc=False),
        name="sc_invert",
    )(pad_starts, code)


def _expert_kernel(be_ref, first_ref, nxt_ref, nused_ref, ids_a_ref, ids_b_ref,
                   xn_hbm, wg_hbm, wu_hbm, wd_hbm, out_hbm,
                   xbuf, ybuf, wg_st, wu_st, wd_st, wg_bf, wu_bf, wd_bf, wsem, gsem, ssem, *, t):
    g = pl.program_id(0)
    rows = MOE_BLOCK
    k = xbuf.shape[1] // rows

    def fetch(e):
        return (pltpu.make_async_copy(wg_hbm.at[e], wg_st, wsem.at[0]),
                pltpu.make_async_copy(wu_hbm.at[e], wu_st, wsem.at[1]),
                pltpu.make_async_copy(wd_hbm.at[e], wd_st, wsem.at[2]))

    def token_of(p):
        if t & (t - 1) == 0:
            return p & (t - 1)
        return jnp.where(p >= 2 * t, p - 2 * t, jnp.where(p >= t, p - t, p))

    def gather_row(ids_ref, off, r, slot):
        tok = token_of(ids_ref[0, 0, off + r])
        return pltpu.make_async_copy(xn_hbm.at[pl.ds(pl.multiple_of(tok * k, k), k)],
                                     xbuf.at[slot, pl.ds(r * k, k)], gsem.at[slot])

    def gather_all(slot):
        return pltpu.make_async_copy(xn_hbm.at[pl.ds(0, rows * k)], xbuf.at[slot], gsem.at[slot])

    def scatter_all(slot):
        return pltpu.make_async_copy(ybuf.at[slot], out_hbm.at[pl.ds(0, rows * k)], ssem.at[slot])

    def switch_weights(m):
        @pl.when(first_ref[m] == 1)
        def _():
            for cp in fetch(be_ref[m]):
                cp.wait()
            wg_bf[...] = wg_st[...].astype(BF16)
            wu_bf[...] = wu_st[...].astype(BF16)
            wd_bf[...] = wd_st[...].astype(BF16)

            @pl.when(nxt_ref[m] >= 0)
            def _():
                for cp in fetch(nxt_ref[m]):
                    cp.start(priority=1)

    def phase(slot, next_ids, next_off, prev_ids, prev_off):
        other = 1 - slot
        for r in range(rows):
            gather_row(next_ids, next_off, r, other).start()
        for r in range(rows):
            dst = prev_ids[0, 0, prev_off + r]
            pltpu.make_async_copy(ybuf.at[other, pl.ds(r * k, k)],
                                  out_hbm.at[pl.ds(pl.multiple_of(dst * k, k), k)],
                                  ssem.at[other]).start(priority=r % 2)
        xb = _load_tile_rows(xbuf, rows, k, lead=(slot,)).astype(BF16)
        hid = jax.nn.silu(_dot(xb, wg_bf[...])) * _dot(xb, wu_bf[...])
        _store_tile_rows(ybuf, 0, _dot(hid.astype(BF16), wd_bf[...]), lead=(slot,))

    @pl.when(g == 0)
    def _():
        for cp in fetch(be_ref[0]):
            cp.start()

        def first_block(r, carry):
            gather_row(ids_a_ref, rows, r, 0).start()
            return carry
        lax.fori_loop(0, rows, first_block, 0)
        ybuf[...] = jnp.zeros_like(ybuf)
        for half in range(2):
            cp = pltpu.make_async_copy(
                ybuf.at[half], out_hbm.at[pl.ds((2 * t + half * rows) * k, rows * k)], ssem.at[half])
            cp.start()
            cp.wait()

    nused = nused_ref[0]

    def run_phase(slot, m, off):
        @pl.when(m <= nused)
        def _():
            switch_weights(m)
            gather_all(slot).wait()

            @pl.when(m >= 1)
            def _():
                scatter_all(slot).wait()
            phase(slot, ids_b_ref, off, ids_a_ref, off)

            @pl.when(m == nused)
            def _():
                gather_all(1 - slot).wait()
                scatter_all(1 - slot).wait()

    run_phase(0, 2 * g, 0)
    run_phase(1, 2 * g + 1, rows)


def _expert(blk_expert, first, nxt, nused, ids, xn3, w_gate, w_up, w_down):
    d, eh = w_gate.shape[1:]
    k = d // LANE
    t = xn3.shape[0] // k
    steps = ids.shape[0] - 1
    assert blk_expert.shape[0] == 2 * steps
    hbm = pl.BlockSpec(memory_space=pl.ANY)
    idx_block = lambda f: pl.BlockSpec((1, 1, 2 * MOE_BLOCK), f, memory_space=pltpu.SMEM)
    grid_spec = pltpu.PrefetchScalarGridSpec(
        num_scalar_prefetch=4,
        grid=(steps,),
        in_specs=[idx_block(lambda g, *_: (g, 0, 0)), idx_block(lambda g, *_: (g + 1, 0, 0)),
                  hbm, hbm, hbm, hbm],
        out_specs=hbm,
        scratch_shapes=[pltpu.VMEM((2, MOE_BLOCK * k, LANE), F32), pltpu.VMEM((2, MOE_BLOCK * k, LANE), F32),
                        pltpu.VMEM((d, eh), F32), pltpu.VMEM((d, eh), F32), pltpu.VMEM((eh, d), F32),
                        pltpu.VMEM((d, eh), BF16), pltpu.VMEM((d, eh), BF16), pltpu.VMEM((eh, d), BF16),
                        pltpu.SemaphoreType.DMA((3,)), pltpu.SemaphoreType.DMA((2,)),
                        pltpu.SemaphoreType.DMA((2,))],
    )
    return pl.pallas_call(
        functools.partial(_expert_kernel, t=t),
        out_shape=jax.ShapeDtypeStruct(((2 * t + 2 * MOE_BLOCK) * k, LANE), F32),
        grid_spec=grid_spec,
        compiler_params=_params(("arbitrary",)),
        name="expert",
    )(blk_expert, first, nxt, nused, ids, ids, xn3, w_gate, w_up, w_down)


def _combine_kernel(h2_ref, y0_ref, y1_ref, route_ref, g_ref, o_ref):
    tm, d = h2_ref.shape
    route = route_ref[...]
    y0 = _load_tile_rows(y0_ref, tm, d // LANE)
    y1 = _load_tile_rows(y1_ref, tm, d // LANE)
    y = route[:, 2:3] * y0 + route[:, 3:4] * y1
    o_ref[...] = _rms(h2_ref[...] + y, g_ref[...])


def _combine(h2, ypairs, route, g, tm):
    t, d = h2.shape
    k = d // LANE
    assert ypairs.shape[0] % (tm * k) == 0
    return pl.pallas_call(
        _combine_kernel,
        out_shape=jax.ShapeDtypeStruct((t, d), F32),
        grid=(t // tm,),
        in_specs=[pl.BlockSpec((tm, d), lambda i: (i, 0)),
                  pl.BlockSpec((tm * k, LANE), lambda i: (i, 0)),
                  pl.BlockSpec((tm * k, LANE), lambda i: (i + t // tm, 0)),
                  pl.BlockSpec((tm, LANE), lambda i: (i, 0)),
                  _const_spec(g.shape)],
        out_specs=pl.BlockSpec((tm, d), lambda i: (i, 0)),
        compiler_params=_params(("parallel",)),
        name="combine",
    )(h2, ypairs, ypairs, route, g)


def kernel(x, mem, norm_mix, w_in, w_gate, b_gate, sgu_norm, w_spatial, b_spatial, w_branch_a,
           w_branch_b, rel_bias, w_o, norm_x, norm_mem, w_xq, w_xkv, w_xo, norm_ffn,
           w_router_group, b_router_group, w_router_expert, b_router_expert,
           w_e_gate, w_e_up, w_e_down, norm_final):
    batch, seq, d = x.shape
    assert norm_mix.shape[0] == 1, "one layer"
    assert seq % MOBA_BLOCK == 0 and d % LANE == 0
    t = batch * seq
    tm_in, tm_post = 512, 512
    row = lambda a: a.reshape(1, -1).astype(F32)
    bf = lambda a: a.astype(BF16)

    x2d = x.reshape(t, d)
    kvm = _memkv(mem.reshape(-1, d), row(norm_mem[0]), bf(w_xkv[0]))

    width = w_in.shape[2] // 5
    bsp_full = jnp.repeat(b_spatial[0].T, width // SGU_GROUPS, axis=1)
    sgu, q, k, v, kmean = _inproj(x2d, row(norm_mix[0]), bf(w_in[0]), row(sgu_norm[0]),
                                  w_spatial[0], bsp_full, tm_in)

    rel_t = rel_bias.T.astype(F32)
    own, prev = _relbias(rel_t, MOBA_BLOCK)
    att = _attn(rel_t[:, NUM_BUCKETS - 1], q, k, v,
                kmean.reshape(batch, seq // MOBA_BLOCK, width), own, prev, batch, seq)

    lane_pad = LANE - N_GROUPS - N_EXPERTS
    w_router = jnp.concatenate([w_router_group[0], w_router_expert[0], jnp.zeros((d, lane_pad), F32)], axis=1)
    b_router = jnp.concatenate([b_router_group[0], b_router_expert[0], jnp.zeros((lane_pad,), F32)])
    wr_hi = bf(w_router)
    wr_lo = bf(w_router - wr_hi.astype(F32))
    consts = [row(norm_mix[0]), bf(w_gate[0]), row(b_gate[0]), bf(w_branch_a[0]), bf(w_branch_b[0]),
              bf(w_o[0]), row(norm_x[0]), bf(w_xq[0]), bf(w_xo[0]), row(norm_ffn[0]),
              jnp.concatenate([wr_hi, wr_lo], axis=1), row(b_router)]
    assert 2 * t <= 1 << RANK_BITS
    h2, xn3, route, code, cnt = _post(x2d, sgu, att, kvm, consts, tm_post, seq)

    counts = cnt[0, :N_EXPERTS].astype(jnp.int32)
    padded = (counts + MOE_BLOCK - 1) // MOE_BLOCK * MOE_BLOCK
    pad_ends = jnp.cumsum(padded)
    pad_starts = pad_ends - padded
    nblk = -(-(2 * t) // MOE_BLOCK) + N_EXPERTS
    assert nblk % 2 == 0
    nphase = nblk + 2
    blk_idx = jnp.arange(nphase, dtype=jnp.int32)
    blk_expert = jnp.minimum(
        jnp.sum((pad_ends[None, :] <= blk_idx[:, None] * MOE_BLOCK).astype(jnp.int32), axis=1),
        N_EXPERTS - 1)
    nused = pad_ends[-1] // MOE_BLOCK
    prev_expert = jnp.concatenate([jnp.full((1,), -1, jnp.int32), blk_expert[:-1]])
    first = ((blk_idx < nused) & (blk_expert != prev_expert)).astype(jnp.int32)
    eidx = jnp.arange(N_EXPERTS, dtype=jnp.int32)
    later = (eidx[None, :] > eidx[:, None]) & (padded[None, :] > 0)
    next_expert = jnp.min(jnp.where(later, eidx[None, :], N_EXPERTS), axis=1)
    next_expert = jnp.where(next_expert == N_EXPERTS, -1, next_expert)
    onehot = (blk_expert[:, None] == eidx[None, :]).astype(jnp.int32)
    nxt = jnp.where(first == 1, jnp.sum(onehot * next_expert[None, :], axis=1), -1).astype(jnp.int32)
    seg_left = jnp.sum(onehot * (pad_starts + counts)[None, :], axis=1) - blk_idx * MOE_BLOCK
    nvalid = jnp.where(blk_idx < nused, jnp.clip(seg_left, 0, MOE_BLOCK), 0).astype(jnp.int32)[:nblk]

    inv = _sc_invert(pad_starts, code, nblk * MOE_BLOCK)[:, 0].reshape(nblk, MOE_BLOCK)
    row_in_blk = jnp.arange(MOE_BLOCK, dtype=jnp.int32)[None, :]
    pad_ids = _pad_pair_id(jnp.arange(nblk, dtype=jnp.int32)[:, None] * MOE_BLOCK + row_in_blk, t)
    inv = jnp.where(row_in_blk < nvalid[:, None], inv, pad_ids).reshape(nblk * MOE_BLOCK)
    dummy = lambda m: _pad_pair_id(m * MOE_BLOCK + jnp.arange(MOE_BLOCK, dtype=jnp.int32), t)
    ids = jnp.concatenate([dummy(-1), inv] + [dummy(nblk + m) for m in range(3)])
    ypairs = _expert(blk_expert, first, nxt, nused.reshape(1), ids.reshape(-1, 1, 2 * MOE_BLOCK), xn3,
                     w_e_gate[0], w_e_up[0], w_e_down[0])
    out = _combine(h2, ypairs, route, row(norm_final), tm_post)
    return out.reshape(batch, seq, d)
```

```python
import functools
import math

import numpy as np
import jax
import jax.numpy as jnp
from jax import lax
from jax.experimental import pallas as pl
from jax.experimental.pallas import tpu as pltpu
from jax.experimental.pallas import tpu_sc as plsc

F32 = jnp.float32
BF16 = jnp.bfloat16

EPS = 1e-6
NEG = -1e30
LOG2E = math.log2(math.e)
LANE = 128
ONES_ROWS = 16
RANK_BITS = 20
SC_WINDOW = 128

SGU_GROUPS = 8
CHUNK = 128
ATT_HEADS = 8
ATT_HEAD_DIM = 128
MOBA_BLOCK = 256
MOBA_TOPK = 3
NUM_BUCKETS = 32
MAX_DISTANCE = 128
X_HEADS = 4
X_HEAD_DIM = 128
N_GROUPS = 8
EXPERTS_PER_GROUP = 8
N_EXPERTS = N_GROUPS * EXPERTS_PER_GROUP
MOE_BLOCK = 256

VMEM_LIMIT = 56 * 1024 * 1024

_NT = (((1,), (1,)), ((), ()))


def _rms(x, g):
    return x * lax.rsqrt(jnp.mean(x * x, axis=-1, keepdims=True) + EPS) * g


def _gelu(x):
    c = math.sqrt(2.0 / math.pi)
    return x * (0.5 * (1.0 + jnp.tanh(c * (x + 0.044715 * (x * x * x)))))


def _dot(a, b):
    return jnp.dot(a, b, preferred_element_type=F32)


def _store_tile_rows(ref, first_row, val, lead=()):
    n, d = val.shape
    k = d // LANE
    for j in range(k):
        ref[lead + (pl.ds(first_row * k + j, n, stride=k), slice(None))] = val[:, j * LANE:(j + 1) * LANE]


def _load_tile_rows(ref, n, k, lead=()):
    return jnp.concatenate([ref[lead + (pl.ds(j, n, stride=k), slice(None))] for j in range(k)], axis=1)


def _const_spec(shape):
    nd = len(shape)
    return pl.BlockSpec(shape, lambda *_: (0,) * nd, pipeline_mode=pl.Buffered(1))


def _params(sem):
    return pltpu.CompilerParams(dimension_semantics=sem, vmem_limit_bytes=VMEM_LIMIT)


def _memkv_kernel(mem_ref, g_ref, w_ref, o_ref):
    mn = _rms(mem_ref[...], g_ref[...]).astype(BF16)
    o_ref[...] = _dot(mn, w_ref[...]).astype(BF16)


def _memkv(mem2d, g, w_bf):
    n, d = mem2d.shape
    tm = 256
    return pl.pallas_call(
        _memkv_kernel,
        out_shape=jax.ShapeDtypeStruct((n, w_bf.shape[1]), BF16),
        grid=(n // tm,),
        in_specs=[pl.BlockSpec((tm, d), lambda i: (i, 0)),
                  _const_spec(g.shape), _const_spec(w_bf.shape)],
        out_specs=pl.BlockSpec((tm, w_bf.shape[1]), lambda i: (i, 0)),
        compiler_params=_params(("parallel",)),
        name="memkv",
    )(mem2d, g, w_bf)


def _inproj_kernel(x_ref, g_ref, win_ref, sg_ref, wsp_ref, bsp_ref,
                   sgu_ref, q_ref, k_ref, v_ref, km_ref, *, tm, width, scale):
    xn = _rms(x_ref[...], g_ref[...]).astype(BF16)
    u = _gelu(_dot(xn, win_ref[:, 0:width]))
    vv = _gelu(_dot(xn, win_ref[:, width:2 * width]))
    vn = _rms(vv, sg_ref[...]).astype(BF16)
    gd = width // SGU_GROUPS
    row = lax.broadcasted_iota(jnp.int32, (CHUNK, CHUNK), 0)
    col = lax.broadcasted_iota(jnp.int32, (CHUNK, CHUNK), 1)
    causal = col <= row
    wsp = [jnp.where(causal, wsp_ref[g], 0.0).astype(BF16) for g in range(SGU_GROUPS)]
    bias = bsp_ref[...]
    for c in range(tm // CHUNK):
        rs = slice(c * CHUNK, (c + 1) * CHUNK)
        mixed = jnp.concatenate(
            [_dot(wsp[g], vn[rs, g * gd:(g + 1) * gd]) for g in range(SGU_GROUPS)], axis=1)
        sgu_ref[rs, :] = (u[rs, :] * (mixed + bias)).astype(BF16)
    q_ref[...] = (_dot(xn, win_ref[:, 2 * width:3 * width]) * scale).astype(BF16)
    k = _dot(xn, win_ref[:, 3 * width:4 * width])
    k_ref[...] = k.astype(BF16)
    for j in range(tm // MOBA_BLOCK):
        km_ref[0, j:j + 1, :] = jnp.mean(k[j * MOBA_BLOCK:(j + 1) * MOBA_BLOCK, :], axis=0, keepdims=True)
    v_ref[...] = _dot(xn, win_ref[:, 4 * width:5 * width]).astype(BF16)


def _inproj(x2d, g, win_bf, sg, wsp, bsp_full, tm):
    t, d = x2d.shape
    width = win_bf.shape[1] // 5
    nkm = tm // MOBA_BLOCK
    row_spec = pl.BlockSpec((tm, width), lambda i: (i, 0))
    act = jax.ShapeDtypeStruct((t, width), BF16)
    return pl.pallas_call(
        functools.partial(_inproj_kernel, tm=tm, width=width, scale=ATT_HEAD_DIM ** -0.5 * LOG2E),
        out_shape=(act, act, act, act, jax.ShapeDtypeStruct((t // tm, nkm, width), F32)),
        grid=(t // tm,),
        in_specs=[pl.BlockSpec((tm, d), lambda i: (i, 0)),
                  _const_spec(g.shape), _const_spec(win_bf.shape), _const_spec(sg.shape),
                  _const_spec(wsp.shape), _const_spec(bsp_full.shape)],
        out_specs=(row_spec, row_spec, row_spec, row_spec,
                   pl.BlockSpec((1, nkm, width), lambda i: (i, 0, 0))),
        compiler_params=_params(("parallel",)),
        name="inproj",
    )(x2d, g, win_bf, sg, wsp, bsp_full)


def _bucket_uppers():
    d = np.arange(0, 4 * MAX_DISTANCE)
    max_exact = NUM_BUCKETS // 2
    nf = np.maximum(d, 1).astype(np.float32)
    large = max_exact + (np.log(nf / max_exact) / math.log(MAX_DISTANCE / max_exact)
                         * (NUM_BUCKETS - max_exact)).astype(np.int32)
    bucket = np.where(d < max_exact, d, np.minimum(large, NUM_BUCKETS - 1))
    return [int(d[bucket > b].min()) for b in range(NUM_BUCKETS - 1)]


def _relbias_kernel(rb_ref, own_ref, prev_ref, *, tq, uppers):
    h = pl.program_id(0)
    key = lax.broadcasted_iota(jnp.int32, (tq, tq), 0)
    qry = lax.broadcasted_iota(jnp.int32, (tq, tq), 1)

    def table(dist):
        val = jnp.full((tq, tq), rb_ref[h, NUM_BUCKETS - 1] * LOG2E, F32)
        for b in range(NUM_BUCKETS - 2, -1, -1):
            val = jnp.where(dist < uppers[b], rb_ref[h, b] * LOG2E, val)
        return val

    d = qry - key
    own_ref[0] = jnp.where(d >= 0, table(d), NEG)
    prev_ref[0] = table(d + tq)


def _relbias(rel_bias_t, tq):
    nh = rel_bias_t.shape[0]
    tile = jax.ShapeDtypeStruct((nh, tq, tq), F32)
    spec = pl.BlockSpec((1, tq, tq), lambda h: (h, 0, 0))
    return pl.pallas_call(
        functools.partial(_relbias_kernel, tq=tq, uppers=_bucket_uppers()),
        out_shape=(tile, tile),
        grid=(nh,),
        in_specs=[pl.BlockSpec(memory_space=pltpu.SMEM)],
        out_specs=(spec, spec),
        compiler_params=_params(("parallel",)),
        name="relbias",
    )(rel_bias_t)


def _attn_kernel(c31_ref, q_ref, k_ref, v_ref, km_ref, own_ref, prev_ref, o_ref,
                 vt_scr, add_scr, s_scr, acc_scr, *, tq, nb, hg):
    hd = ATT_HEAD_DIM
    heads = range(hg)
    cols = [slice(g * hd, (g + 1) * hd) for g in heads]
    c31 = [c31_ref[pl.program_id(1) * hg + g] * LOG2E for g in heads]
    for g in heads:
        for n in range(nb):
            vt_scr[g, n // 2, :hd, (n % 2) * tq:(n % 2 + 1) * tq] = (
                v_ref[n * tq:(n + 1) * tq, cols[g]].astype(F32).T.astype(BF16))
        vt_scr[g, :, hd:, :] = jnp.ones((nb // 2, ONES_ROWS, 2 * tq), BF16)

    km2 = []
    for g in heads:
        km = km_ref[0, :, cols[g]]
        km_hi = km.astype(BF16)
        km2.append(jnp.concatenate([km_hi, (km - km_hi.astype(F32)).astype(BF16)], axis=1))

    def part_max(s):
        return jnp.max(s.reshape(s.shape[0] // 8, 8, tq), axis=0)

    def q_tile(cur, carry):
        rows = pl.ds(pl.multiple_of(cur * tq, tq), tq)
        q = [q_ref[rows, cols[g]] for g in heads]
        blk = lax.broadcasted_iota(jnp.int32, (nb, tq), 0)
        past = blk < cur

        negm_prev = []
        for g in heads:
            gate = lax.dot_general(km2[g], jnp.concatenate([q[g], q[g]], axis=1), _NT,
                                   preferred_element_type=F32)
            gv = jnp.where(past, gate, NEG)
            sel = jnp.zeros((nb, tq), F32)
            for _ in range(MOBA_TOPK):
                mx = jnp.max(gv, axis=0, keepdims=True)
                first = jnp.min(jnp.where(gv == mx, blk, nb), axis=0, keepdims=True)
                hit = blk == first
                sel = jnp.where(hit & past, 1.0, sel)
                gv = jnp.where(hit, -jnp.inf, gv)
            negm = (sel - 1.0) * (-NEG)
            add_scr[g] = jnp.where(blk < cur - 1, negm + c31[g], NEG)
            negm_prev.append(jnp.sum(jnp.where(blk == cur - 1, negm, 0.0), axis=0, keepdims=True))

        def far_pair(i, mx8):
            out = []
            for g in heads:
                kb = k_ref[pl.ds(pl.multiple_of(i * (2 * tq), 2 * tq), 2 * tq), cols[g]]
                s2 = lax.dot_general(kb, q[g], _NT, preferred_element_type=F32)
                s_lo = s2[:tq] + add_scr[g, pl.ds(2 * i, 1), :]
                s_hi = s2[tq:] + add_scr[g, pl.ds(2 * i + 1, 1), :]
                s_scr[g, i, :tq, :] = s_lo
                s_scr[g, i, tq:, :] = s_hi
                out.append(jnp.maximum(mx8[g], jnp.maximum(part_max(s_lo), part_max(s_hi))))
            return tuple(out)

        nfar = jnp.maximum(cur - 1, 0)
        mx8 = lax.fori_loop(0, (nfar + 1) // 2, far_pair,
                            tuple(jnp.full((8, tq), NEG, F32) for _ in heads))

        def one_block(n, add_tiles, mx8):
            out = []
            for g in heads:
                kb = k_ref[pl.ds(pl.multiple_of(n * tq, tq), tq), cols[g]]
                s = lax.dot_general(kb, q[g], _NT, preferred_element_type=F32) + add_tiles[g]
                s_scr[g, n // 2, pl.ds(pl.multiple_of((n % 2) * tq, tq), tq), :] = s
                out.append(jnp.maximum(mx8[g], part_max(s)))
            return tuple(out)

        mx8 = lax.cond(cur >= 1,
                       lambda a: one_block(cur - 1, [prev_ref[g] + negm_prev[g] for g in heads], a),
                       lambda a: a, mx8)
        mx8 = one_block(cur, [own_ref[g] for g in heads], mx8)

        @pl.when(cur % 2 == 0)
        def _():
            for g in heads:
                s_scr[g, cur // 2, tq:, :] = jnp.full((tq, tq), NEG, F32)

        m = [jnp.max(mx8[g], axis=0, keepdims=True) for g in heads]
        acc_scr[...] = jnp.zeros_like(acc_scr)

        def pv_pair(i, c):
            for g in heads:
                p = jnp.exp2((s_scr[g, i] - m[g]).astype(BF16))
                acc_scr[g] += _dot(vt_scr[g, i], p)
            return c

        lax.fori_loop(0, cur // 2 + 1, pv_pair, 0)
        for g in heads:
            acc = acc_scr[g]
            o_ref[rows, cols[g]] = (acc[:hd] / acc[hd:hd + 1]).T.astype(BF16)
        return carry

    lax.fori_loop(0, nb, q_tile, 0)


def _attn(c31, q, k, v, kmean, own, prev, batch, seq):
    t, width = q.shape
    tq = MOBA_BLOCK
    nb = seq // tq
    hd = ATT_HEAD_DIM
    hg = 4
    nh = width // hd
    assert nb % 2 == 0 and nh % hg == 0
    seq_spec = pl.BlockSpec((seq, hg * hd), lambda b, h: (b, h), pipeline_mode=pl.Buffered(1))
    seq_in_spec = pl.BlockSpec((seq, hg * hd), lambda b, h: (b, h))
    tile_spec = pl.BlockSpec((hg, tq, tq), lambda b, h: (h, 0, 0))
    return pl.pallas_call(
        functools.partial(_attn_kernel, tq=tq, nb=nb, hg=hg),
        out_shape=jax.ShapeDtypeStruct((t, width), BF16),
        grid=(batch, nh // hg),
        in_specs=[pl.BlockSpec(memory_space=pltpu.SMEM),
                  seq_in_spec, seq_in_spec, seq_in_spec,
                  pl.BlockSpec((1, nb, hg * hd), lambda b, h: (b, 0, h)),
                  tile_spec, tile_spec],
        out_specs=seq_spec,
        scratch_shapes=[pltpu.VMEM((hg, nb // 2, hd + ONES_ROWS, 2 * tq), BF16),
                        pltpu.VMEM((hg, nb, tq), F32),
                        pltpu.VMEM((hg, nb // 2, 2 * tq, tq), F32),
                        pltpu.VMEM((hg, hd + ONES_ROWS, tq), F32)],
        compiler_params=_params(("parallel", "parallel")),
        name="attn",
    )(c31, q, k, v, kmean, own, prev)


def _post_kernel(x_ref, sgu_ref, att_ref, kvm_ref, nmix_ref, wg_ref, bg_ref, wa_ref, wb_ref,
                 wo_ref, nx_ref, wxq_ref, wxo_ref, nffn_ref, wr_ref, br_ref,
                 h2_ref, xn3_ref, route_ref, code_ref, cnt_ref, carry_scr, *, tm, ts, d, xw):
    @pl.when(pl.program_id(0) == 0)
    def _():
        carry_scr[...] = jnp.zeros_like(carry_scr)

    lane = lax.broadcasted_iota(jnp.int32, (ts, LANE), 1)
    r = lax.broadcasted_iota(jnp.int32, (ts, ts), 0)
    c = lax.broadcasted_iota(jnp.int32, (ts, ts), 1)
    tri = jnp.where(c < r, 1.0, 0.0).astype(BF16)

    def sub_tile(rows):
        x = x_ref[rows, :]
        xn = _rms(x, nmix_ref[...]).astype(BF16)
        gates = jax.nn.sigmoid(_dot(xn, wg_ref[...]) + bg_ref[...])
        ya = _dot(sgu_ref[rows, :], wa_ref[...])
        yb = _dot(att_ref[rows, :], wb_ref[...])
        merged = (gates[:, :d] * ya + gates[:, d:] * yb).astype(BF16)
        h1 = x + _dot(merged, wo_ref[...])

        xn2 = _rms(h1, nx_ref[...]).astype(BF16)
        qx = (_dot(xn2, wxq_ref[...]) * (X_HEAD_DIM ** -0.5)).astype(BF16)
        outs = []
        for hh in range(X_HEADS):
            cs = slice(hh * X_HEAD_DIM, (hh + 1) * X_HEAD_DIM)
            vs = slice(xw + hh * X_HEAD_DIM, xw + (hh + 1) * X_HEAD_DIM)
            s = lax.dot_general(qx[:, cs], kvm_ref[:, cs], _NT, preferred_element_type=F32)
            p = jnp.exp(s - jnp.max(s, axis=-1, keepdims=True))
            o = _dot(p.astype(BF16), kvm_ref[:, vs]) / jnp.sum(p, axis=-1, keepdims=True)
            outs.append(o.astype(BF16))
        h2 = h1 + _dot(jnp.concatenate(outs, axis=1), wxo_ref[...])
        h2_ref[rows, :] = h2

        xn3 = _rms(h2, nffn_ref[...])
        _store_tile_rows(xn3_ref, rows.start, xn3)
        x_hi = xn3.astype(BF16)
        x_lo = (xn3 - x_hi.astype(F32)).astype(BF16)
        terms = _dot(jnp.concatenate([x_hi, x_lo], axis=0), wr_ref[...])
        logits = ((terms[:ts, :LANE] + terms[ts:, :LANE]) + (terms[:ts, LANE:] + terms[ts:, LANE:])
                  + br_ref[...])
        is_g = lane < N_GROUPS
        gl = jnp.where(is_g, logits, -jnp.inf)
        gmax = jnp.max(gl, axis=-1, keepdims=True)
        gsel = jnp.min(jnp.where(gl == gmax, lane, LANE), axis=-1, keepdims=True)
        pg = 1.0 / jnp.sum(jnp.exp(gl - gmax), axis=-1, keepdims=True)
        eidx = lane - N_GROUPS
        in_group = (eidx >= 0) & (eidx < N_EXPERTS) & ((eidx // EXPERTS_PER_GROUP) == gsel)
        el = jnp.where(in_group, logits, -jnp.inf)
        m1 = jnp.max(el, axis=-1, keepdims=True)
        i1 = jnp.min(jnp.where(el == m1, lane, LANE), axis=-1, keepdims=True)
        el2 = jnp.where(lane == i1, -jnp.inf, el)
        m2 = jnp.max(el2, axis=-1, keepdims=True)
        i2 = jnp.min(jnp.where(el2 == m2, lane, LANE), axis=-1, keepdims=True)
        e2 = jnp.exp(m2 - m1)
        w1 = pg / (1.0 + e2)
        w2 = pg * e2 / (1.0 + e2)
        eid1 = i1 - N_GROUPS
        eid2 = i2 - N_GROUPS

        oh1f = jnp.where(lane == eid1, 1.0, 0.0)
        oh2f = jnp.where(lane == eid2, 1.0, 0.0)
        c1 = _dot(tri, oh1f.astype(BF16))
        c2 = _dot(tri, oh2f.astype(BF16))
        tot1 = jnp.sum(oh1f, axis=0, keepdims=True)
        tot2 = jnp.sum(oh2f, axis=0, keepdims=True)
        return eid1, eid2, w1, w2, oh1f, oh2f, c1, c2, tot1, tot2

    parts = [sub_tile(slice(j * ts, (j + 1) * ts)) for j in range(tm // ts)]

    carry = carry_scr[...]
    for j, (eid1, eid2, w1, w2, oh1f, oh2f, c1, c2, tot1, tot2) in enumerate(parts):
        r1 = jnp.sum(oh1f * (c1 + carry), axis=-1, keepdims=True)
        r2 = jnp.sum(oh2f * (c2 + carry + tot1), axis=-1, keepdims=True)
        carry = carry + tot1 + tot2
        route = jnp.where(lane == 0, eid1.astype(F32), 0.0)
        route = jnp.where(lane == 1, eid2.astype(F32), route)
        route = jnp.where(lane == 2, w1, route)
        route = jnp.where(lane == 3, w2, route)
        route = jnp.where(lane == 4, r1, route)
        route = jnp.where(lane == 5, r2, route)
        route_ref[j * ts:(j + 1) * ts, :] = route
        route_t = route.T
        for slot in range(2):
            code = ((route_t[slot:slot + 1].astype(jnp.int32) << RANK_BITS)
                    | route_t[4 + slot:5 + slot].astype(jnp.int32))
            code_ref[slot:slot + 1, j * ts:(j + 1) * ts] = code
    carry_scr[...] = carry
    cnt_ref[...] = carry


def _post(x2d, sgu, att, kvm, consts, tm, seq):
    t, d = x2d.shape
    mlen = kvm.shape[0] // (t // seq)
    xw = kvm.shape[1] // 2
    tiles_per_batch = seq // tm
    row = lambda w: pl.BlockSpec((tm, w), lambda i: (i, 0))
    return pl.pallas_call(
        functools.partial(_post_kernel, tm=tm, ts=min(tm, 256), d=d, xw=xw),
        out_shape=(jax.ShapeDtypeStruct((t, d), F32), jax.ShapeDtypeStruct((t * d // LANE, LANE), F32),
                   jax.ShapeDtypeStruct((t, LANE), F32), jax.ShapeDtypeStruct((2, t), jnp.int32),
                   jax.ShapeDtypeStruct((1, LANE), F32)),
        grid=(t // tm,),
        in_specs=[row(d), row(sgu.shape[1]), row(att.shape[1]),
                  pl.BlockSpec((mlen, kvm.shape[1]), lambda i: (i // tiles_per_batch, 0))]
                 + [_const_spec(c.shape) for c in consts],
        out_specs=(row(d), pl.BlockSpec((tm * d // LANE, LANE), lambda i: (i, 0)), row(LANE),
                   pl.BlockSpec((2, tm), lambda i: (0, i)), pl.BlockSpec((1, LANE), lambda i: (0, 0))),
        scratch_shapes=[pltpu.VMEM((1, LANE), F32)],
        compiler_params=_params(("arbitrary",)),
        name="post",
    )(x2d, sgu, att, kvm, *consts)


def _pad_pair_id(row, t):
    return 2 * t + (row & (2 * MOE_BLOCK - 1))


def _sc_invert(pad_starts, code, n):
    info = plsc.get_sparse_core_info()
    nw = info.num_cores * info.num_subcores
    lanes = info.num_lanes
    nslot, t = code.shape
    per_slot = nw // nslot
    chunk = t // per_slot
    nwin = chunk // SC_WINDOW
    assert per_slot * nslot == nw and chunk * per_slot == t and nwin * SC_WINDOW == chunk

    def body(start_hbm, code_hbm, o_hbm, start_v, code_v, dest_v, vals_v):
        wid = lax.axis_index("s") * info.num_cores + lax.axis_index("c")
        slot = wid // per_slot
        off = pl.multiple_of((wid % per_slot) * chunk, chunk)
        pltpu.sync_copy(start_hbm, start_v)
        pltpu.sync_copy(code_hbm.at[slot, pl.ds(off, chunk)], code_v)
        lane = lax.iota(jnp.int32, lanes)

        @pl.loop(0, nwin)
        def _(j):
            first = slot * t + off + j * SC_WINDOW
            for l in range(0, SC_WINDOW, lanes):
                c = code_v[pl.ds(j * SC_WINDOW + l, lanes)]
                seg = plsc.load_gather(start_v, [lax.shift_right_logical(c, RANK_BITS)])
                dest_v[j, pl.ds(l, lanes)] = seg + (c & ((1 << RANK_BITS) - 1))
            for r in range(SC_WINDOW):
                vals_v[r, :] = lane + (first + r)
            pltpu.sync_copy(vals_v, o_hbm.at[dest_v.at[j]])

    return pl.kernel(
        body,
        out_type=jax.ShapeDtypeStruct((n, lanes), jnp.int32),
        mesh=plsc.VectorSubcoreMesh(core_axis_name="c", subcore_axis_name="s"),
        scratch_types=[pltpu.VMEM((N_EXPERTS,), jnp.int32), pltpu.VMEM((chunk,), jnp.int32),
                       pltpu.VMEM((nwin, SC_WINDOW), jnp.int32), pltpu.VMEM((SC_WINDOW, lanes), jnp.int32)],
        compiler_params=pltpu.CompilerParams(needs_layout_passes=False, use_tc_tiling_on_sc=False),
        name="sc_invert",
    )(pad_starts, code)


def _expert_kernel(be_ref, first_ref, nxt_ref, nused_ref, ids_a_ref, ids_b_ref,
                   xn_hbm, wg_hbm, wu_hbm, wd_hbm, out_hbm,
                   xbuf, ybuf, wg_st, wu_st, wd_st, wg_bf, wu_bf, wd_bf, wsem, gsem, ssem, *, t):
    g = pl.program_id(0)
    rows = MOE_BLOCK
    k = xbuf.shape[1] // rows

    def fetch(e):
        return (pltpu.make_async_copy(wg_hbm.at[e], wg_st, wsem.at[0]),
                pltpu.make_async_copy(wu_hbm.at[e], wu_st, wsem.at[1]),
                pltpu.make_async_copy(wd_hbm.at[e], wd_st, wsem.at[2]))

    def token_of(p):
        if t & (t - 1) == 0:
            return p & (t - 1)
        return jnp.where(p >= 2 * t, p - 2 * t, jnp.where(p >= t, p - t, p))

    def gather_row(ids_ref, off, r, slot):
        tok = token_of(ids_ref[0, 0, off + r])
        return pltpu.make_async_copy(xn_hbm.at[pl.ds(pl.multiple_of(tok * k, k), k)],
                                     xbuf.at[slot, pl.ds(r * k, k)], gsem.at[slot])

    def gather_all(slot):
        return pltpu.make_async_copy(xn_hbm.at[pl.ds(0, rows * k)], xbuf.at[slot], gsem.at[slot])

    def scatter_all(slot):
        return pltpu.make_async_copy(ybuf.at[slot], out_hbm.at[pl.ds(0, rows * k)], ssem.at[slot])

    def switch_weights(m):
        @pl.when(first_ref[m] == 1)
        def _():
            for cp in fetch(be_ref[m]):
                cp.wait()
            wg_bf[...] = wg_st[...].astype(BF16)
            wu_bf[...] = wu_st[...].astype(BF16)
            wd_bf[...] = wd_st[...].astype(BF16)

            @pl.when(nxt_ref[m] >= 0)
            def _():
                for cp in fetch(nxt_ref[m]):
                    cp.start()

    def phase(slot, next_ids, next_off, prev_ids, prev_off):
        other = 1 - slot
        for r in range(rows):
            gather_row(next_ids, next_off, r, other).start(priority=r % 2)
        for r in range(rows):
            dst = prev_ids[0, 0, prev_off + r]
            pltpu.make_async_copy(ybuf.at[other, pl.ds(r * k, k)],
                                  out_hbm.at[pl.ds(pl.multiple_of(dst * k, k), k)],
                                  ssem.at[other]).start(priority=r % 2)
        xb = _load_tile_rows(xbuf, rows, k, lead=(slot,)).astype(BF16)
        hid = jax.nn.silu(_dot(xb, wg_bf[...])) * _dot(xb, wu_bf[...])
        _store_tile_rows(ybuf, 0, _dot(hid.astype(BF16), wd_bf[...]), lead=(slot,))

    @pl.when(g == 0)
    def _():
        for cp in fetch(be_ref[0]):
            cp.start()

        def first_block(r, carry):
            gather_row(ids_a_ref, rows, r, 0).start()
            return carry
        lax.fori_loop(0, rows, first_block, 0)
        ybuf[...] = jnp.zeros_like(ybuf)
        for half in range(2):
            cp = pltpu.make_async_copy(
                ybuf.at[half], out_hbm.at[pl.ds((2 * t + half * rows) * k, rows * k)], ssem.at[half])
            cp.start()
            cp.wait()

    nused = nused_ref[0]

    def run_phase(slot, m, off):
        @pl.when(m <= nused)
        def _():
            switch_weights(m)
            gather_all(slot).wait()

            @pl.when(m >= 1)
            def _():
                scatter_all(slot).wait()
            phase(slot, ids_b_ref, off, ids_a_ref, off)

            @pl.when(m == nused)
            def _():
                gather_all(1 - slot).wait()
                scatter_all(1 - slot).wait()

    run_phase(0, 2 * g, 0)
    run_phase(1, 2 * g + 1, rows)


def _expert(blk_expert, first, nxt, nused, ids, xn3, w_gate, w_up, w_down):
    d, eh = w_gate.shape[1:]
    k = d // LANE
    t = xn3.shape[0] // k
    steps = ids.shape[0] - 1
    assert blk_expert.shape[0] == 2 * steps
    hbm = pl.BlockSpec(memory_space=pl.ANY)
    idx_block = lambda f: pl.BlockSpec((1, 1, 2 * MOE_BLOCK), f, memory_space=pltpu.SMEM)
    grid_spec = pltpu.PrefetchScalarGridSpec(
        num_scalar_prefetch=4,
        grid=(steps,),
        in_specs=[idx_block(lambda g, *_: (g, 0, 0)), idx_block(lambda g, *_: (g + 1, 0, 0)),
                  hbm, hbm, hbm, hbm],
        out_specs=hbm,
        scratch_shapes=[pltpu.VMEM((2, MOE_BLOCK * k, LANE), F32), pltpu.VMEM((2, MOE_BLOCK * k, LANE), F32),
                        pltpu.VMEM((d, eh), F32), pltpu.VMEM((d, eh), F32), pltpu.VMEM((eh, d), F32),
                        pltpu.VMEM((d, eh), BF16), pltpu.VMEM((d, eh), BF16), pltpu.VMEM((eh, d), BF16),
                        pltpu.SemaphoreType.DMA((3,)), pltpu.SemaphoreType.DMA((2,)),
                        pltpu.SemaphoreType.DMA((2,))],
    )
    return pl.pallas_call(
        functools.partial(_expert_kernel, t=t),
        out_shape=jax.ShapeDtypeStruct(((2 * t + 2 * MOE_BLOCK) * k, LANE), F32),
        grid_spec=grid_spec,
        compiler_params=_params(("arbitrary",)),
        name="expert",
    )(blk_expert, first, nxt, nused, ids, ids, xn3, w_gate, w_up, w_down)


def _combine_kernel(h2_ref, y0_ref, y1_ref, route_ref, g_ref, o_ref):
    tm, d = h2_ref.shape
    route = route_ref[...]
    y0 = _load_tile_rows(y0_ref, tm, d // LANE)
    y1 = _load_tile_rows(y1_ref, tm, d // LANE)
    y = route[:, 2:3] * y0 + route[:, 3:4] * y1
    o_ref[...] = _rms(h2_ref[...] + y, g_ref[...])


def _combine(h2, ypairs, route, g, tm):
    t, d = h2.shape
    k = d // LANE
    assert ypairs.shape[0] % (tm * k) == 0
    return pl.pallas_call(
        _combine_kernel,
        out_shape=jax.ShapeDtypeStruct((t, d), F32),
        grid=(t // tm,),
        in_specs=[pl.BlockSpec((tm, d), lambda i: (i, 0)),
                  pl.BlockSpec((tm * k, LANE), lambda i: (i, 0)),
                  pl.BlockSpec((tm * k, LANE), lambda i: (i + t // tm, 0)),
                  pl.BlockSpec((tm, LANE), lambda i: (i, 0)),
                  _const_spec(g.shape)],
        out_specs=pl.BlockSpec((tm, d), lambda i: (i, 0)),
        compiler_params=_params(("parallel",)),
        name="combine",
    )(h2, ypairs, ypairs, route, g)


def kernel(x, mem, norm_mix, w_in, w_gate, b_gate, sgu_norm, w_spatial, b_spatial, w_branch_a,
           w_branch_b, rel_bias, w_o, norm_x, norm_mem, w_xq, w_xkv, w_xo, norm_ffn,
           w_router_group, b_router_group, w_router_expert, b_router_expert,
           w_e_gate, w_e_up, w_e_down, norm_final):
    batch, seq, d = x.shape
    assert norm_mix.shape[0] == 1, "one layer"
    assert seq % MOBA_BLOCK == 0 and d % LANE == 0
    t = batch * seq
    tm_in, tm_post = 512, 512
    row = lambda a: a.reshape(1, -1).astype(F32)
    bf = lambda a: a.astype(BF16)

    x2d = x.reshape(t, d)
    kvm = _memkv(mem.reshape(-1, d), row(norm_mem[0]), bf(w_xkv[0]))

    width = w_in.shape[2] // 5
    bsp_full = jnp.repeat(b_spatial[0].T, width // SGU_GROUPS, axis=1)
    sgu, q, k, v, kmean = _inproj(x2d, row(norm_mix[0]), bf(w_in[0]), row(sgu_norm[0]),
                                  w_spatial[0], bsp_full, tm_in)

    rel_t = rel_bias.T.astype(F32)
    own, prev = _relbias(rel_t, MOBA_BLOCK)
    att = _attn(rel_t[:, NUM_BUCKETS - 1], q, k, v,
                kmean.reshape(batch, seq // MOBA_BLOCK, width), own, prev, batch, seq)

    lane_pad = LANE - N_GROUPS - N_EXPERTS
    w_router = jnp.concatenate([w_router_group[0], w_router_expert[0], jnp.zeros((d, lane_pad), F32)], axis=1)
    b_router = jnp.concatenate([b_router_group[0], b_router_expert[0], jnp.zeros((lane_pad,), F32)])
    wr_hi = bf(w_router)
    wr_lo = bf(w_router - wr_hi.astype(F32))
    consts = [row(norm_mix[0]), bf(w_gate[0]), row(b_gate[0]), bf(w_branch_a[0]), bf(w_branch_b[0]),
              bf(w_o[0]), row(norm_x[0]), bf(w_xq[0]), bf(w_xo[0]), row(norm_ffn[0]),
              jnp.concatenate([wr_hi, wr_lo], axis=1), row(b_router)]
    assert 2 * t <= 1 << RANK_BITS
    h2, xn3, route, code, cnt = _post(x2d, sgu, att, kvm, consts, tm_post, seq)

    counts = cnt[0, :N_EXPERTS].astype(jnp.int32)
    padded = (counts + MOE_BLOCK - 1) // MOE_BLOCK * MOE_BLOCK
    pad_ends = jnp.cumsum(padded)
    pad_starts = pad_ends - padded
    nblk = -(-(2 * t) // MOE_BLOCK) + N_EXPERTS
    assert nblk % 2 == 0
    nphase = nblk + 2
    blk_idx = jnp.arange(nphase, dtype=jnp.int32)
    blk_expert = jnp.minimum(
        jnp.sum((pad_ends[None, :] <= blk_idx[:, None] * MOE_BLOCK).astype(jnp.int32), axis=1),
        N_EXPERTS - 1)
    nused = pad_ends[-1] // MOE_BLOCK
    prev_expert = jnp.concatenate([jnp.full((1,), -1, jnp.int32), blk_expert[:-1]])
    first = ((blk_idx < nused) & (blk_expert != prev_expert)).astype(jnp.int32)
    eidx = jnp.arange(N_EXPERTS, dtype=jnp.int32)
    later = (eidx[None, :] > eidx[:, None]) & (padded[None, :] > 0)
    next_expert = jnp.min(jnp.where(later, eidx[None, :], N_EXPERTS), axis=1)
    next_expert = jnp.where(next_expert == N_EXPERTS, -1, next_expert)
    onehot = (blk_expert[:, None] == eidx[None, :]).astype(jnp.int32)
    nxt = jnp.where(first == 1, jnp.sum(onehot * next_expert[None, :], axis=1), -1).astype(jnp.int32)
    seg_left = jnp.sum(onehot * (pad_starts + counts)[None, :], axis=1) - blk_idx * MOE_BLOCK
    nvalid = jnp.where(blk_idx < nused, jnp.clip(seg_left, 0, MOE_BLOCK), 0).astype(jnp.int32)[:nblk]

    inv = _sc_invert(pad_starts, code, nblk * MOE_BLOCK)[:, 0].reshape(nblk, MOE_BLOCK)
    row_in_blk = jnp.arange(MOE_BLOCK, dtype=jnp.int32)[None, :]
    pad_ids = _pad_pair_id(jnp.arange(nblk, dtype=jnp.int32)[:, None] * MOE_BLOCK + row_in_blk, t)
    inv = jnp.where(row_in_blk < nvalid[:, None], inv, pad_ids).reshape(nblk * MOE_BLOCK)
    dummy = lambda m: _pad_pair_id(m * MOE_BLOCK + jnp.arange(MOE_BLOCK, dtype=jnp.int32), t)
    ids = jnp.concatenate([dummy(-1), inv] + [dummy(nblk + m) for m in range(3)])
    ypairs = _expert(blk_expert, first, nxt, nused.reshape(1), ids.reshape(-1, 1, 2 * MOE_BLOCK), xn3,
                     w_e_gate[0], w_e_up[0], w_e_down[0])
    out = _combine(h2, ypairs, route, row(norm_final), tm_post)
    return out.reshape(batch, seq, d)
```

```python
import functools
import math

import numpy as np
import jax
import jax.numpy as jnp
from jax import lax
from jax.experimental import pallas as pl
from jax.experimental.pallas import tpu as pltpu
from jax.experimental.pallas import tpu_sc as plsc

F32 = jnp.float32
BF16 = jnp.bfloat16

EPS = 1e-6
NEG = -1e30
LOG2E = math.log2(math.e)
LANE = 128
ONES_ROWS = 16
RANK_BITS = 20
SC_WINDOW = 128

SGU_GROUPS = 8
CHUNK = 128
ATT_HEADS = 8
ATT_HEAD_DIM = 128
MOBA_BLOCK = 256
MOBA_TOPK = 3
NUM_BUCKETS = 32
MAX_DISTANCE = 128
X_HEADS = 4
X_HEAD_DIM = 128
N_GROUPS = 8
EXPERTS_PER_GROUP = 8
N_EXPERTS = N_GROUPS * EXPERTS_PER_GROUP
MOE_BLOCK = 256

VMEM_LIMIT = 56 * 1024 * 1024

_NT = (((1,), (1,)), ((), ()))


def _rms(x, g):
    return x * lax.rsqrt(jnp.mean(x * x, axis=-1, keepdims=True) + EPS) * g


def _gelu(x):
    c = math.sqrt(2.0 / math.pi)
    return x * (0.5 * (1.0 + jnp.tanh(c * (x + 0.044715 * (x * x * x)))))


def _dot(a, b):
    return jnp.dot(a, b, preferred_element_type=F32)


def _store_tile_rows(ref, first_row, val, lead=()):
    n, d = val.shape
    k = d // LANE
    for j in range(k):
        ref[lead + (pl.ds(first_row * k + j, n, stride=k), slice(None))] = val[:, j * LANE:(j + 1) * LANE]


def _load_tile_rows(ref, n, k, lead=()):
    return jnp.concatenate([ref[lead + (pl.ds(j, n, stride=k), slice(None))] for j in range(k)], axis=1)


def _const_spec(shape):
    nd = len(shape)
    return pl.BlockSpec(shape, lambda *_: (0,) * nd, pipeline_mode=pl.Buffered(1))


def _params(sem):
    return pltpu.CompilerParams(dimension_semantics=sem, vmem_limit_bytes=VMEM_LIMIT)


def _memkv_kernel(mem_ref, g_ref, w_ref, o_ref):
    mn = _rms(mem_ref[...], g_ref[...]).astype(BF16)
    o_ref[...] = _dot(mn, w_ref[...]).astype(BF16)


def _memkv(mem2d, g, w_bf):
    n, d = mem2d.shape
    tm = 256
    return pl.pallas_call(
        _memkv_kernel,
        out_shape=jax.ShapeDtypeStruct((n, w_bf.shape[1]), BF16),
        grid=(n // tm,),
        in_specs=[pl.BlockSpec((tm, d), lambda i: (i, 0)),
                  _const_spec(g.shape), _const_spec(w_bf.shape)],
        out_specs=pl.BlockSpec((tm, w_bf.shape[1]), lambda i: (i, 0)),
        compiler_params=_params(("parallel",)),
        name="memkv",
    )(mem2d, g, w_bf)


def _inproj_kernel(x_ref, g_ref, win_ref, sg_ref, wsp_ref, bsp_ref,
                   sgu_ref, q_ref, k_ref, v_ref, km_ref, *, tm, width, scale):
    xn = _rms(x_ref[...], g_ref[...]).astype(BF16)
    u = _gelu(_dot(xn, win_ref[:, 0:width]))
    vv = _gelu(_dot(xn, win_ref[:, width:2 * width]))
    vn = _rms(vv, sg_ref[...]).astype(BF16)
    gd = width // SGU_GROUPS
    row = lax.broadcasted_iota(jnp.int32, (CHUNK, CHUNK), 0)
    col = lax.broadcasted_iota(jnp.int32, (CHUNK, CHUNK), 1)
    causal = col <= row
    wsp = [jnp.where(causal, wsp_ref[g], 0.0).astype(BF16) for g in range(SGU_GROUPS)]
    bias = bsp_ref[...]
    for c in range(tm // CHUNK):
        rs = slice(c * CHUNK, (c + 1) * CHUNK)
        mixed = jnp.concatenate(
            [_dot(wsp[g], vn[rs, g * gd:(g + 1) * gd]) for g in range(SGU_GROUPS)], axis=1)
        sgu_ref[rs, :] = (u[rs, :] * (mixed + bias)).astype(BF16)
    q_ref[...] = (_dot(xn, win_ref[:, 2 * width:3 * width]) * scale).astype(BF16)
    k = _dot(xn, win_ref[:, 3 * width:4 * width])
    k_ref[...] = k.astype(BF16)
    for j in range(tm // MOBA_BLOCK):
        km_ref[0, j:j + 1, :] = jnp.mean(k[j * MOBA_BLOCK:(j + 1) * MOBA_BLOCK, :], axis=0, keepdims=True)
    v_ref[...] = _dot(xn, win_ref[:, 4 * width:5 * width]).astype(BF16)


def _inproj(x2d, g, win_bf, sg, wsp, bsp_full, tm):
    t, d = x2d.shape
    width = win_bf.shape[1] // 5
    nkm = tm // MOBA_BLOCK
    row_spec = pl.BlockSpec((tm, width), lambda i: (i, 0))
    act = jax.ShapeDtypeStruct((t, width), BF16)
    return pl.pallas_call(
        functools.partial(_inproj_kernel, tm=tm, width=width, scale=ATT_HEAD_DIM ** -0.5 * LOG2E),
        out_shape=(act, act, act, act, jax.ShapeDtypeStruct((t // tm, nkm, width), F32)),
        grid=(t // tm,),
        in_specs=[pl.BlockSpec((tm, d), lambda i: (i, 0)),
                  _const_spec(g.shape), _const_spec(win_bf.shape), _const_spec(sg.shape),
                  _const_spec(wsp.shape), _const_spec(bsp_full.shape)],
        out_specs=(row_spec, row_spec, row_spec, row_spec,
                   pl.BlockSpec((1, nkm, width), lambda i: (i, 0, 0))),
        compiler_params=_params(("parallel",)),
        name="inproj",
    )(x2d, g, win_bf, sg, wsp, bsp_full)


def _bucket_uppers():
    d = np.arange(0, 4 * MAX_DISTANCE)
    max_exact = NUM_BUCKETS // 2
    nf = np.maximum(d, 1).astype(np.float32)
    large = max_exact + (np.log(nf / max_exact) / math.log(MAX_DISTANCE / max_exact)
                         * (NUM_BUCKETS - max_exact)).astype(np.int32)
    bucket = np.where(d < max_exact, d, np.minimum(large, NUM_BUCKETS - 1))
    return [int(d[bucket > b].min()) for b in range(NUM_BUCKETS - 1)]


def _relbias_kernel(rb_ref, own_ref, prev_ref, *, tq, uppers):
    h = pl.program_id(0)
    key = lax.broadcasted_iota(jnp.int32, (tq, tq), 0)
    qry = lax.broadcasted_iota(jnp.int32, (tq, tq), 1)

    def table(dist):
        val = jnp.full((tq, tq), rb_ref[h, NUM_BUCKETS - 1] * LOG2E, F32)
        for b in range(NUM_BUCKETS - 2, -1, -1):
            val = jnp.where(dist < uppers[b], rb_ref[h, b] * LOG2E, val)
        return val

    d = qry - key
    own_ref[0] = jnp.where(d >= 0, table(d), NEG)
    prev_ref[0] = table(d + tq)


def _relbias(rel_bias_t, tq):
    nh = rel_bias_t.shape[0]
    tile = jax.ShapeDtypeStruct((nh, tq, tq), F32)
    spec = pl.BlockSpec((1, tq, tq), lambda h: (h, 0, 0))
    return pl.pallas_call(
        functools.partial(_relbias_kernel, tq=tq, uppers=_bucket_uppers()),
        out_shape=(tile, tile),
        grid=(nh,),
        in_specs=[pl.BlockSpec(memory_space=pltpu.SMEM)],
        out_specs=(spec, spec),
        compiler_params=_params(("parallel",)),
        name="relbias",
    )(rel_bias_t)


def _attn_kernel(c31_ref, q_ref, k_ref, v_ref, km_ref, own_ref, prev_ref, o_ref,
                 vt_scr, add_scr, s_scr, acc_scr, *, tq, nb, hg):
    hd = ATT_HEAD_DIM
    heads = range(hg)
    cols = [slice(g * hd, (g + 1) * hd) for g in heads]
    c31 = [c31_ref[pl.program_id(1) * hg + g] * LOG2E for g in heads]
    for g in heads:
        for n in range(nb):
            vt_scr[g, n // 2, :hd, (n % 2) * tq:(n % 2 + 1) * tq] = (
                v_ref[n * tq:(n + 1) * tq, cols[g]].astype(F32).T.astype(BF16))
        vt_scr[g, :, hd:, :] = jnp.ones((nb // 2, ONES_ROWS, 2 * tq), BF16)

    km2 = []
    for g in heads:
        km = km_ref[0, :, cols[g]]
        km_hi = km.astype(BF16)
        km2.append(jnp.concatenate([km_hi, (km - km_hi.astype(F32)).astype(BF16)], axis=1))

    def part_max(s):
        return jnp.max(s.reshape(s.shape[0] // 8, 8, tq), axis=0)

    def q_tile(cur, carry):
        rows = pl.ds(pl.multiple_of(cur * tq, tq), tq)
        q = [q_ref[rows, cols[g]] for g in heads]
        blk = lax.broadcasted_iota(jnp.int32, (nb, tq), 0)
        past = blk < cur

        negm_prev = []
        for g in heads:
            gate = lax.dot_general(km2[g], jnp.concatenate([q[g], q[g]], axis=1), _NT,
                                   preferred_element_type=F32)
            gv = jnp.where(past, gate, NEG)
            sel = jnp.zeros((nb, tq), F32)
            for _ in range(MOBA_TOPK):
                mx = jnp.max(gv, axis=0, keepdims=True)
                first = jnp.min(jnp.where(gv == mx, blk, nb), axis=0, keepdims=True)
                hit = blk == first
                sel = jnp.where(hit & past, 1.0, sel)
                gv = jnp.where(hit, -jnp.inf, gv)
            negm = (sel - 1.0) * (-NEG)
            add_scr[g] = jnp.where(blk < cur - 1, negm + c31[g], NEG)
            negm_prev.append(jnp.sum(jnp.where(blk == cur - 1, negm, 0.0), axis=0, keepdims=True))

        def far_pair(i, mx8):
            out = []
            for g in heads:
                kb = k_ref[pl.ds(pl.multiple_of(i * (2 * tq), 2 * tq), 2 * tq), cols[g]]
                s2 = lax.dot_general(kb, q[g], _NT, preferred_element_type=F32)
                s_lo = s2[:tq] + add_scr[g, pl.ds(2 * i, 1), :]
                s_hi = s2[tq:] + add_scr[g, pl.ds(2 * i + 1, 1), :]
                s_scr[g, i, :tq, :] = s_lo
                s_scr[g, i, tq:, :] = s_hi
                out.append(jnp.maximum(mx8[g], jnp.maximum(part_max(s_lo), part_max(s_hi))))
            return tuple(out)

        nfar = jnp.maximum(cur - 1, 0)
        mx8 = lax.fori_loop(0, (nfar + 1) // 2, far_pair,
                            tuple(jnp.full((8, tq), NEG, F32) for _ in heads))

        def one_block(n, add_tiles, mx8):
            out = []
            for g in heads:
                kb = k_ref[pl.ds(pl.multiple_of(n * tq, tq), tq), cols[g]]
                s = lax.dot_general(kb, q[g], _NT, preferred_element_type=F32) + add_tiles[g]
                s_scr[g, n // 2, pl.ds(pl.multiple_of((n % 2) * tq, tq), tq), :] = s
                out.append(jnp.maximum(mx8[g], part_max(s)))
            return tuple(out)

        mx8 = lax.cond(cur >= 1,
                       lambda a: one_block(cur - 1, [prev_ref[g] + negm_prev[g] for g in heads], a),
                       lambda a: a, mx8)
        mx8 = one_block(cur, [own_ref[g] for g in heads], mx8)

        @pl.when(cur % 2 == 0)
        def _():
            for g in heads:
                s_scr[g, cur // 2, tq:, :] = jnp.full((tq, tq), NEG, F32)

        m = [jnp.max(mx8[g], axis=0, keepdims=True) for g in heads]
        acc_scr[...] = jnp.zeros_like(acc_scr)

        def pv_pair(i, c):
            for g in heads:
                p = jnp.exp2((s_scr[g, i] - m[g]).astype(BF16))
                acc_scr[g] += _dot(vt_scr[g, i], p)
            return c

        lax.fori_loop(0, cur // 2 + 1, pv_pair, 0)
        for g in heads:
            acc = acc_scr[g]
            o_ref[rows, cols[g]] = (acc[:hd] / acc[hd:hd + 1]).T.astype(BF16)
        return carry

    lax.fori_loop(0, nb, q_tile, 0)


def _attn(c31, q, k, v, kmean, own, prev, batch, seq):
    t, width = q.shape
    tq = MOBA_BLOCK
    nb = seq // tq
    hd = ATT_HEAD_DIM
    hg = 4
    nh = width // hd
    assert nb % 2 == 0 and nh % hg == 0
    seq_spec = pl.BlockSpec((seq, hg * hd), lambda b, h: (b, h), pipeline_mode=pl.Buffered(1))
    seq_in_spec = pl.BlockSpec((seq, hg * hd), lambda b, h: (b, h))
    tile_spec = pl.BlockSpec((hg, tq, tq), lambda b, h: (h, 0, 0))
    return pl.pallas_call(
        functools.partial(_attn_kernel, tq=tq, nb=nb, hg=hg),
        out_shape=jax.ShapeDtypeStruct((t, width), BF16),
        grid=(batch, nh // hg),
        in_specs=[pl.BlockSpec(memory_space=pltpu.SMEM),
                  seq_in_spec, seq_in_spec, seq_in_spec,
                  pl.BlockSpec((1, nb, hg * hd), lambda b, h: (b, 0, h)),
                  tile_spec, tile_spec],
        out_specs=seq_spec,
        scratch_shapes=[pltpu.VMEM((hg, nb // 2, hd + ONES_ROWS, 2 * tq), BF16),
                        pltpu.VMEM((hg, nb, tq), F32),
                        pltpu.VMEM((hg, nb // 2, 2 * tq, tq), F32),
                        pltpu.VMEM((hg, hd + ONES_ROWS, tq), F32)],
        compiler_params=_params(("parallel", "parallel")),
        name="attn",
    )(c31, q, k, v, kmean, own, prev)


def _post_kernel(x_ref, sgu_ref, att_ref, kvm_ref, nmix_ref, wg_ref, bg_ref, wa_ref, wb_ref,
                 wo_ref, nx_ref, wxq_ref, wxo_ref, nffn_ref, wr_ref, br_ref,
                 h2_ref, xn3_ref, route_ref, code_ref, cnt_ref, carry_scr, *, tm, ts, d, xw):
    @pl.when(pl.program_id(0) == 0)
    def _():
        carry_scr[...] = jnp.zeros_like(carry_scr)

    lane = lax.broadcasted_iota(jnp.int32, (ts, LANE), 1)
    r = lax.broadcasted_iota(jnp.int32, (ts, ts), 0)
    c = lax.broadcasted_iota(jnp.int32, (ts, ts), 1)
    tri = jnp.where(c < r, 1.0, 0.0).astype(BF16)

    def sub_tile(rows):
        x = x_ref[rows, :]
        xn = _rms(x, nmix_ref[...]).astype(BF16)
        gates = jax.nn.sigmoid(_dot(xn, wg_ref[...]) + bg_ref[...])
        ya = _dot(sgu_ref[rows, :], wa_ref[...])
        yb = _dot(att_ref[rows, :], wb_ref[...])
        merged = (gates[:, :d] * ya + gates[:, d:] * yb).astype(BF16)
        h1 = x + _dot(merged, wo_ref[...])

        xn2 = _rms(h1, nx_ref[...]).astype(BF16)
        qx = (_dot(xn2, wxq_ref[...]) * (X_HEAD_DIM ** -0.5)).astype(BF16)
        outs = []
        for hh in range(X_HEADS):
            cs = slice(hh * X_HEAD_DIM, (hh + 1) * X_HEAD_DIM)
            vs = slice(xw + hh * X_HEAD_DIM, xw + (hh + 1) * X_HEAD_DIM)
            s = lax.dot_general(qx[:, cs], kvm_ref[:, cs], _NT, preferred_element_type=F32)
            p = jnp.exp(s - jnp.max(s, axis=-1, keepdims=True))
            o = _dot(p.astype(BF16), kvm_ref[:, vs]) / jnp.sum(p, axis=-1, keepdims=True)
            outs.append(o.astype(BF16))
        h2 = h1 + _dot(jnp.concatenate(outs, axis=1), wxo_ref[...])
        h2_ref[rows, :] = h2

        xn3 = _rms(h2, nffn_ref[...])
        _store_tile_rows(xn3_ref, rows.start, xn3)
        x_hi = xn3.astype(BF16)
        x_lo = (xn3 - x_hi.astype(F32)).astype(BF16)
        terms = _dot(jnp.concatenate([x_hi, x_lo], axis=0), wr_ref[...])
        logits = ((terms[:ts, :LANE] + terms[ts:, :LANE]) + (terms[:ts, LANE:] + terms[ts:, LANE:])
                  + br_ref[...])
        is_g = lane < N_GROUPS
        gl = jnp.where(is_g, logits, -jnp.inf)
        gmax = jnp.max(gl, axis=-1, keepdims=True)
        gsel = jnp.min(jnp.where(gl == gmax, lane, LANE), axis=-1, keepdims=True)
        pg = 1.0 / jnp.sum(jnp.exp(gl - gmax), axis=-1, keepdims=True)
        eidx = lane - N_GROUPS
        in_group = (eidx >= 0) & (eidx < N_EXPERTS) & ((eidx // EXPERTS_PER_GROUP) == gsel)
        el = jnp.where(in_group, logits, -jnp.inf)
        m1 = jnp.max(el, axis=-1, keepdims=True)
        i1 = jnp.min(jnp.where(el == m1, lane, LANE), axis=-1, keepdims=True)
        el2 = jnp.where(lane == i1, -jnp.inf, el)
        m2 = jnp.max(el2, axis=-1, keepdims=True)
        i2 = jnp.min(jnp.where(el2 == m2, lane, LANE), axis=-1, keepdims=True)
        e2 = jnp.exp(m2 - m1)
        w1 = pg / (1.0 + e2)
        w2 = pg * e2 / (1.0 + e2)
        eid1 = i1 - N_GROUPS
        eid2 = i2 - N_GROUPS

        oh1f = jnp.where(lane == eid1, 1.0, 0.0)
        oh2f = jnp.where(lane == eid2, 1.0, 0.0)
        c1 = _dot(tri, oh1f.astype(BF16))
        c2 = _dot(tri, oh2f.astype(BF16))
        tot1 = jnp.sum(oh1f, axis=0, keepdims=True)
        tot2 = jnp.sum(oh2f, axis=0, keepdims=True)
        return eid1, eid2, w1, w2, oh1f, oh2f, c1, c2, tot1, tot2

    parts = [sub_tile(slice(j * ts, (j + 1) * ts)) for j in range(tm // ts)]

    carry = carry_scr[...]
    for j, (eid1, eid2, w1, w2, oh1f, oh2f, c1, c2, tot1, tot2) in enumerate(parts):
        r1 = jnp.sum(oh1f * (c1 + carry), axis=-1, keepdims=True)
        r2 = jnp.sum(oh2f * (c2 + carry + tot1), axis=-1, keepdims=True)
        carry = carry + tot1 + tot2
        route = jnp.where(lane == 0, eid1.astype(F32), 0.0)
        route = jnp.where(lane == 1, eid2.astype(F32), route)
        route = jnp.where(lane == 2, w1, route)
        route = jnp.where(lane == 3, w2, route)
        route = jnp.where(lane == 4, r1, route)
        route = jnp.where(lane == 5, r2, route)
        route_ref[j * ts:(j + 1) * ts, :] = route
        route_t = route.T
        for slot in range(2):
            code = ((route_t[slot:slot + 1].astype(jnp.int32) << RANK_BITS)
                    | route_t[4 + slot:5 + slot].astype(jnp.int32))
            code_ref[slot:slot + 1, j * ts:(j + 1) * ts] = code
    carry_scr[...] = carry
    cnt_ref[...] = carry


def _post(x2d, sgu, att, kvm, consts, tm, seq):
    t, d = x2d.shape
    mlen = kvm.shape[0] // (t // seq)
    xw = kvm.shape[1] // 2
    tiles_per_batch = seq // tm
    row = lambda w: pl.BlockSpec((tm, w), lambda i: (i, 0))
    return pl.pallas_call(
        functools.partial(_post_kernel, tm=tm, ts=min(tm, 256), d=d, xw=xw),
        out_shape=(jax.ShapeDtypeStruct((t, d), F32), jax.ShapeDtypeStruct((t * d // LANE, LANE), F32),
                   jax.ShapeDtypeStruct((t, LANE), F32), jax.ShapeDtypeStruct((2, t), jnp.int32),
                   jax.ShapeDtypeStruct((1, LANE), F32)),
        grid=(t // tm,),
        in_specs=[row(d), row(sgu.shape[1]), row(att.shape[1]),
                  pl.BlockSpec((mlen, kvm.shape[1]), lambda i: (i // tiles_per_batch, 0))]
                 + [_const_spec(c.shape) for c in consts],
        out_specs=(row(d), pl.BlockSpec((tm * d // LANE, LANE), lambda i: (i, 0)), row(LANE),
                   pl.BlockSpec((2, tm), lambda i: (0, i)), pl.BlockSpec((1, LANE), lambda i: (0, 0))),
        scratch_shapes=[pltpu.VMEM((1, LANE), F32)],
        compiler_params=_params(("arbitrary",)),
        name="post",
    )(x2d, sgu, att, kvm, *consts)


def _pad_pair_id(row, t):
    return 2 * t + (row & (2 * MOE_BLOCK - 1))


def _sc_invert(pad_starts, code, n):
    info = plsc.get_sparse_core_info()
    nw = info.num_cores * info.num_subcores
    lanes = info.num_lanes
    nslot, t = code.shape
    per_slot = nw // nslot
    chunk = t // per_slot
    nwin = chunk // SC_WINDOW
    assert per_slot * nslot == nw and chunk * per_slot == t and nwin * SC_WINDOW == chunk

    def body(start_hbm, code_hbm, o_hbm, start_v, code_v, dest_v, vals_v):
        wid = lax.axis_index("s") * info.num_cores + lax.axis_index("c")
        slot = wid // per_slot
        off = pl.multiple_of((wid % per_slot) * chunk, chunk)
        pltpu.sync_copy(start_hbm, start_v)
        pltpu.sync_copy(code_hbm.at[slot, pl.ds(off, chunk)], code_v)
        lane = lax.iota(jnp.int32, lanes)

        @pl.loop(0, nwin)
        def _(j):
            first = slot * t + off + j * SC_WINDOW
            for l in range(0, SC_WINDOW, lanes):
                c = code_v[pl.ds(j * SC_WINDOW + l, lanes)]
                seg = plsc.load_gather(start_v, [lax.shift_right_logical(c, RANK_BITS)])
                dest_v[j, pl.ds(l, lanes)] = seg + (c & ((1 << RANK_BITS) - 1))
            for r in range(SC_WINDOW):
                vals_v[r, :] = lane + (first + r)
            pltpu.sync_copy(vals_v, o_hbm.at[dest_v.at[j]])

    return pl.kernel(
        body,
        out_type=jax.ShapeDtypeStruct((n, lanes), jnp.int32),
        mesh=plsc.VectorSubcoreMesh(core_axis_name="c", subcore_axis_name="s"),
        scratch_types=[pltpu.VMEM((N_EXPERTS,), jnp.int32), pltpu.VMEM((chunk,), jnp.int32),
                       pltpu.VMEM((nwin, SC_WINDOW), jnp.int32), pltpu.VMEM((SC_WINDOW, lanes), jnp.int32)],
        compiler_params=pltpu.CompilerParams(needs_layout_passes=False, use_tc_tiling_on_sc=False),
        name="sc_invert",
    )(pad_starts, code)


def _expert_kernel(be_ref, first_ref, nxt_ref, nused_ref, ids_a_ref, ids_b_ref,
                   xn_hbm, wg_hbm, wu_hbm, wd_hbm, out_hbm,
                   xbuf, ybuf, wg_st, wu_st, wd_st, wg_bf, wu_bf, wd_bf, wsem, gsem, ssem, *, t):
    g = pl.program_id(0)
    rows = MOE_BLOCK
    k = xbuf.shape[1] // rows

    def fetch(e):
        return (pltpu.make_async_copy(wg_hbm.at[e], wg_st, wsem.at[0]),
                pltpu.make_async_copy(wu_hbm.at[e], wu_st, wsem.at[1]),
                pltpu.make_async_copy(wd_hbm.at[e], wd_st, wsem.at[2]))

    def token_of(p):
        if t & (t - 1) == 0:
            return p & (t - 1)
        return jnp.where(p >= 2 * t, p - 2 * t, jnp.where(p >= t, p - t, p))

    def gather_row(ids_ref, off, r, slot):
        tok = token_of(ids_ref[0, 0, off + r])
        return pltpu.make_async_copy(xn_hbm.at[pl.ds(pl.multiple_of(tok * k, k), k)],
                                     xbuf.at[slot, pl.ds(r * k, k)], gsem.at[slot])

    def gather_all(slot):
        return pltpu.make_async_copy(xn_hbm.at[pl.ds(0, rows * k)], xbuf.at[slot], gsem.at[slot])

    def scatter_all(slot):
        return pltpu.make_async_copy(ybuf.at[slot], out_hbm.at[pl.ds(0, rows * k)], ssem.at[slot])

    def switch_weights(m):
        @pl.when(first_ref[m] == 1)
        def _():
            for cp in fetch(be_ref[m]):
                cp.wait()
            wg_bf[...] = wg_st[...].astype(BF16)
            wu_bf[...] = wu_st[...].astype(BF16)
            wd_bf[...] = wd_st[...].astype(BF16)

            @pl.when(nxt_ref[m] >= 0)
            def _():
                for cp in fetch(nxt_ref[m]):
                    cp.start()

    def phase(slot, next_ids, next_off, prev_ids, prev_off):
        other = 1 - slot
        xb = _load_tile_rows(xbuf, rows, k, lead=(slot,)).astype(BF16)
        gate = jax.nn.silu(_dot(xb, wg_bf[...]))
        for r in range(rows):
            gather_row(next_ids, next_off, r, other).start(priority=r % 2)
        hid = (gate * _dot(xb, wu_bf[...])).astype(BF16)
        for r in range(rows):
            dst = prev_ids[0, 0, prev_off + r]
            pltpu.make_async_copy(ybuf.at[other, pl.ds(r * k, k)],
                                  out_hbm.at[pl.ds(pl.multiple_of(dst * k, k), k)],
                                  ssem.at[other]).start(priority=r % 2)
        _store_tile_rows(ybuf, 0, _dot(hid, wd_bf[...]), lead=(slot,))

    @pl.when(g == 0)
    def _():
        for cp in fetch(be_ref[0]):
            cp.start()

        def first_block(r, carry):
            gather_row(ids_a_ref, rows, r, 0).start()
            return carry
        lax.fori_loop(0, rows, first_block, 0)
        ybuf[...] = jnp.zeros_like(ybuf)
        for half in range(2):
            cp = pltpu.make_async_copy(
                ybuf.at[half], out_hbm.at[pl.ds((2 * t + half * rows) * k, rows * k)], ssem.at[half])
            cp.start()
            cp.wait()

    nused = nused_ref[0]

    def run_phase(slot, m, off):
        @pl.when(m <= nused)
        def _():
            switch_weights(m)
            gather_all(slot).wait()

            @pl.when(m >= 1)
            def _():
                scatter_all(slot).wait()
            phase(slot, ids_b_ref, off, ids_a_ref, off)

            @pl.when(m == nused)
            def _():
                gather_all(1 - slot).wait()
                scatter_all(1 - slot).wait()

    run_phase(0, 2 * g, 0)
    run_phase(1, 2 * g + 1, rows)


def _expert(blk_expert, first, nxt, nused, ids, xn3, w_gate, w_up, w_down):
    d, eh = w_gate.shape[1:]
    k = d // LANE
    t = xn3.shape[0] // k
    steps = ids.shape[0] - 1
    assert blk_expert.shape[0] == 2 * steps
    hbm = pl.BlockSpec(memory_space=pl.ANY)
    idx_block = lambda f: pl.BlockSpec((1, 1, 2 * MOE_BLOCK), f, memory_space=pltpu.SMEM)
    grid_spec = pltpu.PrefetchScalarGridSpec(
        num_scalar_prefetch=4,
        grid=(steps,),
        in_specs=[idx_block(lambda g, *_: (g, 0, 0)), idx_block(lambda g, *_: (g + 1, 0, 0)),
                  hbm, hbm, hbm, hbm],
        out_specs=hbm,
        scratch_shapes=[pltpu.VMEM((2, MOE_BLOCK * k, LANE), F32), pltpu.VMEM((2, MOE_BLOCK * k, LANE), F32),
                        pltpu.VMEM((d, eh), F32), pltpu.VMEM((d, eh), F32), pltpu.VMEM((eh, d), F32),
                        pltpu.VMEM((d, eh), BF16), pltpu.VMEM((d, eh), BF16), pltpu.VMEM((eh, d), BF16),
                        pltpu.SemaphoreType.DMA((3,)), pltpu.SemaphoreType.DMA((2,)),
                        pltpu.SemaphoreType.DMA((2,))],
    )
    return pl.pallas_call(
        functools.partial(_expert_kernel, t=t),
        out_shape=jax.ShapeDtypeStruct(((2 * t + 2 * MOE_BLOCK) * k, LANE), F32),
        grid_spec=grid_spec,
        compiler_params=_params(("arbitrary",)),
        name="expert",
    )(blk_expert, first, nxt, nused, ids, ids, xn3, w_gate, w_up, w_down)


def _combine_kernel(h2_ref, y0_ref, y1_ref, route_ref, g_ref, o_ref):
    tm, d = h2_ref.shape
    route = route_ref[...]
    y0 = _load_tile_rows(y0_ref, tm, d // LANE)
    y1 = _load_tile_rows(y1_ref, tm, d // LANE)
    y = route[:, 2:3] * y0 + route[:, 3:4] * y1
    o_ref[...] = _rms(h2_ref[...] + y, g_ref[...])


def _combine(h2, ypairs, route, g, tm):
    t, d = h2.shape
    k = d // LANE
    assert ypairs.shape[0] % (tm * k) == 0
    return pl.pallas_call(
        _combine_kernel,
        out_shape=jax.ShapeDtypeStruct((t, d), F32),
        grid=(t // tm,),
        in_specs=[pl.BlockSpec((tm, d), lambda i: (i, 0)),
                  pl.BlockSpec((tm * k, LANE), lambda i: (i, 0)),
                  pl.BlockSpec((tm * k, LANE), lambda i: (i + t // tm, 0)),
                  pl.BlockSpec((tm, LANE), lambda i: (i, 0)),
                  _const_spec(g.shape)],
        out_specs=pl.BlockSpec((tm, d), lambda i: (i, 0)),
        compiler_params=_params(("parallel",)),
        name="combine",
    )(h2, ypairs, ypairs, route, g)


def kernel(x, mem, norm_mix, w_in, w_gate, b_gate, sgu_norm, w_spatial, b_spatial, w_branch_a,
           w_branch_b, rel_bias, w_o, norm_x, norm_mem, w_xq, w_xkv, w_xo, norm_ffn,
           w_router_group, b_router_group, w_router_expert, b_router_expert,
           w_e_gate, w_e_up, w_e_down, norm_final):
    batch, seq, d = x.shape
    assert norm_mix.shape[0] == 1, "one layer"
    assert seq % MOBA_BLOCK == 0 and d % LANE == 0
    t = batch * seq
    tm_in, tm_post = 512, 512
    row = lambda a: a.reshape(1, -1).astype(F32)
    bf = lambda a: a.astype(BF16)

    x2d = x.reshape(t, d)
    kvm = _memkv(mem.reshape(-1, d), row(norm_mem[0]), bf(w_xkv[0]))

    width = w_in.shape[2] // 5
    bsp_full = jnp.repeat(b_spatial[0].T, width // SGU_GROUPS, axis=1)
    sgu, q, k, v, kmean = _inproj(x2d, row(norm_mix[0]), bf(w_in[0]), row(sgu_norm[0]),
                                  w_spatial[0], bsp_full, tm_in)

    rel_t = rel_bias.T.astype(F32)
    own, prev = _relbias(rel_t, MOBA_BLOCK)
    att = _attn(rel_t[:, NUM_BUCKETS - 1], q, k, v,
                kmean.reshape(batch, seq // MOBA_BLOCK, width), own, prev, batch, seq)

    lane_pad = LANE - N_GROUPS - N_EXPERTS
    w_router = jnp.concatenate([w_router_group[0], w_router_expert[0], jnp.zeros((d, lane_pad), F32)], axis=1)
    b_router = jnp.concatenate([b_router_group[0], b_router_expert[0], jnp.zeros((lane_pad,), F32)])
    wr_hi = bf(w_router)
    wr_lo = bf(w_router - wr_hi.astype(F32))
    consts = [row(norm_mix[0]), bf(w_gate[0]), row(b_gate[0]), bf(w_branch_a[0]), bf(w_branch_b[0]),
              bf(w_o[0]), row(norm_x[0]), bf(w_xq[0]), bf(w_xo[0]), row(norm_ffn[0]),
              jnp.concatenate([wr_hi, wr_lo], axis=1), row(b_router)]
    assert 2 * t <= 1 << RANK_BITS
    h2, xn3, route, code, cnt = _post(x2d, sgu, att, kvm, consts, tm_post, seq)

    counts = cnt[0, :N_EXPERTS].astype(jnp.int32)
    padded = (counts + MOE_BLOCK - 1) // MOE_BLOCK * MOE_BLOCK
    pad_ends = jnp.cumsum(padded)
    pad_starts = pad_ends - padded
    nblk = -(-(2 * t) // MOE_BLOCK) + N_EXPERTS
    assert nblk % 2 == 0
    nphase = nblk + 2
    blk_idx = jnp.arange(nphase, dtype=jnp.int32)
    blk_expert = jnp.minimum(
        jnp.sum((pad_ends[None, :] <= blk_idx[:, None] * MOE_BLOCK).astype(jnp.int32), axis=1),
        N_EXPERTS - 1)
    nused = pad_ends[-1] // MOE_BLOCK
    prev_expert = jnp.concatenate([jnp.full((1,), -1, jnp.int32), blk_expert[:-1]])
    first = ((blk_idx < nused) & (blk_expert != prev_expert)).astype(jnp.int32)
    eidx = jnp.arange(N_EXPERTS, dtype=jnp.int32)
    later = (eidx[None, :] > eidx[:, None]) & (padded[None, :] > 0)
    next_expert = jnp.min(jnp.where(later, eidx[None, :], N_EXPERTS), axis=1)
    next_expert = jnp.where(next_expert == N_EXPERTS, -1, next_expert)
    onehot = (blk_expert[:, None] == eidx[None, :]).astype(jnp.int32)
    nxt = jnp.where(first == 1, jnp.sum(onehot * next_expert[None, :], axis=1), -1).astype(jnp.int32)
    seg_left = jnp.sum(onehot * (pad_starts + counts)[None, :], axis=1) - blk_idx * MOE_BLOCK
    nvalid = jnp.where(blk_idx < nused, jnp.clip(seg_left, 0, MOE_BLOCK), 0).astype(jnp.int32)[:nblk]

    inv = _sc_invert(pad_starts, code, nblk * MOE_BLOCK)[:, 0].reshape(nblk, MOE_BLOCK)
    row_in_blk = jnp.arange(MOE_BLOCK, dtype=jnp.int32)[None, :]
    pad_ids = _pad_pair_id(jnp.arange(nblk, dtype=jnp.int32)[:, None] * MOE_BLOCK + row_in_blk, t)
    inv = jnp.where(row_in_blk < nvalid[:, None], inv, pad_ids).reshape(nblk * MOE_BLOCK)
    dummy = lambda m: _pad_pair_id(m * MOE_BLOCK + jnp.arange(MOE_BLOCK, dtype=jnp.int32), t)
    ids = jnp.concatenate([dummy(-1), inv] + [dummy(nblk + m) for m in range(3)])
    ypairs = _expert(blk_expert, first, nxt, nused.reshape(1), ids.reshape(-1, 1, 2 * MOE_BLOCK), xn3,
                     w_e_gate[0], w_e_up[0], w_e_down[0])
    out = _combine(h2, ypairs, route, row(norm_final), tm_post)
    return out.reshape(batch, seq, d)
```

```python
import functools
import math

import numpy as np
import jax
import jax.numpy as jnp
from jax import lax
from jax.experimental import pallas as pl
from jax.experimental.pallas import tpu as pltpu
from jax.experimental.pallas import tpu_sc as plsc

F32 = jnp.float32
BF16 = jnp.bfloat16

EPS = 1e-6
NEG = -1e30
LOG2E = math.log2(math.e)
LANE = 128
ONES_ROWS = 16
RANK_BITS = 20
SC_WINDOW = 128

SGU_GROUPS = 8
CHUNK = 128
ATT_HEADS = 8
ATT_HEAD_DIM = 128
MOBA_BLOCK = 256
MOBA_TOPK = 3
NUM_BUCKETS = 32
MAX_DISTANCE = 128
X_HEADS = 4
X_HEAD_DIM = 128
N_GROUPS = 8
EXPERTS_PER_GROUP = 8
N_EXPERTS = N_GROUPS * EXPERTS_PER_GROUP
MOE_BLOCK = 256

VMEM_LIMIT = 56 * 1024 * 1024

_NT = (((1,), (1,)), ((), ()))


def _rms(x, g):
    return x * lax.rsqrt(jnp.mean(x * x, axis=-1, keepdims=True) + EPS) * g


def _gelu(x):
    c = math.sqrt(2.0 / math.pi)
    return x * (0.5 * (1.0 + jnp.tanh(c * (x + 0.044715 * (x * x * x)))))


def _dot(a, b):
    return jnp.dot(a, b, preferred_element_type=F32)


def _store_tile_rows(ref, first_row, val, lead=()):
    n, d = val.shape
    k = d // LANE
    for j in range(k):
        ref[lead + (pl.ds(first_row * k + j, n, stride=k), slice(None))] = val[:, j * LANE:(j + 1) * LANE]


def _load_tile_rows(ref, n, k, lead=()):
    return jnp.concatenate([ref[lead + (pl.ds(j, n, stride=k), slice(None))] for j in range(k)], axis=1)


def _const_spec(shape):
    nd = len(shape)
    return pl.BlockSpec(shape, lambda *_: (0,) * nd, pipeline_mode=pl.Buffered(1))


def _params(sem):
    return pltpu.CompilerParams(dimension_semantics=sem, vmem_limit_bytes=VMEM_LIMIT)


def _memkv_kernel(mem_ref, g_ref, w_ref, o_ref):
    mn = _rms(mem_ref[...], g_ref[...]).astype(BF16)
    o_ref[...] = _dot(mn, w_ref[...]).astype(BF16)


def _memkv(mem2d, g, w_bf):
    n, d = mem2d.shape
    tm = 256
    return pl.pallas_call(
        _memkv_kernel,
        out_shape=jax.ShapeDtypeStruct((n, w_bf.shape[1]), BF16),
        grid=(n // tm,),
        in_specs=[pl.BlockSpec((tm, d), lambda i: (i, 0)),
                  _const_spec(g.shape), _const_spec(w_bf.shape)],
        out_specs=pl.BlockSpec((tm, w_bf.shape[1]), lambda i: (i, 0)),
        compiler_params=_params(("parallel",)),
        name="memkv",
    )(mem2d, g, w_bf)


def _inproj_kernel(x_ref, g_ref, win_ref, sg_ref, wsp_ref, bsp_ref,
                   sgu_ref, q_ref, k_ref, v_ref, km_ref, *, tm, width, scale):
    xn = _rms(x_ref[...], g_ref[...]).astype(BF16)
    u = _gelu(_dot(xn, win_ref[:, 0:width]))
    vv = _gelu(_dot(xn, win_ref[:, width:2 * width]))
    vn = _rms(vv, sg_ref[...]).astype(BF16)
    gd = width // SGU_GROUPS
    row = lax.broadcasted_iota(jnp.int32, (CHUNK, CHUNK), 0)
    col = lax.broadcasted_iota(jnp.int32, (CHUNK, CHUNK), 1)
    causal = col <= row
    wsp = [jnp.where(causal, wsp_ref[g], 0.0).astype(BF16) for g in range(SGU_GROUPS)]
    bias = bsp_ref[...]
    for c in range(tm // CHUNK):
        rs = slice(c * CHUNK, (c + 1) * CHUNK)
        mixed = jnp.concatenate(
            [_dot(wsp[g], vn[rs, g * gd:(g + 1) * gd]) for g in range(SGU_GROUPS)], axis=1)
        sgu_ref[rs, :] = (u[rs, :] * (mixed + bias)).astype(BF16)
    q_ref[...] = (_dot(xn, win_ref[:, 2 * width:3 * width]) * scale).astype(BF16)
    k = _dot(xn, win_ref[:, 3 * width:4 * width])
    k_ref[...] = k.astype(BF16)
    for j in range(tm // MOBA_BLOCK):
        km_ref[0, j:j + 1, :] = jnp.mean(k[j * MOBA_BLOCK:(j + 1) * MOBA_BLOCK, :], axis=0, keepdims=True)
    v_ref[...] = _dot(xn, win_ref[:, 4 * width:5 * width]).astype(BF16)


def _inproj(x2d, g, win_bf, sg, wsp, bsp_full, tm):
    t, d = x2d.shape
    width = win_bf.shape[1] // 5
    nkm = tm // MOBA_BLOCK
    row_spec = pl.BlockSpec((tm, width), lambda i: (i, 0))
    act = jax.ShapeDtypeStruct((t, width), BF16)
    return pl.pallas_call(
        functools.partial(_inproj_kernel, tm=tm, width=width, scale=ATT_HEAD_DIM ** -0.5 * LOG2E),
        out_shape=(act, act, act, act, jax.ShapeDtypeStruct((t // tm, nkm, width), F32)),
        grid=(t // tm,),
        in_specs=[pl.BlockSpec((tm, d), lambda i: (i, 0)),
                  _const_spec(g.shape), _const_spec(win_bf.shape), _const_spec(sg.shape),
                  _const_spec(wsp.shape), _const_spec(bsp_full.shape)],
        out_specs=(row_spec, row_spec, row_spec, row_spec,
                   pl.BlockSpec((1, nkm, width), lambda i: (i, 0, 0))),
        compiler_params=_params(("parallel",)),
        name="inproj",
    )(x2d, g, win_bf, sg, wsp, bsp_full)


def _bucket_uppers():
    d = np.arange(0, 4 * MAX_DISTANCE)
    max_exact = NUM_BUCKETS // 2
    nf = np.maximum(d, 1).astype(np.float32)
    large = max_exact + (np.log(nf / max_exact) / math.log(MAX_DISTANCE / max_exact)
                         * (NUM_BUCKETS - max_exact)).astype(np.int32)
    bucket = np.where(d < max_exact, d, np.minimum(large, NUM_BUCKETS - 1))
    return [int(d[bucket > b].min()) for b in range(NUM_BUCKETS - 1)]


def _relbias_kernel(rb_ref, own_ref, prev_ref, *, tq, uppers):
    h = pl.program_id(0)
    key = lax.broadcasted_iota(jnp.int32, (tq, tq), 0)
    qry = lax.broadcasted_iota(jnp.int32, (tq, tq), 1)

    def table(dist):
        val = jnp.full((tq, tq), rb_ref[h, NUM_BUCKETS - 1] * LOG2E, F32)
        for b in range(NUM_BUCKETS - 2, -1, -1):
            val = jnp.where(dist < uppers[b], rb_ref[h, b] * LOG2E, val)
        return val

    d = qry - key
    own_ref[0] = jnp.where(d >= 0, table(d), NEG)
    prev_ref[0] = table(d + tq)


def _relbias(rel_bias_t, tq):
    nh = rel_bias_t.shape[0]
    tile = jax.ShapeDtypeStruct((nh, tq, tq), F32)
    spec = pl.BlockSpec((1, tq, tq), lambda h: (h, 0, 0))
    return pl.pallas_call(
        functools.partial(_relbias_kernel, tq=tq, uppers=_bucket_uppers()),
        out_shape=(tile, tile),
        grid=(nh,),
        in_specs=[pl.BlockSpec(memory_space=pltpu.SMEM)],
        out_specs=(spec, spec),
        compiler_params=_params(("parallel",)),
        name="relbias",
    )(rel_bias_t)


def _attn_kernel(c31_ref, q_ref, k_ref, v_ref, km_ref, own_ref, prev_ref, o_ref,
                 vt_scr, add_scr, s_scr, acc_scr, *, tq, nb, hg):
    hd = ATT_HEAD_DIM
    heads = range(hg)
    cols = [slice(g * hd, (g + 1) * hd) for g in heads]
    c31 = [c31_ref[pl.program_id(1) * hg + g] * LOG2E for g in heads]
    for g in heads:
        for n in range(nb):
            vt_scr[g, n // 2, :hd, (n % 2) * tq:(n % 2 + 1) * tq] = (
                v_ref[n * tq:(n + 1) * tq, cols[g]].astype(F32).T.astype(BF16))
        vt_scr[g, :, hd:, :] = jnp.ones((nb // 2, ONES_ROWS, 2 * tq), BF16)

    km2 = []
    for g in heads:
        km = km_ref[0, :, cols[g]]
        km_hi = km.astype(BF16)
        km2.append(jnp.concatenate([km_hi, (km - km_hi.astype(F32)).astype(BF16)], axis=1))

    def part_max(s):
        return jnp.max(s.reshape(s.shape[0] // 8, 8, tq), axis=0)

    def q_tile(cur, carry):
        rows = pl.ds(pl.multiple_of(cur * tq, tq), tq)
        q = [q_ref[rows, cols[g]] for g in heads]
        blk = lax.broadcasted_iota(jnp.int32, (nb, tq), 0)
        past = blk < cur

        negm_prev = []
        for g in heads:
            gate = lax.dot_general(km2[g], jnp.concatenate([q[g], q[g]], axis=1), _NT,
                                   preferred_element_type=F32)
            gv = jnp.where(past, gate, NEG)
            sel = jnp.zeros((nb, tq), F32)
            for _ in range(MOBA_TOPK):
                mx = jnp.max(gv, axis=0, keepdims=True)
                first = jnp.min(jnp.where(gv == mx, blk, nb), axis=0, keepdims=True)
                hit = blk == first
                sel = jnp.where(hit & past, 1.0, sel)
                gv = jnp.where(hit, -jnp.inf, gv)
            negm = (sel - 1.0) * (-NEG)
            add_scr[g] = jnp.where(blk < cur - 1, negm + c31[g], NEG)
            negm_prev.append(jnp.sum(jnp.where(blk == cur - 1, negm, 0.0), axis=0, keepdims=True))

        def far_pair(i, mx8):
            out = []
            for g in heads:
                kb = k_ref[pl.ds(pl.multiple_of(i * (2 * tq), 2 * tq), 2 * tq), cols[g]]
                s2 = lax.dot_general(kb, q[g], _NT, preferred_element_type=F32)
                s_lo = s2[:tq] + add_scr[g, pl.ds(2 * i, 1), :]
                s_hi = s2[tq:] + add_scr[g, pl.ds(2 * i + 1, 1), :]
                s_scr[g, i, :tq, :] = s_lo
                s_scr[g, i, tq:, :] = s_hi
                out.append(jnp.maximum(mx8[g], jnp.maximum(part_max(s_lo), part_max(s_hi))))
            return tuple(out)

        nfar = jnp.maximum(cur - 1, 0)
        mx8 = lax.fori_loop(0, (nfar + 1) // 2, far_pair,
                            tuple(jnp.full((8, tq), NEG, F32) for _ in heads))

        def one_block(n, add_tiles, mx8):
            out = []
            for g in heads:
                kb = k_ref[pl.ds(pl.multiple_of(n * tq, tq), tq), cols[g]]
                s = lax.dot_general(kb, q[g], _NT, preferred_element_type=F32) + add_tiles[g]
                s_scr[g, n // 2, pl.ds(pl.multiple_of((n % 2) * tq, tq), tq), :] = s
                out.append(jnp.maximum(mx8[g], part_max(s)))
            return tuple(out)

        mx8 = lax.cond(cur >= 1,
                       lambda a: one_block(cur - 1, [prev_ref[g] + negm_prev[g] for g in heads], a),
                       lambda a: a, mx8)
        mx8 = one_block(cur, [own_ref[g] for g in heads], mx8)

        @pl.when(cur % 2 == 0)
        def _():
            for g in heads:
                s_scr[g, cur // 2, tq:, :] = jnp.full((tq, tq), NEG, F32)

        m = [jnp.max(mx8[g], axis=0, keepdims=True) for g in heads]
        acc_scr[...] = jnp.zeros_like(acc_scr)

        def pv_pair(i, c):
            for g in heads:
                p = jnp.exp2((s_scr[g, i] - m[g]).astype(BF16))
                acc_scr[g] += _dot(vt_scr[g, i], p)
            return c

        lax.fori_loop(0, cur // 2 + 1, pv_pair, 0)
        for g in heads:
            acc = acc_scr[g]
            o_ref[rows, cols[g]] = (acc[:hd] / acc[hd:hd + 1]).T.astype(BF16)
        return carry

    lax.fori_loop(0, nb, q_tile, 0)


def _attn(c31, q, k, v, kmean, own, prev, batch, seq):
    t, width = q.shape
    tq = MOBA_BLOCK
    nb = seq // tq
    hd = ATT_HEAD_DIM
    hg = 4
    nh = width // hd
    assert nb % 2 == 0 and nh % hg == 0
    seq_spec = pl.BlockSpec((seq, hg * hd), lambda b, h: (b, h), pipeline_mode=pl.Buffered(1))
    seq_in_spec = pl.BlockSpec((seq, hg * hd), lambda b, h: (b, h))
    tile_spec = pl.BlockSpec((hg, tq, tq), lambda b, h: (h, 0, 0))
    return pl.pallas_call(
        functools.partial(_attn_kernel, tq=tq, nb=nb, hg=hg),
        out_shape=jax.ShapeDtypeStruct((t, width), BF16),
        grid=(batch, nh // hg),
        in_specs=[pl.BlockSpec(memory_space=pltpu.SMEM),
                  seq_in_spec, seq_in_spec, seq_in_spec,
                  pl.BlockSpec((1, nb, hg * hd), lambda b, h: (b, 0, h)),
                  tile_spec, tile_spec],
        out_specs=seq_spec,
        scratch_shapes=[pltpu.VMEM((hg, nb // 2, hd + ONES_ROWS, 2 * tq), BF16),
                        pltpu.VMEM((hg, nb, tq), F32),
                        pltpu.VMEM((hg, nb // 2, 2 * tq, tq), F32),
                        pltpu.VMEM((hg, hd + ONES_ROWS, tq), F32)],
        compiler_params=_params(("parallel", "parallel")),
        name="attn",
    )(c31, q, k, v, kmean, own, prev)


def _post_kernel(x_ref, sgu_ref, att_ref, kvm_ref, nmix_ref, wg_ref, bg_ref, wa_ref, wb_ref,
                 wo_ref, nx_ref, wxq_ref, wxo_ref, nffn_ref, wr_ref, br_ref,
                 h2_ref, xn3_ref, route_ref, code_ref, cnt_ref, carry_scr, *, tm, ts, d, xw):
    @pl.when(pl.program_id(0) == 0)
    def _():
        carry_scr[...] = jnp.zeros_like(carry_scr)

    lane = lax.broadcasted_iota(jnp.int32, (ts, LANE), 1)
    r = lax.broadcasted_iota(jnp.int32, (ts, ts), 0)
    c = lax.broadcasted_iota(jnp.int32, (ts, ts), 1)
    tri = jnp.where(c < r, 1.0, 0.0).astype(BF16)

    def sub_tile(rows):
        x = x_ref[rows, :]
        xn = _rms(x, nmix_ref[...]).astype(BF16)
        gates = jax.nn.sigmoid(_dot(xn, wg_ref[...]) + bg_ref[...])
        ya = _dot(sgu_ref[rows, :], wa_ref[...])
        yb = _dot(att_ref[rows, :], wb_ref[...])
        merged = (gates[:, :d] * ya + gates[:, d:] * yb).astype(BF16)
        h1 = x + _dot(merged, wo_ref[...])

        xn2 = _rms(h1, nx_ref[...]).astype(BF16)
        qx = (_dot(xn2, wxq_ref[...]) * (X_HEAD_DIM ** -0.5)).astype(BF16)
        outs = []
        for hh in range(X_HEADS):
            cs = slice(hh * X_HEAD_DIM, (hh + 1) * X_HEAD_DIM)
            vs = slice(xw + hh * X_HEAD_DIM, xw + (hh + 1) * X_HEAD_DIM)
            s = lax.dot_general(qx[:, cs], kvm_ref[:, cs], _NT, preferred_element_type=F32)
            p = jnp.exp(s - jnp.max(s, axis=-1, keepdims=True))
            o = _dot(p.astype(BF16), kvm_ref[:, vs]) / jnp.sum(p, axis=-1, keepdims=True)
            outs.append(o.astype(BF16))
        h2 = h1 + _dot(jnp.concatenate(outs, axis=1), wxo_ref[...])
        h2_ref[rows, :] = h2

        xn3 = _rms(h2, nffn_ref[...])
        _store_tile_rows(xn3_ref, rows.start, xn3)
        x_hi = xn3.astype(BF16)
        x_lo = (xn3 - x_hi.astype(F32)).astype(BF16)
        terms = _dot(jnp.concatenate([x_hi, x_lo], axis=0), wr_ref[...])
        logits = ((terms[:ts, :LANE] + terms[ts:, :LANE]) + (terms[:ts, LANE:] + terms[ts:, LANE:])
                  + br_ref[...])
        is_g = lane < N_GROUPS
        gl = jnp.where(is_g, logits, -jnp.inf)
        gmax = jnp.max(gl, axis=-1, keepdims=True)
        gsel = jnp.min(jnp.where(gl == gmax, lane, LANE), axis=-1, keepdims=True)
        pg = 1.0 / jnp.sum(jnp.exp(gl - gmax), axis=-1, keepdims=True)
        eidx = lane - N_GROUPS
        in_group = (eidx >= 0) & (eidx < N_EXPERTS) & ((eidx // EXPERTS_PER_GROUP) == gsel)
        el = jnp.where(in_group, logits, -jnp.inf)
        m1 = jnp.max(el, axis=-1, keepdims=True)
        i1 = jnp.min(jnp.where(el == m1, lane, LANE), axis=-1, keepdims=True)
        el2 = jnp.where(lane == i1, -jnp.inf, el)
        m2 = jnp.max(el2, axis=-1, keepdims=True)
        i2 = jnp.min(jnp.where(el2 == m2, lane, LANE), axis=-1, keepdims=True)
        e2 = jnp.exp(m2 - m1)
        w1 = pg / (1.0 + e2)
        w2 = pg * e2 / (1.0 + e2)
        eid1 = i1 - N_GROUPS
        eid2 = i2 - N_GROUPS

        oh1f = jnp.where(lane == eid1, 1.0, 0.0)
        oh2f = jnp.where(lane == eid2, 1.0, 0.0)
        c1 = _dot(tri, oh1f.astype(BF16))
        c2 = _dot(tri, oh2f.astype(BF16))
        tot1 = jnp.sum(oh1f, axis=0, keepdims=True)
        tot2 = jnp.sum(oh2f, axis=0, keepdims=True)
        return eid1, eid2, w1, w2, oh1f, oh2f, c1, c2, tot1, tot2

    parts = [sub_tile(slice(j * ts, (j + 1) * ts)) for j in range(tm // ts)]

    carry = carry_scr[...]
    for j, (eid1, eid2, w1, w2, oh1f, oh2f, c1, c2, tot1, tot2) in enumerate(parts):
        r1 = jnp.sum(oh1f * (c1 + carry), axis=-1, keepdims=True)
        r2 = jnp.sum(oh2f * (c2 + carry + tot1), axis=-1, keepdims=True)
        carry = carry + tot1 + tot2
        route = jnp.where(lane == 0, eid1.astype(F32), 0.0)
        route = jnp.where(lane == 1, eid2.astype(F32), route)
        route = jnp.where(lane == 2, w1, route)
        route = jnp.where(lane == 3, w2, route)
        route = jnp.where(lane == 4, r1, route)
        route = jnp.where(lane == 5, r2, route)
        route_ref[j * ts:(j + 1) * ts, :] = route
        route_t = route.T
        for slot in range(2):
            code = ((route_t[slot:slot + 1].astype(jnp.int32) << RANK_BITS)
                    | route_t[4 + slot:5 + slot].astype(jnp.int32))
            code_ref[slot:slot + 1, j * ts:(j + 1) * ts] = code
    carry_scr[...] = carry
    cnt_ref[...] = carry


def _post(x2d, sgu, att, kvm, consts, tm, seq):
    t, d = x2d.shape
    mlen = kvm.shape[0] // (t // seq)
    xw = kvm.shape[1] // 2
    tiles_per_batch = seq // tm
    row = lambda w: pl.BlockSpec((tm, w), lambda i: (i, 0))
    return pl.pallas_call(
        functools.partial(_post_kernel, tm=tm, ts=min(tm, 256), d=d, xw=xw),
        out_shape=(jax.ShapeDtypeStruct((t, d), F32), jax.ShapeDtypeStruct((t * d // LANE, LANE), F32),
                   jax.ShapeDtypeStruct((t, LANE), F32), jax.ShapeDtypeStruct((2, t), jnp.int32),
                   jax.ShapeDtypeStruct((1, LANE), F32)),
        grid=(t // tm,),
        in_specs=[row(d), row(sgu.shape[1]), row(att.shape[1]),
                  pl.BlockSpec((mlen, kvm.shape[1]), lambda i: (i // tiles_per_batch, 0))]
                 + [_const_spec(c.shape) for c in consts],
        out_specs=(row(d), pl.BlockSpec((tm * d // LANE, LANE), lambda i: (i, 0)), row(LANE),
                   pl.BlockSpec((2, tm), lambda i: (0, i)), pl.BlockSpec((1, LANE), lambda i: (0, 0))),
        scratch_shapes=[pltpu.VMEM((1, LANE), F32)],
        compiler_params=_params(("arbitrary",)),
        name="post",
    )(x2d, sgu, att, kvm, *consts)


def _pad_pair_id(row, t):
    return 2 * t + (row & (2 * MOE_BLOCK - 1))


def _sc_invert(pad_starts, code, n):
    info = plsc.get_sparse_core_info()
    nw = info.num_cores * info.num_subcores
    lanes = info.num_lanes
    nslot, t = code.shape
    per_slot = nw // nslot
    chunk = t // per_slot
    nwin = chunk // SC_WINDOW
    assert per_slot * nslot == nw and chunk * per_slot == t and nwin * SC_WINDOW == chunk

    def body(start_hbm, code_hbm, o_hbm, start_v, code_v, dest_v, vals_v):
        wid = lax.axis_index("s") * info.num_cores + lax.axis_index("c")
        slot = wid // per_slot
        off = pl.multiple_of((wid % per_slot) * chunk, chunk)
        pltpu.sync_copy(start_hbm, start_v)
        pltpu.sync_copy(code_hbm.at[slot, pl.ds(off, chunk)], code_v)
        lane = lax.iota(jnp.int32, lanes)

        @pl.loop(0, nwin)
        def _(j):
            first = slot * t + off + j * SC_WINDOW
            for l in range(0, SC_WINDOW, lanes):
                c = code_v[pl.ds(j * SC_WINDOW + l, lanes)]
                seg = plsc.load_gather(start_v, [lax.shift_right_logical(c, RANK_BITS)])
                dest_v[j, pl.ds(l, lanes)] = seg + (c & ((1 << RANK_BITS) - 1))
            for r in range(SC_WINDOW):
                vals_v[r, :] = lane + (first + r)
            pltpu.sync_copy(vals_v, o_hbm.at[dest_v.at[j]])

    return pl.kernel(
        body,
        out_type=jax.ShapeDtypeStruct((n, lanes), jnp.int32),
        mesh=plsc.VectorSubcoreMesh(core_axis_name="c", subcore_axis_name="s"),
        scratch_types=[pltpu.VMEM((N_EXPERTS,), jnp.int32), pltpu.VMEM((chunk,), jnp.int32),
                       pltpu.VMEM((nwin, SC_WINDOW), jnp.int32), pltpu.VMEM((SC_WINDOW, lanes), jnp.int32)],
        compiler_params=pltpu.CompilerParams(needs_layout_passes=False, use_tc_tiling_on_sc=False),
        name="sc_invert",
    )(pad_starts, code)


def _expert_kernel(be_ref, first_ref, nxt_ref, nused_ref, ids_a_ref, ids_b_ref,
                   xn_hbm, wg_hbm, wu_hbm, wd_hbm, out_hbm,
                   xbuf, ybuf, wg_st, wu_st, wd_st, wg_bf, wu_bf, wd_bf, wsem, gsem, ssem, *, t):
    g = pl.program_id(0)
    rows = MOE_BLOCK
    k = xbuf.shape[1] // rows

    def fetch(e):
        return (pltpu.make_async_copy(wg_hbm.at[e], wg_st, wsem.at[0]),
                pltpu.make_async_copy(wu_hbm.at[e], wu_st, wsem.at[1]),
                pltpu.make_async_copy(wd_hbm.at[e], wd_st, wsem.at[2]))

    def token_of(p):
        if t & (t - 1) == 0:
            return p & (t - 1)
        return jnp.where(p >= 2 * t, p - 2 * t, jnp.where(p >= t, p - t, p))

    def gather_row(ids_ref, off, r, slot):
        tok = token_of(ids_ref[0, 0, off + r])
        return pltpu.make_async_copy(xn_hbm.at[pl.ds(pl.multiple_of(tok * k, k), k)],
                                     xbuf.at[slot, pl.ds(r * k, k)], gsem.at[slot])

    def gather_all(slot):
        return pltpu.make_async_copy(xn_hbm.at[pl.ds(0, rows * k)], xbuf.at[slot], gsem.at[slot])

    def scatter_all(slot):
        return pltpu.make_async_copy(ybuf.at[slot], out_hbm.at[pl.ds(0, rows * k)], ssem.at[slot])

    def switch_weights(m):
        @pl.when(first_ref[m] == 1)
        def _():
            for cp in fetch(be_ref[m]):
                cp.wait()
            wg_bf[...] = wg_st[...].astype(BF16)
            wu_bf[...] = wu_st[...].astype(BF16)
            wd_bf[...] = wd_st[...].astype(BF16)

            @pl.when(nxt_ref[m] >= 0)
            def _():
                for cp in fetch(nxt_ref[m]):
                    cp.start()

    def phase(slot, next_ids, next_off, prev_ids, prev_off):
        other = 1 - slot
        for r in range(rows):
            gather_row(next_ids, next_off, r, other).start(priority=0)
        for r in range(rows):
            dst = prev_ids[0, 0, prev_off + r]
            pltpu.make_async_copy(ybuf.at[other, pl.ds(r * k, k)],
                                  out_hbm.at[pl.ds(pl.multiple_of(dst * k, k), k)],
                                  ssem.at[other]).start(priority=1)
        xb = _load_tile_rows(xbuf, rows, k, lead=(slot,)).astype(BF16)
        hid = jax.nn.silu(_dot(xb, wg_bf[...])) * _dot(xb, wu_bf[...])
        _store_tile_rows(ybuf, 0, _dot(hid.astype(BF16), wd_bf[...]), lead=(slot,))

    @pl.when(g == 0)
    def _():
        for cp in fetch(be_ref[0]):
            cp.start()

        def first_block(r, carry):
            gather_row(ids_a_ref, rows, r, 0).start()
            return carry
        lax.fori_loop(0, rows, first_block, 0)
        ybuf[...] = jnp.zeros_like(ybuf)
        for half in range(2):
            cp = pltpu.make_async_copy(
                ybuf.at[half], out_hbm.at[pl.ds((2 * t + half * rows) * k, rows * k)], ssem.at[half])
            cp.start()
            cp.wait()

    nused = nused_ref[0]

    def run_phase(slot, m, off):
        @pl.when(m <= nused)
        def _():
            switch_weights(m)
            gather_all(slot).wait()

            @pl.when(m >= 1)
            def _():
                scatter_all(slot).wait()
            phase(slot, ids_b_ref, off, ids_a_ref, off)

            @pl.when(m == nused)
            def _():
                gather_all(1 - slot).wait()
                scatter_all(1 - slot).wait()

    run_phase(0, 2 * g, 0)
    run_phase(1, 2 * g + 1, rows)


def _expert(blk_expert, first, nxt, nused, ids, xn3, w_gate, w_up, w_down):
    d, eh = w_gate.shape[1:]
    k = d // LANE
    t = xn3.shape[0] // k
    steps = ids.shape[0] - 1
    assert blk_expert.shape[0] == 2 * steps
    hbm = pl.BlockSpec(memory_space=pl.ANY)
    idx_block = lambda f: pl.BlockSpec((1, 1, 2 * MOE_BLOCK), f, memory_space=pltpu.SMEM)
    grid_spec = pltpu.PrefetchScalarGridSpec(
        num_scalar_prefetch=4,
        grid=(steps,),
        in_specs=[idx_block(lambda g, *_: (g, 0, 0)), idx_block(lambda g, *_: (g + 1, 0, 0)),
                  hbm, hbm, hbm, hbm],
        out_specs=hbm,
        scratch_shapes=[pltpu.VMEM((2, MOE_BLOCK * k, LANE), F32), pltpu.VMEM((2, MOE_BLOCK * k, LANE), F32),
                        pltpu.VMEM((d, eh), F32), pltpu.VMEM((d, eh), F32), pltpu.VMEM((eh, d), F32),
                        pltpu.VMEM((d, eh), BF16), pltpu.VMEM((d, eh), BF16), pltpu.VMEM((eh, d), BF16),
                        pltpu.SemaphoreType.DMA((3,)), pltpu.SemaphoreType.DMA((2,)),
                        pltpu.SemaphoreType.DMA((2,))],
    )
    return pl.pallas_call(
        functools.partial(_expert_kernel, t=t),
        out_shape=jax.ShapeDtypeStruct(((2 * t + 2 * MOE_BLOCK) * k, LANE), F32),
        grid_spec=grid_spec,
        compiler_params=_params(("arbitrary",)),
        name="expert",
    )(blk_expert, first, nxt, nused, ids, ids, xn3, w_gate, w_up, w_down)


def _combine_kernel(h2_ref, y0_ref, y1_ref, route_ref, g_ref, o_ref):
    tm, d = h2_ref.shape
    route = route_ref[...]
    y0 = _load_tile_rows(y0_ref, tm, d // LANE)
    y1 = _load_tile_rows(y1_ref, tm, d // LANE)
    y = route[:, 2:3] * y0 + route[:, 3:4] * y1
    o_ref[...] = _rms(h2_ref[...] + y, g_ref[...])


def _combine(h2, ypairs, route, g, tm):
    t, d = h2.shape
    k = d // LANE
    assert ypairs.shape[0] % (tm * k) == 0
    return pl.pallas_call(
        _combine_kernel,
        out_shape=jax.ShapeDtypeStruct((t, d), F32),
        grid=(t // tm,),
        in_specs=[pl.BlockSpec((tm, d), lambda i: (i, 0)),
                  pl.BlockSpec((tm * k, LANE), lambda i: (i, 0)),
                  pl.BlockSpec((tm * k, LANE), lambda i: (i + t // tm, 0)),
                  pl.BlockSpec((tm, LANE), lambda i: (i, 0)),
                  _const_spec(g.shape)],
        out_specs=pl.BlockSpec((tm, d), lambda i: (i, 0)),
        compiler_params=_params(("parallel",)),
        name="combine",
    )(h2, ypairs, ypairs, route, g)


def kernel(x, mem, norm_mix, w_in, w_gate, b_gate, sgu_norm, w_spatial, b_spatial, w_branch_a,
           w_branch_b, rel_bias, w_o, norm_x, norm_mem, w_xq, w_xkv, w_xo, norm_ffn,
           w_router_group, b_router_group, w_router_expert, b_router_expert,
           w_e_gate, w_e_up, w_e_down, norm_final):
    batch, seq, d = x.shape
    assert norm_mix.shape[0] == 1, "one layer"
    assert seq % MOBA_BLOCK == 0 and d % LANE == 0
    t = batch * seq
    tm_in, tm_post = 512, 512
    row = lambda a: a.reshape(1, -1).astype(F32)
    bf = lambda a: a.astype(BF16)

    x2d = x.reshape(t, d)
    kvm = _memkv(mem.reshape(-1, d), row(norm_mem[0]), bf(w_xkv[0]))

    width = w_in.shape[2] // 5
    bsp_full = jnp.repeat(b_spatial[0].T, width // SGU_GROUPS, axis=1)
    sgu, q, k, v, kmean = _inproj(x2d, row(norm_mix[0]), bf(w_in[0]), row(sgu_norm[0]),
                                  w_spatial[0], bsp_full, tm_in)

    rel_t = rel_bias.T.astype(F32)
    own, prev = _relbias(rel_t, MOBA_BLOCK)
    att = _attn(rel_t[:, NUM_BUCKETS - 1], q, k, v,
                kmean.reshape(batch, seq // MOBA_BLOCK, width), own, prev, batch, seq)

    lane_pad = LANE - N_GROUPS - N_EXPERTS
    w_router = jnp.concatenate([w_router_group[0], w_router_expert[0], jnp.zeros((d, lane_pad), F32)], axis=1)
    b_router = jnp.concatenate([b_router_group[0], b_router_expert[0], jnp.zeros((lane_pad,), F32)])
    wr_hi = bf(w_router)
    wr_lo = bf(w_router - wr_hi.astype(F32))
    consts = [row(norm_mix[0]), bf(w_gate[0]), row(b_gate[0]), bf(w_branch_a[0]), bf(w_branch_b[0]),
              bf(w_o[0]), row(norm_x[0]), bf(w_xq[0]), bf(w_xo[0]), row(norm_ffn[0]),
              jnp.concatenate([wr_hi, wr_lo], axis=1), row(b_router)]
    assert 2 * t <= 1 << RANK_BITS
    h2, xn3, route, code, cnt = _post(x2d, sgu, att, kvm, consts, tm_post, seq)

    counts = cnt[0, :N_EXPERTS].astype(jnp.int32)
    padded = (counts + MOE_BLOCK - 1) // MOE_BLOCK * MOE_BLOCK
    pad_ends = jnp.cumsum(padded)
    pad_starts = pad_ends - padded
    nblk = -(-(2 * t) // MOE_BLOCK) + N_EXPERTS
    assert nblk % 2 == 0
    nphase = nblk + 2
    blk_idx = jnp.arange(nphase, dtype=jnp.int32)
    blk_expert = jnp.minimum(
        jnp.sum((pad_ends[None, :] <= blk_idx[:, None] * MOE_BLOCK).astype(jnp.int32), axis=1),
        N_EXPERTS - 1)
    nused = pad_ends[-1] // MOE_BLOCK
    prev_expert = jnp.concatenate([jnp.full((1,), -1, jnp.int32), blk_expert[:-1]])
    first = ((blk_idx < nused) & (blk_expert != prev_expert)).astype(jnp.int32)
    eidx = jnp.arange(N_EXPERTS, dtype=jnp.int32)
    later = (eidx[None, :] > eidx[:, None]) & (padded[None, :] > 0)
    next_expert = jnp.min(jnp.where(later, eidx[None, :], N_EXPERTS), axis=1)
    next_expert = jnp.where(next_expert == N_EXPERTS, -1, next_expert)
    onehot = (blk_expert[:, None] == eidx[None, :]).astype(jnp.int32)
    nxt = jnp.where(first == 1, jnp.sum(onehot * next_expert[None, :], axis=1), -1).astype(jnp.int32)
    seg_left = jnp.sum(onehot * (pad_starts + counts)[None, :], axis=1) - blk_idx * MOE_BLOCK
    nvalid = jnp.where(blk_idx < nused, jnp.clip(seg_left, 0, MOE_BLOCK), 0).astype(jnp.int32)[:nblk]

    inv = _sc_invert(pad_starts, code, nblk * MOE_BLOCK)[:, 0].reshape(nblk, MOE_BLOCK)
    row_in_blk = jnp.arange(MOE_BLOCK, dtype=jnp.int32)[None, :]
    pad_ids = _pad_pair_id(jnp.arange(nblk, dtype=jnp.int32)[:, None] * MOE_BLOCK + row_in_blk, t)
    inv = jnp.where(row_in_blk < nvalid[:, None], inv, pad_ids).reshape(nblk * MOE_BLOCK)
    dummy = lambda m: _pad_pair_id(m * MOE_BLOCK + jnp.arange(MOE_BLOCK, dtype=jnp.int32), t)
    ids = jnp.concatenate([dummy(-1), inv] + [dummy(nblk + m) for m in range(3)])
    ypairs = _expert(blk_expert, first, nxt, nused.reshape(1), ids.reshape(-1, 1, 2 * MOE_BLOCK), xn3,
                     w_e_gate[0], w_e_up[0], w_e_down[0])
    out = _combine(h2, ypairs, route, row(norm_final), tm_post)
    return out.reshape(batch, seq, d)
```
